```python
import math
import jax, jax.numpy as jnp
from jax import lax
import numpy as np

D_MODEL = 2048
BATCH = 8
SEQ = 2048
DEPTH = 1
DEC_BATCH = 128
DEC_SEQ = 1
PAST_LEN = 2048
PAGE_SIZE = 128

N_HEADS = 8
HEAD_DIM = 128
ATTN_WIDTH = N_HEADS * HEAD_DIM
MOBA_BLOCK = 256
MOBA_TOPK = 3
Q_CHUNK = 16
CONV_CH = D_MODEL // 2
CONV_WIDTH = 31
D_FF = 4 * D_MODEL
NUM_BUCKETS = 32
MAX_DISTANCE = 128
LN_EPS = 1e-5
NEG_INF = -1e30
DN_ALPHA = (2 * DEPTH) ** 0.25
DN_BETA = (8 * DEPTH) ** -0.25
IN_COLS = 3 * ATTN_WIDTH + 2 * CONV_CH + 2 * D_MODEL
IN_SPLITS = [ATTN_WIDTH, 2 * ATTN_WIDTH, 3 * ATTN_WIDTH, 3 * ATTN_WIDTH + CONV_CH,
             3 * ATTN_WIDTH + 2 * CONV_CH, 3 * ATTN_WIDTH + 2 * CONV_CH + D_MODEL]

kernel_name = "moba_conformer_gated_hybrid_decode_step"


def layer_norm(x, g, b):
    xf = x.astype(jnp.float32)
    mu = jnp.mean(xf, axis=-1, keepdims=True)
    var = jnp.mean(jnp.square(xf - mu), axis=-1, keepdims=True)
    return ((xf - mu) * lax.rsqrt(var + LN_EPS)).astype(x.dtype) * g + b


def rel_bucket(dist):
    n = jnp.maximum(dist, 0)
    max_exact = NUM_BUCKETS // 2
    nf = jnp.maximum(n, 1).astype(jnp.float32)
    large = max_exact + (jnp.log(nf / max_exact) / math.log(MAX_DISTANCE / max_exact)
                         * (NUM_BUCKETS - max_exact)).astype(jnp.int32)
    large = jnp.minimum(large, NUM_BUCKETS - 1)
    return jnp.where(n < max_exact, n, large)


def moba_attention(q, k, v, q_pos0, rel_bias):
    n, tq, h, dh = q.shape
    l = k.shape[1]
    nb = max(-(-l // MOBA_BLOCK), MOBA_TOPK)
    pad = nb * MOBA_BLOCK - l
    kb = jnp.pad(k, ((0, 0), (0, pad), (0, 0), (0, 0))).reshape(n, nb, MOBA_BLOCK, h, dh).transpose(0, 3, 1, 2, 4)
    vb = jnp.pad(v, ((0, 0), (0, pad), (0, 0), (0, 0))).reshape(n, nb, MOBA_BLOCK, h, dh).transpose(0, 3, 1, 2, 4)
    kmean = jnp.mean(kb.astype(jnp.float32), axis=3)
    chunk = Q_CHUNK if tq % Q_CHUNK == 0 else tq
    nc = tq // chunk
    qc = q.transpose(0, 2, 1, 3).reshape(n, h, nc, chunk, dh).transpose(2, 0, 1, 3, 4)
    starts = q_pos0 + jnp.arange(nc, dtype=jnp.int32) * chunk
    bi = jnp.arange(n)[:, None, None, None]
    hi = jnp.arange(h)[None, :, None, None]
    head_idx = jnp.arange(h)[:, None, None, None]
    bias_t = rel_bias.T
    scale = HEAD_DIM ** -0.5

    def one_chunk(args):
        qi, p0 = args
        pos = p0 + jnp.arange(chunk, dtype=jnp.int32)
        cur = pos // MOBA_BLOCK
        gate = jnp.einsum('bhcd,bhnd->bhcn', qi.astype(jnp.float32), kmean)
        fully_past = jnp.arange(nb)[None, :] < cur[:, None]
        gate = jnp.where(fully_past, gate, NEG_INF)
        _, top = lax.top_k(gate, MOBA_TOPK)
        top_valid = top < cur[:, None]
        own = jnp.broadcast_to(cur[:, None], top.shape[:-1] + (1,)).astype(top.dtype)
        sel = jnp.concatenate([top, own], axis=-1)
        slot_valid = jnp.concatenate([top_valid, jnp.ones(own.shape, bool)], axis=-1)
        ks = kb[bi, hi, sel]
        vs = vb[bi, hi, sel]
        kpos = sel[..., None] * MOBA_BLOCK + jnp.arange(MOBA_BLOCK, dtype=jnp.int32)
        dist = pos[:, None, None] - kpos
        mask = slot_valid[..., None] & (dist >= 0)
        bias = bias_t[head_idx, rel_bucket(dist)].astype(jnp.float32)
        logits = jnp.einsum('bhcd,bhcskd->bhcsk', qi, ks).astype(jnp.float32) * scale + bias
        logits = jnp.where(mask, logits, NEG_INF)
        shp = logits.shape
        probs = jax.nn.softmax(logits.reshape(shp[:3] + (-1,)), axis=-1).reshape(shp)
        return jnp.einsum('bhcsk,bhcskd->bhcd', probs.astype(vs.dtype), vs)

    outs = lax.map(one_chunk, (qc, starts))
    return outs.transpose(1, 0, 3, 2, 4).reshape(n, tq, h * dh)


def decoder_layer(x, k_past, v_past, conv_buf, q_pos0, rel_bias, params):
    (w_in, w_attn_out, conv_w, conv_b, conv_ln_g, conv_ln_b, w_conv_out, w_out,
     ln1_g, ln1_b, w_ff1, w_ff2, ln2_g, ln2_b) = params
    n, t, _ = x.shape
    proj = x @ w_in
    q, k, v, glu_val, glu_gate, gate_a, gate_c = jnp.split(proj, IN_SPLITS, axis=-1)
    q = q.reshape(n, t, N_HEADS, HEAD_DIM)
    k = k.reshape(n, t, N_HEADS, HEAD_DIM)
    v = v.reshape(n, t, N_HEADS, HEAD_DIM)
    if k_past is None:
        k_all, v_all = k, v
    else:
        k_all = jnp.concatenate([k_past, k], axis=1)
        v_all = jnp.concatenate([v_past, v], axis=1)
    a = moba_attention(q, k_all, v_all, q_pos0, rel_bias) @ w_attn_out
    u = glu_val * jax.nn.sigmoid(glu_gate)
    ub = jnp.concatenate([conv_buf, u], axis=1)
    c = lax.conv_general_dilated(ub, conv_w[:, None, :], window_strides=(1,), padding='VALID',
                                 dimension_numbers=('NWC', 'WIO', 'NWC'),
                                 feature_group_count=CONV_CH) + conv_b
    c = jax.nn.silu(layer_norm(c, conv_ln_g, conv_ln_b)) @ w_conv_out
    new_buf = ub[:, -(CONV_WIDTH - 1):]
    mixed = (jax.nn.sigmoid(gate_a) * a + jax.nn.sigmoid(gate_c) * c) @ w_out
    x1 = layer_norm(DN_ALPHA * x + mixed, ln1_g, ln1_b)
    hid = jnp.square(jax.nn.relu(x1 @ w_ff1)) @ w_ff2
    x2 = layer_norm(DN_ALPHA * x1 + hid, ln2_g, ln2_b)
    return x2, k, v, new_buf


def setup_inputs(seed: int = 0) -> dict:
    key = jax.random.key(seed)
    ks = jax.random.split(key, 32)

    def nrm(k_, shape, scale):
        return jax.random.normal(k_, shape, jnp.float32) * scale

    n_pages = PAST_LEN // PAGE_SIZE
    n_used = DEC_BATCH * n_pages
    n_pool = n_used + max(1, n_used // 4)
    perm = jax.random.permutation(ks[0], n_pool)
    page_table = perm[:n_used].reshape(DEC_BATCH, n_pages).astype(jnp.int32)

    w_in = jnp.concatenate([
        nrm(ks[1], (DEPTH, D_MODEL, 2 * ATTN_WIDTH), D_MODEL ** -0.5),
        nrm(ks[2], (DEPTH, D_MODEL, ATTN_WIDTH), D_MODEL ** -0.5 * DN_BETA),
        nrm(ks[3], (DEPTH, D_MODEL, 2 * CONV_CH + 2 * D_MODEL), D_MODEL ** -0.5)], axis=-1)

    return {
        "x_prompt": nrm(ks[4], (BATCH, SEQ, D_MODEL), 1.0),
        "x_sample": nrm(ks[5], (DEC_BATCH, DEC_SEQ, D_MODEL), 1.0),
        "cache_k": nrm(ks[6], (DEPTH, n_pool, PAGE_SIZE, N_HEADS, HEAD_DIM), 1.0),
        "cache_v": nrm(ks[7], (DEPTH, n_pool, PAGE_SIZE, N_HEADS, HEAD_DIM), 1.0),
        "state_conv": nrm(ks[8], (DEPTH, DEC_BATCH, CONV_WIDTH - 1, CONV_CH), 0.5),
        "page_table": page_table,
        "rel_bias": nrm(ks[9], (NUM_BUCKETS, N_HEADS), 0.1),
        "w_in": w_in,
        "w_attn_out": nrm(ks[10], (DEPTH, ATTN_WIDTH, D_MODEL), ATTN_WIDTH ** -0.5 * DN_BETA),
        "conv_w": nrm(ks[11], (DEPTH, CONV_WIDTH, CONV_CH), CONV_WIDTH ** -0.5),
        "conv_b": nrm(ks[12], (DEPTH, CONV_CH), 0.02),
        "conv_ln_g": 1.0 + nrm(ks[13], (DEPTH, CONV_CH), 0.02),
        "conv_ln_b": nrm(ks[14], (DEPTH, CONV_CH), 0.02),
        "w_conv_out": nrm(ks[15], (DEPTH, CONV_CH, D_MODEL), CONV_CH ** -0.5 * DN_BETA),
        "w_out": nrm(ks[16], (DEPTH, D_MODEL, D_MODEL), D_MODEL ** -0.5 * DN_BETA),
        "ln1_g": 1.0 + nrm(ks[17], (DEPTH, D_MODEL), 0.02),
        "ln1_b": nrm(ks[18], (DEPTH, D_MODEL), 0.02),
        "w_ff1": nrm(ks[19], (DEPTH, D_MODEL, D_FF), D_MODEL ** -0.5 * DN_BETA),
        "w_ff2": nrm(ks[20], (DEPTH, D_FF, D_MODEL), D_FF ** -0.5 * DN_BETA),
        "ln2_g": 1.0 + nrm(ks[21], (DEPTH, D_MODEL), 0.02),
        "ln2_b": nrm(ks[22], (DEPTH, D_MODEL), 0.02),
    }


def reference(x_prompt, x_sample, cache_k, cache_v, state_conv, page_table, rel_bias, w_in,
              w_attn_out, conv_w, conv_b, conv_ln_g, conv_ln_b, w_conv_out, w_out,
              ln1_g, ln1_b, w_ff1, w_ff2, ln2_g, ln2_b):
    n_seq, n_pages = page_table.shape
    past_len = n_pages * cache_k.shape[2]
    hp, hs = x_prompt, x_sample
    kp_l, vp_l, cp_l, ks_l, vs_l, cs_l = [], [], [], [], [], []
    for l in range(DEPTH):
        params = (w_in[l], w_attn_out[l], conv_w[l], conv_b[l], conv_ln_g[l], conv_ln_b[l],
                  w_conv_out[l], w_out[l], ln1_g[l], ln1_b[l], w_ff1[l], w_ff2[l], ln2_g[l], ln2_b[l])
        zero_buf = jnp.zeros((hp.shape[0], CONV_WIDTH - 1, CONV_CH), hp.dtype)
        hp, kp, vp, cp = decoder_layer(hp, None, None, zero_buf, 0, rel_bias, params)
        k_past = cache_k[l][page_table].reshape(n_seq, past_len, N_HEADS, HEAD_DIM)
        v_past = cache_v[l][page_table].reshape(n_seq, past_len, N_HEADS, HEAD_DIM)
        hs, ksm, vsm, csm = decoder_layer(hs, k_past, v_past, state_conv[l], past_len, rel_bias, params)
        kp_l.append(kp); vp_l.append(vp); cp_l.append(cp)
        ks_l.append(ksm); vs_l.append(vsm); cs_l.append(csm)
    new_k_prompt = jnp.stack(kp_l)
    new_v_prompt = jnp.stack(vp_l)
    new_conv_prompt = jnp.stack(cp_l)
    new_k_sample = jnp.stack(ks_l)
    new_v_sample = jnp.stack(vs_l)
    new_conv_sample = jnp.stack(cs_l)
    return (hp, hs, new_k_prompt, new_v_prompt, new_conv_prompt, new_k_sample, new_v_sample, new_conv_sample)
```

```python
import functools
import math

import numpy as np
import jax
import jax.numpy as jnp
from jax import lax
from jax.experimental import pallas as pl
from jax.experimental.pallas import tpu as pltpu

MOBA_BLOCK = 256
MOBA_TOPK = 3
MAX_DISTANCE = 128
LN_EPS = 1e-5
NEG_INF = -1e30

V7X_VMEM_BYTES = 64 * 1024 * 1024
V7X_LANES = 128
V7X_SUBLANES = 8

F32 = jnp.float32
BF16 = jnp.bfloat16


def _vmem_limit(block_bytes):
    return int(min(max(2 * block_bytes + (16 << 20), 32 << 20), V7X_VMEM_BYTES - (8 << 20)))


def _params(semantics, block_bytes):
    return pltpu.CompilerParams(dimension_semantics=semantics, vmem_limit_bytes=_vmem_limit(block_bytes))


def _tile(n, want):
    t = min(n, want)
    while n % t:
        t -= 1
    return t


def _sigmoid(x):
    return 1.0 / (1.0 + jnp.exp(-x))


def _layer_norm(y, g, b):
    mu = jnp.mean(y, axis=-1, keepdims=True)
    d = y - mu
    var = jnp.mean(d * d, axis=-1, keepdims=True)
    return d * lax.rsqrt(var + LN_EPS) * g + b


def _dot(a, b):
    return jnp.dot(a, b, preferred_element_type=F32)


def _dot_nt(a, b):
    return lax.dot_general(a, b, (((1,), (1,)), ((), ())), preferred_element_type=F32)


def _proj_q_kernel(x_ref, w_ref, o_ref, *, scale):
    o_ref[...] = (_dot(x_ref[...], w_ref[...]) * scale).astype(o_ref.dtype)


def _proj_kv_kernel(x_ref, w_ref, of_ref, ob_ref, *rest):
    acc = _dot(x_ref[...], w_ref[...])
    of_ref[...] = acc
    ob_ref[...] = acc.astype(ob_ref.dtype)
    if rest:
        (km_ref,) = rest
        tm, tn = acc.shape
        km_ref[...] = jnp.sum(acc.reshape(tm // MOBA_BLOCK, MOBA_BLOCK, tn), axis=1) * (1.0 / MOBA_BLOCK)


def _proj_glu_kernel(x_ref, wv_ref, wg_ref, u_ref):
    x = x_ref[...]
    u_ref[...] = _dot(x, wv_ref[...]) * _sigmoid(_dot(x, wg_ref[...]))


def _proj_q(xb, w, col0, ncols, scale, tm, out_dtype):
    m, d = xb.shape
    tn = _tile(ncols, 1024)
    tm = _tile(m, tm)
    blocks = tm * d * 2 + d * tn * 2 + tm * tn * 4
    return pl.pallas_call(
        functools.partial(_proj_q_kernel, scale=scale),
        grid=(m // tm, ncols // tn),
        in_specs=[pl.BlockSpec((tm, d), lambda i, j: (i, 0)),
                  pl.BlockSpec((d, tn), lambda i, j: (0, col0 // tn + j))],
        out_specs=pl.BlockSpec((tm, tn), lambda i, j: (i, j)),
        out_shape=jax.ShapeDtypeStruct((m, ncols), out_dtype),
        compiler_params=_params(("parallel", "parallel"), blocks),
        name="proj_q",
    )(xb, w)


def _proj_kv(xb, w, col0, ncols, tm, block_means):
    m, d = xb.shape
    tn = _tile(ncols, 512)
    tm = _tile(m, tm)
    out_specs = [pl.BlockSpec((tm, tn), lambda i, j: (i, j)), pl.BlockSpec((tm, tn), lambda i, j: (i, j))]
    out_shape = [jax.ShapeDtypeStruct((m, ncols), F32), jax.ShapeDtypeStruct((m, ncols), BF16)]
    if block_means:
        nb = tm // MOBA_BLOCK
        out_specs.append(pl.BlockSpec((None, nb, tn), lambda i, j: (i, 0, j)))
        out_shape.append(jax.ShapeDtypeStruct((m // tm, nb, ncols), F32))
    blocks = tm * d * 2 + d * tn * 2 + tm * tn * 6
    return pl.pallas_call(
        _proj_kv_kernel,
        grid=(m // tm, ncols // tn),
        in_specs=[pl.BlockSpec((tm, d), lambda i, j: (i, 0)),
                  pl.BlockSpec((d, tn), lambda i, j: (0, col0 // tn + j))],
        out_specs=out_specs,
        out_shape=out_shape,
        compiler_params=_params(("parallel", "parallel"), blocks),
        name="proj_k" if block_means else "proj_v",
    )(xb, w)


def _proj_glu(xb, w, col_val, col_gate, ncols, tm):
    m, d = xb.shape
    tn = _tile(ncols, 512)
    tm = _tile(m, tm)
    blocks = tm * d * 2 + 2 * d * tn * 2 + tm * tn * 4
    return pl.pallas_call(
        _proj_glu_kernel,
        grid=(m // tm, ncols // tn),
        in_specs=[pl.BlockSpec((tm, d), lambda i, j: (i, 0)),
                  pl.BlockSpec((d, tn), lambda i, j: (0, col_val // tn + j)),
                  pl.BlockSpec((d, tn), lambda i, j: (0, col_gate // tn + j))],
        out_specs=pl.BlockSpec((tm, tn), lambda i, j: (i, j)),
        out_shape=jax.ShapeDtypeStruct((m, ncols), F32),
        compiler_params=_params(("parallel", "parallel"), blocks),
        name="proj_glu",
    )(xb, w, w)


def _rel_bucket_np(dist, num_buckets):
    n = np.maximum(dist, 0)
    max_exact = num_buckets // 2
    nf = np.maximum(n, 1).astype(np.float32)
    large = max_exact + (np.log(nf / np.float32(max_exact)) / np.float32(math.log(MAX_DISTANCE / max_exact))
                         * np.float32(num_buckets - max_exact)).astype(np.int32)
    large = np.minimum(large, num_buckets - 1)
    return np.where(n < max_exact, n, large).astype(np.int32)


def _bias_tiles_kernel(rb_ref, bucket_ref, o_ref, *, num_buckets):
    h = pl.program_id(0)
    for t in range(bucket_ref.shape[0]):
        bucket = bucket_ref[t]
        acc = jnp.full(bucket.shape, NEG_INF, F32)
        for b in range(num_buckets):
            acc = jnp.where(bucket == b, rb_ref[b, h], acc)
        o_ref[t] = acc


def _prompt_bias_tiles(rel_bias):
    num_buckets, n_heads = rel_bias.shape
    assert MAX_DISTANCE <= MOBA_BLOCK + 1
    r = np.arange(MOBA_BLOCK)[:, None]
    c = np.arange(MOBA_BLOCK)[None, :]
    diag = np.where(r - c >= 0, _rel_bucket_np(r - c, num_buckets), -1)
    prev = _rel_bucket_np(r - c + MOBA_BLOCK, num_buckets)
    far = _rel_bucket_np(r - c + 2 * MOBA_BLOCK, num_buckets)
    buckets = jnp.asarray(np.stack([diag, prev, far]).astype(np.int32))
    return pl.pallas_call(
        functools.partial(_bias_tiles_kernel, num_buckets=num_buckets),
        grid=(n_heads,),
        in_specs=[pl.BlockSpec(memory_space=pltpu.SMEM),
                  pl.BlockSpec((3, MOBA_BLOCK, MOBA_BLOCK), lambda h: (0, 0, 0))],
        out_specs=pl.BlockSpec((None, 3, MOBA_BLOCK, MOBA_BLOCK), lambda h: (h, 0, 0, 0)),
        out_shape=jax.ShapeDtypeStruct((n_heads, 3, MOBA_BLOCK, MOBA_BLOCK), F32),
        compiler_params=_params(("arbitrary",), 6 * MOBA_BLOCK * MOBA_BLOCK * 4),
        name="bias_tiles",
    )(rel_bias, buckets)


def _block_penalty_t(gate_t, n_valid):
    row = lax.broadcasted_iota(jnp.int32, gate_t.shape, 0)
    valid = row < n_valid
    pen = jnp.zeros(gate_t.shape, F32)
    for j in range(n_valid):
        gj = gate_t[j:j + 1, :]
        beats = ((gate_t > gj) | ((gate_t == gj) & (row < j))) & valid
        rank = jnp.sum(beats.astype(F32), axis=0, keepdims=True)
        pen = jnp.where(row == j, jnp.where(rank < MOBA_TOPK, 0.0, NEG_INF), pen)
    return pen


def _prompt_attn_kernel(q_ref, k_ref, v_ref, km_ref, bias_ref, o_ref, kx_ref, pen_ref):
    t, dh = q_ref.shape
    blk = MOBA_BLOCK
    nblk = t // blk
    row = lax.broadcasted_iota(jnp.int32, (t, dh), 0)
    col = lax.broadcasted_iota(jnp.int32, (t, dh), 1)
    kx_ref[:, :dh] = k_ref[...]
    kx_ref[:, dh:] = (col == lax.shift_right_logical(row, blk.bit_length() - 1)).astype(BF16)
    km = km_ref[...]
    km_hi = km.astype(BF16)
    km_lo = (km - km_hi.astype(F32)).astype(BF16)
    eye = (lax.broadcasted_iota(jnp.int32, (blk, blk), 0) == lax.broadcasted_iota(jnp.int32, (blk, blk), 1)).astype(BF16)
    pen_ref[...] = jnp.zeros(pen_ref.shape, F32)
    for i in range(nblk):
        qi = q_ref[i * blk:(i + 1) * blk, :]
        nk = (i + 1) * blk
        if i > MOBA_TOPK:
            gate_t = _dot_nt(km_hi, qi) + _dot_nt(km_lo, qi)
            pen_ref[0:nblk, :] = _block_penalty_t(gate_t, i)
            pen = _dot_nt(eye, pen_ref[...].astype(BF16)).astype(BF16)
            s = _dot_nt(jnp.concatenate([qi, pen], axis=1), kx_ref[0:nk, :])
        else:
            s = _dot_nt(qi, k_ref[0:nk, :])
        bias = [bias_ref[2]] * (i - 1) + ([bias_ref[1]] if i > 0 else []) + [bias_ref[0]]
        s = s + (jnp.concatenate(bias, axis=1) if len(bias) > 1 else bias[0])
        m = jnp.max(s, axis=1, keepdims=True)
        p = jnp.exp(s - m)
        l = jnp.sum(p, axis=1, keepdims=True)
        o = _dot(p.astype(BF16), v_ref[0:nk, :])
        o_ref[i * blk:(i + 1) * blk, :] = (o / l).astype(o_ref.dtype)


def _prompt_attention(q, k, v, kmean, bias_tiles, n_heads):
    n, t, width = q.shape
    dh = width // n_heads
    nblk = t // MOBA_BLOCK
    assert dh == V7X_LANES and t % MOBA_BLOCK == 0 and nblk <= V7X_LANES and MOBA_BLOCK & (MOBA_BLOCK - 1) == 0
    seq_spec = pl.BlockSpec((None, t, dh), lambda h, b: (b, 0, h))
    blocks = 4 * t * dh * 2 + 3 * MOBA_BLOCK * MOBA_BLOCK * 4 + t * 2 * dh * 2 + 6 * MOBA_BLOCK * t * 4
    return pl.pallas_call(
        _prompt_attn_kernel,
        grid=(n_heads, n),
        in_specs=[seq_spec, seq_spec, seq_spec,
                  pl.BlockSpec((None, nblk, dh), lambda h, b: (b, 0, h)),
                  pl.BlockSpec((None, 3, MOBA_BLOCK, MOBA_BLOCK), lambda h, b: (h, 0, 0, 0))],
        out_specs=seq_spec,
        out_shape=jax.ShapeDtypeStruct((n, t, width), BF16),
        scratch_shapes=[pltpu.VMEM((t, 2 * dh), BF16), pltpu.VMEM((dh, MOBA_BLOCK), F32)],
        compiler_params=_params(("parallel", "parallel"), blocks),
        name="prompt_attn",
    )(q, k, v, kmean, bias_tiles)


def _decode_attn_kernel(pt_ref, q_ref, kn_ref, vn_ref, relb_ref, *refs, n_pages, page, pages_per_block, buckets):
    del pt_ref
    k_refs = refs[:n_pages]
    v_refs = refs[n_pages:2 * n_pages]
    o_ref = refs[2 * n_pages]
    h, dh = q_ref.shape
    q = q_ref[...]
    ones = jnp.ones((dh, dh), BF16)

    def lane_sum(x):
        return _dot(x.astype(BF16), ones)

    m_p, l_p, o_p, g_p = [], [], [], []
    for p in range(n_pages):
        kp = k_refs[p][...]
        s = lane_sum((kp * q[None]).reshape(page * h, dh)).reshape(page, h, dh)
        g_p.append(jnp.sum(s, axis=0))
        bias = jnp.stack([relb_ref[buckets[p * page + t]] for t in range(page)]) \
            if len(set(buckets[p * page:(p + 1) * page])) > 1 else relb_ref[buckets[p * page]][None]
        s = s + bias
        m = jnp.max(s, axis=0)
        e = jnp.exp(s - m[None])
        m_p.append(m)
        l_p.append(jnp.sum(e, axis=0))
        o_p.append(jnp.sum(e * v_refs[p][...], axis=0))

    n_blocks = n_pages // pages_per_block
    gate = [sum(g_p[j * pages_per_block:(j + 1) * pages_per_block]) for j in range(n_blocks)]
    s_own = lane_sum(q * kn_ref[...]) + relb_ref[0]
    m_tot = s_own
    sel = []
    for j in range(n_blocks):
        rank = jnp.zeros((h, dh), F32)
        for j2 in range(n_blocks):
            if j2 != j:
                beats = (gate[j2] > gate[j]) | ((gate[j2] == gate[j]) & (j2 < j))
                rank = rank + beats.astype(F32)
        sel.append(rank < MOBA_TOPK)
        for p in range(j * pages_per_block, (j + 1) * pages_per_block):
            m_tot = jnp.maximum(m_tot, jnp.where(sel[j], m_p[p], NEG_INF))
    w_own = jnp.exp(s_own - m_tot)
    num = w_own * vn_ref[...]
    den = w_own
    for j in range(n_blocks):
        for p in range(j * pages_per_block, (j + 1) * pages_per_block):
            w = jnp.where(sel[j], jnp.exp(m_p[p] - m_tot), 0.0)
            num = num + w * o_p[p]
            den = den + w * l_p[p]
    o_ref[...] = (num / den).astype(o_ref.dtype)


def _decode_attention(q, k_new, v_new, cache_k, cache_v, page_table, rel_bias):
    s, h, dh = q.shape
    _, page, _, _ = cache_k.shape
    n_pages = page_table.shape[1]
    past = n_pages * page
    num_buckets = rel_bias.shape[0]
    assert dh == V7X_LANES and MOBA_BLOCK % page == 0 and past % MOBA_BLOCK == 0
    buckets = tuple(int(b) for b in _rel_bucket_np(past - np.arange(past), num_buckets))
    relb = jnp.broadcast_to(rel_bias[:, :, None], (num_buckets, h, dh))
    tok_spec = pl.BlockSpec((None, h, dh), lambda b, pt: (b, 0, 0))
    page_specs = [pl.BlockSpec((None, page, h, dh), lambda b, pt, p=p: (pt[b, p], 0, 0, 0)) for p in range(n_pages)]
    blocks = 2 * n_pages * page * h * dh * 4 + num_buckets * h * dh * 4
    kernel = functools.partial(_decode_attn_kernel, n_pages=n_pages, page=page,
                               pages_per_block=MOBA_BLOCK // page, buckets=buckets)
    return pl.pallas_call(
        kernel,
        grid_spec=pltpu.PrefetchScalarGridSpec(
            num_scalar_prefetch=1,
            grid=(s,),
            in_specs=[tok_spec, tok_spec, tok_spec,
                      pl.BlockSpec((num_buckets, h, dh), lambda b, pt: (0, 0, 0))] + page_specs + page_specs,
            out_specs=tok_spec),
        out_shape=jax.ShapeDtypeStruct((s, h, dh), BF16),
        compiler_params=_params(("parallel",), blocks),
        name="decode_attn",
    )(page_table, q, k_new, v_new, relb, *([cache_k] * n_pages), *([cache_v] * n_pages))


CONV_HALO = 32
CONV_ROWS = 64


def _prompt_conv_kernel(halo_ref, u_ref, w_ref, cb_ref, g_ref, b_ref, o_ref, ext_ref, conv_ref):
    tt, ch = u_ref.shape
    kw = w_ref.shape[0]
    first = CONV_HALO - (kw - 1)

    @pl.when(pl.program_id(1) == 0)
    def _():
        ext_ref[0:CONV_HALO, :] = jnp.zeros((CONV_HALO, ch), F32)

    @pl.when(pl.program_id(1) > 0)
    def _():
        ext_ref[0:CONV_HALO, :] = halo_ref[...]

    ext_ref[CONV_HALO:CONV_HALO + tt, :] = u_ref[...]
    rows = min(CONV_ROWS, tt)
    for c0 in range(0, ch, V7X_LANES):
        cs = slice(c0, c0 + V7X_LANES)
        for r0 in range(0, tt, rows):
            acc = jnp.zeros((rows, V7X_LANES), F32)
            for b in range(V7X_SUBLANES):
                taps = [s for s in range(first, first + kw) if s % V7X_SUBLANES == b]
                if not taps:
                    continue
                xb = ext_ref[pl.ds(r0 + taps[0], taps[-1] - taps[0] + rows), cs]
                for s in taps:
                    acc = acc + xb[s - taps[0]:s - taps[0] + rows] * w_ref[s - first:s - first + 1, cs]
            conv_ref[r0:r0 + rows, cs] = acc
    y = _layer_norm(conv_ref[...] + cb_ref[...], g_ref[...], b_ref[...])
    o_ref[...] = (y * _sigmoid(y)).astype(o_ref.dtype)


def _prompt_conv(u, conv_w, conv_b, ln_g, ln_b):
    n, t, ch = u.shape
    kw = conv_w.shape[0]
    tt = _tile(t, 128)
    assert kw - 1 <= CONV_HALO and tt % CONV_HALO == 0 and ch % V7X_LANES == 0
    vec = pl.BlockSpec((1, ch), lambda b, i: (0, 0))
    blocks = (2 * tt + 2 * CONV_HALO) * ch * 4 + tt * ch * 2
    return pl.pallas_call(
        _prompt_conv_kernel,
        grid=(n, t // tt),
        in_specs=[pl.BlockSpec((None, CONV_HALO, ch), lambda b, i: (b, jnp.maximum(i * (tt // CONV_HALO) - 1, 0), 0)),
                  pl.BlockSpec((None, tt, ch), lambda b, i: (b, i, 0)),
                  pl.BlockSpec((kw, ch), lambda b, i: (0, 0)), vec, vec, vec],
        out_specs=pl.BlockSpec((None, tt, ch), lambda b, i: (b, i, 0)),
        out_shape=jax.ShapeDtypeStruct((n, t, ch), BF16),
        scratch_shapes=[pltpu.VMEM((CONV_HALO + tt, ch), F32), pltpu.VMEM((tt, ch), F32)],
        compiler_params=_params(("parallel", "arbitrary"), blocks),
        name="prompt_conv",
    )(u, u, conv_w, conv_b.reshape(1, ch), ln_g.reshape(1, ch), ln_b.reshape(1, ch))


def _decode_conv_kernel(state_ref, u_ref, w_ref, cb_ref, g_ref, b_ref, o_ref, new_state_ref):
    kw = w_ref.shape[0]
    w_hist = w_ref[0:kw - 1, :]
    w_last = w_ref[kw - 1:kw, :]
    for b in range(state_ref.shape[0]):
        u = u_ref[b]
        conv = jnp.sum(state_ref[b] * w_hist, axis=0, keepdims=True) + u * w_last
        y = _layer_norm(conv + cb_ref[...], g_ref[...], b_ref[...])
        o_ref[b] = (y * _sigmoid(y)).astype(o_ref.dtype)
        new_state_ref[b, 0:kw - 2, :] = state_ref[b, 1:kw - 1, :]
        new_state_ref[b, kw - 2:kw - 1, :] = u


def _decode_conv(state, u, conv_w, conv_b, ln_g, ln_b):
    s, hist, ch = state.shape
    kw = conv_w.shape[0]
    ts = _tile(s, 16)
    vec = pl.BlockSpec((1, ch), lambda i: (0, 0))
    tok = pl.BlockSpec((ts, 1, ch), lambda i: (i, 0, 0))
    hist_spec = pl.BlockSpec((ts, hist, ch), lambda i: (i, 0, 0))
    blocks = 2 * ts * 32 * ch * 4 + 2 * ts * V7X_SUBLANES * ch * 4
    return pl.pallas_call(
        _decode_conv_kernel,
        grid=(s // ts,),
        in_specs=[hist_spec, tok, pl.BlockSpec((kw, ch), lambda i: (0, 0)), vec, vec, vec],
        out_specs=[tok, hist_spec],
        out_shape=[jax.ShapeDtypeStruct((s, 1, ch), F32), jax.ShapeDtypeStruct((s, hist, ch), F32)],
        compiler_params=_params(("parallel",), blocks),
        name="decode_conv",
    )(state, u, conv_w, conv_b.reshape(1, ch), ln_g.reshape(1, ch), ln_b.reshape(1, ch))


def _merge_kernel(x_ref, a_ref, c_ref, wga_ref, wgc_ref, wao_ref, wco_ref, o_ref):
    x = x_ref[...]
    mixed = (_sigmoid(_dot(x, wga_ref[...])) * _dot(a_ref[...], wao_ref[...])
             + _sigmoid(_dot(x, wgc_ref[...])) * _dot(c_ref[...], wco_ref[...]))
    o_ref[...] = mixed.astype(o_ref.dtype)


def _merge(xb, attn, cn, w_in, col_ga, col_gc, w_ao, w_co, tm):
    m, d = xb.shape
    wa = attn.shape[1]
    wc = cn.shape[1]
    tn = _tile(d, 512)
    tm = _tile(m, tm)
    blocks = tm * (d + wa + wc + tn) * 2 + (2 * d + wa + wc) * tn * 2
    return pl.pallas_call(
        _merge_kernel,
        grid=(m // tm, d // tn),
        in_specs=[pl.BlockSpec((tm, d), lambda i, j: (i, 0)),
                  pl.BlockSpec((tm, wa), lambda i, j: (i, 0)),
                  pl.BlockSpec((tm, wc), lambda i, j: (i, 0)),
                  pl.BlockSpec((d, tn), lambda i, j: (0, col_ga // tn + j)),
                  pl.BlockSpec((d, tn), lambda i, j: (0, col_gc // tn + j)),
                  pl.BlockSpec((wa, tn), lambda i, j: (0, j)),
                  pl.BlockSpec((wc, tn), lambda i, j: (0, j))],
        out_specs=pl.BlockSpec((tm, tn), lambda i, j: (i, j)),
        out_shape=jax.ShapeDtypeStruct((m, d), BF16),
        compiler_params=_params(("parallel", "parallel"), blocks),
        name="merge",
    )(xb, attn, cn, w_in, w_in, w_ao, w_co)


def _out_ln_kernel(x_ref, mixed_ref, w_ref, g_ref, b_ref, of_ref, ob_ref, *, alpha):
    y = _layer_norm(alpha * x_ref[...] + _dot(mixed_ref[...], w_ref[...]), g_ref[...], b_ref[...])
    of_ref[...] = y
    ob_ref[...] = y.astype(ob_ref.dtype)


def _out_ln(x, mixed, w_out, g, b, alpha, tm):
    m, d = x.shape
    tm = _tile(m, tm)
    vec = pl.BlockSpec((1, d), lambda i: (0, 0))
    row = pl.BlockSpec((tm, d), lambda i: (i, 0))
    blocks = tm * d * (4 + 2 + 4 + 2) + d * d * 2
    return pl.pallas_call(
        functools.partial(_out_ln_kernel, alpha=alpha),
        grid=(m // tm,),
        in_specs=[row, row, pl.BlockSpec((d, d), lambda i: (0, 0)), vec, vec],
        out_specs=[row, row],
        out_shape=[jax.ShapeDtypeStruct((m, d), F32), jax.ShapeDtypeStruct((m, d), BF16)],
        compiler_params=_params(("parallel",), blocks),
        name="out_ln",
    )(x, mixed, w_out, g.reshape(1, d), b.reshape(1, d))


def _mlp_ln_kernel(xf_ref, xb_ref, w1_ref, w2_ref, g_ref, b_ref, o_ref, *, alpha):
    f = pl.program_id(1)
    hid = jnp.maximum(_dot(xb_ref[...], w1_ref[...]), 0.0)
    part = _dot((hid * hid).astype(BF16), w2_ref[...])

    @pl.when(f == 0)
    def _():
        o_ref[...] = part

    @pl.when(f > 0)
    def _():
        o_ref[...] += part

    @pl.when(f == pl.num_programs(1) - 1)
    def _():
        o_ref[...] = _layer_norm(alpha * xf_ref[...] + o_ref[...], g_ref[...], b_ref[...])


def _mlp_ln(xf, xb, w1, w2, g, b, alpha, tm):
    m, d = xf.shape
    dff = w1.shape[1]
    tf = _tile(dff, 1024)
    tm = _tile(m, tm)
    vec = pl.BlockSpec((1, d), lambda i, f: (0, 0))
    row = pl.BlockSpec((tm, d), lambda i, f: (i, 0))
    blocks = tm * d * (4 + 2 + 4) + 2 * d * tf * 2 + tm * tf * 4
    return pl.pallas_call(
        functools.partial(_mlp_ln_kernel, alpha=alpha),
        grid=(m // tm, dff // tf),
        in_specs=[row, row, pl.BlockSpec((d, tf), lambda i, f: (0, f)), pl.BlockSpec((tf, d), lambda i, f: (f, 0)),
                  vec, vec],
        out_specs=row,
        out_shape=jax.ShapeDtypeStruct((m, d), F32),
        compiler_params=_params(("parallel", "arbitrary"), blocks),
        name="mlp_ln",
    )(xf, xb, w1, w2, g.reshape(1, d), b.reshape(1, d))


def _dense_tail(x2d, xb, attn, cn, lw, alpha, tm):
    mixed = _merge(xb, attn, cn, lw["w_in"], lw["col_ga"], lw["col_gc"], lw["w_attn_out"], lw["w_conv_out"], tm=2 * tm)
    x1f, x1b = _out_ln(x2d, mixed, lw["w_out"], lw["ln1_g"], lw["ln1_b"], alpha, tm)
    return _mlp_ln(x1f, x1b, lw["w_ff1"], lw["w_ff2"], lw["ln2_g"], lw["ln2_b"], alpha, tm)


def kernel(x_prompt, x_sample, cache_k, cache_v, state_conv, page_table, rel_bias, w_in, w_attn_out, conv_w, conv_b,
           conv_ln_g, conv_ln_b, w_conv_out, w_out, ln1_g, ln1_b, w_ff1, w_ff2, ln2_g, ln2_b):
    depth = w_in.shape[0]
    n, t, d = x_prompt.shape
    s, ts, _ = x_sample.shape
    n_heads, dh = cache_k.shape[-2:]
    aw = n_heads * dh
    ch = conv_w.shape[-1]
    alpha = (2 * depth) ** 0.25
    scale = dh ** -0.5
    assert ts == 1 and w_in.shape[-1] == 3 * aw + 2 * ch + 2 * d
    col_k, col_v, col_val, col_gate = aw, 2 * aw, 3 * aw, 3 * aw + ch
    col_ga, col_gc = 3 * aw + 2 * ch, 3 * aw + 2 * ch + d

    bias_tiles = _prompt_bias_tiles(rel_bias)
    hp = x_prompt.reshape(n * t, d)
    hs = x_sample.reshape(s, d)
    outs = [[] for _ in range(6)]
    for l in range(depth):
        lw = dict(w_in=w_in[l].astype(BF16), w_attn_out=w_attn_out[l].astype(BF16),
                  w_conv_out=w_conv_out[l].astype(BF16), w_out=w_out[l].astype(BF16),
                  w_ff1=w_ff1[l].astype(BF16), w_ff2=w_ff2[l].astype(BF16),
                  ln1_g=ln1_g[l], ln1_b=ln1_b[l], ln2_g=ln2_g[l], ln2_b=ln2_b[l], col_ga=col_ga, col_gc=col_gc)
        wi = lw["w_in"]

        xb = hp.astype(BF16)
        q = _proj_q(xb, wi, 0, aw, scale, tm=1024, out_dtype=BF16)
        kf, kb, kmean = _proj_kv(xb, wi, col_k, aw, tm=t, block_means=True)
        vf, vb = _proj_kv(xb, wi, col_v, aw, tm=t, block_means=False)
        u = _proj_glu(xb, wi, col_val, col_gate, ch, tm=1024)
        attn = _prompt_attention(q.reshape(n, t, aw), kb.reshape(n, t, aw), vb.reshape(n, t, aw), kmean,
                                 bias_tiles, n_heads)
        u3 = u.reshape(n, t, ch)
        cn = _prompt_conv(u3, conv_w[l], conv_b[l], conv_ln_g[l], conv_ln_b[l])
        hp_next = _dense_tail(hp, xb, attn.reshape(n * t, aw), cn.reshape(n * t, ch), lw, alpha, tm=512)
        outs[0].append(kf.reshape(n, t, n_heads, dh))
        outs[1].append(vf.reshape(n, t, n_heads, dh))
        outs[2].append(u3[:, t - (conv_w.shape[1] - 1):, :])

        xsb = hs.astype(BF16)
        qs = _proj_q(xsb, wi, 0, aw, scale, tm=s, out_dtype=F32)
        ksf, _ = _proj_kv(xsb, wi, col_k, aw, tm=s, block_means=False)
        vsf, _ = _proj_kv(xsb, wi, col_v, aw, tm=s, block_means=False)
        us = _proj_glu(xsb, wi, col_val, col_gate, ch, tm=s)
        attn_s = _decode_attention(qs.reshape(s, n_heads, dh), ksf.reshape(s, n_heads, dh), vsf.reshape(s, n_heads, dh),
                                   cache_k[l], cache_v[l], page_table, rel_bias)
        cn_s, new_state = _decode_conv(state_conv[l], us.reshape(s, 1, ch), conv_w[l], conv_b[l], conv_ln_g[l],
                                       conv_ln_b[l])
        hs_next = _dense_tail(hs, xsb, attn_s.reshape(s, aw), cn_s.reshape(s, ch).astype(BF16), lw, alpha, tm=s)
        outs[3].append(ksf.reshape(s, 1, n_heads, dh))
        outs[4].append(vsf.reshape(s, 1, n_heads, dh))
        outs[5].append(new_state)
        hp, hs = hp_next, hs_next

    return (hp.reshape(n, t, d), hs.reshape(s, 1, d)) + tuple(jnp.stack(o) for o in outs)
```

```python
import functools
import math

import numpy as np
import jax
import jax.numpy as jnp
from jax import lax
from jax.experimental import pallas as pl
from jax.experimental.pallas import tpu as pltpu

MOBA_BLOCK = 256
MOBA_TOPK = 3
MAX_DISTANCE = 128
LN_EPS = 1e-5
NEG_INF = -1e30

V7X_VMEM_BYTES = 64 * 1024 * 1024
V7X_LANES = 128
V7X_SUBLANES = 8

F32 = jnp.float32
BF16 = jnp.bfloat16


def _vmem_limit(block_bytes):
    return int(min(max(2 * block_bytes + (16 << 20), 32 << 20), V7X_VMEM_BYTES - (8 << 20)))


def _params(semantics, block_bytes):
    return pltpu.CompilerParams(dimension_semantics=semantics, vmem_limit_bytes=_vmem_limit(block_bytes))


def _tile(n, want):
    t = min(n, want)
    while n % t:
        t -= 1
    return t


def _sigmoid(x):
    return 1.0 / (1.0 + jnp.exp(-x))


def _layer_norm(y, g, b):
    mu = jnp.mean(y, axis=-1, keepdims=True)
    d = y - mu
    var = jnp.mean(d * d, axis=-1, keepdims=True)
    return d * lax.rsqrt(var + LN_EPS) * g + b


def _dot(a, b):
    return jnp.dot(a, b, preferred_element_type=F32)


def _dot_nt(a, b):
    return lax.dot_general(a, b, (((1,), (1,)), ((), ())), preferred_element_type=F32)


def _proj_q_kernel(x_ref, w_ref, o_ref, xb_ref, *, scale):
    xb = x_ref[...].astype(BF16)
    xb_ref[...] = xb
    o_ref[...] = (_dot(xb, w_ref[...]) * scale).astype(o_ref.dtype)


def _proj_kv_kernel(x_ref, w_ref, of_ref, ob_ref, *rest):
    acc = _dot(x_ref[...], w_ref[...])
    of_ref[...] = acc
    ob_ref[...] = acc.astype(ob_ref.dtype)
    if rest:
        (km_ref,) = rest
        tm, tn = acc.shape
        km_ref[...] = jnp.sum(acc.reshape(tm // MOBA_BLOCK, MOBA_BLOCK, tn), axis=1) * (1.0 / MOBA_BLOCK)


def _proj_glu_kernel(x_ref, wv_ref, wg_ref, u_ref):
    x = x_ref[...]
    u_ref[...] = _dot(x, wv_ref[...]) * _sigmoid(_dot(x, wg_ref[...]))


def _proj_q(x, w, col0, ncols, scale, tm, out_dtype):
    m, d = x.shape
    tn = _tile(ncols, 1024)
    tm = _tile(m, tm)
    blocks = tm * d * 6 + d * tn * 2 + tm * tn * 4
    return pl.pallas_call(
        functools.partial(_proj_q_kernel, scale=scale),
        grid=(m // tm, ncols // tn),
        in_specs=[pl.BlockSpec((tm, d), lambda i, j: (i, 0)),
                  pl.BlockSpec((d, tn), lambda i, j: (0, col0 // tn + j))],
        out_specs=[pl.BlockSpec((tm, tn), lambda i, j: (i, j)), pl.BlockSpec((tm, d), lambda i, j: (i, 0))],
        out_shape=[jax.ShapeDtypeStruct((m, ncols), out_dtype), jax.ShapeDtypeStruct((m, d), BF16)],
        compiler_params=_params(("parallel", "arbitrary"), blocks),
        name="proj_q",
    )(x, w)


def _proj_kv(xb, w, col0, ncols, tm, block_means):
    m, d = xb.shape
    tn = _tile(ncols, 512)
    tm = _tile(m, tm)
    out_specs = [pl.BlockSpec((tm, tn), lambda i, j: (i, j)), pl.BlockSpec((tm, tn), lambda i, j: (i, j))]
    out_shape = [jax.ShapeDtypeStruct((m, ncols), F32), jax.ShapeDtypeStruct((m, ncols), BF16)]
    if block_means:
        nb = tm // MOBA_BLOCK
        out_specs.append(pl.BlockSpec((None, nb, tn), lambda i, j: (i, 0, j)))
        out_shape.append(jax.ShapeDtypeStruct((m // tm, nb, ncols), F32))
    blocks = tm * d * 2 + d * tn * 2 + tm * tn * 6
    return pl.pallas_call(
        _proj_kv_kernel,
        grid=(m // tm, ncols // tn),
        in_specs=[pl.BlockSpec((tm, d), lambda i, j: (i, 0)),
                  pl.BlockSpec((d, tn), lambda i, j: (0, col0 // tn + j))],
        out_specs=out_specs,
        out_shape=out_shape,
        compiler_params=_params(("parallel", "parallel"), blocks),
        name="proj_k" if block_means else "proj_v",
    )(xb, w)


def _proj_glu(xb, w, col_val, col_gate, ncols, tm):
    m, d = xb.shape
    tn = _tile(ncols, 512)
    tm = _tile(m, tm)
    blocks = tm * d * 2 + 2 * d * tn * 2 + tm * tn * 4
    return pl.pallas_call(
        _proj_glu_kernel,
        grid=(m // tm, ncols // tn),
        in_specs=[pl.BlockSpec((tm, d), lambda i, j: (i, 0)),
                  pl.BlockSpec((d, tn), lambda i, j: (0, col_val // tn + j)),
                  pl.BlockSpec((d, tn), lambda i, j: (0, col_gate // tn + j))],
        out_specs=pl.BlockSpec((tm, tn), lambda i, j: (i, j)),
        out_shape=jax.ShapeDtypeStruct((m, ncols), F32),
        compiler_params=_params(("parallel", "parallel"), blocks),
        name="proj_glu",
    )(xb, w, w)


def _rel_bucket_np(dist, num_buckets):
    n = np.maximum(dist, 0)
    max_exact = num_buckets // 2
    nf = np.maximum(n, 1).astype(np.float32)
    large = max_exact + (np.log(nf / np.float32(max_exact)) / np.float32(math.log(MAX_DISTANCE / max_exact))
                         * np.float32(num_buckets - max_exact)).astype(np.int32)
    large = np.minimum(large, num_buckets - 1)
    return np.where(n < max_exact, n, large).astype(np.int32)


def _bias_tiles_kernel(rb_ref, bucket_ref, o_ref, *, num_buckets):
    h = pl.program_id(0)
    for t in range(bucket_ref.shape[0]):
        bucket = bucket_ref[t]
        acc = jnp.full(bucket.shape, NEG_INF, F32)
        for b in range(num_buckets):
            acc = jnp.where(bucket == b, rb_ref[b, h], acc)
        o_ref[t] = acc


def _prompt_bias_tiles(rel_bias):
    num_buckets, n_heads = rel_bias.shape
    assert MAX_DISTANCE <= MOBA_BLOCK + 1
    r = np.arange(MOBA_BLOCK)[:, None]
    c = np.arange(MOBA_BLOCK)[None, :]
    diag = np.where(r - c >= 0, _rel_bucket_np(r - c, num_buckets), -1)
    prev = _rel_bucket_np(r - c + MOBA_BLOCK, num_buckets)
    far = _rel_bucket_np(r - c + 2 * MOBA_BLOCK, num_buckets)
    buckets = jnp.asarray(np.stack([diag, prev, far]).astype(np.int32))
    return pl.pallas_call(
        functools.partial(_bias_tiles_kernel, num_buckets=num_buckets),
        grid=(n_heads,),
        in_specs=[pl.BlockSpec(memory_space=pltpu.SMEM),
                  pl.BlockSpec((3, MOBA_BLOCK, MOBA_BLOCK), lambda h: (0, 0, 0))],
        out_specs=pl.BlockSpec((None, 3, MOBA_BLOCK, MOBA_BLOCK), lambda h: (h, 0, 0, 0)),
        out_shape=jax.ShapeDtypeStruct((n_heads, 3, MOBA_BLOCK, MOBA_BLOCK), F32),
        compiler_params=_params(("arbitrary",), 6 * MOBA_BLOCK * MOBA_BLOCK * 4),
        name="bias_tiles",
    )(rel_bias, buckets)


def _block_penalty_t(gate_t, n_valid):
    row = lax.broadcasted_iota(jnp.int32, gate_t.shape, 0)
    valid = row < n_valid
    pen = jnp.zeros(gate_t.shape, F32)
    for j in range(n_valid):
        gj = gate_t[j:j + 1, :]
        beats = ((gate_t > gj) | ((gate_t == gj) & (row < j))) & valid
        rank = jnp.sum(beats.astype(F32), axis=0, keepdims=True)
        pen = jnp.where(row == j, jnp.where(rank < MOBA_TOPK, 0.0, NEG_INF), pen)
    return pen


def _prompt_attn_kernel(q_ref, k_ref, v_ref, km_ref, bias_ref, o_ref, kx_ref, pen_ref):
    t, dh = q_ref.shape
    blk = MOBA_BLOCK
    nblk = t // blk
    row = lax.broadcasted_iota(jnp.int32, (t, dh), 0)
    col = lax.broadcasted_iota(jnp.int32, (t, dh), 1)
    kx_ref[:, :dh] = k_ref[...]
    kx_ref[:, dh:] = (col == lax.shift_right_logical(row, blk.bit_length() - 1)).astype(BF16)
    km = km_ref[...]
    km_hi = km.astype(BF16)
    km_lo = (km - km_hi.astype(F32)).astype(BF16)
    eye = (lax.broadcasted_iota(jnp.int32, (blk, blk), 0) == lax.broadcasted_iota(jnp.int32, (blk, blk), 1)).astype(BF16)
    pen_ref[...] = jnp.zeros(pen_ref.shape, F32)
    for i in range(nblk):
        qi = q_ref[i * blk:(i + 1) * blk, :]
        nk = (i + 1) * blk
        if i > MOBA_TOPK:
            gate_t = _dot_nt(km_hi, qi) + _dot_nt(km_lo, qi)
            pen_ref[0:nblk, :] = _block_penalty_t(gate_t, i)
            pen = _dot_nt(eye, pen_ref[...].astype(BF16)).astype(BF16)
            s = _dot_nt(jnp.concatenate([qi, pen], axis=1), kx_ref[0:nk, :])
        else:
            s = _dot_nt(qi, k_ref[0:nk, :])
        bias = [bias_ref[2]] * (i - 1) + ([bias_ref[1]] if i > 0 else []) + [bias_ref[0]]
        s = s + (jnp.concatenate(bias, axis=1) if len(bias) > 1 else bias[0])
        m = jnp.max(s, axis=1, keepdims=True)
        p = jnp.exp(s - m)
        l = jnp.sum(p, axis=1, keepdims=True)
        o = _dot(p.astype(BF16), v_ref[0:nk, :])
        o_ref[i * blk:(i + 1) * blk, :] = (o / l).astype(o_ref.dtype)


def _prompt_attention(q, k, v, kmean, bias_tiles, n_heads):
    n, t, width = q.shape
    dh = width // n_heads
    nblk = t // MOBA_BLOCK
    assert dh == V7X_LANES and t % MOBA_BLOCK == 0 and nblk <= V7X_LANES and MOBA_BLOCK & (MOBA_BLOCK - 1) == 0
    seq_spec = pl.BlockSpec((None, t, dh), lambda h, b: (b, 0, h))
    blocks = 4 * t * dh * 2 + 3 * MOBA_BLOCK * MOBA_BLOCK * 4 + t * 2 * dh * 2 + 6 * MOBA_BLOCK * t * 4
    return pl.pallas_call(
        _prompt_attn_kernel,
        grid=(n_heads, n),
        in_specs=[seq_spec, seq_spec, seq_spec,
                  pl.BlockSpec((None, nblk, dh), lambda h, b: (b, 0, h)),
                  pl.BlockSpec((None, 3, MOBA_BLOCK, MOBA_BLOCK), lambda h, b: (h, 0, 0, 0))],
        out_specs=seq_spec,
        out_shape=jax.ShapeDtypeStruct((n, t, width), BF16),
        scratch_shapes=[pltpu.VMEM((t, 2 * dh), BF16), pltpu.VMEM((dh, MOBA_BLOCK), F32)],
        compiler_params=_params(("parallel", "parallel"), blocks),
        name="prompt_attn",
    )(q, k, v, kmean, bias_tiles)


def _decode_attn_kernel(pt_ref, q_ref, kn_ref, vn_ref, relb_ref, *refs, n_pages, page, pages_per_block, buckets):
    del pt_ref
    k_refs = refs[:n_pages]
    v_refs = refs[n_pages:2 * n_pages]
    o_ref = refs[2 * n_pages]
    h, dh = q_ref.shape
    q = q_ref[...]
    ones = jnp.ones((dh, dh), BF16)

    def lane_sum(x):
        return _dot(x.astype(BF16), ones)

    m_p, l_p, o_p, g_p = [], [], [], []
    for p in range(n_pages):
        kp = k_refs[p][...]
        s = lane_sum((kp * q[None]).reshape(page * h, dh)).reshape(page, h, dh)
        g_p.append(jnp.sum(s, axis=0))
        bias = jnp.stack([relb_ref[buckets[p * page + t]] for t in range(page)]) \
            if len(set(buckets[p * page:(p + 1) * page])) > 1 else relb_ref[buckets[p * page]][None]
        s = s + bias
        m = jnp.max(s, axis=0)
        e = jnp.exp(s - m[None])
        m_p.append(m)
        l_p.append(jnp.sum(e, axis=0))
        o_p.append(jnp.sum(e * v_refs[p][...], axis=0))

    n_blocks = n_pages // pages_per_block
    gate = [sum(g_p[j * pages_per_block:(j + 1) * pages_per_block]) for j in range(n_blocks)]
    s_own = lane_sum(q * kn_ref[...]) + relb_ref[0]
    m_tot = s_own
    sel = []
    for j in range(n_blocks):
        rank = jnp.zeros((h, dh), F32)
        for j2 in range(n_blocks):
            if j2 != j:
                beats = (gate[j2] > gate[j]) | ((gate[j2] == gate[j]) & (j2 < j))
                rank = rank + beats.astype(F32)
        sel.append(rank < MOBA_TOPK)
        for p in range(j * pages_per_block, (j + 1) * pages_per_block):
            m_tot = jnp.maximum(m_tot, jnp.where(sel[j], m_p[p], NEG_INF))
    w_own = jnp.exp(s_own - m_tot)
    num = w_own * vn_ref[...]
    den = w_own
    for j in range(n_blocks):
        for p in range(j * pages_per_block, (j + 1) * pages_per_block):
            w = jnp.where(sel[j], jnp.exp(m_p[p] - m_tot), 0.0)
            num = num + w * o_p[p]
            den = den + w * l_p[p]
    o_ref[...] = (num / den).astype(o_ref.dtype)


def _decode_attention(q, k_new, v_new, cache_k, cache_v, page_table, rel_bias):
    s, h, dh = q.shape
    _, page, _, _ = cache_k.shape
    n_pages = page_table.shape[1]
    past = n_pages * page
    num_buckets = rel_bias.shape[0]
    assert dh == V7X_LANES and MOBA_BLOCK % page == 0 and past % MOBA_BLOCK == 0
    buckets = tuple(int(b) for b in _rel_bucket_np(past - np.arange(past), num_buckets))
    relb = jnp.broadcast_to(rel_bias[:, :, None], (num_buckets, h, dh))
    tok_spec = pl.BlockSpec((None, h, dh), lambda b, pt: (b, 0, 0))
    page_specs = [pl.BlockSpec((None, page, h, dh), lambda b, pt, p=p: (pt[b, p], 0, 0, 0)) for p in range(n_pages)]
    blocks = 2 * n_pages * page * h * dh * 4 + num_buckets * h * dh * 4
    kernel = functools.partial(_decode_attn_kernel, n_pages=n_pages, page=page,
                               pages_per_block=MOBA_BLOCK // page, buckets=buckets)
    return pl.pallas_call(
        kernel,
        grid_spec=pltpu.PrefetchScalarGridSpec(
            num_scalar_prefetch=1,
            grid=(s,),
            in_specs=[tok_spec, tok_spec, tok_spec,
                      pl.BlockSpec((num_buckets, h, dh), lambda b, pt: (0, 0, 0))] + page_specs + page_specs,
            out_specs=tok_spec),
        out_shape=jax.ShapeDtypeStruct((s, h, dh), BF16),
        compiler_params=_params(("parallel",), blocks),
        name="decode_attn",
    )(page_table, q, k_new, v_new, relb, *([cache_k] * n_pages), *([cache_v] * n_pages))


CONV_HALO = 32
CONV_ROWS = 64


def _prompt_conv_kernel(halo_ref, u_ref, w_ref, cb_ref, g_ref, b_ref, o_ref, ext_ref, conv_ref):
    tt, ch = u_ref.shape
    kw = w_ref.shape[0]
    first = CONV_HALO - (kw - 1)

    @pl.when(pl.program_id(1) == 0)
    def _():
        ext_ref[0:CONV_HALO, :] = jnp.zeros((CONV_HALO, ch), F32)

    @pl.when(pl.program_id(1) > 0)
    def _():
        ext_ref[0:CONV_HALO, :] = halo_ref[...]

    ext_ref[CONV_HALO:CONV_HALO + tt, :] = u_ref[...]
    rows = min(CONV_ROWS, tt)
    span = rows + CONV_HALO
    for c0 in range(0, ch, V7X_LANES):
        cs = slice(c0, c0 + V7X_LANES)
        for r0 in range(0, tt, rows):
            x = ext_ref[r0:r0 + span, cs]
            acc = jnp.zeros((rows, V7X_LANES), F32)
            for b in range(V7X_SUBLANES):
                taps = [s for s in range(first, first + kw) if s % V7X_SUBLANES == b]
                assert all(s + rows <= span for s in taps)
                xb = x if b == 0 else pltpu.roll(x, span - b, axis=0)
                for s in taps:
                    acc = acc + xb[s - b:s - b + rows] * w_ref[s - first:s - first + 1, cs]
            conv_ref[r0:r0 + rows, cs] = acc
    y = _layer_norm(conv_ref[...] + cb_ref[...], g_ref[...], b_ref[...])
    o_ref[...] = (y * _sigmoid(y)).astype(o_ref.dtype)


def _prompt_conv(u, conv_w, conv_b, ln_g, ln_b):
    n, t, ch = u.shape
    kw = conv_w.shape[0]
    tt = _tile(t, 128)
    assert kw - 1 <= CONV_HALO and tt % CONV_HALO == 0 and ch % V7X_LANES == 0
    vec = pl.BlockSpec((1, ch), lambda b, i: (0, 0))
    blocks = (2 * tt + 2 * CONV_HALO) * ch * 4 + tt * ch * 2
    return pl.pallas_call(
        _prompt_conv_kernel,
        grid=(n, t // tt),
        in_specs=[pl.BlockSpec((None, CONV_HALO, ch), lambda b, i: (b, jnp.maximum(i * (tt // CONV_HALO) - 1, 0), 0)),
                  pl.BlockSpec((None, tt, ch), lambda b, i: (b, i, 0)),
                  pl.BlockSpec((kw, ch), lambda b, i: (0, 0)), vec, vec, vec],
        out_specs=pl.BlockSpec((None, tt, ch), lambda b, i: (b, i, 0)),
        out_shape=jax.ShapeDtypeStruct((n, t, ch), BF16),
        scratch_shapes=[pltpu.VMEM((CONV_HALO + tt, ch), F32), pltpu.VMEM((tt, ch), F32)],
        compiler_params=_params(("parallel", "arbitrary"), blocks),
        name="prompt_conv",
    )(u, u, conv_w, conv_b.reshape(1, ch), ln_g.reshape(1, ch), ln_b.reshape(1, ch))


def _decode_conv_kernel(state_ref, u_ref, w_ref, cb_ref, g_ref, b_ref, o_ref, new_state_ref):
    kw = w_ref.shape[0]
    w_hist = w_ref[0:kw - 1, :]
    w_last = w_ref[kw - 1:kw, :]
    for b in range(state_ref.shape[0]):
        u = u_ref[b]
        conv = jnp.sum(state_ref[b] * w_hist, axis=0, keepdims=True) + u * w_last
        y = _layer_norm(conv + cb_ref[...], g_ref[...], b_ref[...])
        o_ref[b] = (y * _sigmoid(y)).astype(o_ref.dtype)
        new_state_ref[b, 0:kw - 2, :] = state_ref[b, 1:kw - 1, :]
        new_state_ref[b, kw - 2:kw - 1, :] = u


def _decode_conv(state, u, conv_w, conv_b, ln_g, ln_b):
    s, hist, ch = state.shape
    kw = conv_w.shape[0]
    ts = _tile(s, 16)
    vec = pl.BlockSpec((1, ch), lambda i: (0, 0))
    tok = pl.BlockSpec((ts, 1, ch), lambda i: (i, 0, 0))
    hist_spec = pl.BlockSpec((ts, hist, ch), lambda i: (i, 0, 0))
    blocks = 2 * ts * 32 * ch * 4 + 2 * ts * V7X_SUBLANES * ch * 4
    return pl.pallas_call(
        _decode_conv_kernel,
        grid=(s // ts,),
        in_specs=[hist_spec, tok, pl.BlockSpec((kw, ch), lambda i: (0, 0)), vec, vec, vec],
        out_specs=[tok, hist_spec],
        out_shape=[jax.ShapeDtypeStruct((s, 1, ch), F32), jax.ShapeDtypeStruct((s, hist, ch), F32)],
        compiler_params=_params(("parallel",), blocks),
        name="decode_conv",
    )(state, u, conv_w, conv_b.reshape(1, ch), ln_g.reshape(1, ch), ln_b.reshape(1, ch))


def _merge_kernel(x_ref, a_ref, c_ref, wga_ref, wgc_ref, wao_ref, wco_ref, o_ref):
    x = x_ref[...]
    mixed = (_sigmoid(_dot(x, wga_ref[...])) * _dot(a_ref[...], wao_ref[...])
             + _sigmoid(_dot(x, wgc_ref[...])) * _dot(c_ref[...], wco_ref[...]))
    o_ref[...] = mixed.astype(o_ref.dtype)


def _merge(xb, attn, cn, w_in, col_ga, col_gc, w_ao, w_co, tm):
    m, d = xb.shape
    wa = attn.shape[1]
    wc = cn.shape[1]
    tn = _tile(d, 512)
    tm = _tile(m, tm)
    blocks = tm * (d + wa + wc + tn) * 2 + (2 * d + wa + wc) * tn * 2
    return pl.pallas_call(
        _merge_kernel,
        grid=(m // tm, d // tn),
        in_specs=[pl.BlockSpec((tm, d), lambda i, j: (i, 0)),
                  pl.BlockSpec((tm, wa), lambda i, j: (i, 0)),
                  pl.BlockSpec((tm, wc), lambda i, j: (i, 0)),
                  pl.BlockSpec((d, tn), lambda i, j: (0, col_ga // tn + j)),
                  pl.BlockSpec((d, tn), lambda i, j: (0, col_gc // tn + j)),
                  pl.BlockSpec((wa, tn), lambda i, j: (0, j)),
                  pl.BlockSpec((wc, tn), lambda i, j: (0, j))],
        out_specs=pl.BlockSpec((tm, tn), lambda i, j: (i, j)),
        out_shape=jax.ShapeDtypeStruct((m, d), BF16),
        compiler_params=_params(("parallel", "parallel"), blocks),
        name="merge",
    )(xb, attn, cn, w_in, w_in, w_ao, w_co)


def _out_ln_kernel(x_ref, mixed_ref, w_ref, g_ref, b_ref, of_ref, ob_ref, *, alpha):
    y = _layer_norm(alpha * x_ref[...] + _dot(mixed_ref[...], w_ref[...]), g_ref[...], b_ref[...])
    of_ref[...] = y
    ob_ref[...] = y.astype(ob_ref.dtype)


def _out_ln(x, mixed, w_out, g, b, alpha, tm):
    m, d = x.shape
    tm = _tile(m, tm)
    vec = pl.BlockSpec((1, d), lambda i: (0, 0))
    row = pl.BlockSpec((tm, d), lambda i: (i, 0))
    blocks = tm * d * (4 + 2 + 4 + 2) + d * d * 2
    return pl.pallas_call(
        functools.partial(_out_ln_kernel, alpha=alpha),
        grid=(m // tm,),
        in_specs=[row, row, pl.BlockSpec((d, d), lambda i: (0, 0)), vec, vec],
        out_specs=[row, row],
        out_shape=[jax.ShapeDtypeStruct((m, d), F32), jax.ShapeDtypeStruct((m, d), BF16)],
        compiler_params=_params(("parallel",), blocks),
        name="out_ln",
    )(x, mixed, w_out, g.reshape(1, d), b.reshape(1, d))


def _mlp_ln_kernel(xf_ref, xb_ref, w1_ref, w2_ref, g_ref, b_ref, o_ref, *, alpha):
    f = pl.program_id(1)

    @pl.when(f == 0)
    def _():
        o_ref[...] = alpha * xf_ref[...]

    hid = jnp.maximum(_dot(xb_ref[...], w1_ref[...]), 0.0)
    o_ref[...] += _dot((hid * hid).astype(BF16), w2_ref[...])

    @pl.when(f == pl.num_programs(1) - 1)
    def _():
        o_ref[...] = _layer_norm(o_ref[...], g_ref[...], b_ref[...])


def _mlp_ln(xf, xb, w1, w2, g, b, alpha, tm):
    m, d = xf.shape
    dff = w1.shape[1]
    tf = _tile(dff, 1024)
    tm = _tile(m, tm)
    vec = pl.BlockSpec((1, d), lambda i, f: (0, 0))
    row = pl.BlockSpec((tm, d), lambda i, f: (i, 0))
    blocks = tm * d * (4 + 2 + 4) + 2 * d * tf * 2 + tm * tf * 4
    return pl.pallas_call(
        functools.partial(_mlp_ln_kernel, alpha=alpha),
        grid=(m // tm, dff // tf),
        in_specs=[row, row, pl.BlockSpec((d, tf), lambda i, f: (0, f)), pl.BlockSpec((tf, d), lambda i, f: (f, 0)),
                  vec, vec],
        out_specs=row,
        out_shape=jax.ShapeDtypeStruct((m, d), F32),
        compiler_params=_params(("parallel", "arbitrary"), blocks),
        name="mlp_ln",
    )(xf, xb, w1, w2, g.reshape(1, d), b.reshape(1, d))


def _dense_tail(x2d, xb, attn, cn, lw, alpha, tm):
    mixed = _merge(xb, attn, cn, lw["w_in"], lw["col_ga"], lw["col_gc"], lw["w_attn_out"], lw["w_conv_out"], tm=2 * tm)
    x1f, x1b = _out_ln(x2d, mixed, lw["w_out"], lw["ln1_g"], lw["ln1_b"], alpha, tm)
    return _mlp_ln(x1f, x1b, lw["w_ff1"], lw["w_ff2"], lw["ln2_g"], lw["ln2_b"], alpha, tm)


def kernel(x_prompt, x_sample, cache_k, cache_v, state_conv, page_table, rel_bias, w_in, w_attn_out, conv_w, conv_b,
           conv_ln_g, conv_ln_b, w_conv_out, w_out, ln1_g, ln1_b, w_ff1, w_ff2, ln2_g, ln2_b):
    depth = w_in.shape[0]
    n, t, d = x_prompt.shape
    s, ts, _ = x_sample.shape
    n_heads, dh = cache_k.shape[-2:]
    aw = n_heads * dh
    ch = conv_w.shape[-1]
    alpha = (2 * depth) ** 0.25
    scale = dh ** -0.5
    assert ts == 1 and w_in.shape[-1] == 3 * aw + 2 * ch + 2 * d
    col_k, col_v, col_val, col_gate = aw, 2 * aw, 3 * aw, 3 * aw + ch
    col_ga, col_gc = 3 * aw + 2 * ch, 3 * aw + 2 * ch + d

    bias_tiles = _prompt_bias_tiles(rel_bias)
    hp = x_prompt.reshape(n * t, d)
    hs = x_sample.reshape(s, d)
    outs = [[] for _ in range(6)]
    for l in range(depth):
        lw = dict(w_in=w_in[l].astype(BF16), w_attn_out=w_attn_out[l].astype(BF16),
                  w_conv_out=w_conv_out[l].astype(BF16), w_out=w_out[l].astype(BF16),
                  w_ff1=w_ff1[l].astype(BF16), w_ff2=w_ff2[l].astype(BF16),
                  ln1_g=ln1_g[l], ln1_b=ln1_b[l], ln2_g=ln2_g[l], ln2_b=ln2_b[l], col_ga=col_ga, col_gc=col_gc)
        wi = lw["w_in"]

        q, xb = _proj_q(hp, wi, 0, aw, scale, tm=1024, out_dtype=BF16)
        kf, kb, kmean = _proj_kv(xb, wi, col_k, aw, tm=t, block_means=True)
        vf, vb = _proj_kv(xb, wi, col_v, aw, tm=t, block_means=False)
        u = _proj_glu(xb, wi, col_val, col_gate, ch, tm=1024)
        attn = _prompt_attention(q.reshape(n, t, aw), kb.reshape(n, t, aw), vb.reshape(n, t, aw), kmean,
                                 bias_tiles, n_heads)
        u3 = u.reshape(n, t, ch)
        cn = _prompt_conv(u3, conv_w[l], conv_b[l], conv_ln_g[l], conv_ln_b[l])
        hp_next = _dense_tail(hp, xb, attn.reshape(n * t, aw), cn.reshape(n * t, ch), lw, alpha, tm=512)
        outs[0].append(kf.reshape(n, t, n_heads, dh))
        outs[1].append(vf.reshape(n, t, n_heads, dh))
        outs[2].append(u3[:, t - (conv_w.shape[1] - 1):, :])

        qs, xsb = _proj_q(hs, wi, 0, aw, scale, tm=s, out_dtype=F32)
        ksf, _ = _proj_kv(xsb, wi, col_k, aw, tm=s, block_means=False)
        vsf, _ = _proj_kv(xsb, wi, col_v, aw, tm=s, block_means=False)
        us = _proj_glu(xsb, wi, col_val, col_gate, ch, tm=s)
        attn_s = _decode_attention(qs.reshape(s, n_heads, dh), ksf.reshape(s, n_heads, dh), vsf.reshape(s, n_heads, dh),
                                   cache_k[l], cache_v[l], page_table, rel_bias)
        cn_s, new_state = _decode_conv(state_conv[l], us.reshape(s, 1, ch), conv_w[l], conv_b[l], conv_ln_g[l],
                                       conv_ln_b[l])
        hs_next = _dense_tail(hs, xsb, attn_s.reshape(s, aw), cn_s.reshape(s, ch).astype(BF16), lw, alpha, tm=s)
        outs[3].append(ksf.reshape(s, 1, n_heads, dh))
        outs[4].append(vsf.reshape(s, 1, n_heads, dh))
        outs[5].append(new_state)
        hp, hs = hp_next, hs_next

    return (hp.reshape(n, t, d), hs.reshape(s, 1, d)) + tuple(jnp.stack(o) for o in outs)
```

```python
import functools
import math

import numpy as np
import jax
import jax.numpy as jnp
from jax import lax
from jax.experimental import pallas as pl
from jax.experimental.pallas import tpu as pltpu

MOBA_BLOCK = 256
MOBA_TOPK = 3
MAX_DISTANCE = 128
LN_EPS = 1e-5
NEG_INF = -1e30

V7X_VMEM_BYTES = 64 * 1024 * 1024
V7X_LANES = 128
V7X_SUBLANES = 8

F32 = jnp.float32
BF16 = jnp.bfloat16


def _vmem_limit(block_bytes):
    return int(min(max(2 * block_bytes + (16 << 20), 32 << 20), V7X_VMEM_BYTES - (4 << 20)))


def _params(semantics, block_bytes):
    return pltpu.CompilerParams(dimension_semantics=semantics, vmem_limit_bytes=_vmem_limit(block_bytes))


def _tile(n, want):
    t = min(n, want)
    while n % t:
        t -= 1
    return t


def _sigmoid(x):
    return 1.0 / (1.0 + jnp.exp(-x))


def _layer_norm(y, g, b):
    mu = jnp.mean(y, axis=-1, keepdims=True)
    d = y - mu
    var = jnp.mean(d * d, axis=-1, keepdims=True)
    return d * lax.rsqrt(var + LN_EPS) * g + b


def _dot(a, b):
    return jnp.dot(a, b, preferred_element_type=F32)


def _dot_nt(a, b):
    return lax.dot_general(a, b, (((1,), (1,)), ((), ())), preferred_element_type=F32)


def _proj_q_kernel(x_ref, w_ref, o_ref, xb_ref, *, scale):
    xb = x_ref[...].astype(BF16)
    xb_ref[...] = xb
    o_ref[...] = (_dot(xb, w_ref[...]) * scale).astype(o_ref.dtype)


def _proj_kv_kernel(x_ref, w_ref, of_ref, ob_ref, *rest):
    acc = _dot(x_ref[...], w_ref[...])
    of_ref[...] = acc
    ob_ref[...] = acc.astype(ob_ref.dtype)
    if rest:
        (km_ref,) = rest
        tm, tn = acc.shape
        km_ref[...] = jnp.sum(acc.reshape(tm // MOBA_BLOCK, MOBA_BLOCK, tn), axis=1) * (1.0 / MOBA_BLOCK)


def _proj_glu_kernel(x_ref, wv_ref, wg_ref, u_ref):
    x = x_ref[...]
    u_ref[...] = _dot(x, wv_ref[...]) * _sigmoid(_dot(x, wg_ref[...]))


def _proj_q(x, w, col0, ncols, scale, tm, out_dtype):
    m, d = x.shape
    tn = _tile(ncols, 1024)
    tm = _tile(m, tm)
    blocks = tm * d * 6 + d * tn * 2 + tm * tn * 4
    return pl.pallas_call(
        functools.partial(_proj_q_kernel, scale=scale),
        grid=(m // tm, ncols // tn),
        in_specs=[pl.BlockSpec((tm, d), lambda i, j: (i, 0)),
                  pl.BlockSpec((d, tn), lambda i, j: (0, col0 // tn + j))],
        out_specs=[pl.BlockSpec((tm, tn), lambda i, j: (i, j)), pl.BlockSpec((tm, d), lambda i, j: (i, 0))],
        out_shape=[jax.ShapeDtypeStruct((m, ncols), out_dtype), jax.ShapeDtypeStruct((m, d), BF16)],
        compiler_params=_params(("parallel", "arbitrary"), blocks),
        name="proj_q",
    )(x, w)


def _proj_kv(xb, w, col0, ncols, tm, block_means):
    m, d = xb.shape
    tn = _tile(ncols, 512)
    tm = _tile(m, tm)
    out_specs = [pl.BlockSpec((tm, tn), lambda i, j: (i, j)), pl.BlockSpec((tm, tn), lambda i, j: (i, j))]
    out_shape = [jax.ShapeDtypeStruct((m, ncols), F32), jax.ShapeDtypeStruct((m, ncols), BF16)]
    if block_means:
        nb = tm // MOBA_BLOCK
        out_specs.append(pl.BlockSpec((None, nb, tn), lambda i, j: (i, 0, j)))
        out_shape.append(jax.ShapeDtypeStruct((m // tm, nb, ncols), F32))
    blocks = tm * d * 2 + d * tn * 2 + tm * tn * 6
    return pl.pallas_call(
        _proj_kv_kernel,
        grid=(m // tm, ncols // tn),
        in_specs=[pl.BlockSpec((tm, d), lambda i, j: (i, 0)),
                  pl.BlockSpec((d, tn), lambda i, j: (0, col0 // tn + j))],
        out_specs=out_specs,
        out_shape=out_shape,
        compiler_params=_params(("parallel", "parallel"), blocks),
        name="proj_k" if block_means else "proj_v",
    )(xb, w)


def _proj_glu(xb, w, col_val, col_gate, ncols, tm):
    m, d = xb.shape
    tn = _tile(ncols, 512)
    tm = _tile(m, tm)
    blocks = tm * d * 2 + 2 * d * tn * 2 + tm * tn * 4
    return pl.pallas_call(
        _proj_glu_kernel,
        grid=(m // tm, ncols // tn),
        in_specs=[pl.BlockSpec((tm, d), lambda i, j: (i, 0)),
                  pl.BlockSpec((d, tn), lambda i, j: (0, col_val // tn + j)),
                  pl.BlockSpec((d, tn), lambda i, j: (0, col_gate // tn + j))],
        out_specs=pl.BlockSpec((tm, tn), lambda i, j: (i, j)),
        out_shape=jax.ShapeDtypeStruct((m, ncols), F32),
        compiler_params=_params(("parallel", "parallel"), blocks),
        name="proj_glu",
    )(xb, w, w)


def _rel_bucket_np(dist, num_buckets):
    n = np.maximum(dist, 0)
    max_exact = num_buckets // 2
    nf = np.maximum(n, 1).astype(np.float32)
    large = max_exact + (np.log(nf / np.float32(max_exact)) / np.float32(math.log(MAX_DISTANCE / max_exact))
                         * np.float32(num_buckets - max_exact)).astype(np.int32)
    large = np.minimum(large, num_buckets - 1)
    return np.where(n < max_exact, n, large).astype(np.int32)


def _bias_tiles_kernel(rb_ref, bucket_ref, o_ref, *, num_buckets):
    h = pl.program_id(0)
    for t in range(bucket_ref.shape[0]):
        bucket = bucket_ref[t]
        acc = jnp.full(bucket.shape, NEG_INF, F32)
        for b in range(num_buckets):
            acc = jnp.where(bucket == b, rb_ref[b, h], acc)
        o_ref[t] = acc


def _prompt_bias_tiles(rel_bias):
    num_buckets, n_heads = rel_bias.shape
    assert MAX_DISTANCE <= MOBA_BLOCK + 1
    r = np.arange(MOBA_BLOCK)[:, None]
    c = np.arange(MOBA_BLOCK)[None, :]
    diag = np.where(r - c >= 0, _rel_bucket_np(r - c, num_buckets), -1)
    prev = _rel_bucket_np(r - c + MOBA_BLOCK, num_buckets)
    far = _rel_bucket_np(r - c + 2 * MOBA_BLOCK, num_buckets)
    buckets = jnp.asarray(np.stack([diag, prev, far]).astype(np.int32))
    return pl.pallas_call(
        functools.partial(_bias_tiles_kernel, num_buckets=num_buckets),
        grid=(n_heads,),
        in_specs=[pl.BlockSpec(memory_space=pltpu.SMEM),
                  pl.BlockSpec((3, MOBA_BLOCK, MOBA_BLOCK), lambda h: (0, 0, 0))],
        out_specs=pl.BlockSpec((None, 3, MOBA_BLOCK, MOBA_BLOCK), lambda h: (h, 0, 0, 0)),
        out_shape=jax.ShapeDtypeStruct((n_heads, 3, MOBA_BLOCK, MOBA_BLOCK), F32),
        compiler_params=_params(("arbitrary",), 6 * MOBA_BLOCK * MOBA_BLOCK * 4),
        name="bias_tiles",
    )(rel_bias, buckets)


def _block_penalty_t(gate_t, n_valid):
    row = lax.broadcasted_iota(jnp.int32, gate_t.shape, 0)
    valid = row < n_valid
    pen = jnp.zeros(gate_t.shape, F32)
    for j in range(n_valid):
        gj = gate_t[j:j + 1, :]
        beats = ((gate_t > gj) | ((gate_t == gj) & (row < j))) & valid
        rank = jnp.sum(beats.astype(F32), axis=0, keepdims=True)
        pen = jnp.where(row == j, jnp.where(rank < MOBA_TOPK, 0.0, NEG_INF), pen)
    return pen


def _prompt_attn_kernel(q_ref, k_ref, v_ref, km_ref, bias_ref, o_ref, kx_ref, pen_ref):
    t, dh = q_ref.shape
    blk = MOBA_BLOCK
    nblk = t // blk
    row = lax.broadcasted_iota(jnp.int32, (t, dh), 0)
    col = lax.broadcasted_iota(jnp.int32, (t, dh), 1)
    kx_ref[:, :dh] = k_ref[...]
    kx_ref[:, dh:] = (col == lax.shift_right_logical(row, blk.bit_length() - 1)).astype(BF16)
    km = km_ref[...]
    km_hi = km.astype(BF16)
    km_lo = (km - km_hi.astype(F32)).astype(BF16)
    eye = (lax.broadcasted_iota(jnp.int32, (blk, blk), 0) == lax.broadcasted_iota(jnp.int32, (blk, blk), 1)).astype(BF16)
    pen_ref[...] = jnp.zeros(pen_ref.shape, F32)
    for i in range(nblk):
        qi = q_ref[i * blk:(i + 1) * blk, :]
        nk = (i + 1) * blk
        if i > MOBA_TOPK:
            gate_t = _dot_nt(km_hi, qi) + _dot_nt(km_lo, qi)
            pen_ref[0:nblk, :] = _block_penalty_t(gate_t, i)
            pen = _dot_nt(eye, pen_ref[...].astype(BF16)).astype(BF16)
            s = _dot_nt(jnp.concatenate([qi, pen], axis=1), kx_ref[0:nk, :])
        else:
            s = _dot_nt(qi, k_ref[0:nk, :])
        bias = [bias_ref[2]] * (i - 1) + ([bias_ref[1]] if i > 0 else []) + [bias_ref[0]]
        s = s + (jnp.concatenate(bias, axis=1) if len(bias) > 1 else bias[0])
        m = jnp.max(s, axis=1, keepdims=True)
        p = jnp.exp(s - m)
        l = jnp.sum(p, axis=1, keepdims=True)
        o = _dot(p.astype(BF16), v_ref[0:nk, :])
        o_ref[i * blk:(i + 1) * blk, :] = (o / l).astype(o_ref.dtype)


def _prompt_attention(q, k, v, kmean, bias_tiles, n_heads):
    n, t, width = q.shape
    dh = width // n_heads
    nblk = t // MOBA_BLOCK
    assert dh == V7X_LANES and t % MOBA_BLOCK == 0 and nblk <= V7X_LANES and MOBA_BLOCK & (MOBA_BLOCK - 1) == 0
    seq_spec = pl.BlockSpec((None, t, dh), lambda h, b: (b, 0, h))
    blocks = 4 * t * dh * 2 + 3 * MOBA_BLOCK * MOBA_BLOCK * 4 + t * 2 * dh * 2 + 6 * MOBA_BLOCK * t * 4
    return pl.pallas_call(
        _prompt_attn_kernel,
        grid=(n_heads, n),
        in_specs=[seq_spec, seq_spec, seq_spec,
                  pl.BlockSpec((None, nblk, dh), lambda h, b: (b, 0, h)),
                  pl.BlockSpec((None, 3, MOBA_BLOCK, MOBA_BLOCK), lambda h, b: (h, 0, 0, 0))],
        out_specs=seq_spec,
        out_shape=jax.ShapeDtypeStruct((n, t, width), BF16),
        scratch_shapes=[pltpu.VMEM((t, 2 * dh), BF16), pltpu.VMEM((dh, MOBA_BLOCK), F32)],
        compiler_params=_params(("parallel", "parallel"), blocks),
        name="prompt_attn",
    )(q, k, v, kmean, bias_tiles)


def _lane_sums(xs):
    dh = xs[0].shape[-1]
    r = lax.broadcasted_iota(jnp.int32, (2 * dh, 2 * dh), 0)
    c = lax.broadcasted_iota(jnp.int32, (2 * dh, 2 * dh), 1)
    pair_ones = ((r < dh) == (c < dh)).astype(BF16)
    outs = []
    for i in range(0, len(xs) - 1, 2):
        y = _dot(jnp.concatenate([xs[i].astype(BF16), xs[i + 1].astype(BF16)], axis=1), pair_ones)
        outs += [y[:, :dh], y[:, dh:]]
    if len(xs) % 2:
        outs.append(_dot(xs[-1].astype(BF16), jnp.ones((dh, dh), BF16)))
    return outs


def _page_bias(relb_ref, page_buckets):
    if len(set(page_buckets)) == 1:
        return relb_ref[page_buckets[0]][None]
    return jnp.stack([relb_ref[b] for b in page_buckets])


def _decode_pages(part, q_ref, kn_ref, vn_ref, relb_ref, k_refs, v_refs, o_ref, m_ref, l_ref, g_ref, acc_ref, *,
                  n_pages, pages_per_block, buckets):
    pps = len(k_refs)
    parts = n_pages // pps
    page, h, dh = k_refs[0].shape
    q = q_ref[...]
    sums = _lane_sums([(k_refs[p][...] * q[None]).reshape(page * h, dh) for p in range(pps)])
    for p in range(pps):
        s = sums[p].reshape(page, h, dh)
        gsum = jnp.sum(s, axis=0)
        page_buckets = [buckets[(a * pps + p) * page:(a * pps + p + 1) * page] for a in range(parts)]
        bias = _page_bias(relb_ref, page_buckets[0])
        for a in range(1, parts):
            if page_buckets[a] != page_buckets[0]:
                bias = jnp.where(part == a, _page_bias(relb_ref, page_buckets[a]), bias)
        s = s + bias
        m = jnp.max(s, axis=0)
        e = jnp.exp(s - m[None])
        idx = part * pps + p
        m_ref[idx] = m
        l_ref[idx] = jnp.sum(e, axis=0)
        g_ref[idx] = gsum
        acc_ref[idx] = jnp.sum(e * v_refs[p][...], axis=0)

    def merge():
        n_blocks = n_pages // pages_per_block
        gate = [sum(g_ref[p] for p in range(j * pages_per_block, (j + 1) * pages_per_block)) for j in range(n_blocks)]
        s_own = _lane_sums([q * kn_ref[...]])[0] + relb_ref[0]
        m_tot = s_own
        sel = []
        for j in range(n_blocks):
            rank = jnp.zeros((h, dh), F32)
            for j2 in range(n_blocks):
                if j2 != j:
                    beats = (gate[j2] > gate[j]) | ((gate[j2] == gate[j]) & (j2 < j))
                    rank = rank + beats.astype(F32)
            sel.append(rank < MOBA_TOPK)
            for p in range(j * pages_per_block, (j + 1) * pages_per_block):
                m_tot = jnp.maximum(m_tot, jnp.where(sel[j], m_ref[p], NEG_INF))
        w_own = jnp.exp(s_own - m_tot)
        num = w_own * vn_ref[...]
        den = w_own
        for j in range(n_blocks):
            for p in range(j * pages_per_block, (j + 1) * pages_per_block):
                w = jnp.where(sel[j], jnp.exp(m_ref[p] - m_tot), 0.0)
                num = num + w * acc_ref[p]
                den = den + w * l_ref[p]
        o_ref[...] = (num / den).astype(o_ref.dtype)

    if parts == 1:
        merge()
    else:
        pl.when(part == parts - 1)(merge)


def _decode_attn_kernel(pt_ref, q_ref, kn_ref, vn_ref, relb_ref, *refs, n_pages, pages_per_block, buckets):
    del pt_ref
    _decode_pages(0, q_ref, kn_ref, vn_ref, relb_ref, refs[:n_pages], refs[n_pages:2 * n_pages], *refs[2 * n_pages:],
                  n_pages=n_pages, pages_per_block=pages_per_block, buckets=buckets)


def _decode_specs(page_table, cache_k, rel_bias, h, dh, pages_per_step, seq_of, part_of):
    page = cache_k.shape[1]
    n_pages = page_table.shape[1]
    past = n_pages * page
    num_buckets = rel_bias.shape[0]
    assert dh == V7X_LANES and MOBA_BLOCK % page == 0 and past % MOBA_BLOCK == 0 and n_pages % pages_per_step == 0
    buckets = tuple(int(b) for b in _rel_bucket_np(past - np.arange(past), num_buckets))
    relb = jnp.broadcast_to(rel_bias[:, :, None], (num_buckets, h, dh))
    tok_spec = pl.BlockSpec((None, h, dh), lambda *g: (seq_of(*g[:-1]), 0, 0))
    relb_spec = pl.BlockSpec((num_buckets, h, dh), lambda *g: (0, 0, 0))
    page_specs = [pl.BlockSpec((None, page, h, dh),
                               lambda *g, p=p: (g[-1][seq_of(*g[:-1]), part_of(*g[:-1]) * pages_per_step + p], 0, 0, 0))
                  for p in range(pages_per_step)]
    scratch = [pltpu.VMEM((n_pages, h, dh), F32)] * 4
    return buckets, relb, tok_spec, relb_spec, page_specs, scratch


def _decode_attention(q, k_new, v_new, cache_k, cache_v, page_table, rel_bias):
    s, h, dh = q.shape
    page = cache_k.shape[1]
    n_pages = page_table.shape[1]
    buckets, relb, tok_spec, relb_spec, page_specs, scratch = _decode_specs(
        page_table, cache_k, rel_bias, h, dh, n_pages, seq_of=lambda b: b, part_of=lambda b: 0)
    blocks = 2 * n_pages * page * h * dh * 4 + relb.size * 4
    kernel = functools.partial(_decode_attn_kernel, n_pages=n_pages, pages_per_block=MOBA_BLOCK // page,
                               buckets=buckets)
    return pl.pallas_call(
        kernel,
        grid_spec=pltpu.PrefetchScalarGridSpec(
            num_scalar_prefetch=1,
            grid=(s,),
            in_specs=[tok_spec, tok_spec, tok_spec, relb_spec] + page_specs + page_specs,
            out_specs=tok_spec,
            scratch_shapes=scratch),
        out_shape=jax.ShapeDtypeStruct((s, h, dh), BF16),
        compiler_params=_params(("parallel",), blocks),
        name="decode_attn",
    )(page_table, q, k_new, v_new, relb, *([cache_k] * n_pages), *([cache_v] * n_pages))


CONV_HALO = 32
CONV_ROWS = 64


def _prompt_conv_kernel(halo_ref, u_ref, w_ref, cb_ref, g_ref, b_ref, o_ref, ext_ref, conv_ref):
    tt, ch = u_ref.shape
    kw = w_ref.shape[0]
    first = CONV_HALO - (kw - 1)

    @pl.when(pl.program_id(1) == 0)
    def _():
        ext_ref[0:CONV_HALO, :] = jnp.zeros((CONV_HALO, ch), F32)

    @pl.when(pl.program_id(1) > 0)
    def _():
        ext_ref[0:CONV_HALO, :] = halo_ref[...]

    ext_ref[CONV_HALO:CONV_HALO + tt, :] = u_ref[...]
    rows = min(CONV_ROWS, tt)
    span = rows + CONV_HALO
    for c0 in range(0, ch, V7X_LANES):
        cs = slice(c0, c0 + V7X_LANES)
        for r0 in range(0, tt, rows):
            x = ext_ref[r0:r0 + span, cs]
            acc = jnp.zeros((rows, V7X_LANES), F32)
            for b in range(V7X_SUBLANES):
                taps = [s for s in range(first, first + kw) if s % V7X_SUBLANES == b]
                assert all(s + rows <= span for s in taps)
                xb = x if b == 0 else pltpu.roll(x, span - b, axis=0)
                for s in taps:
                    acc = acc + xb[s - b:s - b + rows] * w_ref[s - first:s - first + 1, cs]
            conv_ref[r0:r0 + rows, cs] = acc
    y = _layer_norm(conv_ref[...] + cb_ref[...], g_ref[...], b_ref[...])
    o_ref[...] = (y * _sigmoid(y)).astype(o_ref.dtype)


def _prompt_conv(u, conv_w, conv_b, ln_g, ln_b):
    n, t, ch = u.shape
    kw = conv_w.shape[0]
    tt = _tile(t, 128)
    assert kw - 1 <= CONV_HALO and tt % CONV_HALO == 0 and ch % V7X_LANES == 0
    vec = pl.BlockSpec((1, ch), lambda b, i: (0, 0))
    blocks = (2 * tt + 2 * CONV_HALO) * ch * 4 + tt * ch * 2
    return pl.pallas_call(
        _prompt_conv_kernel,
        grid=(n, t // tt),
        in_specs=[pl.BlockSpec((None, CONV_HALO, ch), lambda b, i: (b, jnp.maximum(i * (tt // CONV_HALO) - 1, 0), 0)),
                  pl.BlockSpec((None, tt, ch), lambda b, i: (b, i, 0)),
                  pl.BlockSpec((kw, ch), lambda b, i: (0, 0)), vec, vec, vec],
        out_specs=pl.BlockSpec((None, tt, ch), lambda b, i: (b, i, 0)),
        out_shape=jax.ShapeDtypeStruct((n, t, ch), BF16),
        scratch_shapes=[pltpu.VMEM((CONV_HALO + tt, ch), F32), pltpu.VMEM((tt, ch), F32)],
        compiler_params=_params(("parallel", "arbitrary"), blocks),
        name="prompt_conv",
    )(u, u, conv_w, conv_b.reshape(1, ch), ln_g.reshape(1, ch), ln_b.reshape(1, ch))


def _decode_conv_kernel(state_ref, u_ref, w_ref, cb_ref, g_ref, b_ref, o_ref, new_state_ref):
    kw = w_ref.shape[0]
    w_hist = w_ref[0:kw - 1, :]
    w_last = w_ref[kw - 1:kw, :]
    for b in range(state_ref.shape[0]):
        u = u_ref[b]
        conv = jnp.sum(state_ref[b] * w_hist, axis=0, keepdims=True) + u * w_last
        y = _layer_norm(conv + cb_ref[...], g_ref[...], b_ref[...])
        o_ref[b] = (y * _sigmoid(y)).astype(o_ref.dtype)
        new_state_ref[b, 0:kw - 2, :] = state_ref[b, 1:kw - 1, :]
        new_state_ref[b, kw - 2:kw - 1, :] = u


def _decode_conv(state, u, conv_w, conv_b, ln_g, ln_b):
    s, hist, ch = state.shape
    kw = conv_w.shape[0]
    ts = _tile(s, 16)
    vec = pl.BlockSpec((1, ch), lambda i: (0, 0))
    tok = pl.BlockSpec((ts, 1, ch), lambda i: (i, 0, 0))
    hist_spec = pl.BlockSpec((ts, hist, ch), lambda i: (i, 0, 0))
    blocks = 2 * ts * 32 * ch * 4 + 2 * ts * V7X_SUBLANES * ch * 4
    return pl.pallas_call(
        _decode_conv_kernel,
        grid=(s // ts,),
        in_specs=[hist_spec, tok, pl.BlockSpec((kw, ch), lambda i: (0, 0)), vec, vec, vec],
        out_specs=[tok, hist_spec],
        out_shape=[jax.ShapeDtypeStruct((s, 1, ch), F32), jax.ShapeDtypeStruct((s, hist, ch), F32)],
        compiler_params=_params(("parallel",), blocks),
        name="decode_conv",
    )(state, u, conv_w, conv_b.reshape(1, ch), ln_g.reshape(1, ch), ln_b.reshape(1, ch))


def _merge_kernel(x_ref, a_ref, c_ref, wga_ref, wgc_ref, wao_ref, wco_ref, o_ref):
    x = x_ref[...]
    mixed = (_sigmoid(_dot(x, wga_ref[...])) * _dot(a_ref[...], wao_ref[...])
             + _sigmoid(_dot(x, wgc_ref[...])) * _dot(c_ref[...], wco_ref[...]))
    o_ref[...] = mixed.astype(o_ref.dtype)


def _merge(xb, attn, cn, w_in, col_ga, col_gc, w_ao, w_co, tm):
    m, d = xb.shape
    wa = attn.shape[1]
    wc = cn.shape[1]
    tn = _tile(d, 512)
    tm = _tile(m, tm)
    blocks = tm * (d + wa + wc + tn) * 2 + (2 * d + wa + wc) * tn * 2
    return pl.pallas_call(
        _merge_kernel,
        grid=(m // tm, d // tn),
        in_specs=[pl.BlockSpec((tm, d), lambda i, j: (i, 0)),
                  pl.BlockSpec((tm, wa), lambda i, j: (i, 0)),
                  pl.BlockSpec((tm, wc), lambda i, j: (i, 0)),
                  pl.BlockSpec((d, tn), lambda i, j: (0, col_ga // tn + j)),
                  pl.BlockSpec((d, tn), lambda i, j: (0, col_gc // tn + j)),
                  pl.BlockSpec((wa, tn), lambda i, j: (0, j)),
                  pl.BlockSpec((wc, tn), lambda i, j: (0, j))],
        out_specs=pl.BlockSpec((tm, tn), lambda i, j: (i, j)),
        out_shape=jax.ShapeDtypeStruct((m, d), BF16),
        compiler_params=_params(("parallel", "parallel"), blocks),
        name="merge",
    )(xb, attn, cn, w_in, w_in, w_ao, w_co)


def _out_ln_kernel(x_ref, mixed_ref, w_ref, g_ref, b_ref, of_ref, *maybe_ob_ref, alpha):
    y = _layer_norm(alpha * x_ref[...] + _dot(mixed_ref[...], w_ref[...]), g_ref[...], b_ref[...])
    of_ref[...] = y
    for ob_ref in maybe_ob_ref:
        ob_ref[...] = y.astype(ob_ref.dtype)


def _out_ln(x, mixed, w_out, g, b, alpha, tm, emit_bf16):
    m, d = x.shape
    tm = _tile(m, tm)
    vec = pl.BlockSpec((1, d), lambda i: (0, 0))
    row = pl.BlockSpec((tm, d), lambda i: (i, 0))
    blocks = tm * d * (4 + 2 + 4 + 2) + d * d * 2
    return pl.pallas_call(
        functools.partial(_out_ln_kernel, alpha=alpha),
        grid=(m // tm,),
        in_specs=[row, row, pl.BlockSpec((d, d), lambda i: (0, 0)), vec, vec],
        out_specs=[row, row] if emit_bf16 else [row],
        out_shape=[jax.ShapeDtypeStruct((m, d), F32)] + ([jax.ShapeDtypeStruct((m, d), BF16)] if emit_bf16 else []),
        compiler_params=_params(("parallel",), blocks),
        name="out_ln",
    )(x, mixed, w_out, g.reshape(1, d), b.reshape(1, d))


def _mlp_ln_kernel(xf_ref, xb_ref, w1_ref, w2_ref, g_ref, b_ref, o_ref, *, alpha):
    f = pl.program_id(1)

    @pl.when(f == 0)
    def _():
        o_ref[...] = alpha * xf_ref[...]

    hid = jnp.maximum(_dot(xb_ref[...], w1_ref[...]), 0.0)
    o_ref[...] += _dot((hid * hid).astype(BF16), w2_ref[...])

    @pl.when(f == pl.num_programs(1) - 1)
    def _():
        o_ref[...] = _layer_norm(o_ref[...], g_ref[...], b_ref[...])


def _mlp_ln(xf, xb, w1, w2, g, b, alpha, tm):
    m, d = xf.shape
    dff = w1.shape[1]
    tf = _tile(dff, 1024)
    tm = _tile(m, tm)
    vec = pl.BlockSpec((1, d), lambda i, f: (0, 0))
    row = pl.BlockSpec((tm, d), lambda i, f: (i, 0))
    blocks = tm * d * (4 + 2 + 4) + 2 * d * tf * 2 + tm * tf * 4
    return pl.pallas_call(
        functools.partial(_mlp_ln_kernel, alpha=alpha),
        grid=(m // tm, dff // tf),
        in_specs=[row, row, pl.BlockSpec((d, tf), lambda i, f: (0, f)), pl.BlockSpec((tf, d), lambda i, f: (f, 0)),
                  vec, vec],
        out_specs=row,
        out_shape=jax.ShapeDtypeStruct((m, d), F32),
        compiler_params=_params(("parallel", "arbitrary"), blocks),
        name="mlp_ln",
    )(xf, xb, w1, w2, g.reshape(1, d), b.reshape(1, d))


def _mlp_ln_decode_kernel(pt_ref, xf_ref, w1_ref, w2_ref, g_ref, b_ref, q_ref, kn_ref, vn_ref, relb_ref, *refs,
                          alpha, pages_per_step, n_pages, pages_per_block, buckets):
    del pt_ref
    k_refs, v_refs = refs[:pages_per_step], refs[pages_per_step:2 * pages_per_step]
    o_ref, ao_ref, xb_ref, m_ref, l_ref, gs_ref, acc_ref = refs[2 * pages_per_step:]
    f = pl.program_id(1)
    nf = pl.num_programs(1)

    @pl.when(f == 0)
    def _():
        x = xf_ref[...]
        o_ref[...] = alpha * x
        xb_ref[...] = x.astype(BF16)

    hid = jnp.maximum(_dot(xb_ref[...], w1_ref[...]), 0.0)
    o_ref[...] += _dot((hid * hid).astype(BF16), w2_ref[...])
    part = lax.rem(pl.program_id(0) * nf + f, n_pages // pages_per_step)
    _decode_pages(part, q_ref, kn_ref, vn_ref, relb_ref, k_refs, v_refs, ao_ref, m_ref, l_ref, gs_ref, acc_ref,
                  n_pages=n_pages, pages_per_block=pages_per_block, buckets=buckets)

    @pl.when(f == nf - 1)
    def _():
        o_ref[...] = _layer_norm(o_ref[...], g_ref[...], b_ref[...])


def _mlp_tiles(m, dff, tm):
    return _tile(m, tm), _tile(dff, 1024)


def _fused_pages_per_step(m, dff, tm, s, n_pages):
    tm, tf = _mlp_tiles(m, dff, tm)
    steps = (m // tm) * (dff // tf)
    if (s * n_pages) % steps or n_pages % ((s * n_pages) // steps):
        return None
    return (s * n_pages) // steps


def _mlp_ln_decode(xf, w1, w2, g, b, alpha, tm, q, k_new, v_new, cache_k, cache_v, page_table, rel_bias, pps):
    m, d = xf.shape
    dff = w1.shape[1]
    tm, tf = _mlp_tiles(m, dff, tm)
    nf = dff // tf
    s, h, dh = q.shape
    page = cache_k.shape[1]
    n_pages = page_table.shape[1]
    parts = n_pages // pps
    assert (m // tm) * nf == s * parts
    buckets, relb, tok_spec, relb_spec, page_specs, scratch = _decode_specs(
        page_table, cache_k, rel_bias, h, dh, pps,
        seq_of=lambda i, f: (i * nf + f) // parts, part_of=lambda i, f: (i * nf + f) % parts)
    vec = pl.BlockSpec((1, d), lambda i, f, pt: (0, 0))
    row = pl.BlockSpec((tm, d), lambda i, f, pt: (i, 0))
    blocks = tm * d * 9 + 2 * d * tf * 2 + tm * tf * 4 + 2 * pps * page * h * dh * 4 + relb.size * 4
    kernel = functools.partial(_mlp_ln_decode_kernel, alpha=alpha, pages_per_step=pps, n_pages=n_pages,
                               pages_per_block=MOBA_BLOCK // page, buckets=buckets)
    return pl.pallas_call(
        kernel,
        grid_spec=pltpu.PrefetchScalarGridSpec(
            num_scalar_prefetch=1,
            grid=(m // tm, nf),
            in_specs=[row, pl.BlockSpec((d, tf), lambda i, f, pt: (0, f)), pl.BlockSpec((tf, d), lambda i, f, pt: (f, 0)),
                      vec, vec, tok_spec, tok_spec, tok_spec, relb_spec] + page_specs + page_specs,
            out_specs=[row, tok_spec],
            scratch_shapes=[pltpu.VMEM((tm, d), BF16)] + scratch),
        out_shape=[jax.ShapeDtypeStruct((m, d), F32), jax.ShapeDtypeStruct((s, h, dh), BF16)],
        compiler_params=_params(("arbitrary", "arbitrary"), blocks),
        name="mlp_ln_decode_attn",
    )(page_table, xf, w1, w2, g.reshape(1, d), b.reshape(1, d), q, k_new, v_new, relb,
      *([cache_k] * pps), *([cache_v] * pps))


def _merge_out(x2d, xb, attn, cn, lw, alpha, tm, emit_bf16):
    mixed = _merge(xb, attn, cn, lw["w_in"], lw["col_ga"], lw["col_gc"], lw["w_attn_out"], lw["w_conv_out"], tm=2 * tm)
    return _out_ln(x2d, mixed, lw["w_out"], lw["ln1_g"], lw["ln1_b"], alpha, tm, emit_bf16)


def kernel(x_prompt, x_sample, cache_k, cache_v, state_conv, page_table, rel_bias, w_in, w_attn_out, conv_w, conv_b,
           conv_ln_g, conv_ln_b, w_conv_out, w_out, ln1_g, ln1_b, w_ff1, w_ff2, ln2_g, ln2_b):
    depth = w_in.shape[0]
    n, t, d = x_prompt.shape
    s, ts, _ = x_sample.shape
    n_heads, dh = cache_k.shape[-2:]
    aw = n_heads * dh
    ch = conv_w.shape[-1]
    alpha = (2 * depth) ** 0.25
    scale = dh ** -0.5
    assert ts == 1 and w_in.shape[-1] == 3 * aw + 2 * ch + 2 * d
    col_k, col_v, col_val, col_gate = aw, 2 * aw, 3 * aw, 3 * aw + ch
    col_ga, col_gc = 3 * aw + 2 * ch, 3 * aw + 2 * ch + d

    bias_tiles = _prompt_bias_tiles(rel_bias)
    hp = x_prompt.reshape(n * t, d)
    hs = x_sample.reshape(s, d)
    outs = [[] for _ in range(6)]
    for l in range(depth):
        lw = dict(w_in=w_in[l].astype(BF16), w_attn_out=w_attn_out[l].astype(BF16),
                  w_conv_out=w_conv_out[l].astype(BF16), w_out=w_out[l].astype(BF16),
                  w_ff1=w_ff1[l].astype(BF16), w_ff2=w_ff2[l].astype(BF16),
                  ln1_g=ln1_g[l], ln1_b=ln1_b[l], ln2_g=ln2_g[l], ln2_b=ln2_b[l], col_ga=col_ga, col_gc=col_gc)
        wi = lw["w_in"]

        q, xb = _proj_q(hp, wi, 0, aw, scale, tm=1024, out_dtype=BF16)
        kf, kb, kmean = _proj_kv(xb, wi, col_k, aw, tm=t, block_means=True)
        vf, vb = _proj_kv(xb, wi, col_v, aw, tm=t, block_means=False)
        u = _proj_glu(xb, wi, col_val, col_gate, ch, tm=1024)
        attn = _prompt_attention(q.reshape(n, t, aw), kb.reshape(n, t, aw), vb.reshape(n, t, aw), kmean,
                                 bias_tiles, n_heads)
        u3 = u.reshape(n, t, ch)
        cn = _prompt_conv(u3, conv_w[l], conv_b[l], conv_ln_g[l], conv_ln_b[l])
        outs[0].append(kf.reshape(n, t, n_heads, dh))
        outs[1].append(vf.reshape(n, t, n_heads, dh))
        outs[2].append(u3[:, t - (conv_w.shape[1] - 1):, :])

        qs, xsb = _proj_q(hs, wi, 0, aw, scale, tm=s, out_dtype=F32)
        ksf, _ = _proj_kv(xsb, wi, col_k, aw, tm=s, block_means=False)
        vsf, _ = _proj_kv(xsb, wi, col_v, aw, tm=s, block_means=False)
        us = _proj_glu(xsb, wi, col_val, col_gate, ch, tm=s)
        dec = (qs.reshape(s, n_heads, dh), ksf.reshape(s, n_heads, dh), vsf.reshape(s, n_heads, dh),
               cache_k[l], cache_v[l], page_table, rel_bias)

        mlp_w = (lw["w_ff1"], lw["w_ff2"], lw["ln2_g"], lw["ln2_b"], alpha)
        pps = _fused_pages_per_step(n * t, w_ff1.shape[-1], 512, s, page_table.shape[1])
        x1 = _merge_out(hp, xb, attn.reshape(n * t, aw), cn.reshape(n * t, ch), lw, alpha, 512, emit_bf16=pps is None)
        if pps is None:
            hp_next = _mlp_ln(x1[0], x1[1], *mlp_w, tm=512)
            attn_s = _decode_attention(*dec)
        else:
            hp_next, attn_s = _mlp_ln_decode(x1[0], *mlp_w, 512, *dec, pps)

        cn_s, new_state = _decode_conv(state_conv[l], us.reshape(s, 1, ch), conv_w[l], conv_b[l], conv_ln_g[l],
                                       conv_ln_b[l])
        x1s = _merge_out(hs, xsb, attn_s.reshape(s, aw), cn_s.reshape(s, ch).astype(BF16), lw, alpha, s, emit_bf16=True)
        hs_next = _mlp_ln(x1s[0], x1s[1], *mlp_w, tm=s)
        outs[3].append(ksf.reshape(s, 1, n_heads, dh))
        outs[4].append(vsf.reshape(s, 1, n_heads, dh))
        outs[5].append(new_state)
        hp, hs = hp_next, hs_next

    return (hp.reshape(n, t, d), hs.reshape(s, 1, d)) + tuple(jnp.stack(o) for o in outs)
```

```python
import functools
import math

import numpy as np
import jax
import jax.numpy as jnp
from jax import lax
from jax.experimental import pallas as pl
from jax.experimental.pallas import tpu as pltpu

MOBA_BLOCK = 256
MOBA_TOPK = 3
MAX_DISTANCE = 128
LN_EPS = 1e-5
NEG_INF = -1e30

V7X_VMEM_BYTES = 64 * 1024 * 1024
V7X_LANES = 128
V7X_SUBLANES = 8

F32 = jnp.float32
BF16 = jnp.bfloat16


def _vmem_limit(block_bytes):
    return int(min(max(2 * block_bytes + (16 << 20), 32 << 20), V7X_VMEM_BYTES - (4 << 20)))


def _params(semantics, block_bytes):
    return pltpu.CompilerParams(dimension_semantics=semantics, vmem_limit_bytes=_vmem_limit(block_bytes))


def _tile(n, want):
    t = min(n, want)
    while n % t:
        t -= 1
    return t


def _sigmoid(x):
    return 1.0 / (1.0 + jnp.exp(-x))


def _layer_norm(y, g, b):
    mu = jnp.mean(y, axis=-1, keepdims=True)
    d = y - mu
    var = jnp.mean(d * d, axis=-1, keepdims=True)
    return d * lax.rsqrt(var + LN_EPS) * g + b


def _dot(a, b):
    return jnp.dot(a, b, preferred_element_type=F32)


def _dot_nt(a, b):
    return lax.dot_general(a, b, (((1,), (1,)), ((), ())), preferred_element_type=F32)


def _proj_q_kernel(x_ref, w_ref, o_ref, xb_ref, *, scale):
    xb = x_ref[...].astype(BF16)
    xb_ref[...] = xb
    o_ref[...] = (_dot(xb, w_ref[...]) * scale).astype(o_ref.dtype)


def _proj_kv_kernel(x_ref, w_ref, of_ref, ob_ref, *rest):
    acc = _dot(x_ref[...], w_ref[...])
    of_ref[...] = acc
    ob_ref[...] = acc.astype(ob_ref.dtype)
    if rest:
        (km_ref,) = rest
        tm, tn = acc.shape
        km_ref[...] = jnp.sum(acc.reshape(tm // MOBA_BLOCK, MOBA_BLOCK, tn), axis=1) * (1.0 / MOBA_BLOCK)


def _proj_glu_kernel(x_ref, wv_ref, wg_ref, u_ref):
    x = x_ref[...]
    u_ref[...] = _dot(x, wv_ref[...]) * _sigmoid(_dot(x, wg_ref[...]))


def _proj_q(x, w, col0, ncols, scale, tm, out_dtype):
    m, d = x.shape
    tn = _tile(ncols, 1024)
    tm = _tile(m, tm)
    blocks = tm * d * 6 + d * tn * 2 + tm * tn * 4
    return pl.pallas_call(
        functools.partial(_proj_q_kernel, scale=scale),
        grid=(m // tm, ncols // tn),
        in_specs=[pl.BlockSpec((tm, d), lambda i, j: (i, 0)),
                  pl.BlockSpec((d, tn), lambda i, j: (0, col0 // tn + j))],
        out_specs=[pl.BlockSpec((tm, tn), lambda i, j: (i, j)), pl.BlockSpec((tm, d), lambda i, j: (i, 0))],
        out_shape=[jax.ShapeDtypeStruct((m, ncols), out_dtype), jax.ShapeDtypeStruct((m, d), BF16)],
        compiler_params=_params(("parallel", "arbitrary"), blocks),
        name="proj_q",
    )(x, w)


def _proj_kv(xb, w, col0, ncols, tm, block_means):
    m, d = xb.shape
    tn = _tile(ncols, 512)
    tm = _tile(m, tm)
    out_specs = [pl.BlockSpec((tm, tn), lambda i, j: (i, j)), pl.BlockSpec((tm, tn), lambda i, j: (i, j))]
    out_shape = [jax.ShapeDtypeStruct((m, ncols), F32), jax.ShapeDtypeStruct((m, ncols), BF16)]
    if block_means:
        nb = tm // MOBA_BLOCK
        out_specs.append(pl.BlockSpec((None, nb, tn), lambda i, j: (i, 0, j)))
        out_shape.append(jax.ShapeDtypeStruct((m // tm, nb, ncols), F32))
    blocks = tm * d * 2 + d * tn * 2 + tm * tn * 6
    return pl.pallas_call(
        _proj_kv_kernel,
        grid=(m // tm, ncols // tn),
        in_specs=[pl.BlockSpec((tm, d), lambda i, j: (i, 0)),
                  pl.BlockSpec((d, tn), lambda i, j: (0, col0 // tn + j))],
        out_specs=out_specs,
        out_shape=out_shape,
        compiler_params=_params(("parallel", "parallel"), blocks),
        name="proj_k" if block_means else "proj_v",
    )(xb, w)


def _proj_glu(xb, w, col_val, col_gate, ncols, tm):
    m, d = xb.shape
    tn = _tile(ncols, 512)
    tm = _tile(m, tm)
    blocks = tm * d * 2 + 2 * d * tn * 2 + tm * tn * 4
    return pl.pallas_call(
        _proj_glu_kernel,
        grid=(m // tm, ncols // tn),
        in_specs=[pl.BlockSpec((tm, d), lambda i, j: (i, 0)),
                  pl.BlockSpec((d, tn), lambda i, j: (0, col_val // tn + j)),
                  pl.BlockSpec((d, tn), lambda i, j: (0, col_gate // tn + j))],
        out_specs=pl.BlockSpec((tm, tn), lambda i, j: (i, j)),
        out_shape=jax.ShapeDtypeStruct((m, ncols), F32),
        compiler_params=_params(("parallel", "parallel"), blocks),
        name="proj_glu",
    )(xb, w, w)


def _rel_bucket_np(dist, num_buckets):
    n = np.maximum(dist, 0)
    max_exact = num_buckets // 2
    nf = np.maximum(n, 1).astype(np.float32)
    large = max_exact + (np.log(nf / np.float32(max_exact)) / np.float32(math.log(MAX_DISTANCE / max_exact))
                         * np.float32(num_buckets - max_exact)).astype(np.int32)
    large = np.minimum(large, num_buckets - 1)
    return np.where(n < max_exact, n, large).astype(np.int32)


def _bias_tiles_kernel(rb_ref, bucket_ref, o_ref, *, num_buckets):
    h = pl.program_id(0)
    for t in range(bucket_ref.shape[0]):
        bucket = bucket_ref[t]
        acc = jnp.full(bucket.shape, NEG_INF, F32)
        for b in range(num_buckets):
            acc = jnp.where(bucket == b, rb_ref[b, h], acc)
        o_ref[t] = acc


def _prompt_bias_tiles(rel_bias):
    num_buckets, n_heads = rel_bias.shape
    assert MAX_DISTANCE <= MOBA_BLOCK + 1
    r = np.arange(MOBA_BLOCK)[:, None]
    c = np.arange(MOBA_BLOCK)[None, :]
    diag = np.where(r - c >= 0, _rel_bucket_np(r - c, num_buckets), -1)
    prev = _rel_bucket_np(r - c + MOBA_BLOCK, num_buckets)
    far = _rel_bucket_np(r - c + 2 * MOBA_BLOCK, num_buckets)
    buckets = jnp.asarray(np.stack([diag, prev, far]).astype(np.int32))
    return pl.pallas_call(
        functools.partial(_bias_tiles_kernel, num_buckets=num_buckets),
        grid=(n_heads,),
        in_specs=[pl.BlockSpec(memory_space=pltpu.SMEM),
                  pl.BlockSpec((3, MOBA_BLOCK, MOBA_BLOCK), lambda h: (0, 0, 0))],
        out_specs=pl.BlockSpec((None, 3, MOBA_BLOCK, MOBA_BLOCK), lambda h: (h, 0, 0, 0)),
        out_shape=jax.ShapeDtypeStruct((n_heads, 3, MOBA_BLOCK, MOBA_BLOCK), F32),
        compiler_params=_params(("arbitrary",), 6 * MOBA_BLOCK * MOBA_BLOCK * 4),
        name="bias_tiles",
    )(rel_bias, buckets)


def _block_penalty_t(gate_t, n_valid):
    row = lax.broadcasted_iota(jnp.int32, gate_t.shape, 0)
    valid = row < n_valid
    pen = jnp.zeros(gate_t.shape, F32)
    for j in range(n_valid):
        gj = gate_t[j:j + 1, :]
        beats = ((gate_t > gj) | ((gate_t == gj) & (row < j))) & valid
        rank = jnp.sum(beats.astype(F32), axis=0, keepdims=True)
        pen = jnp.where(row == j, jnp.where(rank < MOBA_TOPK, 0.0, NEG_INF), pen)
    return pen


def _prompt_attn_kernel(q_ref, k_ref, v_ref, km_ref, bias_ref, o_ref, kx_ref, pen_ref):
    t, dh = q_ref.shape
    blk = MOBA_BLOCK
    nblk = t // blk
    row = lax.broadcasted_iota(jnp.int32, (t, dh), 0)
    col = lax.broadcasted_iota(jnp.int32, (t, dh), 1)
    kx_ref[:, :dh] = k_ref[...]
    kx_ref[:, dh:] = (col == lax.shift_right_logical(row, blk.bit_length() - 1)).astype(BF16)
    km = km_ref[...]
    km_hi = km.astype(BF16)
    km_lo = (km - km_hi.astype(F32)).astype(BF16)
    eye = (lax.broadcasted_iota(jnp.int32, (blk, blk), 0) == lax.broadcasted_iota(jnp.int32, (blk, blk), 1)).astype(BF16)
    pen_ref[...] = jnp.zeros(pen_ref.shape, F32)
    for i in range(nblk):
        qi = q_ref[i * blk:(i + 1) * blk, :]
        nk = (i + 1) * blk
        if i > MOBA_TOPK:
            gate_t = _dot_nt(km_hi, qi) + _dot_nt(km_lo, qi)
            pen_ref[0:nblk, :] = _block_penalty_t(gate_t, i)
            pen = _dot_nt(eye, pen_ref[...].astype(BF16)).astype(BF16)
            s = _dot_nt(jnp.concatenate([qi, pen], axis=1), kx_ref[0:nk, :])
        else:
            s = _dot_nt(qi, k_ref[0:nk, :])
        bias = [bias_ref[2]] * (i - 1) + ([bias_ref[1]] if i > 0 else []) + [bias_ref[0]]
        s = s + (jnp.concatenate(bias, axis=1) if len(bias) > 1 else bias[0])
        m = jnp.max(s, axis=1, keepdims=True)
        p = jnp.exp(s - m)
        l = jnp.sum(p, axis=1, keepdims=True)
        o = _dot(p.astype(BF16), v_ref[0:nk, :])
        o_ref[i * blk:(i + 1) * blk, :] = (o / l).astype(o_ref.dtype)


def _prompt_attention(q, k, v, kmean, bias_tiles, n_heads):
    n, t, width = q.shape
    dh = width // n_heads
    nblk = t // MOBA_BLOCK
    assert dh == V7X_LANES and t % MOBA_BLOCK == 0 and nblk <= V7X_LANES and MOBA_BLOCK & (MOBA_BLOCK - 1) == 0
    seq_spec = pl.BlockSpec((None, t, dh), lambda h, b: (b, 0, h))
    blocks = 4 * t * dh * 2 + 3 * MOBA_BLOCK * MOBA_BLOCK * 4 + t * 2 * dh * 2 + 6 * MOBA_BLOCK * t * 4
    return pl.pallas_call(
        _prompt_attn_kernel,
        grid=(n_heads, n),
        in_specs=[seq_spec, seq_spec, seq_spec,
                  pl.BlockSpec((None, nblk, dh), lambda h, b: (b, 0, h)),
                  pl.BlockSpec((None, 3, MOBA_BLOCK, MOBA_BLOCK), lambda h, b: (h, 0, 0, 0))],
        out_specs=seq_spec,
        out_shape=jax.ShapeDtypeStruct((n, t, width), BF16),
        scratch_shapes=[pltpu.VMEM((t, 2 * dh), BF16), pltpu.VMEM((dh, MOBA_BLOCK), F32)],
        compiler_params=_params(("parallel", "parallel"), blocks),
        name="prompt_attn",
    )(q, k, v, kmean, bias_tiles)


def _lane_sums(xs):
    dh = xs[0].shape[-1]
    r = lax.broadcasted_iota(jnp.int32, (2 * dh, 2 * dh), 0)
    c = lax.broadcasted_iota(jnp.int32, (2 * dh, 2 * dh), 1)
    pair_ones = ((r < dh) == (c < dh)).astype(BF16)
    outs = []
    for i in range(0, len(xs) - 1, 2):
        y = _dot(jnp.concatenate([xs[i].astype(BF16), xs[i + 1].astype(BF16)], axis=1), pair_ones)
        outs += [y[:, :dh], y[:, dh:]]
    if len(xs) % 2:
        outs.append(_dot(xs[-1].astype(BF16), jnp.ones((dh, dh), BF16)))
    return outs


def _page_bias(relb_ref, page_buckets):
    if len(set(page_buckets)) == 1:
        return relb_ref[page_buckets[0]][None]
    return jnp.stack([relb_ref[b] for b in page_buckets])


def _decode_pages(part, q_ref, kn_ref, vn_ref, relb_ref, k_refs, v_refs, o_ref, m_ref, l_ref, g_ref, acc_ref, *,
                  n_pages, pages_per_block, buckets):
    pps = len(k_refs)
    parts = n_pages // pps
    page, h, dh = k_refs[0].shape
    q = q_ref[...]
    sums = _lane_sums([(k_refs[p][...] * q[None]).reshape(page * h, dh) for p in range(pps)])
    for p in range(pps):
        s = sums[p].reshape(page, h, dh)
        gsum = jnp.sum(s, axis=0)
        page_buckets = [buckets[(a * pps + p) * page:(a * pps + p + 1) * page] for a in range(parts)]
        bias = _page_bias(relb_ref, page_buckets[0])
        for a in range(1, parts):
            if page_buckets[a] != page_buckets[0]:
                bias = jnp.where(part == a, _page_bias(relb_ref, page_buckets[a]), bias)
        s = s + bias
        m = jnp.max(s, axis=0)
        e = jnp.exp(s - m[None])
        idx = part * pps + p
        m_ref[idx] = m
        l_ref[idx] = jnp.sum(e, axis=0)
        g_ref[idx] = gsum
        acc_ref[idx] = jnp.sum(e * v_refs[p][...], axis=0)

    def merge():
        n_blocks = n_pages // pages_per_block
        gate = [sum(g_ref[p] for p in range(j * pages_per_block, (j + 1) * pages_per_block)) for j in range(n_blocks)]
        s_own = _lane_sums([q * kn_ref[...]])[0] + relb_ref[0]
        m_tot = s_own
        sel = []
        for j in range(n_blocks):
            rank = jnp.zeros((h, dh), F32)
            for j2 in range(n_blocks):
                if j2 != j:
                    beats = (gate[j2] > gate[j]) | ((gate[j2] == gate[j]) & (j2 < j))
                    rank = rank + beats.astype(F32)
            sel.append(rank < MOBA_TOPK)
            for p in range(j * pages_per_block, (j + 1) * pages_per_block):
                m_tot = jnp.maximum(m_tot, jnp.where(sel[j], m_ref[p], NEG_INF))
        w_own = jnp.exp(s_own - m_tot)
        num = w_own * vn_ref[...]
        den = w_own
        for j in range(n_blocks):
            for p in range(j * pages_per_block, (j + 1) * pages_per_block):
                w = jnp.where(sel[j], jnp.exp(m_ref[p] - m_tot), 0.0)
                num = num + w * acc_ref[p]
                den = den + w * l_ref[p]
        o_ref[...] = (num / den).astype(o_ref.dtype)

    if parts == 1:
        merge()
    else:
        pl.when(part == parts - 1)(merge)


def _decode_attn_kernel(pt_ref, q_ref, kn_ref, vn_ref, relb_ref, *refs, n_pages, pages_per_block, buckets):
    del pt_ref
    _decode_pages(0, q_ref, kn_ref, vn_ref, relb_ref, refs[:n_pages], refs[n_pages:2 * n_pages], *refs[2 * n_pages:],
                  n_pages=n_pages, pages_per_block=pages_per_block, buckets=buckets)


def _decode_specs(page_table, cache_k, rel_bias, h, dh, pages_per_step, seq_of, part_of):
    page = cache_k.shape[1]
    n_pages = page_table.shape[1]
    past = n_pages * page
    num_buckets = rel_bias.shape[0]
    assert dh == V7X_LANES and MOBA_BLOCK % page == 0 and past % MOBA_BLOCK == 0 and n_pages % pages_per_step == 0
    buckets = tuple(int(b) for b in _rel_bucket_np(past - np.arange(past), num_buckets))
    relb = jnp.broadcast_to(rel_bias[:, :, None], (num_buckets, h, dh))
    tok_spec = pl.BlockSpec((None, h, dh), lambda *g: (seq_of(*g[:-1]), 0, 0))
    relb_spec = pl.BlockSpec((num_buckets, h, dh), lambda *g: (0, 0, 0))
    page_specs = [pl.BlockSpec((None, page, h, dh),
                               lambda *g, p=p: (g[-1][seq_of(*g[:-1]), part_of(*g[:-1]) * pages_per_step + p], 0, 0, 0))
                  for p in range(pages_per_step)]
    scratch = [pltpu.VMEM((n_pages, h, dh), F32)] * 4
    return buckets, relb, tok_spec, relb_spec, page_specs, scratch


def _decode_attention(q, k_new, v_new, cache_k, cache_v, page_table, rel_bias):
    s, h, dh = q.shape
    page = cache_k.shape[1]
    n_pages = page_table.shape[1]
    buckets, relb, tok_spec, relb_spec, page_specs, scratch = _decode_specs(
        page_table, cache_k, rel_bias, h, dh, n_pages, seq_of=lambda b: b, part_of=lambda b: 0)
    blocks = 2 * n_pages * page * h * dh * 4 + relb.size * 4
    kernel = functools.partial(_decode_attn_kernel, n_pages=n_pages, pages_per_block=MOBA_BLOCK // page,
                               buckets=buckets)
    return pl.pallas_call(
        kernel,
        grid_spec=pltpu.PrefetchScalarGridSpec(
            num_scalar_prefetch=1,
            grid=(s,),
            in_specs=[tok_spec, tok_spec, tok_spec, relb_spec] + page_specs + page_specs,
            out_specs=tok_spec,
            scratch_shapes=scratch),
        out_shape=jax.ShapeDtypeStruct((s, h, dh), BF16),
        compiler_params=_params(("parallel",), blocks),
        name="decode_attn",
    )(page_table, q, k_new, v_new, relb, *([cache_k] * n_pages), *([cache_v] * n_pages))


CONV_HALO = 32
CONV_ROWS = 64


def _prompt_conv_kernel(halo_ref, u_ref, w_ref, cb_ref, g_ref, b_ref, o_ref, ext_ref, conv_ref):
    tt, ch = u_ref.shape
    kw = w_ref.shape[0]
    first = CONV_HALO - (kw - 1)

    @pl.when(pl.program_id(1) == 0)
    def _():
        ext_ref[0:CONV_HALO, :] = jnp.zeros((CONV_HALO, ch), F32)

    @pl.when(pl.program_id(1) > 0)
    def _():
        ext_ref[0:CONV_HALO, :] = halo_ref[...]

    ext_ref[CONV_HALO:CONV_HALO + tt, :] = u_ref[...]
    rows = min(CONV_ROWS, tt)
    span = rows + CONV_HALO
    for c0 in range(0, ch, V7X_LANES):
        cs = slice(c0, c0 + V7X_LANES)
        for r0 in range(0, tt, rows):
            x = ext_ref[r0:r0 + span, cs]
            acc = jnp.zeros((rows, V7X_LANES), F32)
            for b in range(V7X_SUBLANES):
                taps = [s for s in range(first, first + kw) if s % V7X_SUBLANES == b]
                assert all(s + rows <= span for s in taps)
                xb = x if b == 0 else pltpu.roll(x, span - b, axis=0)
                for s in taps:
                    acc = acc + xb[s - b:s - b + rows] * w_ref[s - first:s - first + 1, cs]
            conv_ref[r0:r0 + rows, cs] = acc
    y = _layer_norm(conv_ref[...] + cb_ref[...], g_ref[...], b_ref[...])
    o_ref[...] = (y * _sigmoid(y)).astype(o_ref.dtype)


def _prompt_conv(u, conv_w, conv_b, ln_g, ln_b):
    n, t, ch = u.shape
    kw = conv_w.shape[0]
    tt = _tile(t, 128)
    assert kw - 1 <= CONV_HALO and tt % CONV_HALO == 0 and ch % V7X_LANES == 0
    vec = pl.BlockSpec((1, ch), lambda b, i: (0, 0))
    blocks = (2 * tt + 2 * CONV_HALO) * ch * 4 + tt * ch * 2
    return pl.pallas_call(
        _prompt_conv_kernel,
        grid=(n, t // tt),
        in_specs=[pl.BlockSpec((None, CONV_HALO, ch), lambda b, i: (b, jnp.maximum(i * (tt // CONV_HALO) - 1, 0), 0)),
                  pl.BlockSpec((None, tt, ch), lambda b, i: (b, i, 0)),
                  pl.BlockSpec((kw, ch), lambda b, i: (0, 0)), vec, vec, vec],
        out_specs=pl.BlockSpec((None, tt, ch), lambda b, i: (b, i, 0)),
        out_shape=jax.ShapeDtypeStruct((n, t, ch), BF16),
        scratch_shapes=[pltpu.VMEM((CONV_HALO + tt, ch), F32), pltpu.VMEM((tt, ch), F32)],
        compiler_params=_params(("parallel", "arbitrary"), blocks),
        name="prompt_conv",
    )(u, u, conv_w, conv_b.reshape(1, ch), ln_g.reshape(1, ch), ln_b.reshape(1, ch))


def _decode_conv_kernel(state_ref, u_ref, w_ref, cb_ref, g_ref, b_ref, o_ref, new_state_ref):
    kw = w_ref.shape[0]
    w_hist = w_ref[0:kw - 1, :]
    w_last = w_ref[kw - 1:kw, :]
    for b in range(state_ref.shape[0]):
        u = u_ref[b]
        conv = jnp.sum(state_ref[b] * w_hist, axis=0, keepdims=True) + u * w_last
        y = _layer_norm(conv + cb_ref[...], g_ref[...], b_ref[...])
        o_ref[b] = (y * _sigmoid(y)).astype(o_ref.dtype)
        new_state_ref[b, 0:kw - 2, :] = state_ref[b, 1:kw - 1, :]
        new_state_ref[b, kw - 2:kw - 1, :] = u


def _decode_conv(state, u, conv_w, conv_b, ln_g, ln_b):
    s, hist, ch = state.shape
    kw = conv_w.shape[0]
    ts = _tile(s, 16)
    vec = pl.BlockSpec((1, ch), lambda i: (0, 0))
    tok = pl.BlockSpec((ts, 1, ch), lambda i: (i, 0, 0))
    hist_spec = pl.BlockSpec((ts, hist, ch), lambda i: (i, 0, 0))
    blocks = 2 * ts * 32 * ch * 4 + 2 * ts * V7X_SUBLANES * ch * 4
    return pl.pallas_call(
        _decode_conv_kernel,
        grid=(s // ts,),
        in_specs=[hist_spec, tok, pl.BlockSpec((kw, ch), lambda i: (0, 0)), vec, vec, vec],
        out_specs=[tok, hist_spec],
        out_shape=[jax.ShapeDtypeStruct((s, 1, ch), F32), jax.ShapeDtypeStruct((s, hist, ch), F32)],
        compiler_params=_params(("parallel",), blocks),
        name="decode_conv",
    )(state, u, conv_w, conv_b.reshape(1, ch), ln_g.reshape(1, ch), ln_b.reshape(1, ch))


def _merge_kernel(x_ref, a_ref, c_ref, wga_ref, wgc_ref, wao_ref, wco_ref, o_ref):
    x = x_ref[...]
    mixed = (_sigmoid(_dot(x, wga_ref[...])) * _dot(a_ref[...], wao_ref[...])
             + _sigmoid(_dot(x, wgc_ref[...])) * _dot(c_ref[...], wco_ref[...]))
    o_ref[...] = mixed.astype(o_ref.dtype)


def _merge(xb, attn, cn, w_in, col_ga, col_gc, w_ao, w_co, tm):
    m, d = xb.shape
    wa = attn.shape[1]
    wc = cn.shape[1]
    tn = _tile(d, 512)
    tm = _tile(m, tm)
    blocks = tm * (d + wa + wc + tn) * 2 + (2 * d + wa + wc) * tn * 2
    return pl.pallas_call(
        _merge_kernel,
        grid=(m // tm, d // tn),
        in_specs=[pl.BlockSpec((tm, d), lambda i, j: (i, 0)),
                  pl.BlockSpec((tm, wa), lambda i, j: (i, 0)),
                  pl.BlockSpec((tm, wc), lambda i, j: (i, 0)),
                  pl.BlockSpec((d, tn), lambda i, j: (0, col_ga // tn + j)),
                  pl.BlockSpec((d, tn), lambda i, j: (0, col_gc // tn + j)),
                  pl.BlockSpec((wa, tn), lambda i, j: (0, j)),
                  pl.BlockSpec((wc, tn), lambda i, j: (0, j))],
        out_specs=pl.BlockSpec((tm, tn), lambda i, j: (i, j)),
        out_shape=jax.ShapeDtypeStruct((m, d), BF16),
        compiler_params=_params(("parallel", "parallel"), blocks),
        name="merge",
    )(xb, attn, cn, w_in, w_in, w_ao, w_co)


def _out_ln_kernel(x_ref, mixed_ref, w_ref, g_ref, b_ref, of_ref, *maybe_ob_ref, alpha):
    y = _layer_norm(alpha * x_ref[...] + _dot(mixed_ref[...], w_ref[...]), g_ref[...], b_ref[...])
    of_ref[...] = y
    for ob_ref in maybe_ob_ref:
        ob_ref[...] = y.astype(ob_ref.dtype)


def _out_ln(x, mixed, w_out, g, b, alpha, tm, emit_bf16):
    m, d = x.shape
    tm = _tile(m, tm)
    vec = pl.BlockSpec((1, d), lambda i: (0, 0))
    row = pl.BlockSpec((tm, d), lambda i: (i, 0))
    blocks = tm * d * (4 + 2 + 4 + 2) + d * d * 2
    return pl.pallas_call(
        functools.partial(_out_ln_kernel, alpha=alpha),
        grid=(m // tm,),
        in_specs=[row, row, pl.BlockSpec((d, d), lambda i: (0, 0)), vec, vec],
        out_specs=[row, row] if emit_bf16 else [row],
        out_shape=[jax.ShapeDtypeStruct((m, d), F32)] + ([jax.ShapeDtypeStruct((m, d), BF16)] if emit_bf16 else []),
        compiler_params=_params(("parallel",), blocks),
        name="out_ln",
    )(x, mixed, w_out, g.reshape(1, d), b.reshape(1, d))


def _mlp_ln_kernel(xf_ref, xb_ref, w1_ref, w2_ref, g_ref, b_ref, o_ref, *, alpha):
    f = pl.program_id(1)

    @pl.when(f == 0)
    def _():
        o_ref[...] = alpha * xf_ref[...]

    hid = jnp.maximum(_dot(xb_ref[...], w1_ref[...]), 0.0)
    o_ref[...] += _dot((hid * hid).astype(BF16), w2_ref[...])

    @pl.when(f == pl.num_programs(1) - 1)
    def _():
        o_ref[...] = _layer_norm(o_ref[...], g_ref[...], b_ref[...])


def _mlp_ln(xf, xb, w1, w2, g, b, alpha, tm):
    m, d = xf.shape
    dff = w1.shape[1]
    tf = _tile(dff, 1024)
    tm = _tile(m, tm)
    vec = pl.BlockSpec((1, d), lambda i, f: (0, 0))
    row = pl.BlockSpec((tm, d), lambda i, f: (i, 0))
    blocks = tm * d * (4 + 2 + 4) + 2 * d * tf * 2 + tm * tf * 4
    return pl.pallas_call(
        functools.partial(_mlp_ln_kernel, alpha=alpha),
        grid=(m // tm, dff // tf),
        in_specs=[row, row, pl.BlockSpec((d, tf), lambda i, f: (0, f)), pl.BlockSpec((tf, d), lambda i, f: (f, 0)),
                  vec, vec],
        out_specs=row,
        out_shape=jax.ShapeDtypeStruct((m, d), F32),
        compiler_params=_params(("parallel", "arbitrary"), blocks),
        name="mlp_ln",
    )(xf, xb, w1, w2, g.reshape(1, d), b.reshape(1, d))


def _mlp_ln_decode_kernel(pt_ref, xf_ref, w1_ref, w2_ref, g_ref, b_ref, q_ref, kn_ref, vn_ref, relb_ref, *refs,
                          alpha, pages_per_step, n_pages, pages_per_block, buckets):
    del pt_ref
    k_refs, v_refs = refs[:pages_per_step], refs[pages_per_step:2 * pages_per_step]
    o_ref, ao_ref, xb_ref, m_ref, l_ref, gs_ref, acc_ref = refs[2 * pages_per_step:]
    f = pl.program_id(1)
    nf = pl.num_programs(1)

    @pl.when(f == 0)
    def _():
        x = xf_ref[...]
        o_ref[...] = alpha * x
        xb_ref[...] = x.astype(BF16)

    hid = jnp.maximum(_dot(xb_ref[...], w1_ref[...]), 0.0)
    hid = (hid * hid).astype(BF16)
    d = o_ref.shape[1]
    cw = _tile(d, 512)
    for c0 in range(0, d, cw):
        o_ref[:, c0:c0 + cw] += _dot(hid, w2_ref[:, c0:c0 + cw])
    part = lax.rem(pl.program_id(0) * nf + f, n_pages // pages_per_step)
    _decode_pages(part, q_ref, kn_ref, vn_ref, relb_ref, k_refs, v_refs, ao_ref, m_ref, l_ref, gs_ref, acc_ref,
                  n_pages=n_pages, pages_per_block=pages_per_block, buckets=buckets)

    @pl.when(f == nf - 1)
    def _():
        o_ref[...] = _layer_norm(o_ref[...], g_ref[...], b_ref[...])


FUSED_MLP_ROWS = 1024
FUSED_MLP_COLS = 512


def _fused_pages_per_step(m, dff, s, n_pages):
    steps = (m // _tile(m, FUSED_MLP_ROWS)) * (dff // _tile(dff, FUSED_MLP_COLS))
    if (s * n_pages) % steps or n_pages % ((s * n_pages) // steps):
        return None
    return (s * n_pages) // steps


def _mlp_ln_decode(xf, w1, w2, g, b, alpha, q, k_new, v_new, cache_k, cache_v, page_table, rel_bias, pps):
    m, d = xf.shape
    dff = w1.shape[1]
    tm, tf = _tile(m, FUSED_MLP_ROWS), _tile(dff, FUSED_MLP_COLS)
    nf = dff // tf
    s, h, dh = q.shape
    page = cache_k.shape[1]
    n_pages = page_table.shape[1]
    parts = n_pages // pps
    assert (m // tm) * nf == s * parts
    buckets, relb, tok_spec, relb_spec, page_specs, scratch = _decode_specs(
        page_table, cache_k, rel_bias, h, dh, pps,
        seq_of=lambda i, f: (i * nf + f) // parts, part_of=lambda i, f: (i * nf + f) % parts)
    vec = pl.BlockSpec((1, d), lambda i, f, pt: (0, 0))
    row = pl.BlockSpec((tm, d), lambda i, f, pt: (i, 0), pipeline_mode=pl.Buffered(1))
    blocks = tm * d * 5 + 2 * d * tf * 2 + tm * tf * 4 + 2 * pps * page * h * dh * 4 + relb.size * 4
    kernel = functools.partial(_mlp_ln_decode_kernel, alpha=alpha, pages_per_step=pps, n_pages=n_pages,
                               pages_per_block=MOBA_BLOCK // page, buckets=buckets)
    return pl.pallas_call(
        kernel,
        grid_spec=pltpu.PrefetchScalarGridSpec(
            num_scalar_prefetch=1,
            grid=(m // tm, nf),
            in_specs=[row, pl.BlockSpec((d, tf), lambda i, f, pt: (0, f)), pl.BlockSpec((tf, d), lambda i, f, pt: (f, 0)),
                      vec, vec, tok_spec, tok_spec, tok_spec, relb_spec] + page_specs + page_specs,
            out_specs=[row, tok_spec],
            scratch_shapes=[pltpu.VMEM((tm, d), BF16)] + scratch),
        out_shape=[jax.ShapeDtypeStruct((m, d), F32), jax.ShapeDtypeStruct((s, h, dh), BF16)],
        compiler_params=_params(("arbitrary", "arbitrary"), blocks),
        name="mlp_ln_decode_attn",
    )(page_table, xf, w1, w2, g.reshape(1, d), b.reshape(1, d), q, k_new, v_new, relb,
      *([cache_k] * pps), *([cache_v] * pps))


def _merge_out(x2d, xb, attn, cn, lw, alpha, tm, emit_bf16):
    mixed = _merge(xb, attn, cn, lw["w_in"], lw["col_ga"], lw["col_gc"], lw["w_attn_out"], lw["w_conv_out"], tm=2 * tm)
    return _out_ln(x2d, mixed, lw["w_out"], lw["ln1_g"], lw["ln1_b"], alpha, tm, emit_bf16)


def kernel(x_prompt, x_sample, cache_k, cache_v, state_conv, page_table, rel_bias, w_in, w_attn_out, conv_w, conv_b,
           conv_ln_g, conv_ln_b, w_conv_out, w_out, ln1_g, ln1_b, w_ff1, w_ff2, ln2_g, ln2_b):
    depth = w_in.shape[0]
    n, t, d = x_prompt.shape
    s, ts, _ = x_sample.shape
    n_heads, dh = cache_k.shape[-2:]
    aw = n_heads * dh
    ch = conv_w.shape[-1]
    alpha = (2 * depth) ** 0.25
    scale = dh ** -0.5
    assert ts == 1 and w_in.shape[-1] == 3 * aw + 2 * ch + 2 * d
    col_k, col_v, col_val, col_gate = aw, 2 * aw, 3 * aw, 3 * aw + ch
    col_ga, col_gc = 3 * aw + 2 * ch, 3 * aw + 2 * ch + d

    bias_tiles = _prompt_bias_tiles(rel_bias)
    hp = x_prompt.reshape(n * t, d)
    hs = x_sample.reshape(s, d)
    outs = [[] for _ in range(6)]
    for l in range(depth):
        lw = dict(w_in=w_in[l].astype(BF16), w_attn_out=w_attn_out[l].astype(BF16),
                  w_conv_out=w_conv_out[l].astype(BF16), w_out=w_out[l].astype(BF16),
                  w_ff1=w_ff1[l].astype(BF16), w_ff2=w_ff2[l].astype(BF16),
                  ln1_g=ln1_g[l], ln1_b=ln1_b[l], ln2_g=ln2_g[l], ln2_b=ln2_b[l], col_ga=col_ga, col_gc=col_gc)
        wi = lw["w_in"]

        q, xb = _proj_q(hp, wi, 0, aw, scale, tm=1024, out_dtype=BF16)
        kf, kb, kmean = _proj_kv(xb, wi, col_k, aw, tm=t, block_means=True)
        vf, vb = _proj_kv(xb, wi, col_v, aw, tm=t, block_means=False)
        u = _proj_glu(xb, wi, col_val, col_gate, ch, tm=1024)
        attn = _prompt_attention(q.reshape(n, t, aw), kb.reshape(n, t, aw), vb.reshape(n, t, aw), kmean,
                                 bias_tiles, n_heads)
        u3 = u.reshape(n, t, ch)
        cn = _prompt_conv(u3, conv_w[l], conv_b[l], conv_ln_g[l], conv_ln_b[l])
        outs[0].append(kf.reshape(n, t, n_heads, dh))
        outs[1].append(vf.reshape(n, t, n_heads, dh))
        outs[2].append(u3[:, t - (conv_w.shape[1] - 1):, :])

        qs, xsb = _proj_q(hs, wi, 0, aw, scale, tm=s, out_dtype=F32)
        ksf, _ = _proj_kv(xsb, wi, col_k, aw, tm=s, block_means=False)
        vsf, _ = _proj_kv(xsb, wi, col_v, aw, tm=s, block_means=False)
        us = _proj_glu(xsb, wi, col_val, col_gate, ch, tm=s)
        dec = (qs.reshape(s, n_heads, dh), ksf.reshape(s, n_heads, dh), vsf.reshape(s, n_heads, dh),
               cache_k[l], cache_v[l], page_table, rel_bias)

        mlp_w = (lw["w_ff1"], lw["w_ff2"], lw["ln2_g"], lw["ln2_b"], alpha)
        pps = _fused_pages_per_step(n * t, w_ff1.shape[-1], s, page_table.shape[1])
        x1 = _merge_out(hp, xb, attn.reshape(n * t, aw), cn.reshape(n * t, ch), lw, alpha, 512, emit_bf16=pps is None)
        if pps is None:
            hp_next = _mlp_ln(x1[0], x1[1], *mlp_w, tm=512)
            attn_s = _decode_attention(*dec)
        else:
            hp_next, attn_s = _mlp_ln_decode(x1[0], *mlp_w, *dec, pps)

        cn_s, new_state = _decode_conv(state_conv[l], us.reshape(s, 1, ch), conv_w[l], conv_b[l], conv_ln_g[l],
                                       conv_ln_b[l])
        x1s = _merge_out(hs, xsb, attn_s.reshape(s, aw), cn_s.reshape(s, ch).astype(BF16), lw, alpha, s, emit_bf16=True)
        hs_next = _mlp_ln(x1s[0], x1s[1], *mlp_w, tm=s)
        outs[3].append(ksf.reshape(s, 1, n_heads, dh))
        outs[4].append(vsf.reshape(s, 1, n_heads, dh))
        outs[5].append(new_state)
        hp, hs = hp_next, hs_next

    return (hp.reshape(n, t, d), hs.reshape(s, 1, d)) + tuple(jnp.stack(o) for o in outs)
```

```python
import functools
import math

import numpy as np
import jax
import jax.numpy as jnp
from jax import lax
from jax.experimental import pallas as pl
from jax.experimental.pallas import tpu as pltpu

MOBA_BLOCK = 256
MOBA_TOPK = 3
MAX_DISTANCE = 128
LN_EPS = 1e-5
NEG_INF = -1e30

V7X_VMEM_BYTES = 64 * 1024 * 1024
V7X_LANES = 128
V7X_SUBLANES = 8

F32 = jnp.float32
BF16 = jnp.bfloat16


def _vmem_limit(block_bytes):
    return int(min(max(2 * block_bytes + (16 << 20), 32 << 20), V7X_VMEM_BYTES - (4 << 20)))


def _params(semantics, block_bytes):
    return pltpu.CompilerParams(dimension_semantics=semantics, vmem_limit_bytes=_vmem_limit(block_bytes))


def _tile(n, want):
    t = min(n, want)
    while n % t:
        t -= 1
    return t


def _sigmoid(x):
    return 1.0 / (1.0 + jnp.exp(-x))


def _layer_norm(y, g, b):
    mu = jnp.mean(y, axis=-1, keepdims=True)
    d = y - mu
    var = jnp.mean(d * d, axis=-1, keepdims=True)
    return d * lax.rsqrt(var + LN_EPS) * g + b


def _dot(a, b):
    return jnp.dot(a, b, preferred_element_type=F32)


def _dot_nt(a, b):
    return lax.dot_general(a, b, (((1,), (1,)), ((), ())), preferred_element_type=F32)


def _proj_qkv_kernel(x_ref, w_ref, xb_ref, q_ref, kf_ref, kb_ref, vf_ref, vb_ref, *, scale):
    aw = q_ref.shape[1]
    xb = x_ref[...].astype(BF16)
    xb_ref[...] = xb
    q_ref[...] = (_dot(xb, w_ref[:, 0:aw]) * scale).astype(q_ref.dtype)
    for c, (f_ref, b_ref) in enumerate(((kf_ref, kb_ref), (vf_ref, vb_ref)), start=1):
        acc = _dot(xb, w_ref[:, c * aw:(c + 1) * aw])
        f_ref[...] = acc
        b_ref[...] = acc.astype(b_ref.dtype)


def _proj_glu_kernel(x_ref, wv_ref, wg_ref, u_ref):
    x = x_ref[...]
    u_ref[...] = _dot(x, wv_ref[...]) * _sigmoid(_dot(x, wg_ref[...]))


def _proj_qkv(x, w, aw, scale, tm, q_dtype):
    m, d = x.shape
    tm = _tile(m, tm)
    row = lambda width: pl.BlockSpec((tm, width), lambda i: (i, 0))
    blocks = tm * d * 6 + tm * aw * 16 + d * 3 * aw
    return pl.pallas_call(
        functools.partial(_proj_qkv_kernel, scale=scale),
        grid=(m // tm,),
        in_specs=[row(d), pl.BlockSpec((d, 3 * aw), lambda i: (0, 0), pipeline_mode=pl.Buffered(1))],
        out_specs=[row(d)] + [row(aw)] * 5,
        out_shape=[jax.ShapeDtypeStruct((m, d), BF16), jax.ShapeDtypeStruct((m, aw), q_dtype)]
        + [jax.ShapeDtypeStruct((m, aw), dt) for dt in (F32, BF16, F32, BF16)],
        compiler_params=_params(("parallel",), blocks),
        name="proj_qkv",
    )(x, w)


def _proj_glu(xb, w, col_val, col_gate, ncols, tm):
    m, d = xb.shape
    tn = _tile(ncols, 512)
    tm = _tile(m, tm)
    blocks = tm * d * 2 + 2 * d * tn * 2 + tm * tn * 4
    return pl.pallas_call(
        _proj_glu_kernel,
        grid=(m // tm, ncols // tn),
        in_specs=[pl.BlockSpec((tm, d), lambda i, j: (i, 0)),
                  pl.BlockSpec((d, tn), lambda i, j: (0, col_val // tn + j)),
                  pl.BlockSpec((d, tn), lambda i, j: (0, col_gate // tn + j))],
        out_specs=pl.BlockSpec((tm, tn), lambda i, j: (i, j)),
        out_shape=jax.ShapeDtypeStruct((m, ncols), F32),
        compiler_params=_params(("parallel", "parallel"), blocks),
        name="proj_glu",
    )(xb, w, w)


def _rel_bucket_np(dist, num_buckets):
    n = np.maximum(dist, 0)
    max_exact = num_buckets // 2
    nf = np.maximum(n, 1).astype(np.float32)
    large = max_exact + (np.log(nf / np.float32(max_exact)) / np.float32(math.log(MAX_DISTANCE / max_exact))
                         * np.float32(num_buckets - max_exact)).astype(np.int32)
    large = np.minimum(large, num_buckets - 1)
    return np.where(n < max_exact, n, large).astype(np.int32)


def _bias_tiles_kernel(rb_ref, bucket_ref, o_ref, *, num_buckets):
    h = pl.program_id(0)
    for t in range(bucket_ref.shape[0]):
        bucket = bucket_ref[t]
        acc = jnp.full(bucket.shape, NEG_INF, F32)
        for b in range(num_buckets):
            acc = jnp.where(bucket == b, rb_ref[b, h], acc)
        o_ref[t] = acc


def _prompt_bias_tiles(rel_bias):
    num_buckets, n_heads = rel_bias.shape
    assert MAX_DISTANCE <= MOBA_BLOCK + 1
    r = np.arange(MOBA_BLOCK)[:, None]
    c = np.arange(MOBA_BLOCK)[None, :]
    diag = np.where(r - c >= 0, _rel_bucket_np(r - c, num_buckets), -1)
    prev = _rel_bucket_np(r - c + MOBA_BLOCK, num_buckets)
    far = _rel_bucket_np(r - c + 2 * MOBA_BLOCK, num_buckets)
    buckets = jnp.asarray(np.stack([diag, prev, far]).astype(np.int32))
    return pl.pallas_call(
        functools.partial(_bias_tiles_kernel, num_buckets=num_buckets),
        grid=(n_heads,),
        in_specs=[pl.BlockSpec(memory_space=pltpu.SMEM),
                  pl.BlockSpec((3, MOBA_BLOCK, MOBA_BLOCK), lambda h: (0, 0, 0))],
        out_specs=pl.BlockSpec((None, 3, MOBA_BLOCK, MOBA_BLOCK), lambda h: (h, 0, 0, 0)),
        out_shape=jax.ShapeDtypeStruct((n_heads, 3, MOBA_BLOCK, MOBA_BLOCK), F32),
        compiler_params=_params(("arbitrary",), 6 * MOBA_BLOCK * MOBA_BLOCK * 4),
        name="bias_tiles",
    )(rel_bias, buckets)


def _block_penalty_t(gate_t, n_valid):
    row = lax.broadcasted_iota(jnp.int32, gate_t.shape, 0)
    valid = row < n_valid
    pen = jnp.zeros(gate_t.shape, F32)
    for j in range(n_valid):
        gj = gate_t[j:j + 1, :]
        beats = ((gate_t > gj) | ((gate_t == gj) & (row < j))) & valid
        rank = jnp.sum(beats.astype(F32), axis=0, keepdims=True)
        pen = jnp.where(row == j, jnp.where(rank < MOBA_TOPK, 0.0, NEG_INF), pen)
    return pen


def _prompt_attn_kernel(q_ref, k_ref, v_ref, bias_ref, o_ref, kx_ref, pen_ref):
    t, dh = q_ref.shape
    blk = MOBA_BLOCK
    nblk = t // blk
    row = lax.broadcasted_iota(jnp.int32, (t, dh), 0)
    col = lax.broadcasted_iota(jnp.int32, (t, dh), 1)
    kx_ref[:, :dh] = k_ref[...]
    kx_ref[:, dh:] = (col == lax.shift_right_logical(row, blk.bit_length() - 1)).astype(BF16)
    in_block = (lax.shift_right_logical(lax.broadcasted_iota(jnp.int32, (nblk, t), 1), blk.bit_length() - 1)
                == lax.broadcasted_iota(jnp.int32, (nblk, t), 0))
    km = _dot(jnp.where(in_block, 1.0 / blk, 0.0).astype(BF16), k_ref[...])
    km_hi = km.astype(BF16)
    km_lo = (km - km_hi.astype(F32)).astype(BF16)
    eye = (lax.broadcasted_iota(jnp.int32, (blk, blk), 0) == lax.broadcasted_iota(jnp.int32, (blk, blk), 1)).astype(BF16)
    pen_ref[...] = jnp.zeros(pen_ref.shape, F32)
    def masked_logits(i):
        qi = q_ref[i * blk:(i + 1) * blk, :]
        nk = (i + 1) * blk
        if i <= MOBA_TOPK:
            return _dot_nt(qi, k_ref[0:nk, :])
        gate_t = _dot_nt(km_hi, qi) + _dot_nt(km_lo, qi)
        pen_ref[0:nblk, :] = _block_penalty_t(gate_t, i)
        pen = _dot_nt(eye, pen_ref[...].astype(BF16)).astype(BF16)
        return _dot_nt(jnp.concatenate([qi, pen], axis=1), kx_ref[0:nk, :])

    s_next = masked_logits(0)
    for i in range(nblk):
        nk = (i + 1) * blk
        s = s_next
        if i + 1 < nblk:
            s_next = masked_logits(i + 1)
        bias = [bias_ref[2]] * (i - 1) + ([bias_ref[1]] if i > 0 else []) + [bias_ref[0]]
        s = s + (jnp.concatenate(bias, axis=1) if len(bias) > 1 else bias[0])
        m = jnp.max(s, axis=1, keepdims=True)
        p = jnp.exp(s - m)
        l = jnp.sum(p, axis=1, keepdims=True)
        o = _dot(p.astype(BF16), v_ref[0:nk, :])
        o_ref[i * blk:(i + 1) * blk, :] = (o / l).astype(o_ref.dtype)


def _prompt_attention(q, k, v, bias_tiles, n_heads):
    n, t, width = q.shape
    dh = width // n_heads
    nblk = t // MOBA_BLOCK
    assert dh == V7X_LANES and t % MOBA_BLOCK == 0 and nblk <= V7X_LANES and MOBA_BLOCK & (MOBA_BLOCK - 1) == 0
    seq_spec = pl.BlockSpec((None, t, dh), lambda h, b: (b, 0, h))
    blocks = 4 * t * dh * 2 + 3 * MOBA_BLOCK * MOBA_BLOCK * 4 + t * 2 * dh * 2 + 6 * MOBA_BLOCK * t * 4
    return pl.pallas_call(
        _prompt_attn_kernel,
        grid=(n_heads, n),
        in_specs=[seq_spec, seq_spec, seq_spec,
                  pl.BlockSpec((None, 3, MOBA_BLOCK, MOBA_BLOCK), lambda h, b: (h, 0, 0, 0))],
        out_specs=seq_spec,
        out_shape=jax.ShapeDtypeStruct((n, t, width), BF16),
        scratch_shapes=[pltpu.VMEM((t, 2 * dh), BF16), pltpu.VMEM((dh, MOBA_BLOCK), F32)],
        compiler_params=_params(("parallel", "parallel"), blocks),
        name="prompt_attn",
    )(q, k, v, bias_tiles)


DECODE_ROWS = 32


def _lane_sums(xs):
    dh = xs[0].shape[-1]
    r = lax.broadcasted_iota(jnp.int32, (2 * dh, 2 * dh), 0)
    c = lax.broadcasted_iota(jnp.int32, (2 * dh, 2 * dh), 1)
    pair_ones = ((r < dh) == (c < dh)).astype(BF16)
    outs = []
    for i in range(0, len(xs) - 1, 2):
        y = _dot(jnp.concatenate([xs[i].astype(BF16), xs[i + 1].astype(BF16)], axis=1), pair_ones)
        outs += [y[:, :dh], y[:, dh:]]
    if len(xs) % 2:
        outs.append(_dot(xs[-1].astype(BF16), jnp.ones((dh, dh), BF16)))
    return outs


def _page_bias(relb_ref, page_buckets):
    if len(set(page_buckets)) == 1:
        return relb_ref[page_buckets[0]][None]
    return jnp.stack([relb_ref[b] for b in page_buckets])


def _decode_pages(part, q_ref, kn_ref, vn_ref, relb_ref, k_refs, v_refs, o_ref, m_ref, l_ref, g_ref, acc_ref, *,
                  n_pages, pages_per_block, buckets):
    pps = len(k_refs)
    parts = n_pages // pps
    page, h, dh = k_refs[0].shape
    q = q_ref[...]
    rows = min(DECODE_ROWS, page)
    for p in range(pps):
        page_buckets = [buckets[(a * pps + p) * page:(a * pps + p + 1) * page] for a in range(parts)]
        shared = set(b for pb in page_buckets for b in pb)
        shared = shared.pop() if len(shared) == 1 else None
        m = jnp.full((h, dh), NEG_INF, F32)
        l = acc = gsum = jnp.zeros((h, dh), F32)
        for r0 in range(0, page, 2 * rows):
            groups = [slice(r, r + rows) for r in range(r0, min(r0 + 2 * rows, page), rows)]
            sums = _lane_sums([(k_refs[p][g] * q[None]).reshape(rows * h, dh) for g in groups])
            for g, s in zip(groups, sums):
                s = s.reshape(rows, h, dh)
                gsum = gsum + jnp.sum(s, axis=0)
                if shared is None:
                    bias = _page_bias(relb_ref, page_buckets[0][g])
                    for a in range(1, parts):
                        if page_buckets[a][g] != page_buckets[0][g]:
                            bias = jnp.where(part == a, _page_bias(relb_ref, page_buckets[a][g]), bias)
                    s = s + bias
                m_new = jnp.maximum(m, jnp.max(s, axis=0))
                scale = jnp.exp(m - m_new)
                e = jnp.exp(s - m_new[None])
                l = l * scale + jnp.sum(e, axis=0)
                acc = acc * scale + jnp.sum(e * v_refs[p][g], axis=0)
                m = m_new
        idx = part * pps + p
        m_ref[idx] = m if shared is None else m + relb_ref[shared]
        l_ref[idx] = l
        g_ref[idx] = gsum
        acc_ref[idx] = acc

    def merge():
        n_blocks = n_pages // pages_per_block
        gate = [sum(g_ref[p] for p in range(j * pages_per_block, (j + 1) * pages_per_block)) for j in range(n_blocks)]
        s_own = _lane_sums([q * kn_ref[...]])[0] + relb_ref[0]
        m_tot = s_own
        sel = []
        for j in range(n_blocks):
            rank = jnp.zeros((h, dh), F32)
            for j2 in range(n_blocks):
                if j2 != j:
                    beats = (gate[j2] > gate[j]) | ((gate[j2] == gate[j]) & (j2 < j))
                    rank = rank + beats.astype(F32)
            sel.append(rank < MOBA_TOPK)
            for p in range(j * pages_per_block, (j + 1) * pages_per_block):
                m_tot = jnp.maximum(m_tot, jnp.where(sel[j], m_ref[p], NEG_INF))
        w_own = jnp.exp(s_own - m_tot)
        num = w_own * vn_ref[...]
        den = w_own
        for j in range(n_blocks):
            for p in range(j * pages_per_block, (j + 1) * pages_per_block):
                w = jnp.where(sel[j], jnp.exp(m_ref[p] - m_tot), 0.0)
                num = num + w * acc_ref[p]
                den = den + w * l_ref[p]
        o_ref[...] = (num / den).astype(o_ref.dtype)

    if parts == 1:
        merge()
    else:
        pl.when(part == parts - 1)(merge)


def _decode_attn_kernel(pt_ref, q_ref, kn_ref, vn_ref, relb_ref, *refs, n_pages, pages_per_block, buckets):
    del pt_ref
    _decode_pages(0, q_ref, kn_ref, vn_ref, relb_ref, refs[:n_pages], refs[n_pages:2 * n_pages], *refs[2 * n_pages:],
                  n_pages=n_pages, pages_per_block=pages_per_block, buckets=buckets)


def _decode_specs(page_table, cache_k, rel_bias, h, dh, pages_per_step, seq_of, part_of):
    page = cache_k.shape[1]
    n_pages = page_table.shape[1]
    past = n_pages * page
    num_buckets = rel_bias.shape[0]
    assert dh == V7X_LANES and MOBA_BLOCK % page == 0 and past % MOBA_BLOCK == 0 and n_pages % pages_per_step == 0
    buckets = tuple(int(b) for b in _rel_bucket_np(past - np.arange(past), num_buckets))
    relb = jnp.broadcast_to(rel_bias[:, :, None], (num_buckets, h, dh))
    tok_spec = pl.BlockSpec((None, h, dh), lambda *g: (seq_of(*g[:-1]), 0, 0))
    relb_spec = pl.BlockSpec((num_buckets, h, dh), lambda *g: (0, 0, 0))
    page_specs = [pl.BlockSpec((None, page, h, dh),
                               lambda *g, p=p: (g[-1][seq_of(*g[:-1]), part_of(*g[:-1]) * pages_per_step + p], 0, 0, 0))
                  for p in range(pages_per_step)]
    scratch = [pltpu.VMEM((n_pages, h, dh), F32)] * 4
    return buckets, relb, tok_spec, relb_spec, page_specs, scratch


def _decode_attention(q, k_new, v_new, cache_k, cache_v, page_table, rel_bias):
    s, h, dh = q.shape
    page = cache_k.shape[1]
    n_pages = page_table.shape[1]
    buckets, relb, tok_spec, relb_spec, page_specs, scratch = _decode_specs(
        page_table, cache_k, rel_bias, h, dh, n_pages, seq_of=lambda b: b, part_of=lambda b: 0)
    blocks = 2 * n_pages * page * h * dh * 4 + relb.size * 4
    kernel = functools.partial(_decode_attn_kernel, n_pages=n_pages, pages_per_block=MOBA_BLOCK // page,
                               buckets=buckets)
    return pl.pallas_call(
        kernel,
        grid_spec=pltpu.PrefetchScalarGridSpec(
            num_scalar_prefetch=1,
            grid=(s,),
            in_specs=[tok_spec, tok_spec, tok_spec, relb_spec] + page_specs + page_specs,
            out_specs=tok_spec,
            scratch_shapes=scratch),
        out_shape=jax.ShapeDtypeStruct((s, h, dh), BF16),
        compiler_params=_params(("parallel",), blocks),
        name="decode_attn",
    )(page_table, q, k_new, v_new, relb, *([cache_k] * n_pages), *([cache_v] * n_pages))


CONV_HALO = 32
CONV_ROWS = 64


def _prompt_conv_kernel(halo_ref, u_ref, w_ref, cb_ref, g_ref, b_ref, o_ref, ext_ref, conv_ref):
    tt, ch = u_ref.shape
    kw = w_ref.shape[0]
    first = CONV_HALO - (kw - 1)

    @pl.when(pl.program_id(1) == 0)
    def _():
        ext_ref[0:CONV_HALO, :] = jnp.zeros((CONV_HALO, ch), F32)

    @pl.when(pl.program_id(1) > 0)
    def _():
        ext_ref[0:CONV_HALO, :] = halo_ref[...]

    ext_ref[CONV_HALO:CONV_HALO + tt, :] = u_ref[...]
    rows = min(CONV_ROWS, tt)
    span = rows + CONV_HALO
    for c0 in range(0, ch, V7X_LANES):
        cs = slice(c0, c0 + V7X_LANES)
        for r0 in range(0, tt, rows):
            x = ext_ref[r0:r0 + span, cs]
            acc = jnp.zeros((rows, V7X_LANES), F32)
            for b in range(V7X_SUBLANES):
                taps = [s for s in range(first, first + kw) if s % V7X_SUBLANES == b]
                assert all(s + rows <= span for s in taps)
                xb = x if b == 0 else pltpu.roll(x, span - b, axis=0)
                for s in taps:
                    acc = acc + xb[s - b:s - b + rows] * w_ref[s - first:s - first + 1, cs]
            conv_ref[r0:r0 + rows, cs] = acc
    y = _layer_norm(conv_ref[...] + cb_ref[...], g_ref[...], b_ref[...])
    o_ref[...] = (y * _sigmoid(y)).astype(o_ref.dtype)


def _prompt_conv(u, conv_w, conv_b, ln_g, ln_b):
    n, t, ch = u.shape
    kw = conv_w.shape[0]
    tt = _tile(t, 128)
    assert kw - 1 <= CONV_HALO and tt % CONV_HALO == 0 and ch % V7X_LANES == 0
    vec = pl.BlockSpec((1, ch), lambda b, i: (0, 0))
    blocks = (2 * tt + 2 * CONV_HALO) * ch * 4 + tt * ch * 2
    return pl.pallas_call(
        _prompt_conv_kernel,
        grid=(n, t // tt),
        in_specs=[pl.BlockSpec((None, CONV_HALO, ch), lambda b, i: (b, jnp.maximum(i * (tt // CONV_HALO) - 1, 0), 0)),
                  pl.BlockSpec((None, tt, ch), lambda b, i: (b, i, 0)),
                  pl.BlockSpec((kw, ch), lambda b, i: (0, 0)), vec, vec, vec],
        out_specs=pl.BlockSpec((None, tt, ch), lambda b, i: (b, i, 0)),
        out_shape=jax.ShapeDtypeStruct((n, t, ch), BF16),
        scratch_shapes=[pltpu.VMEM((CONV_HALO + tt, ch), F32), pltpu.VMEM((tt, ch), F32)],
        compiler_params=_params(("parallel", "arbitrary"), blocks),
        name="prompt_conv",
    )(u, u, conv_w, conv_b.reshape(1, ch), ln_g.reshape(1, ch), ln_b.reshape(1, ch))


def _decode_conv_kernel(state_ref, u_ref, w_ref, cb_ref, g_ref, b_ref, o_ref, new_state_ref):
    kw = w_ref.shape[0]
    w_hist = w_ref[0:kw - 1, :]
    w_last = w_ref[kw - 1:kw, :]
    for b in range(state_ref.shape[0]):
        u = u_ref[b]
        conv = jnp.sum(state_ref[b] * w_hist, axis=0, keepdims=True) + u * w_last
        y = _layer_norm(conv + cb_ref[...], g_ref[...], b_ref[...])
        o_ref[b] = (y * _sigmoid(y)).astype(o_ref.dtype)
        new_state_ref[b, 0:kw - 2, :] = state_ref[b, 1:kw - 1, :]
        new_state_ref[b, kw - 2:kw - 1, :] = u


def _decode_conv(state, u, conv_w, conv_b, ln_g, ln_b):
    s, hist, ch = state.shape
    kw = conv_w.shape[0]
    ts = _tile(s, 16)
    vec = pl.BlockSpec((1, ch), lambda i: (0, 0))
    tok = pl.BlockSpec((ts, 1, ch), lambda i: (i, 0, 0))
    hist_spec = pl.BlockSpec((ts, hist, ch), lambda i: (i, 0, 0))
    blocks = 2 * ts * 32 * ch * 4 + 2 * ts * V7X_SUBLANES * ch * 4
    return pl.pallas_call(
        _decode_conv_kernel,
        grid=(s // ts,),
        in_specs=[hist_spec, tok, pl.BlockSpec((kw, ch), lambda i: (0, 0)), vec, vec, vec],
        out_specs=[tok, hist_spec],
        out_shape=[jax.ShapeDtypeStruct((s, 1, ch), F32), jax.ShapeDtypeStruct((s, hist, ch), F32)],
        compiler_params=_params(("parallel",), blocks),
        name="decode_conv",
    )(state, u, conv_w, conv_b.reshape(1, ch), ln_g.reshape(1, ch), ln_b.reshape(1, ch))


def _merge_kernel(x_ref, a_ref, c_ref, wga_ref, wgc_ref, wao_ref, wco_ref, o_ref):
    x = x_ref[...]
    mixed = (_sigmoid(_dot(x, wga_ref[...])) * _dot(a_ref[...], wao_ref[...])
             + _sigmoid(_dot(x, wgc_ref[...])) * _dot(c_ref[...], wco_ref[...]))
    o_ref[...] = mixed.astype(o_ref.dtype)


def _merge(xb, attn, cn, w_in, col_ga, col_gc, w_ao, w_co, tm):
    m, d = xb.shape
    wa = attn.shape[1]
    wc = cn.shape[1]
    tn = _tile(d, 512)
    tm = _tile(m, tm)
    blocks = tm * (d + wa + wc + tn) * 2 + (2 * d + wa + wc) * tn * 2
    return pl.pallas_call(
        _merge_kernel,
        grid=(m // tm, d // tn),
        in_specs=[pl.BlockSpec((tm, d), lambda i, j: (i, 0)),
                  pl.BlockSpec((tm, wa), lambda i, j: (i, 0)),
                  pl.BlockSpec((tm, wc), lambda i, j: (i, 0)),
                  pl.BlockSpec((d, tn), lambda i, j: (0, col_ga // tn + j)),
                  pl.BlockSpec((d, tn), lambda i, j: (0, col_gc // tn + j)),
                  pl.BlockSpec((wa, tn), lambda i, j: (0, j)),
                  pl.BlockSpec((wc, tn), lambda i, j: (0, j))],
        out_specs=pl.BlockSpec((tm, tn), lambda i, j: (i, j)),
        out_shape=jax.ShapeDtypeStruct((m, d), BF16),
        compiler_params=_params(("parallel", "parallel"), blocks),
        name="merge",
    )(xb, attn, cn, w_in, w_in, w_ao, w_co)


def _out_ln_kernel(x_ref, mixed_ref, w_ref, g_ref, b_ref, of_ref, *maybe_ob_ref, alpha):
    y = _layer_norm(alpha * x_ref[...] + _dot(mixed_ref[...], w_ref[...]), g_ref[...], b_ref[...])
    of_ref[...] = y
    for ob_ref in maybe_ob_ref:
        ob_ref[...] = y.astype(ob_ref.dtype)


def _out_ln(x, mixed, w_out, g, b, alpha, tm, emit_bf16):
    m, d = x.shape
    tm = _tile(m, tm)
    vec = pl.BlockSpec((1, d), lambda i: (0, 0))
    row = pl.BlockSpec((tm, d), lambda i: (i, 0))
    blocks = tm * d * (4 + 2 + 4 + 2) + d * d * 2
    return pl.pallas_call(
        functools.partial(_out_ln_kernel, alpha=alpha),
        grid=(m // tm,),
        in_specs=[row, row, pl.BlockSpec((d, d), lambda i: (0, 0)), vec, vec],
        out_specs=[row, row] if emit_bf16 else [row],
        out_shape=[jax.ShapeDtypeStruct((m, d), F32)] + ([jax.ShapeDtypeStruct((m, d), BF16)] if emit_bf16 else []),
        compiler_params=_params(("parallel",), blocks),
        name="out_ln",
    )(x, mixed, w_out, g.reshape(1, d), b.reshape(1, d))


def _mlp_ln_kernel(xf_ref, xb_ref, w1_ref, w2_ref, g_ref, b_ref, o_ref, *, alpha):
    f = pl.program_id(1)

    @pl.when(f == 0)
    def _():
        o_ref[...] = alpha * xf_ref[...]

    hid = jnp.maximum(_dot(xb_ref[...], w1_ref[...]), 0.0)
    o_ref[...] += _dot((hid * hid).astype(BF16), w2_ref[...])

    @pl.when(f == pl.num_programs(1) - 1)
    def _():
        o_ref[...] = _layer_norm(o_ref[...], g_ref[...], b_ref[...])


def _mlp_ln(xf, xb, w1, w2, g, b, alpha, tm):
    m, d = xf.shape
    dff = w1.shape[1]
    tf = _tile(dff, 1024)
    tm = _tile(m, tm)
    vec = pl.BlockSpec((1, d), lambda i, f: (0, 0))
    row = pl.BlockSpec((tm, d), lambda i, f: (i, 0))
    blocks = tm * d * (4 + 2 + 4) + 2 * d * tf * 2 + tm * tf * 4
    return pl.pallas_call(
        functools.partial(_mlp_ln_kernel, alpha=alpha),
        grid=(m // tm, dff // tf),
        in_specs=[row, row, pl.BlockSpec((d, tf), lambda i, f: (0, f)), pl.BlockSpec((tf, d), lambda i, f: (f, 0)),
                  vec, vec],
        out_specs=row,
        out_shape=jax.ShapeDtypeStruct((m, d), F32),
        compiler_params=_params(("parallel", "arbitrary"), blocks),
        name="mlp_ln",
    )(xf, xb, w1, w2, g.reshape(1, d), b.reshape(1, d))


def _mlp_ln_decode_kernel(pt_ref, xf_ref, w1_ref, w2_ref, g_ref, b_ref, q_ref, kn_ref, vn_ref, relb_ref, *refs,
                          alpha, pages_per_step, n_pages, pages_per_block, buckets):
    del pt_ref
    k_refs, v_refs = refs[:pages_per_step], refs[pages_per_step:2 * pages_per_step]
    o_ref, ao_ref, xb_ref, m_ref, l_ref, gs_ref, acc_ref = refs[2 * pages_per_step:]
    f = pl.program_id(1)
    nf = pl.num_programs(1)

    @pl.when(f == 0)
    def _():
        x = xf_ref[...]
        o_ref[...] = alpha * x
        xb_ref[...] = x.astype(BF16)

    hid = jnp.maximum(_dot(xb_ref[...], w1_ref[...]), 0.0)
    o_ref[...] += _dot((hid * hid).astype(BF16), w2_ref[...])
    part = lax.rem(pl.program_id(0) * nf + f, n_pages // pages_per_step)
    _decode_pages(part, q_ref, kn_ref, vn_ref, relb_ref, k_refs, v_refs, ao_ref, m_ref, l_ref, gs_ref, acc_ref,
                  n_pages=n_pages, pages_per_block=pages_per_block, buckets=buckets)

    @pl.when(f == nf - 1)
    def _():
        o_ref[...] = _layer_norm(o_ref[...], g_ref[...], b_ref[...])


FUSED_MLP_ROWS = 512
FUSED_MLP_COLS = 1024


def _fused_pages_per_step(m, dff, s, n_pages):
    steps = (m // _tile(m, FUSED_MLP_ROWS)) * (dff // _tile(dff, FUSED_MLP_COLS))
    if (s * n_pages) % steps or n_pages % ((s * n_pages) // steps):
        return None
    return (s * n_pages) // steps


def _mlp_ln_decode(xf, w1, w2, g, b, alpha, q, k_new, v_new, cache_k, cache_v, page_table, rel_bias, pps):
    m, d = xf.shape
    dff = w1.shape[1]
    tm, tf = _tile(m, FUSED_MLP_ROWS), _tile(dff, FUSED_MLP_COLS)
    nf = dff // tf
    s, h, dh = q.shape
    page = cache_k.shape[1]
    n_pages = page_table.shape[1]
    parts = n_pages // pps
    assert (m // tm) * nf == s * parts
    buckets, relb, tok_spec, relb_spec, page_specs, scratch = _decode_specs(
        page_table, cache_k, rel_bias, h, dh, pps,
        seq_of=lambda i, f: (i * nf + f) // parts, part_of=lambda i, f: (i * nf + f) % parts)
    vec = pl.BlockSpec((1, d), lambda i, f, pt: (0, 0))
    row = pl.BlockSpec((tm, d), lambda i, f, pt: (i, 0))
    blocks = tm * d * 9 + 2 * d * tf * 2 + tm * tf * 4 + 2 * pps * page * h * dh * 4 + relb.size * 4
    kernel = functools.partial(_mlp_ln_decode_kernel, alpha=alpha, pages_per_step=pps, n_pages=n_pages,
                               pages_per_block=MOBA_BLOCK // page, buckets=buckets)
    return pl.pallas_call(
        kernel,
        grid_spec=pltpu.PrefetchScalarGridSpec(
            num_scalar_prefetch=1,
            grid=(m // tm, nf),
            in_specs=[row, pl.BlockSpec((d, tf), lambda i, f, pt: (0, f)), pl.BlockSpec((tf, d), lambda i, f, pt: (f, 0)),
                      vec, vec, tok_spec, tok_spec, tok_spec, relb_spec] + page_specs + page_specs,
            out_specs=[row, tok_spec],
            scratch_shapes=[pltpu.VMEM((tm, d), BF16)] + scratch),
        out_shape=[jax.ShapeDtypeStruct((m, d), F32), jax.ShapeDtypeStruct((s, h, dh), BF16)],
        compiler_params=_params(("arbitrary", "arbitrary"), blocks),
        name="mlp_ln_decode_attn",
    )(page_table, xf, w1, w2, g.reshape(1, d), b.reshape(1, d), q, k_new, v_new, relb,
      *([cache_k] * pps), *([cache_v] * pps))


def _merge_out(x2d, xb, attn, cn, lw, alpha, tm, emit_bf16):
    mixed = _merge(xb, attn, cn, lw["w_in"], lw["col_ga"], lw["col_gc"], lw["w_attn_out"], lw["w_conv_out"], tm=2 * tm)
    return _out_ln(x2d, mixed, lw["w_out"], lw["ln1_g"], lw["ln1_b"], alpha, tm, emit_bf16)


def kernel(x_prompt, x_sample, cache_k, cache_v, state_conv, page_table, rel_bias, w_in, w_attn_out, conv_w, conv_b,
           conv_ln_g, conv_ln_b, w_conv_out, w_out, ln1_g, ln1_b, w_ff1, w_ff2, ln2_g, ln2_b):
    depth = w_in.shape[0]
    n, t, d = x_prompt.shape
    s, ts, _ = x_sample.shape
    n_heads, dh = cache_k.shape[-2:]
    aw = n_heads * dh
    ch = conv_w.shape[-1]
    alpha = (2 * depth) ** 0.25
    scale = dh ** -0.5
    assert ts == 1 and w_in.shape[-1] == 3 * aw + 2 * ch + 2 * d
    col_k, col_v, col_val, col_gate = aw, 2 * aw, 3 * aw, 3 * aw + ch
    col_ga, col_gc = 3 * aw + 2 * ch, 3 * aw + 2 * ch + d

    bias_tiles = _prompt_bias_tiles(rel_bias)
    hp = x_prompt.reshape(n * t, d)
    hs = x_sample.reshape(s, d)
    outs = [[] for _ in range(6)]
    for l in range(depth):
        lw = dict(w_in=w_in[l].astype(BF16), w_attn_out=w_attn_out[l].astype(BF16),
                  w_conv_out=w_conv_out[l].astype(BF16), w_out=w_out[l].astype(BF16),
                  w_ff1=w_ff1[l].astype(BF16), w_ff2=w_ff2[l].astype(BF16),
                  ln1_g=ln1_g[l], ln1_b=ln1_b[l], ln2_g=ln2_g[l], ln2_b=ln2_b[l], col_ga=col_ga, col_gc=col_gc)
        wi = lw["w_in"]

        xb, q, kf, kb, vf, vb = _proj_qkv(hp, wi, aw, scale, tm=512, q_dtype=BF16)
        u = _proj_glu(xb, wi, col_val, col_gate, ch, tm=1024)
        attn = _prompt_attention(q.reshape(n, t, aw), kb.reshape(n, t, aw), vb.reshape(n, t, aw), bias_tiles, n_heads)
        u3 = u.reshape(n, t, ch)
        cn = _prompt_conv(u3, conv_w[l], conv_b[l], conv_ln_g[l], conv_ln_b[l])
        outs[0].append(kf.reshape(n, t, n_heads, dh))
        outs[1].append(vf.reshape(n, t, n_heads, dh))
        outs[2].append(u3[:, t - (conv_w.shape[1] - 1):, :])

        xsb, qs, ksf, _, vsf, _ = _proj_qkv(hs, wi, aw, scale, tm=s, q_dtype=F32)
        us = _proj_glu(xsb, wi, col_val, col_gate, ch, tm=s)
        dec = (qs.reshape(s, n_heads, dh), ksf.reshape(s, n_heads, dh), vsf.reshape(s, n_heads, dh),
               cache_k[l], cache_v[l], page_table, rel_bias)

        mlp_w = (lw["w_ff1"], lw["w_ff2"], lw["ln2_g"], lw["ln2_b"], alpha)
        pps = _fused_pages_per_step(n * t, w_ff1.shape[-1], s, page_table.shape[1])
        x1 = _merge_out(hp, xb, attn.reshape(n * t, aw), cn.reshape(n * t, ch), lw, alpha, 512, emit_bf16=pps is None)
        if pps is None:
            hp_next = _mlp_ln(x1[0], x1[1], *mlp_w, tm=512)
            attn_s = _decode_attention(*dec)
        else:
            hp_next, attn_s = _mlp_ln_decode(x1[0], *mlp_w, *dec, pps)

        cn_s, new_state = _decode_conv(state_conv[l], us.reshape(s, 1, ch), conv_w[l], conv_b[l], conv_ln_g[l],
                                       conv_ln_b[l])
        x1s = _merge_out(hs, xsb, attn_s.reshape(s, aw), cn_s.reshape(s, ch).astype(BF16), lw, alpha, s, emit_bf16=True)
        hs_next = _mlp_ln(x1s[0], x1s[1], *mlp_w, tm=s)
        outs[3].append(ksf.reshape(s, 1, n_heads, dh))
        outs[4].append(vsf.reshape(s, 1, n_heads, dh))
        outs[5].append(new_state)
        hp, hs = hp_next, hs_next

    return (hp.reshape(n, t, d), hs.reshape(s, 1, d)) + tuple(jnp.stack(o) for o in outs)
```

```python
import functools
import math

import numpy as np
import jax
import jax.numpy as jnp
from jax import lax
from jax.experimental import pallas as pl
from jax.experimental.pallas import tpu as pltpu

MOBA_BLOCK = 256
MOBA_TOPK = 3
MAX_DISTANCE = 128
LN_EPS = 1e-5
NEG_INF = -1e30

V7X_VMEM_BYTES = 64 * 1024 * 1024
V7X_LANES = 128
V7X_SUBLANES = 8

F32 = jnp.float32
BF16 = jnp.bfloat16


def _vmem_limit(block_bytes):
    return int(min(max(2 * block_bytes + (16 << 20), 32 << 20), V7X_VMEM_BYTES - (4 << 20)))


def _params(semantics, block_bytes):
    return pltpu.CompilerParams(dimension_semantics=semantics, vmem_limit_bytes=_vmem_limit(block_bytes))


def _tile(n, want):
    t = min(n, want)
    while n % t:
        t -= 1
    return t


def _sigmoid(x):
    return 1.0 / (1.0 + jnp.exp(-x))


def _layer_norm(y, g, b):
    mu = jnp.mean(y, axis=-1, keepdims=True)
    d = y - mu
    var = jnp.mean(d * d, axis=-1, keepdims=True)
    return d * lax.rsqrt(var + LN_EPS) * g + b


def _dot(a, b):
    return jnp.dot(a, b, preferred_element_type=F32)


def _dot_nt(a, b):
    return lax.dot_general(a, b, (((1,), (1,)), ((), ())), preferred_element_type=F32)


def _proj_qkv_kernel(x_ref, w_ref, xb_ref, q_ref, kf_ref, kb_ref, vf_ref, vb_ref, *, scale):
    aw = q_ref.shape[1]
    xb = x_ref[...].astype(BF16)
    xb_ref[...] = xb
    q_ref[...] = (_dot(xb, w_ref[:, 0:aw]) * scale).astype(q_ref.dtype)
    for c, (f_ref, b_ref) in enumerate(((kf_ref, kb_ref), (vf_ref, vb_ref)), start=1):
        acc = _dot(xb, w_ref[:, c * aw:(c + 1) * aw])
        f_ref[...] = acc
        b_ref[...] = acc.astype(b_ref.dtype)


def _proj_glu_kernel(x_ref, wv_ref, wg_ref, u_ref):
    x = x_ref[...]
    u_ref[...] = _dot(x, wv_ref[...]) * _sigmoid(_dot(x, wg_ref[...]))


def _proj_qkv(x, w, aw, scale, tm, q_dtype):
    m, d = x.shape
    tm = _tile(m, tm)
    row = lambda width: pl.BlockSpec((tm, width), lambda i: (i, 0))
    blocks = tm * d * 6 + tm * aw * 16 + d * 3 * aw
    return pl.pallas_call(
        functools.partial(_proj_qkv_kernel, scale=scale),
        grid=(m // tm,),
        in_specs=[row(d), pl.BlockSpec((d, 3 * aw), lambda i: (0, 0), pipeline_mode=pl.Buffered(1))],
        out_specs=[row(d)] + [row(aw)] * 5,
        out_shape=[jax.ShapeDtypeStruct((m, d), BF16), jax.ShapeDtypeStruct((m, aw), q_dtype)]
        + [jax.ShapeDtypeStruct((m, aw), dt) for dt in (F32, BF16, F32, BF16)],
        compiler_params=_params(("parallel",), blocks),
        name="proj_qkv",
    )(x, w)


def _proj_glu(xb, w, col_val, col_gate, ncols, tm):
    m, d = xb.shape
    tn = _tile(ncols, 512)
    tm = _tile(m, tm)
    blocks = tm * d * 2 + 2 * d * tn * 2 + tm * tn * 4
    return pl.pallas_call(
        _proj_glu_kernel,
        grid=(m // tm, ncols // tn),
        in_specs=[pl.BlockSpec((tm, d), lambda i, j: (i, 0)),
                  pl.BlockSpec((d, tn), lambda i, j: (0, col_val // tn + j)),
                  pl.BlockSpec((d, tn), lambda i, j: (0, col_gate // tn + j))],
        out_specs=pl.BlockSpec((tm, tn), lambda i, j: (i, j)),
        out_shape=jax.ShapeDtypeStruct((m, ncols), F32),
        compiler_params=_params(("parallel", "parallel"), blocks),
        name="proj_glu",
    )(xb, w, w)


def _rel_bucket_np(dist, num_buckets):
    n = np.maximum(dist, 0)
    max_exact = num_buckets // 2
    nf = np.maximum(n, 1).astype(np.float32)
    large = max_exact + (np.log(nf / np.float32(max_exact)) / np.float32(math.log(MAX_DISTANCE / max_exact))
                         * np.float32(num_buckets - max_exact)).astype(np.int32)
    large = np.minimum(large, num_buckets - 1)
    return np.where(n < max_exact, n, large).astype(np.int32)


def _bias_tiles_kernel(rb_ref, bucket_ref, o_ref, *, num_buckets):
    h = pl.program_id(0)
    for t in range(bucket_ref.shape[0]):
        bucket = bucket_ref[t]
        acc = jnp.full(bucket.shape, NEG_INF, F32)
        for b in range(num_buckets):
            acc = jnp.where(bucket == b, rb_ref[b, h], acc)
        o_ref[t] = acc


def _prompt_bias_tiles(rel_bias):
    num_buckets, n_heads = rel_bias.shape
    assert MAX_DISTANCE <= MOBA_BLOCK + 1
    r = np.arange(MOBA_BLOCK)[:, None]
    c = np.arange(MOBA_BLOCK)[None, :]
    diag = np.where(r - c >= 0, _rel_bucket_np(r - c, num_buckets), -1)
    prev = _rel_bucket_np(r - c + MOBA_BLOCK, num_buckets)
    far = _rel_bucket_np(r - c + 2 * MOBA_BLOCK, num_buckets)
    buckets = jnp.asarray(np.stack([diag, prev, far]).astype(np.int32))
    return pl.pallas_call(
        functools.partial(_bias_tiles_kernel, num_buckets=num_buckets),
        grid=(n_heads,),
        in_specs=[pl.BlockSpec(memory_space=pltpu.SMEM),
                  pl.BlockSpec((3, MOBA_BLOCK, MOBA_BLOCK), lambda h: (0, 0, 0))],
        out_specs=pl.BlockSpec((None, 3, MOBA_BLOCK, MOBA_BLOCK), lambda h: (h, 0, 0, 0)),
        out_shape=jax.ShapeDtypeStruct((n_heads, 3, MOBA_BLOCK, MOBA_BLOCK), F32),
        compiler_params=_params(("arbitrary",), 6 * MOBA_BLOCK * MOBA_BLOCK * 4),
        name="bias_tiles",
    )(rel_bias, buckets)


def _block_penalty_t(gate_t, n_valid):
    row = lax.broadcasted_iota(jnp.int32, gate_t.shape, 0)
    valid = row < n_valid
    pen = jnp.zeros(gate_t.shape, F32)
    for j in range(n_valid):
        gj = gate_t[j:j + 1, :]
        beats = ((gate_t > gj) | ((gate_t == gj) & (row < j))) & valid
        rank = jnp.sum(beats.astype(F32), axis=0, keepdims=True)
        pen = jnp.where(row == j, jnp.where(rank < MOBA_TOPK, 0.0, NEG_INF), pen)
    return pen


def _prompt_attn_kernel(q_ref, k_ref, v_ref, bias_ref, o_ref, kx_ref, pen_ref):
    t, dh = q_ref.shape
    blk = MOBA_BLOCK
    nblk = t // blk
    row = lax.broadcasted_iota(jnp.int32, (t, dh), 0)
    col = lax.broadcasted_iota(jnp.int32, (t, dh), 1)
    kx_ref[:, :dh] = k_ref[...]
    kx_ref[:, dh:] = (col == lax.shift_right_logical(row, blk.bit_length() - 1)).astype(BF16)
    in_block = (lax.shift_right_logical(lax.broadcasted_iota(jnp.int32, (nblk, t), 1), blk.bit_length() - 1)
                == lax.broadcasted_iota(jnp.int32, (nblk, t), 0))
    km = _dot(jnp.where(in_block, 1.0 / blk, 0.0).astype(BF16), k_ref[...])
    km_hi = km.astype(BF16)
    km_lo = (km - km_hi.astype(F32)).astype(BF16)
    eye = (lax.broadcasted_iota(jnp.int32, (blk, blk), 0) == lax.broadcasted_iota(jnp.int32, (blk, blk), 1)).astype(BF16)
    pen_ref[...] = jnp.zeros(pen_ref.shape, F32)
    def masked_logits(i):
        qi = q_ref[i * blk:(i + 1) * blk, :]
        nk = (i + 1) * blk
        if i <= MOBA_TOPK:
            return _dot_nt(qi, k_ref[0:nk, :])
        gate_t = _dot_nt(km_hi, qi) + _dot_nt(km_lo, qi)
        pen_ref[0:nblk, :] = _block_penalty_t(gate_t, i)
        pen = _dot_nt(eye, pen_ref[...].astype(BF16)).astype(BF16)
        return _dot_nt(jnp.concatenate([qi, pen], axis=1), kx_ref[0:nk, :])

    s_next = masked_logits(0)
    for i in range(nblk):
        nk = (i + 1) * blk
        s = s_next
        if i + 1 < nblk:
            s_next = masked_logits(i + 1)
        bias = [bias_ref[2]] * (i - 1) + ([bias_ref[1]] if i > 0 else []) + [bias_ref[0]]
        s = s + (jnp.concatenate(bias, axis=1) if len(bias) > 1 else bias[0])
        m = jnp.max(s, axis=1, keepdims=True)
        p = jnp.exp(s - m)
        l = jnp.sum(p, axis=1, keepdims=True)
        o = _dot(p.astype(BF16), v_ref[0:nk, :])
        o_ref[i * blk:(i + 1) * blk, :] = (o / l).astype(o_ref.dtype)


def _prompt_attention(q, k, v, bias_tiles, n_heads):
    n, t, width = q.shape
    dh = width // n_heads
    nblk = t // MOBA_BLOCK
    assert dh == V7X_LANES and t % MOBA_BLOCK == 0 and nblk <= V7X_LANES and MOBA_BLOCK & (MOBA_BLOCK - 1) == 0
    seq_spec = pl.BlockSpec((None, t, dh), lambda h, b: (b, 0, h))
    blocks = 4 * t * dh * 2 + 3 * MOBA_BLOCK * MOBA_BLOCK * 4 + t * 2 * dh * 2 + 6 * MOBA_BLOCK * t * 4
    return pl.pallas_call(
        _prompt_attn_kernel,
        grid=(n_heads, n),
        in_specs=[seq_spec, seq_spec, seq_spec,
                  pl.BlockSpec((None, 3, MOBA_BLOCK, MOBA_BLOCK), lambda h, b: (h, 0, 0, 0))],
        out_specs=seq_spec,
        out_shape=jax.ShapeDtypeStruct((n, t, width), BF16),
        scratch_shapes=[pltpu.VMEM((t, 2 * dh), BF16), pltpu.VMEM((dh, MOBA_BLOCK), F32)],
        compiler_params=_params(("parallel", "parallel"), blocks),
        name="prompt_attn",
    )(q, k, v, bias_tiles)


DECODE_ROWS = 32


def _lane_sums(xs):
    dh = xs[0].shape[-1]
    r = lax.broadcasted_iota(jnp.int32, (2 * dh, 2 * dh), 0)
    c = lax.broadcasted_iota(jnp.int32, (2 * dh, 2 * dh), 1)
    pair_ones = ((r < dh) == (c < dh)).astype(BF16)
    outs = []
    for i in range(0, len(xs) - 1, 2):
        y = _dot(jnp.concatenate([xs[i].astype(BF16), xs[i + 1].astype(BF16)], axis=1), pair_ones)
        outs += [y[:, :dh], y[:, dh:]]
    if len(xs) % 2:
        outs.append(_dot(xs[-1].astype(BF16), jnp.ones((dh, dh), BF16)))
    return outs


def _page_bias(relb_ref, page_buckets):
    if len(set(page_buckets)) == 1:
        return relb_ref[page_buckets[0]][None]
    return jnp.stack([relb_ref[b] for b in page_buckets])


def _decode_pages(part, q_ref, kn_ref, vn_ref, relb_ref, k_refs, v_refs, o_ref, m_ref, l_ref, g_ref, acc_ref, *,
                  n_pages, pages_per_block, buckets, before_page=()):
    pps = len(k_refs)
    parts = n_pages // pps
    page, h, dh = k_refs[0].shape
    q = q_ref[...]
    rows = min(DECODE_ROWS, page)
    for p in range(pps):
        if p < len(before_page):
            before_page[p]()
        page_buckets = [buckets[(a * pps + p) * page:(a * pps + p + 1) * page] for a in range(parts)]
        shared = set(b for pb in page_buckets for b in pb)
        shared = shared.pop() if len(shared) == 1 else None
        m = jnp.full((h, dh), NEG_INF, F32)
        l = acc = gsum = jnp.zeros((h, dh), F32)
        for r0 in range(0, page, 2 * rows):
            groups = [slice(r, r + rows) for r in range(r0, min(r0 + 2 * rows, page), rows)]
            sums = _lane_sums([(k_refs[p][g] * q[None]).reshape(rows * h, dh) for g in groups])
            for g, s in zip(groups, sums):
                s = s.reshape(rows, h, dh)
                gsum = gsum + jnp.sum(s, axis=0)
                if shared is None:
                    bias = _page_bias(relb_ref, page_buckets[0][g])
                    for a in range(1, parts):
                        if page_buckets[a][g] != page_buckets[0][g]:
                            bias = jnp.where(part == a, _page_bias(relb_ref, page_buckets[a][g]), bias)
                    s = s + bias
                m_new = jnp.maximum(m, jnp.max(s, axis=0))
                scale = jnp.exp(m - m_new)
                e = jnp.exp(s - m_new[None])
                l = l * scale + jnp.sum(e, axis=0)
                acc = acc * scale + jnp.sum(e * v_refs[p][g], axis=0)
                m = m_new
        idx = part * pps + p
        m_ref[idx] = m if shared is None else m + relb_ref[shared]
        l_ref[idx] = l
        g_ref[idx] = gsum
        acc_ref[idx] = acc

    def merge():
        n_blocks = n_pages // pages_per_block
        gate = [sum(g_ref[p] for p in range(j * pages_per_block, (j + 1) * pages_per_block)) for j in range(n_blocks)]
        s_own = _lane_sums([q * kn_ref[...]])[0] + relb_ref[0]
        m_tot = s_own
        sel = []
        for j in range(n_blocks):
            rank = jnp.zeros((h, dh), F32)
            for j2 in range(n_blocks):
                if j2 != j:
                    beats = (gate[j2] > gate[j]) | ((gate[j2] == gate[j]) & (j2 < j))
                    rank = rank + beats.astype(F32)
            sel.append(rank < MOBA_TOPK)
            for p in range(j * pages_per_block, (j + 1) * pages_per_block):
                m_tot = jnp.maximum(m_tot, jnp.where(sel[j], m_ref[p], NEG_INF))
        w_own = jnp.exp(s_own - m_tot)
        num = w_own * vn_ref[...]
        den = w_own
        for j in range(n_blocks):
            for p in range(j * pages_per_block, (j + 1) * pages_per_block):
                w = jnp.where(sel[j], jnp.exp(m_ref[p] - m_tot), 0.0)
                num = num + w * acc_ref[p]
                den = den + w * l_ref[p]
        o_ref[...] = (num / den).astype(o_ref.dtype)

    if parts == 1:
        merge()
    else:
        pl.when(part == parts - 1)(merge)


def _decode_attn_kernel(pt_ref, q_ref, kn_ref, vn_ref, relb_ref, *refs, n_pages, pages_per_block, buckets):
    del pt_ref
    _decode_pages(0, q_ref, kn_ref, vn_ref, relb_ref, refs[:n_pages], refs[n_pages:2 * n_pages], *refs[2 * n_pages:],
                  n_pages=n_pages, pages_per_block=pages_per_block, buckets=buckets)


def _decode_specs(page_table, cache_k, rel_bias, h, dh, pages_per_step, seq_of, part_of):
    page = cache_k.shape[1]
    n_pages = page_table.shape[1]
    past = n_pages * page
    num_buckets = rel_bias.shape[0]
    assert dh == V7X_LANES and MOBA_BLOCK % page == 0 and past % MOBA_BLOCK == 0 and n_pages % pages_per_step == 0
    buckets = tuple(int(b) for b in _rel_bucket_np(past - np.arange(past), num_buckets))
    relb = jnp.broadcast_to(rel_bias[:, :, None], (num_buckets, h, dh))
    tok_spec = pl.BlockSpec((None, h, dh), lambda *g: (seq_of(*g[:-1]), 0, 0))
    relb_spec = pl.BlockSpec((num_buckets, h, dh), lambda *g: (0, 0, 0))
    page_specs = [pl.BlockSpec((None, page, h, dh),
                               lambda *g, p=p: (g[-1][seq_of(*g[:-1]), part_of(*g[:-1]) * pages_per_step + p], 0, 0, 0))
                  for p in range(pages_per_step)]
    scratch = [pltpu.VMEM((n_pages, h, dh), F32)] * 4
    return buckets, relb, tok_spec, relb_spec, page_specs, scratch


def _decode_attention(q, k_new, v_new, cache_k, cache_v, page_table, rel_bias):
    s, h, dh = q.shape
    page = cache_k.shape[1]
    n_pages = page_table.shape[1]
    buckets, relb, tok_spec, relb_spec, page_specs, scratch = _decode_specs(
        page_table, cache_k, rel_bias, h, dh, n_pages, seq_of=lambda b: b, part_of=lambda b: 0)
    blocks = 2 * n_pages * page * h * dh * 4 + relb.size * 4
    kernel = functools.partial(_decode_attn_kernel, n_pages=n_pages, pages_per_block=MOBA_BLOCK // page,
                               buckets=buckets)
    return pl.pallas_call(
        kernel,
        grid_spec=pltpu.PrefetchScalarGridSpec(
            num_scalar_prefetch=1,
            grid=(s,),
            in_specs=[tok_spec, tok_spec, tok_spec, relb_spec] + page_specs + page_specs,
            out_specs=tok_spec,
            scratch_shapes=scratch),
        out_shape=jax.ShapeDtypeStruct((s, h, dh), BF16),
        compiler_params=_params(("parallel",), blocks),
        name="decode_attn",
    )(page_table, q, k_new, v_new, relb, *([cache_k] * n_pages), *([cache_v] * n_pages))


CONV_HALO = 32
CONV_ROWS = 64


def _prompt_conv_kernel(halo_ref, u_ref, w_ref, cb_ref, g_ref, b_ref, o_ref, ext_ref, conv_ref):
    tt, ch = u_ref.shape
    kw = w_ref.shape[0]
    first = CONV_HALO - (kw - 1)

    @pl.when(pl.program_id(1) == 0)
    def _():
        ext_ref[0:CONV_HALO, :] = jnp.zeros((CONV_HALO, ch), F32)

    @pl.when(pl.program_id(1) > 0)
    def _():
        ext_ref[0:CONV_HALO, :] = halo_ref[...]

    ext_ref[CONV_HALO:CONV_HALO + tt, :] = u_ref[...]
    rows = min(CONV_ROWS, tt)
    span = rows + CONV_HALO
    for c0 in range(0, ch, V7X_LANES):
        cs = slice(c0, c0 + V7X_LANES)
        for r0 in range(0, tt, rows):
            x = ext_ref[r0:r0 + span, cs]
            acc = jnp.zeros((rows, V7X_LANES), F32)
            for b in range(V7X_SUBLANES):
                taps = [s for s in range(first, first + kw) if s % V7X_SUBLANES == b]
                assert all(s + rows <= span for s in taps)
                xb = x if b == 0 else pltpu.roll(x, span - b, axis=0)
                for s in taps:
                    acc = acc + xb[s - b:s - b + rows] * w_ref[s - first:s - first + 1, cs]
            conv_ref[r0:r0 + rows, cs] = acc
    y = _layer_norm(conv_ref[...] + cb_ref[...], g_ref[...], b_ref[...])
    o_ref[...] = (y * _sigmoid(y)).astype(o_ref.dtype)


def _prompt_conv(u, conv_w, conv_b, ln_g, ln_b):
    n, t, ch = u.shape
    kw = conv_w.shape[0]
    tt = _tile(t, 128)
    assert kw - 1 <= CONV_HALO and tt % CONV_HALO == 0 and ch % V7X_LANES == 0
    vec = pl.BlockSpec((1, ch), lambda b, i: (0, 0))
    blocks = (2 * tt + 2 * CONV_HALO) * ch * 4 + tt * ch * 2
    return pl.pallas_call(
        _prompt_conv_kernel,
        grid=(n, t // tt),
        in_specs=[pl.BlockSpec((None, CONV_HALO, ch), lambda b, i: (b, jnp.maximum(i * (tt // CONV_HALO) - 1, 0), 0)),
                  pl.BlockSpec((None, tt, ch), lambda b, i: (b, i, 0)),
                  pl.BlockSpec((kw, ch), lambda b, i: (0, 0)), vec, vec, vec],
        out_specs=pl.BlockSpec((None, tt, ch), lambda b, i: (b, i, 0)),
        out_shape=jax.ShapeDtypeStruct((n, t, ch), BF16),
        scratch_shapes=[pltpu.VMEM((CONV_HALO + tt, ch), F32), pltpu.VMEM((tt, ch), F32)],
        compiler_params=_params(("parallel", "arbitrary"), blocks),
        name="prompt_conv",
    )(u, u, conv_w, conv_b.reshape(1, ch), ln_g.reshape(1, ch), ln_b.reshape(1, ch))


def _decode_conv_kernel(state_ref, u_ref, w_ref, cb_ref, g_ref, b_ref, o_ref, new_state_ref):
    kw = w_ref.shape[0]
    w_hist = w_ref[0:kw - 1, :]
    w_last = w_ref[kw - 1:kw, :]
    for b in range(state_ref.shape[0]):
        u = u_ref[b]
        conv = jnp.sum(state_ref[b] * w_hist, axis=0, keepdims=True) + u * w_last
        y = _layer_norm(conv + cb_ref[...], g_ref[...], b_ref[...])
        o_ref[b] = (y * _sigmoid(y)).astype(o_ref.dtype)
        new_state_ref[b, 0:kw - 2, :] = state_ref[b, 1:kw - 1, :]
        new_state_ref[b, kw - 2:kw - 1, :] = u


def _decode_conv(state, u, conv_w, conv_b, ln_g, ln_b):
    s, hist, ch = state.shape
    kw = conv_w.shape[0]
    ts = _tile(s, 16)
    vec = pl.BlockSpec((1, ch), lambda i: (0, 0))
    tok = pl.BlockSpec((ts, 1, ch), lambda i: (i, 0, 0))
    hist_spec = pl.BlockSpec((ts, hist, ch), lambda i: (i, 0, 0))
    blocks = 2 * ts * 32 * ch * 4 + 2 * ts * V7X_SUBLANES * ch * 4
    return pl.pallas_call(
        _decode_conv_kernel,
        grid=(s // ts,),
        in_specs=[hist_spec, tok, pl.BlockSpec((kw, ch), lambda i: (0, 0)), vec, vec, vec],
        out_specs=[tok, hist_spec],
        out_shape=[jax.ShapeDtypeStruct((s, 1, ch), F32), jax.ShapeDtypeStruct((s, hist, ch), F32)],
        compiler_params=_params(("parallel",), blocks),
        name="decode_conv",
    )(state, u, conv_w, conv_b.reshape(1, ch), ln_g.reshape(1, ch), ln_b.reshape(1, ch))


def _merge_kernel(x_ref, a_ref, c_ref, wga_ref, wgc_ref, wao_ref, wco_ref, o_ref):
    x = x_ref[...]
    mixed = (_sigmoid(_dot(x, wga_ref[...])) * _dot(a_ref[...], wao_ref[...])
             + _sigmoid(_dot(x, wgc_ref[...])) * _dot(c_ref[...], wco_ref[...]))
    o_ref[...] = mixed.astype(o_ref.dtype)


def _merge(xb, attn, cn, w_in, col_ga, col_gc, w_ao, w_co, tm):
    m, d = xb.shape
    wa = attn.shape[1]
    wc = cn.shape[1]
    tn = _tile(d, 512)
    tm = _tile(m, tm)
    blocks = tm * (d + wa + wc + tn) * 2 + (2 * d + wa + wc) * tn * 2
    return pl.pallas_call(
        _merge_kernel,
        grid=(m // tm, d // tn),
        in_specs=[pl.BlockSpec((tm, d), lambda i, j: (i, 0)),
                  pl.BlockSpec((tm, wa), lambda i, j: (i, 0)),
                  pl.BlockSpec((tm, wc), lambda i, j: (i, 0)),
                  pl.BlockSpec((d, tn), lambda i, j: (0, col_ga // tn + j)),
                  pl.BlockSpec((d, tn), lambda i, j: (0, col_gc // tn + j)),
                  pl.BlockSpec((wa, tn), lambda i, j: (0, j)),
                  pl.BlockSpec((wc, tn), lambda i, j: (0, j))],
        out_specs=pl.BlockSpec((tm, tn), lambda i, j: (i, j)),
        out_shape=jax.ShapeDtypeStruct((m, d), BF16),
        compiler_params=_params(("parallel", "parallel"), blocks),
        name="merge",
    )(xb, attn, cn, w_in, w_in, w_ao, w_co)


def _out_ln_kernel(x_ref, mixed_ref, w_ref, g_ref, b_ref, of_ref, *maybe_ob_ref, alpha):
    y = _layer_norm(alpha * x_ref[...] + _dot(mixed_ref[...], w_ref[...]), g_ref[...], b_ref[...])
    of_ref[...] = y
    for ob_ref in maybe_ob_ref:
        ob_ref[...] = y.astype(ob_ref.dtype)


def _out_ln(x, mixed, w_out, g, b, alpha, tm, emit_bf16):
    m, d = x.shape
    tm = _tile(m, tm)
    vec = pl.BlockSpec((1, d), lambda i: (0, 0))
    row = pl.BlockSpec((tm, d), lambda i: (i, 0))
    blocks = tm * d * (4 + 2 + 4 + 2) + d * d * 2
    return pl.pallas_call(
        functools.partial(_out_ln_kernel, alpha=alpha),
        grid=(m // tm,),
        in_specs=[row, row, pl.BlockSpec((d, d), lambda i: (0, 0)), vec, vec],
        out_specs=[row, row] if emit_bf16 else [row],
        out_shape=[jax.ShapeDtypeStruct((m, d), F32)] + ([jax.ShapeDtypeStruct((m, d), BF16)] if emit_bf16 else []),
        compiler_params=_params(("parallel",), blocks),
        name="out_ln",
    )(x, mixed, w_out, g.reshape(1, d), b.reshape(1, d))


def _mlp_ln_kernel(xf_ref, xb_ref, w1_ref, w2_ref, g_ref, b_ref, o_ref, *, alpha):
    f = pl.program_id(1)

    @pl.when(f == 0)
    def _():
        o_ref[...] = alpha * xf_ref[...]

    hid = jnp.maximum(_dot(xb_ref[...], w1_ref[...]), 0.0)
    o_ref[...] += _dot((hid * hid).astype(BF16), w2_ref[...])

    @pl.when(f == pl.num_programs(1) - 1)
    def _():
        o_ref[...] = _layer_norm(o_ref[...], g_ref[...], b_ref[...])


def _mlp_ln(xf, xb, w1, w2, g, b, alpha, tm):
    m, d = xf.shape
    dff = w1.shape[1]
    tf = _tile(dff, 1024)
    tm = _tile(m, tm)
    vec = pl.BlockSpec((1, d), lambda i, f: (0, 0))
    row = pl.BlockSpec((tm, d), lambda i, f: (i, 0))
    blocks = tm * d * (4 + 2 + 4) + 2 * d * tf * 2 + tm * tf * 4
    return pl.pallas_call(
        functools.partial(_mlp_ln_kernel, alpha=alpha),
        grid=(m // tm, dff // tf),
        in_specs=[row, row, pl.BlockSpec((d, tf), lambda i, f: (0, f)), pl.BlockSpec((tf, d), lambda i, f: (f, 0)),
                  vec, vec],
        out_specs=row,
        out_shape=jax.ShapeDtypeStruct((m, d), F32),
        compiler_params=_params(("parallel", "arbitrary"), blocks),
        name="mlp_ln",
    )(xf, xb, w1, w2, g.reshape(1, d), b.reshape(1, d))


def _mlp_ln_decode_kernel(pt_ref, xf_ref, w1_ref, w2_ref, g_ref, b_ref, q_ref, kn_ref, vn_ref, relb_ref, *refs,
                          alpha, pages_per_step, n_pages, pages_per_block, buckets):
    del pt_ref
    k_refs, v_refs = refs[:pages_per_step], refs[pages_per_step:2 * pages_per_step]
    o_ref, ao_ref, xb_ref, hid_ref, m_ref, l_ref, gs_ref, acc_ref = refs[2 * pages_per_step:]
    f = pl.program_id(1)
    nf = pl.num_programs(1)

    @pl.when(f == 0)
    def _():
        x = xf_ref[...]
        o_ref[...] = alpha * x
        xb_ref[...] = x.astype(BF16)

    tf, d = w2_ref.shape
    n_hid = max(1, min(pages_per_step // 2, tf // MLP_CHUNK))
    n_out = max(1, min(pages_per_step - n_hid, d // MLP_CHUNK))

    def hid_chunk(c, w=tf // n_hid):
        hid = jnp.maximum(_dot(xb_ref[...], w1_ref[:, c * w:(c + 1) * w]), 0.0)
        hid_ref[:, c * w:(c + 1) * w] = (hid * hid).astype(BF16)

    def out_chunk(c, w=d // n_out):
        o_ref[:, c * w:(c + 1) * w] += _dot(hid_ref[...], w2_ref[:, c * w:(c + 1) * w])

    chunks = [functools.partial(hid_chunk, c) for c in range(n_hid)] + [functools.partial(out_chunk, c) for c in range(n_out)]
    if len(chunks) > pages_per_step:
        for chunk in chunks:
            chunk()
        chunks = []
    part = lax.rem(pl.program_id(0) * nf + f, n_pages // pages_per_step)
    _decode_pages(part, q_ref, kn_ref, vn_ref, relb_ref, k_refs, v_refs, ao_ref, m_ref, l_ref, gs_ref, acc_ref,
                  n_pages=n_pages, pages_per_block=pages_per_block, buckets=buckets,
                  before_page=chunks)

    @pl.when(f == nf - 1)
    def _():
        o_ref[...] = _layer_norm(o_ref[...], g_ref[...], b_ref[...])


MLP_CHUNK = 256
FUSED_MLP_ROWS = 512
FUSED_MLP_COLS = 1024


def _fused_pages_per_step(m, dff, s, n_pages):
    steps = (m // _tile(m, FUSED_MLP_ROWS)) * (dff // _tile(dff, FUSED_MLP_COLS))
    if (s * n_pages) % steps or n_pages % ((s * n_pages) // steps):
        return None
    return (s * n_pages) // steps


def _mlp_ln_decode(xf, w1, w2, g, b, alpha, q, k_new, v_new, cache_k, cache_v, page_table, rel_bias, pps):
    m, d = xf.shape
    dff = w1.shape[1]
    tm, tf = _tile(m, FUSED_MLP_ROWS), _tile(dff, FUSED_MLP_COLS)
    nf = dff // tf
    s, h, dh = q.shape
    page = cache_k.shape[1]
    n_pages = page_table.shape[1]
    parts = n_pages // pps
    assert (m // tm) * nf == s * parts
    buckets, relb, tok_spec, relb_spec, page_specs, scratch = _decode_specs(
        page_table, cache_k, rel_bias, h, dh, pps,
        seq_of=lambda i, f: (i * nf + f) // parts, part_of=lambda i, f: (i * nf + f) % parts)
    vec = pl.BlockSpec((1, d), lambda i, f, pt: (0, 0))
    row = pl.BlockSpec((tm, d), lambda i, f, pt: (i, 0))
    blocks = tm * d * 9 + 2 * d * tf * 2 + tm * tf * 4 + 2 * pps * page * h * dh * 4 + relb.size * 4
    kernel = functools.partial(_mlp_ln_decode_kernel, alpha=alpha, pages_per_step=pps, n_pages=n_pages,
                               pages_per_block=MOBA_BLOCK // page, buckets=buckets)
    return pl.pallas_call(
        kernel,
        grid_spec=pltpu.PrefetchScalarGridSpec(
            num_scalar_prefetch=1,
            grid=(m // tm, nf),
            in_specs=[row, pl.BlockSpec((d, tf), lambda i, f, pt: (0, f)), pl.BlockSpec((tf, d), lambda i, f, pt: (f, 0)),
                      vec, vec, tok_spec, tok_spec, tok_spec, relb_spec] + page_specs + page_specs,
            out_specs=[row, tok_spec],
            scratch_shapes=[pltpu.VMEM((tm, d), BF16), pltpu.VMEM((tm, tf), BF16)] + scratch),
        out_shape=[jax.ShapeDtypeStruct((m, d), F32), jax.ShapeDtypeStruct((s, h, dh), BF16)],
        compiler_params=_params(("arbitrary", "arbitrary"), blocks),
        name="mlp_ln_decode_attn",
    )(page_table, xf, w1, w2, g.reshape(1, d), b.reshape(1, d), q, k_new, v_new, relb,
      *([cache_k] * pps), *([cache_v] * pps))


def _merge_out(x2d, xb, attn, cn, lw, alpha, tm, emit_bf16):
    mixed = _merge(xb, attn, cn, lw["w_in"], lw["col_ga"], lw["col_gc"], lw["w_attn_out"], lw["w_conv_out"], tm=2 * tm)
    return _out_ln(x2d, mixed, lw["w_out"], lw["ln1_g"], lw["ln1_b"], alpha, tm, emit_bf16)


def kernel(x_prompt, x_sample, cache_k, cache_v, state_conv, page_table, rel_bias, w_in, w_attn_out, conv_w, conv_b,
           conv_ln_g, conv_ln_b, w_conv_out, w_out, ln1_g, ln1_b, w_ff1, w_ff2, ln2_g, ln2_b):
    depth = w_in.shape[0]
    n, t, d = x_prompt.shape
    s, ts, _ = x_sample.shape
    n_heads, dh = cache_k.shape[-2:]
    aw = n_heads * dh
    ch = conv_w.shape[-1]
    alpha = (2 * depth) ** 0.25
    scale = dh ** -0.5
    assert ts == 1 and w_in.shape[-1] == 3 * aw + 2 * ch + 2 * d
    col_k, col_v, col_val, col_gate = aw, 2 * aw, 3 * aw, 3 * aw + ch
    col_ga, col_gc = 3 * aw + 2 * ch, 3 * aw + 2 * ch + d

    bias_tiles = _prompt_bias_tiles(rel_bias)
    hp = x_prompt.reshape(n * t, d)
    hs = x_sample.reshape(s, d)
    outs = [[] for _ in range(6)]
    for l in range(depth):
        lw = dict(w_in=w_in[l].astype(BF16), w_attn_out=w_attn_out[l].astype(BF16),
                  w_conv_out=w_conv_out[l].astype(BF16), w_out=w_out[l].astype(BF16),
                  w_ff1=w_ff1[l].astype(BF16), w_ff2=w_ff2[l].astype(BF16),
                  ln1_g=ln1_g[l], ln1_b=ln1_b[l], ln2_g=ln2_g[l], ln2_b=ln2_b[l], col_ga=col_ga, col_gc=col_gc)
        wi = lw["w_in"]

        xb, q, kf, kb, vf, vb = _proj_qkv(hp, wi, aw, scale, tm=512, q_dtype=BF16)
        u = _proj_glu(xb, wi, col_val, col_gate, ch, tm=1024)
        attn = _prompt_attention(q.reshape(n, t, aw), kb.reshape(n, t, aw), vb.reshape(n, t, aw), bias_tiles, n_heads)
        u3 = u.reshape(n, t, ch)
        cn = _prompt_conv(u3, conv_w[l], conv_b[l], conv_ln_g[l], conv_ln_b[l])
        outs[0].append(kf.reshape(n, t, n_heads, dh))
        outs[1].append(vf.reshape(n, t, n_heads, dh))
        outs[2].append(u3[:, t - (conv_w.shape[1] - 1):, :])

        xsb, qs, ksf, _, vsf, _ = _proj_qkv(hs, wi, aw, scale, tm=s, q_dtype=F32)
        us = _proj_glu(xsb, wi, col_val, col_gate, ch, tm=s)
        dec = (qs.reshape(s, n_heads, dh), ksf.reshape(s, n_heads, dh), vsf.reshape(s, n_heads, dh),
               cache_k[l], cache_v[l], page_table, rel_bias)

        mlp_w = (lw["w_ff1"], lw["w_ff2"], lw["ln2_g"], lw["ln2_b"], alpha)
        pps = _fused_pages_per_step(n * t, w_ff1.shape[-1], s, page_table.shape[1])
        x1 = _merge_out(hp, xb, attn.reshape(n * t, aw), cn.reshape(n * t, ch), lw, alpha, 512, emit_bf16=pps is None)
        if pps is None:
            hp_next = _mlp_ln(x1[0], x1[1], *mlp_w, tm=512)
            attn_s = _decode_attention(*dec)
        else:
            hp_next, attn_s = _mlp_ln_decode(x1[0], *mlp_w, *dec, pps)

        cn_s, new_state = _decode_conv(state_conv[l], us.reshape(s, 1, ch), conv_w[l], conv_b[l], conv_ln_g[l],
                                       conv_ln_b[l])
        x1s = _merge_out(hs, xsb, attn_s.reshape(s, aw), cn_s.reshape(s, ch).astype(BF16), lw, alpha, s, emit_bf16=True)
        hs_next = _mlp_ln(x1s[0], x1s[1], *mlp_w, tm=s)
        outs[3].append(ksf.reshape(s, 1, n_heads, dh))
        outs[4].append(vsf.reshape(s, 1, n_heads, dh))
        outs[5].append(new_state)
        hp, hs = hp_next, hs_next

    return (hp.reshape(n, t, d), hs.reshape(s, 1, d)) + tuple(jnp.stack(o) for o in outs)
```

```python
import functools
import math

import numpy as np
import jax
import jax.numpy as jnp
from jax import lax
from jax.experimental import pallas as pl
from jax.experimental.pallas import tpu as pltpu

MOBA_BLOCK = 256
MOBA_TOPK = 3
MAX_DISTANCE = 128
LN_EPS = 1e-5
NEG_INF = -1e30

V7X_VMEM_BYTES = 64 * 1024 * 1024
V7X_LANES = 128
V7X_SUBLANES = 8

F32 = jnp.float32
BF16 = jnp.bfloat16


def _vmem_limit(block_bytes):
    return int(min(max(2 * block_bytes + (16 << 20), 32 << 20), V7X_VMEM_BYTES - (4 << 20)))


def _params(semantics, block_bytes):
    return pltpu.CompilerParams(dimension_semantics=semantics, vmem_limit_bytes=_vmem_limit(block_bytes))


def _tile(n, want):
    t = min(n, want)
    while n % t:
        t -= 1
    return t


def _sigmoid(x):
    return 1.0 / (1.0 + jnp.exp(-x))


def _layer_norm(y, g, b):
    mu = jnp.mean(y, axis=-1, keepdims=True)
    d = y - mu
    var = jnp.mean(d * d, axis=-1, keepdims=True)
    return d * lax.rsqrt(var + LN_EPS) * g + b


def _dot(a, b):
    return jnp.dot(a, b, preferred_element_type=F32)


def _dot_nt(a, b):
    return lax.dot_general(a, b, (((1,), (1,)), ((), ())), preferred_element_type=F32)


def _proj_qkv_kernel(x_ref, w_ref, xb_ref, q_ref, kf_ref, kb_ref, vf_ref, vb_ref, *, scale):
    aw = q_ref.shape[1]
    xb = x_ref[...].astype(BF16)
    xb_ref[...] = xb
    q_ref[...] = (_dot(xb, w_ref[:, 0:aw]) * scale).astype(q_ref.dtype)
    for c, (f_ref, b_ref) in enumerate(((kf_ref, kb_ref), (vf_ref, vb_ref)), start=1):
        acc = _dot(xb, w_ref[:, c * aw:(c + 1) * aw])
        f_ref[...] = acc
        b_ref[...] = acc.astype(b_ref.dtype)


def _proj_glu_kernel(x_ref, wv_ref, wg_ref, u_ref):
    x = x_ref[...]
    u_ref[...] = _dot(x, wv_ref[...]) * _sigmoid(_dot(x, wg_ref[...]))


def _proj_qkv(x, w, aw, scale, tm, q_dtype):
    m, d = x.shape
    tm = _tile(m, tm)
    row = lambda width: pl.BlockSpec((tm, width), lambda i: (i, 0))
    blocks = tm * d * 6 + tm * aw * 16 + d * 3 * aw
    return pl.pallas_call(
        functools.partial(_proj_qkv_kernel, scale=scale),
        grid=(m // tm,),
        in_specs=[row(d), pl.BlockSpec((d, 3 * aw), lambda i: (0, 0), pipeline_mode=pl.Buffered(1))],
        out_specs=[row(d)] + [row(aw)] * 5,
        out_shape=[jax.ShapeDtypeStruct((m, d), BF16), jax.ShapeDtypeStruct((m, aw), q_dtype)]
        + [jax.ShapeDtypeStruct((m, aw), dt) for dt in (F32, BF16, F32, BF16)],
        compiler_params=_params(("parallel",), blocks),
        name="proj_qkv",
    )(x, w)


def _proj_glu(xb, w, col_val, col_gate, ncols, tm):
    m, d = xb.shape
    tn = _tile(ncols, 512)
    tm = _tile(m, tm)
    blocks = tm * d * 2 + 2 * d * tn * 2 + tm * tn * 4
    return pl.pallas_call(
        _proj_glu_kernel,
        grid=(m // tm, ncols // tn),
        in_specs=[pl.BlockSpec((tm, d), lambda i, j: (i, 0)),
                  pl.BlockSpec((d, tn), lambda i, j: (0, col_val // tn + j)),
                  pl.BlockSpec((d, tn), lambda i, j: (0, col_gate // tn + j))],
        out_specs=pl.BlockSpec((tm, tn), lambda i, j: (i, j)),
        out_shape=jax.ShapeDtypeStruct((m, ncols), F32),
        compiler_params=_params(("parallel", "parallel"), blocks),
        name="proj_glu",
    )(xb, w, w)


def _rel_bucket_np(dist, num_buckets):
    n = np.maximum(dist, 0)
    max_exact = num_buckets // 2
    nf = np.maximum(n, 1).astype(np.float32)
    large = max_exact + (np.log(nf / np.float32(max_exact)) / np.float32(math.log(MAX_DISTANCE / max_exact))
                         * np.float32(num_buckets - max_exact)).astype(np.int32)
    large = np.minimum(large, num_buckets - 1)
    return np.where(n < max_exact, n, large).astype(np.int32)


def _bias_tiles_kernel(rb_ref, bucket_ref, o_ref, *, num_buckets):
    h = pl.program_id(0)
    for t in range(bucket_ref.shape[0]):
        bucket = bucket_ref[t]
        acc = jnp.full(bucket.shape, NEG_INF, F32)
        for b in range(num_buckets):
            acc = jnp.where(bucket == b, rb_ref[b, h], acc)
        o_ref[t] = acc


def _prompt_bias_tiles(rel_bias):
    num_buckets, n_heads = rel_bias.shape
    assert MAX_DISTANCE <= MOBA_BLOCK + 1
    r = np.arange(MOBA_BLOCK)[:, None]
    c = np.arange(MOBA_BLOCK)[None, :]
    diag = np.where(r - c >= 0, _rel_bucket_np(r - c, num_buckets), -1)
    prev = _rel_bucket_np(r - c + MOBA_BLOCK, num_buckets)
    far = _rel_bucket_np(r - c + 2 * MOBA_BLOCK, num_buckets)
    buckets = jnp.asarray(np.stack([diag, prev, far]).astype(np.int32))
    return pl.pallas_call(
        functools.partial(_bias_tiles_kernel, num_buckets=num_buckets),
        grid=(n_heads,),
        in_specs=[pl.BlockSpec(memory_space=pltpu.SMEM),
                  pl.BlockSpec((3, MOBA_BLOCK, MOBA_BLOCK), lambda h: (0, 0, 0))],
        out_specs=pl.BlockSpec((None, 3, MOBA_BLOCK, MOBA_BLOCK), lambda h: (h, 0, 0, 0)),
        out_shape=jax.ShapeDtypeStruct((n_heads, 3, MOBA_BLOCK, MOBA_BLOCK), F32),
        compiler_params=_params(("arbitrary",), 6 * MOBA_BLOCK * MOBA_BLOCK * 4),
        name="bias_tiles",
    )(rel_bias, buckets)


def _block_penalty_t(gate_t, n_valid):
    row = lax.broadcasted_iota(jnp.int32, gate_t.shape, 0)
    valid = row < n_valid
    pen = jnp.zeros(gate_t.shape, F32)
    for j in range(n_valid):
        gj = gate_t[j:j + 1, :]
        beats = ((gate_t > gj) | ((gate_t == gj) & (row < j))) & valid
        rank = jnp.sum(beats.astype(F32), axis=0, keepdims=True)
        pen = jnp.where(row == j, jnp.where(rank < MOBA_TOPK, 0.0, NEG_INF), pen)
    return pen


def _prompt_attn_kernel(q_ref, k_ref, v_ref, bias_ref, o_ref, kx_ref, pen_ref):
    t, dh = q_ref.shape
    blk = MOBA_BLOCK
    nblk = t // blk
    row = lax.broadcasted_iota(jnp.int32, (t, dh), 0)
    col = lax.broadcasted_iota(jnp.int32, (t, dh), 1)
    kx_ref[:, :dh] = k_ref[...]
    kx_ref[:, dh:] = (col == lax.shift_right_logical(row, blk.bit_length() - 1)).astype(BF16)
    in_block = (lax.shift_right_logical(lax.broadcasted_iota(jnp.int32, (nblk, t), 1), blk.bit_length() - 1)
                == lax.broadcasted_iota(jnp.int32, (nblk, t), 0))
    km = _dot(jnp.where(in_block, 1.0 / blk, 0.0).astype(BF16), k_ref[...])
    km_hi = km.astype(BF16)
    km_lo = (km - km_hi.astype(F32)).astype(BF16)
    eye = (lax.broadcasted_iota(jnp.int32, (blk, blk), 0) == lax.broadcasted_iota(jnp.int32, (blk, blk), 1)).astype(BF16)
    pen_ref[...] = jnp.zeros(pen_ref.shape, F32)
    def masked_logits(i):
        qi = q_ref[i * blk:(i + 1) * blk, :]
        nk = (i + 1) * blk
        if i <= MOBA_TOPK:
            return _dot_nt(qi, k_ref[0:nk, :])
        gate_t = _dot_nt(km_hi, qi) + _dot_nt(km_lo, qi)
        pen_ref[0:nblk, :] = _block_penalty_t(gate_t, i)
        pen = _dot_nt(eye, pen_ref[...].astype(BF16)).astype(BF16)
        return _dot_nt(jnp.concatenate([qi, pen], axis=1), kx_ref[0:nk, :])

    s_next = masked_logits(0)
    for i in range(nblk):
        nk = (i + 1) * blk
        s = s_next
        if i + 1 < nblk:
            s_next = masked_logits(i + 1)
        bias = [bias_ref[2]] * (i - 1) + ([bias_ref[1]] if i > 0 else []) + [bias_ref[0]]
        s = s + (jnp.concatenate(bias, axis=1) if len(bias) > 1 else bias[0])
        m = jnp.max(s, axis=1, keepdims=True)
        p = jnp.exp(s - m)
        l = jnp.sum(p, axis=1, keepdims=True)
        o = _dot(p.astype(BF16), v_ref[0:nk, :])
        o_ref[i * blk:(i + 1) * blk, :] = (o / l).astype(o_ref.dtype)


def _prompt_attention(q, k, v, bias_tiles, n_heads):
    n, t, width = q.shape
    dh = width // n_heads
    nblk = t // MOBA_BLOCK
    assert dh == V7X_LANES and t % MOBA_BLOCK == 0 and nblk <= V7X_LANES and MOBA_BLOCK & (MOBA_BLOCK - 1) == 0
    seq_spec = pl.BlockSpec((None, t, dh), lambda h, b: (b, 0, h))
    blocks = 4 * t * dh * 2 + 3 * MOBA_BLOCK * MOBA_BLOCK * 4 + t * 2 * dh * 2 + 6 * MOBA_BLOCK * t * 4
    return pl.pallas_call(
        _prompt_attn_kernel,
        grid=(n_heads, n),
        in_specs=[seq_spec, seq_spec, seq_spec,
                  pl.BlockSpec((None, 3, MOBA_BLOCK, MOBA_BLOCK), lambda h, b: (h, 0, 0, 0))],
        out_specs=seq_spec,
        out_shape=jax.ShapeDtypeStruct((n, t, width), BF16),
        scratch_shapes=[pltpu.VMEM((t, 2 * dh), BF16), pltpu.VMEM((dh, MOBA_BLOCK), F32)],
        compiler_params=_params(("parallel", "parallel"), blocks),
        name="prompt_attn",
    )(q, k, v, bias_tiles)


DECODE_ROWS = 32


def _lane_sums(xs):
    dh = xs[0].shape[-1]
    r = lax.broadcasted_iota(jnp.int32, (2 * dh, 2 * dh), 0)
    c = lax.broadcasted_iota(jnp.int32, (2 * dh, 2 * dh), 1)
    pair_ones = ((r < dh) == (c < dh)).astype(BF16)
    outs = []
    for i in range(0, len(xs) - 1, 2):
        y = _dot(jnp.concatenate([xs[i].astype(BF16), xs[i + 1].astype(BF16)], axis=1), pair_ones)
        outs += [y[:, :dh], y[:, dh:]]
    if len(xs) % 2:
        outs.append(_dot(xs[-1].astype(BF16), jnp.ones((dh, dh), BF16)))
    return outs


def _page_bias(relb_ref, page_buckets):
    if len(set(page_buckets)) == 1:
        return relb_ref[page_buckets[0]][None]
    return jnp.stack([relb_ref[b] for b in page_buckets])


def _decode_pages(part, q_ref, kn_ref, vn_ref, relb_ref, k_refs, v_refs, o_ref, m_ref, l_ref, g_ref, acc_ref, *,
                  n_pages, pages_per_block, buckets, before_page=()):
    pps = len(k_refs)
    parts = n_pages // pps
    page, h, dh = k_refs[0].shape
    q = q_ref[...]
    rows = min(DECODE_ROWS, page)
    for p in range(pps):
        if p < len(before_page):
            before_page[p]()
        page_buckets = [buckets[(a * pps + p) * page:(a * pps + p + 1) * page] for a in range(parts)]
        shared = set(b for pb in page_buckets for b in pb)
        shared = shared.pop() if len(shared) == 1 else None
        m = jnp.full((h, dh), NEG_INF, F32)
        l = acc = gsum = jnp.zeros((h, dh), F32)
        for r0 in range(0, page, 2 * rows):
            groups = [slice(r, r + rows) for r in range(r0, min(r0 + 2 * rows, page), rows)]
            sums = _lane_sums([(k_refs[p][g] * q[None]).reshape(rows * h, dh) for g in groups])
            for g, s in zip(groups, sums):
                s = s.reshape(rows, h, dh)
                gsum = gsum + jnp.sum(s, axis=0)
                if shared is None:
                    bias = _page_bias(relb_ref, page_buckets[0][g])
                    for a in range(1, parts):
                        if page_buckets[a][g] != page_buckets[0][g]:
                            bias = jnp.where(part == a, _page_bias(relb_ref, page_buckets[a][g]), bias)
                    s = s + bias
                m_new = jnp.maximum(m, jnp.max(s, axis=0))
                scale = jnp.exp(m - m_new)
                e = jnp.exp(s - m_new[None])
                l = l * scale + jnp.sum(e, axis=0)
                acc = acc * scale + jnp.sum(e * v_refs[p][g], axis=0)
                m = m_new
        idx = part * pps + p
        m_ref[idx] = m if shared is None else m + relb_ref[shared]
        l_ref[idx] = l
        g_ref[idx] = gsum
        acc_ref[idx] = acc

    def merge():
        n_blocks = n_pages // pages_per_block
        gate = [sum(g_ref[p] for p in range(j * pages_per_block, (j + 1) * pages_per_block)) for j in range(n_blocks)]
        s_own = _lane_sums([q * kn_ref[...]])[0] + relb_ref[0]
        m_tot = s_own
        sel = []
        for j in range(n_blocks):
            rank = jnp.zeros((h, dh), F32)
            for j2 in range(n_blocks):
                if j2 != j:
                    beats = (gate[j2] > gate[j]) | ((gate[j2] == gate[j]) & (j2 < j))
                    rank = rank + beats.astype(F32)
            sel.append(rank < MOBA_TOPK)
            for p in range(j * pages_per_block, (j + 1) * pages_per_block):
                m_tot = jnp.maximum(m_tot, jnp.where(sel[j], m_ref[p], NEG_INF))
        w_own = jnp.exp(s_own - m_tot)
        num = w_own * vn_ref[...]
        den = w_own
        for j in range(n_blocks):
            for p in range(j * pages_per_block, (j + 1) * pages_per_block):
                w = jnp.where(sel[j], jnp.exp(m_ref[p] - m_tot), 0.0)
                num = num + w * acc_ref[p]
                den = den + w * l_ref[p]
        o_ref[...] = (num / den).astype(o_ref.dtype)

    if parts == 1:
        merge()
    else:
        pl.when(part == parts - 1)(merge)


def _decode_attn_kernel(pt_ref, q_ref, kn_ref, vn_ref, relb_ref, *refs, n_pages, pages_per_block, buckets):
    del pt_ref
    _decode_pages(0, q_ref, kn_ref, vn_ref, relb_ref, refs[:n_pages], refs[n_pages:2 * n_pages], *refs[2 * n_pages:],
                  n_pages=n_pages, pages_per_block=pages_per_block, buckets=buckets)


def _decode_specs(page_table, cache_k, rel_bias, h, dh, pages_per_step, seq_of, part_of):
    page = cache_k.shape[1]
    n_pages = page_table.shape[1]
    past = n_pages * page
    num_buckets = rel_bias.shape[0]
    assert dh == V7X_LANES and MOBA_BLOCK % page == 0 and past % MOBA_BLOCK == 0 and n_pages % pages_per_step == 0
    buckets = tuple(int(b) for b in _rel_bucket_np(past - np.arange(past), num_buckets))
    relb = jnp.broadcast_to(rel_bias[:, :, None], (num_buckets, h, dh))
    tok_spec = pl.BlockSpec((None, h, dh), lambda *g: (seq_of(*g[:-1]), 0, 0))
    relb_spec = pl.BlockSpec((num_buckets, h, dh), lambda *g: (0, 0, 0))
    page_specs = [pl.BlockSpec((None, page, h, dh),
                               lambda *g, p=p: (g[-1][seq_of(*g[:-1]), part_of(*g[:-1]) * pages_per_step + p], 0, 0, 0))
                  for p in range(pages_per_step)]
    scratch = [pltpu.VMEM((n_pages, h, dh), F32)] * 4
    return buckets, relb, tok_spec, relb_spec, page_specs, scratch


def _decode_attention(q, k_new, v_new, cache_k, cache_v, page_table, rel_bias):
    s, h, dh = q.shape
    page = cache_k.shape[1]
    n_pages = page_table.shape[1]
    buckets, relb, tok_spec, relb_spec, page_specs, scratch = _decode_specs(
        page_table, cache_k, rel_bias, h, dh, n_pages, seq_of=lambda b: b, part_of=lambda b: 0)
    blocks = 2 * n_pages * page * h * dh * 4 + relb.size * 4
    kernel = functools.partial(_decode_attn_kernel, n_pages=n_pages, pages_per_block=MOBA_BLOCK // page,
                               buckets=buckets)
    return pl.pallas_call(
        kernel,
        grid_spec=pltpu.PrefetchScalarGridSpec(
            num_scalar_prefetch=1,
            grid=(s,),
            in_specs=[tok_spec, tok_spec, tok_spec, relb_spec] + page_specs + page_specs,
            out_specs=tok_spec,
            scratch_shapes=scratch),
        out_shape=jax.ShapeDtypeStruct((s, h, dh), BF16),
        compiler_params=_params(("parallel",), blocks),
        name="decode_attn",
    )(page_table, q, k_new, v_new, relb, *([cache_k] * n_pages), *([cache_v] * n_pages))


CONV_HALO = 32
CONV_ROWS = 64
GLU_CHUNK = 256


def _causal_conv_columns(ext_ref, w_ref, conv_ref, c0, tt):
    kw = w_ref.shape[0]
    first = CONV_HALO - (kw - 1)
    rows = min(CONV_ROWS, tt)
    span = rows + CONV_HALO
    cs = slice(c0, c0 + V7X_LANES)
    for r0 in range(0, tt, rows):
        x = ext_ref[r0:r0 + span, cs]
        acc = jnp.zeros((rows, V7X_LANES), F32)
        for b in range(V7X_SUBLANES):
            taps = [s for s in range(first, first + kw) if s % V7X_SUBLANES == b]
            assert all(s + rows <= span for s in taps)
            xb = x if b == 0 else pltpu.roll(x, span - b, axis=0)
            for s in taps:
                acc = acc + xb[s - b:s - b + rows] * w_ref[s - first:s - first + 1, cs]
        conv_ref[r0:r0 + rows, cs] = acc


def _glu_conv_kernel(x_ref, wv_ref, wg_ref, w_ref, cb_ref, g_ref, b_ref, o_ref, tail_ref, ext_ref, conv_ref):
    tt = x_ref.shape[0]
    ch = wv_ref.shape[1]
    i = pl.program_id(1)

    @pl.when(i == 0)
    def _():
        ext_ref[0:CONV_HALO, :] = jnp.zeros((CONV_HALO, ch), F32)

    @pl.when(i > 0)
    def _():
        ext_ref[0:CONV_HALO, :] = ext_ref[tt:tt + CONV_HALO, :]

    x = x_ref[...]
    chunk = min(GLU_CHUNK, ch)
    for c0 in range(0, ch, chunk):
        cols = slice(c0, c0 + chunk)
        ext_ref[CONV_HALO:CONV_HALO + tt, cols] = _dot(x, wv_ref[:, cols]) * _sigmoid(_dot(x, wg_ref[:, cols]))
        for c in range(c0, c0 + chunk, V7X_LANES):
            _causal_conv_columns(ext_ref, w_ref, conv_ref, c, tt)
    y = _layer_norm(conv_ref[...] + cb_ref[...], g_ref[...], b_ref[...])
    o_ref[...] = (y * _sigmoid(y)).astype(o_ref.dtype)

    @pl.when(i == pl.num_programs(1) - 1)
    def _():
        tail_ref[...] = ext_ref[tt:tt + CONV_HALO, :]


def _prompt_glu_conv(xb, w, col_val, col_gate, conv_w, conv_b, ln_g, ln_b):
    n, t, d = xb.shape
    kw, ch = conv_w.shape
    tt = _tile(t, 256)
    assert kw - 1 <= CONV_HALO <= tt and tt % V7X_SUBLANES == 0 and ch % min(GLU_CHUNK, ch) == 0
    assert col_val % ch == 0 and col_gate % ch == 0
    vec = pl.BlockSpec((1, ch), lambda b, i: (0, 0))
    wspec = lambda col: pl.BlockSpec((d, ch), lambda b, i: (0, col // ch), pipeline_mode=pl.Buffered(1))
    blocks = tt * d * 2 + d * ch * 2 + (2 * tt + 2 * CONV_HALO) * ch * 4 + tt * ch * 2
    return pl.pallas_call(
        _glu_conv_kernel,
        grid=(n, t // tt),
        in_specs=[pl.BlockSpec((None, tt, d), lambda b, i: (b, i, 0)), wspec(col_val), wspec(col_gate),
                  pl.BlockSpec((kw, ch), lambda b, i: (0, 0)), vec, vec, vec],
        out_specs=[pl.BlockSpec((None, tt, ch), lambda b, i: (b, i, 0)),
                   pl.BlockSpec((None, CONV_HALO, ch), lambda b, i: (b, 0, 0))],
        out_shape=[jax.ShapeDtypeStruct((n, t, ch), BF16), jax.ShapeDtypeStruct((n, CONV_HALO, ch), F32)],
        scratch_shapes=[pltpu.VMEM((CONV_HALO + tt, ch), F32), pltpu.VMEM((tt, ch), F32)],
        compiler_params=_params(("parallel", "arbitrary"), blocks),
        name="glu_conv",
    )(xb, w, w, conv_w, conv_b.reshape(1, ch), ln_g.reshape(1, ch), ln_b.reshape(1, ch))


def _decode_conv_kernel(state_ref, u_ref, w_ref, cb_ref, g_ref, b_ref, o_ref, new_state_ref):
    kw = w_ref.shape[0]
    w_hist = w_ref[0:kw - 1, :]
    w_last = w_ref[kw - 1:kw, :]
    for b in range(state_ref.shape[0]):
        u = u_ref[b]
        conv = jnp.sum(state_ref[b] * w_hist, axis=0, keepdims=True) + u * w_last
        y = _layer_norm(conv + cb_ref[...], g_ref[...], b_ref[...])
        o_ref[b] = (y * _sigmoid(y)).astype(o_ref.dtype)
        new_state_ref[b, 0:kw - 2, :] = state_ref[b, 1:kw - 1, :]
        new_state_ref[b, kw - 2:kw - 1, :] = u


def _decode_conv(state, u, conv_w, conv_b, ln_g, ln_b):
    s, hist, ch = state.shape
    kw = conv_w.shape[0]
    ts = _tile(s, 16)
    vec = pl.BlockSpec((1, ch), lambda i: (0, 0))
    tok = pl.BlockSpec((ts, 1, ch), lambda i: (i, 0, 0))
    hist_spec = pl.BlockSpec((ts, hist, ch), lambda i: (i, 0, 0))
    blocks = 2 * ts * 32 * ch * 4 + 2 * ts * V7X_SUBLANES * ch * 4
    return pl.pallas_call(
        _decode_conv_kernel,
        grid=(s // ts,),
        in_specs=[hist_spec, tok, pl.BlockSpec((kw, ch), lambda i: (0, 0)), vec, vec, vec],
        out_specs=[tok, hist_spec],
        out_shape=[jax.ShapeDtypeStruct((s, 1, ch), F32), jax.ShapeDtypeStruct((s, hist, ch), F32)],
        compiler_params=_params(("parallel",), blocks),
        name="decode_conv",
    )(state, u, conv_w, conv_b.reshape(1, ch), ln_g.reshape(1, ch), ln_b.reshape(1, ch))


def _merge_kernel(x_ref, a_ref, c_ref, wga_ref, wgc_ref, wao_ref, wco_ref, o_ref):
    x = x_ref[...]
    mixed = (_sigmoid(_dot(x, wga_ref[...])) * _dot(a_ref[...], wao_ref[...])
             + _sigmoid(_dot(x, wgc_ref[...])) * _dot(c_ref[...], wco_ref[...]))
    o_ref[...] = mixed.astype(o_ref.dtype)


def _merge(xb, attn, cn, w_in, col_ga, col_gc, w_ao, w_co, tm):
    m, d = xb.shape
    wa = attn.shape[1]
    wc = cn.shape[1]
    tn = _tile(d, 512)
    tm = _tile(m, tm)
    blocks = tm * (d + wa + wc + tn) * 2 + (2 * d + wa + wc) * tn * 2
    return pl.pallas_call(
        _merge_kernel,
        grid=(m // tm, d // tn),
        in_specs=[pl.BlockSpec((tm, d), lambda i, j: (i, 0)),
                  pl.BlockSpec((tm, wa), lambda i, j: (i, 0)),
                  pl.BlockSpec((tm, wc), lambda i, j: (i, 0)),
                  pl.BlockSpec((d, tn), lambda i, j: (0, col_ga // tn + j)),
                  pl.BlockSpec((d, tn), lambda i, j: (0, col_gc // tn + j)),
                  pl.BlockSpec((wa, tn), lambda i, j: (0, j)),
                  pl.BlockSpec((wc, tn), lambda i, j: (0, j))],
        out_specs=pl.BlockSpec((tm, tn), lambda i, j: (i, j)),
        out_shape=jax.ShapeDtypeStruct((m, d), BF16),
        compiler_params=_params(("parallel", "parallel"), blocks),
        name="merge",
    )(xb, attn, cn, w_in, w_in, w_ao, w_co)


def _out_ln_kernel(x_ref, mixed_ref, w_ref, g_ref, b_ref, of_ref, *maybe_ob_ref, alpha):
    y = _layer_norm(alpha * x_ref[...] + _dot(mixed_ref[...], w_ref[...]), g_ref[...], b_ref[...])
    of_ref[...] = y
    for ob_ref in maybe_ob_ref:
        ob_ref[...] = y.astype(ob_ref.dtype)


def _out_ln(x, mixed, w_out, g, b, alpha, tm, emit_bf16):
    m, d = x.shape
    tm = _tile(m, tm)
    vec = pl.BlockSpec((1, d), lambda i: (0, 0))
    row = pl.BlockSpec((tm, d), lambda i: (i, 0))
    blocks = tm * d * (4 + 2 + 4 + 2) + d * d * 2
    return pl.pallas_call(
        functools.partial(_out_ln_kernel, alpha=alpha),
        grid=(m // tm,),
        in_specs=[row, row, pl.BlockSpec((d, d), lambda i: (0, 0)), vec, vec],
        out_specs=[row, row] if emit_bf16 else [row],
        out_shape=[jax.ShapeDtypeStruct((m, d), F32)] + ([jax.ShapeDtypeStruct((m, d), BF16)] if emit_bf16 else []),
        compiler_params=_params(("parallel",), blocks),
        name="out_ln",
    )(x, mixed, w_out, g.reshape(1, d), b.reshape(1, d))


def _mlp_ln_kernel(xf_ref, xb_ref, w1_ref, w2_ref, g_ref, b_ref, o_ref, *, alpha):
    f = pl.program_id(1)

    @pl.when(f == 0)
    def _():
        o_ref[...] = alpha * xf_ref[...]

    hid = jnp.maximum(_dot(xb_ref[...], w1_ref[...]), 0.0)
    o_ref[...] += _dot((hid * hid).astype(BF16), w2_ref[...])

    @pl.when(f == pl.num_programs(1) - 1)
    def _():
        o_ref[...] = _layer_norm(o_ref[...], g_ref[...], b_ref[...])


def _mlp_ln(xf, xb, w1, w2, g, b, alpha, tm):
    m, d = xf.shape
    dff = w1.shape[1]
    tf = _tile(dff, 1024)
    tm = _tile(m, tm)
    vec = pl.BlockSpec((1, d), lambda i, f: (0, 0))
    row = pl.BlockSpec((tm, d), lambda i, f: (i, 0))
    blocks = tm * d * (4 + 2 + 4) + 2 * d * tf * 2 + tm * tf * 4
    return pl.pallas_call(
        functools.partial(_mlp_ln_kernel, alpha=alpha),
        grid=(m // tm, dff // tf),
        in_specs=[row, row, pl.BlockSpec((d, tf), lambda i, f: (0, f)), pl.BlockSpec((tf, d), lambda i, f: (f, 0)),
                  vec, vec],
        out_specs=row,
        out_shape=jax.ShapeDtypeStruct((m, d), F32),
        compiler_params=_params(("parallel", "arbitrary"), blocks),
        name="mlp_ln",
    )(xf, xb, w1, w2, g.reshape(1, d), b.reshape(1, d))


def _mlp_ln_decode_kernel(pt_ref, xf_ref, w1_ref, w2_ref, g_ref, b_ref, q_ref, kn_ref, vn_ref, relb_ref, *refs,
                          alpha, pages_per_step, n_pages, pages_per_block, buckets):
    del pt_ref
    k_refs, v_refs = refs[:pages_per_step], refs[pages_per_step:2 * pages_per_step]
    o_ref, ao_ref, xb_ref, hid_ref, m_ref, l_ref, gs_ref, acc_ref = refs[2 * pages_per_step:]
    f = pl.program_id(1)
    nf = pl.num_programs(1)

    @pl.when(f == 0)
    def _():
        x = xf_ref[...]
        o_ref[...] = alpha * x
        xb_ref[...] = x.astype(BF16)

    tf, d = w2_ref.shape
    n_hid = max(1, min(pages_per_step // 2, tf // MLP_CHUNK))
    n_out = max(1, min(pages_per_step - n_hid, d // MLP_CHUNK))

    def hid_chunk(c, w=tf // n_hid):
        hid = jnp.maximum(_dot(xb_ref[...], w1_ref[:, c * w:(c + 1) * w]), 0.0)
        hid_ref[:, c * w:(c + 1) * w] = (hid * hid).astype(BF16)

    def out_chunk(c, w=d // n_out):
        o_ref[:, c * w:(c + 1) * w] += _dot(hid_ref[...], w2_ref[:, c * w:(c + 1) * w])

    chunks = [functools.partial(hid_chunk, c) for c in range(n_hid)] + [functools.partial(out_chunk, c) for c in range(n_out)]
    if len(chunks) > pages_per_step:
        for chunk in chunks:
            chunk()
        chunks = []
    part = lax.rem(pl.program_id(0) * nf + f, n_pages // pages_per_step)
    _decode_pages(part, q_ref, kn_ref, vn_ref, relb_ref, k_refs, v_refs, ao_ref, m_ref, l_ref, gs_ref, acc_ref,
                  n_pages=n_pages, pages_per_block=pages_per_block, buckets=buckets,
                  before_page=chunks)

    @pl.when(f == nf - 1)
    def _():
        o_ref[...] = _layer_norm(o_ref[...], g_ref[...], b_ref[...])


MLP_CHUNK = 256
FUSED_MLP_ROWS = 512
FUSED_MLP_COLS = 1024


def _fused_pages_per_step(m, dff, s, n_pages):
    steps = (m // _tile(m, FUSED_MLP_ROWS)) * (dff // _tile(dff, FUSED_MLP_COLS))
    if (s * n_pages) % steps or n_pages % ((s * n_pages) // steps):
        return None
    return (s * n_pages) // steps


def _mlp_ln_decode(xf, w1, w2, g, b, alpha, q, k_new, v_new, cache_k, cache_v, page_table, rel_bias, pps):
    m, d = xf.shape
    dff = w1.shape[1]
    tm, tf = _tile(m, FUSED_MLP_ROWS), _tile(dff, FUSED_MLP_COLS)
    nf = dff // tf
    s, h, dh = q.shape
    page = cache_k.shape[1]
    n_pages = page_table.shape[1]
    parts = n_pages // pps
    assert (m // tm) * nf == s * parts
    buckets, relb, tok_spec, relb_spec, page_specs, scratch = _decode_specs(
        page_table, cache_k, rel_bias, h, dh, pps,
        seq_of=lambda i, f: (i * nf + f) // parts, part_of=lambda i, f: (i * nf + f) % parts)
    vec = pl.BlockSpec((1, d), lambda i, f, pt: (0, 0))
    row = pl.BlockSpec((tm, d), lambda i, f, pt: (i, 0))
    blocks = tm * d * 9 + 2 * d * tf * 2 + tm * tf * 4 + 2 * pps * page * h * dh * 4 + relb.size * 4
    kernel = functools.partial(_mlp_ln_decode_kernel, alpha=alpha, pages_per_step=pps, n_pages=n_pages,
                               pages_per_block=MOBA_BLOCK // page, buckets=buckets)
    return pl.pallas_call(
        kernel,
        grid_spec=pltpu.PrefetchScalarGridSpec(
            num_scalar_prefetch=1,
            grid=(m // tm, nf),
            in_specs=[row, pl.BlockSpec((d, tf), lambda i, f, pt: (0, f)), pl.BlockSpec((tf, d), lambda i, f, pt: (f, 0)),
                      vec, vec, tok_spec, tok_spec, tok_spec, relb_spec] + page_specs + page_specs,
            out_specs=[row, tok_spec],
            scratch_shapes=[pltpu.VMEM((tm, d), BF16), pltpu.VMEM((tm, tf), BF16)] + scratch),
        out_shape=[jax.ShapeDtypeStruct((m, d), F32), jax.ShapeDtypeStruct((s, h, dh), BF16)],
        compiler_params=_params(("arbitrary", "arbitrary"), blocks),
        name="mlp_ln_decode_attn",
    )(page_table, xf, w1, w2, g.reshape(1, d), b.reshape(1, d), q, k_new, v_new, relb,
      *([cache_k] * pps), *([cache_v] * pps))


def _merge_out(x2d, xb, attn, cn, lw, alpha, tm, emit_bf16):
    mixed = _merge(xb, attn, cn, lw["w_in"], lw["col_ga"], lw["col_gc"], lw["w_attn_out"], lw["w_conv_out"], tm=2 * tm)
    return _out_ln(x2d, mixed, lw["w_out"], lw["ln1_g"], lw["ln1_b"], alpha, tm, emit_bf16)


def kernel(x_prompt, x_sample, cache_k, cache_v, state_conv, page_table, rel_bias, w_in, w_attn_out, conv_w, conv_b,
           conv_ln_g, conv_ln_b, w_conv_out, w_out, ln1_g, ln1_b, w_ff1, w_ff2, ln2_g, ln2_b):
    depth = w_in.shape[0]
    n, t, d = x_prompt.shape
    s, ts, _ = x_sample.shape
    n_heads, dh = cache_k.shape[-2:]
    aw = n_heads * dh
    ch = conv_w.shape[-1]
    alpha = (2 * depth) ** 0.25
    scale = dh ** -0.5
    assert ts == 1 and w_in.shape[-1] == 3 * aw + 2 * ch + 2 * d
    col_val, col_gate = 3 * aw, 3 * aw + ch
    col_ga, col_gc = 3 * aw + 2 * ch, 3 * aw + 2 * ch + d

    bias_tiles = _prompt_bias_tiles(rel_bias)
    hp = x_prompt.reshape(n * t, d)
    hs = x_sample.reshape(s, d)
    outs = [[] for _ in range(6)]
    for l in range(depth):
        lw = dict(w_in=w_in[l].astype(BF16), w_attn_out=w_attn_out[l].astype(BF16),
                  w_conv_out=w_conv_out[l].astype(BF16), w_out=w_out[l].astype(BF16),
                  w_ff1=w_ff1[l].astype(BF16), w_ff2=w_ff2[l].astype(BF16),
                  ln1_g=ln1_g[l], ln1_b=ln1_b[l], ln2_g=ln2_g[l], ln2_b=ln2_b[l], col_ga=col_ga, col_gc=col_gc)
        wi = lw["w_in"]

        xb, q, kf, kb, vf, vb = _proj_qkv(hp, wi, aw, scale, tm=512, q_dtype=BF16)
        attn = _prompt_attention(q.reshape(n, t, aw), kb.reshape(n, t, aw), vb.reshape(n, t, aw), bias_tiles, n_heads)
        cn, u_tail = _prompt_glu_conv(xb.reshape(n, t, d), wi, col_val, col_gate, conv_w[l], conv_b[l],
                                      conv_ln_g[l], conv_ln_b[l])
        outs[0].append(kf.reshape(n, t, n_heads, dh))
        outs[1].append(vf.reshape(n, t, n_heads, dh))
        outs[2].append(u_tail[:, CONV_HALO - (conv_w.shape[1] - 1):, :])

        xsb, qs, ksf, _, vsf, _ = _proj_qkv(hs, wi, aw, scale, tm=s, q_dtype=F32)
        us = _proj_glu(xsb, wi, col_val, col_gate, ch, tm=s)
        dec = (qs.reshape(s, n_heads, dh), ksf.reshape(s, n_heads, dh), vsf.reshape(s, n_heads, dh),
               cache_k[l], cache_v[l], page_table, rel_bias)

        mlp_w = (lw["w_ff1"], lw["w_ff2"], lw["ln2_g"], lw["ln2_b"], alpha)
        pps = _fused_pages_per_step(n * t, w_ff1.shape[-1], s, page_table.shape[1])
        x1 = _merge_out(hp, xb, attn.reshape(n * t, aw), cn.reshape(n * t, ch), lw, alpha, 512, emit_bf16=pps is None)
        if pps is None:
            hp_next = _mlp_ln(x1[0], x1[1], *mlp_w, tm=512)
            attn_s = _decode_attention(*dec)
        else:
            hp_next, attn_s = _mlp_ln_decode(x1[0], *mlp_w, *dec, pps)

        cn_s, new_state = _decode_conv(state_conv[l], us.reshape(s, 1, ch), conv_w[l], conv_b[l], conv_ln_g[l],
                                       conv_ln_b[l])
        x1s = _merge_out(hs, xsb, attn_s.reshape(s, aw), cn_s.reshape(s, ch).astype(BF16), lw, alpha, s, emit_bf16=True)
        hs_next = _mlp_ln(x1s[0], x1s[1], *mlp_w, tm=s)
        outs[3].append(ksf.reshape(s, 1, n_heads, dh))
        outs[4].append(vsf.reshape(s, 1, n_heads, dh))
        outs[5].append(new_state)
        hp, hs = hp_next, hs_next

    return (hp.reshape(n, t, d), hs.reshape(s, 1, d)) + tuple(jnp.stack(o) for o in outs)
```

```python
import functools
import math

import numpy as np
import jax
import jax.numpy as jnp
from jax import lax
from jax.experimental import pallas as pl
from jax.experimental.pallas import tpu as pltpu

MOBA_BLOCK = 256
MOBA_TOPK = 3
MAX_DISTANCE = 128
LN_EPS = 1e-5
NEG_INF = -1e30
LOG2_E = math.log2(math.e)

V7X_VMEM_BYTES = 64 * 1024 * 1024
V7X_LANES = 128
V7X_SUBLANES = 8

F32 = jnp.float32
BF16 = jnp.bfloat16


def _vmem_limit(block_bytes):
    return int(min(max(2 * block_bytes + (16 << 20), 32 << 20), V7X_VMEM_BYTES - (4 << 20)))


def _params(semantics, block_bytes):
    return pltpu.CompilerParams(dimension_semantics=semantics, vmem_limit_bytes=_vmem_limit(block_bytes))


def _tile(n, want):
    t = min(n, want)
    while n % t:
        t -= 1
    return t


def _sigmoid(x):
    return 1.0 / (1.0 + jnp.exp(-x))


def _layer_norm(y, g, b):
    mu = jnp.mean(y, axis=-1, keepdims=True)
    d = y - mu
    var = jnp.mean(d * d, axis=-1, keepdims=True)
    return d * lax.rsqrt(var + LN_EPS) * g + b


def _dot(a, b):
    return jnp.dot(a, b, preferred_element_type=F32)


def _dot_nt(a, b):
    return lax.dot_general(a, b, (((1,), (1,)), ((), ())), preferred_element_type=F32)


def _proj_qkv_kernel(x_ref, w_ref, xb_ref, q_ref, kf_ref, kb_ref, vf_ref, vb_ref, *, scale):
    aw = q_ref.shape[1]
    xb = x_ref[...].astype(BF16)
    xb_ref[...] = xb
    q_ref[...] = (_dot(xb, w_ref[:, 0:aw]) * scale).astype(q_ref.dtype)
    for c, (f_ref, b_ref) in enumerate(((kf_ref, kb_ref), (vf_ref, vb_ref)), start=1):
        acc = _dot(xb, w_ref[:, c * aw:(c + 1) * aw])
        f_ref[...] = acc
        b_ref[...] = acc.astype(b_ref.dtype)


def _proj_glu_kernel(x_ref, wv_ref, wg_ref, u_ref):
    x = x_ref[...]
    u_ref[...] = _dot(x, wv_ref[...]) * _sigmoid(_dot(x, wg_ref[...]))


def _proj_qkv(x, w, aw, scale, tm, q_dtype):
    m, d = x.shape
    tm = _tile(m, tm)
    row = lambda width: pl.BlockSpec((tm, width), lambda i: (i, 0))
    blocks = tm * d * 6 + tm * aw * 16 + d * 3 * aw
    return pl.pallas_call(
        functools.partial(_proj_qkv_kernel, scale=scale),
        grid=(m // tm,),
        in_specs=[row(d), pl.BlockSpec((d, 3 * aw), lambda i: (0, 0), pipeline_mode=pl.Buffered(1))],
        out_specs=[row(d)] + [row(aw)] * 5,
        out_shape=[jax.ShapeDtypeStruct((m, d), BF16), jax.ShapeDtypeStruct((m, aw), q_dtype)]
        + [jax.ShapeDtypeStruct((m, aw), dt) for dt in (F32, BF16, F32, BF16)],
        compiler_params=_params(("parallel",), blocks),
        name="proj_qkv",
    )(x, w)


def _proj_glu(xb, w, col_val, col_gate, ncols, tm):
    m, d = xb.shape
    tn = _tile(ncols, 512)
    tm = _tile(m, tm)
    blocks = tm * d * 2 + 2 * d * tn * 2 + tm * tn * 4
    return pl.pallas_call(
        _proj_glu_kernel,
        grid=(m // tm, ncols // tn),
        in_specs=[pl.BlockSpec((tm, d), lambda i, j: (i, 0)),
                  pl.BlockSpec((d, tn), lambda i, j: (0, col_val // tn + j)),
                  pl.BlockSpec((d, tn), lambda i, j: (0, col_gate // tn + j))],
        out_specs=pl.BlockSpec((tm, tn), lambda i, j: (i, j)),
        out_shape=jax.ShapeDtypeStruct((m, ncols), F32),
        compiler_params=_params(("parallel", "parallel"), blocks),
        name="proj_glu",
    )(xb, w, w)


def _rel_bucket_np(dist, num_buckets):
    n = np.maximum(dist, 0)
    max_exact = num_buckets // 2
    nf = np.maximum(n, 1).astype(np.float32)
    large = max_exact + (np.log(nf / np.float32(max_exact)) / np.float32(math.log(MAX_DISTANCE / max_exact))
                         * np.float32(num_buckets - max_exact)).astype(np.int32)
    large = np.minimum(large, num_buckets - 1)
    return np.where(n < max_exact, n, large).astype(np.int32)


def _bias_tiles_kernel(rb_ref, bucket_ref, o_ref, *, num_buckets):
    h = pl.program_id(0)
    for t in range(bucket_ref.shape[0]):
        bucket = bucket_ref[t]
        acc = jnp.full(bucket.shape, NEG_INF, F32)
        for b in range(num_buckets):
            acc = jnp.where(bucket == b, rb_ref[b, h] * LOG2_E, acc)
        o_ref[t] = acc


def _prompt_bias_tiles(rel_bias):
    num_buckets, n_heads = rel_bias.shape
    assert MAX_DISTANCE <= MOBA_BLOCK + 1
    r = np.arange(MOBA_BLOCK)[:, None]
    c = np.arange(MOBA_BLOCK)[None, :]
    diag = np.where(r - c >= 0, _rel_bucket_np(r - c, num_buckets), -1)
    prev = _rel_bucket_np(r - c + MOBA_BLOCK, num_buckets)
    far = _rel_bucket_np(r - c + 2 * MOBA_BLOCK, num_buckets)
    buckets = jnp.asarray(np.stack([diag, prev, far]).astype(np.int32))
    return pl.pallas_call(
        functools.partial(_bias_tiles_kernel, num_buckets=num_buckets),
        grid=(n_heads,),
        in_specs=[pl.BlockSpec(memory_space=pltpu.SMEM),
                  pl.BlockSpec((3, MOBA_BLOCK, MOBA_BLOCK), lambda h: (0, 0, 0))],
        out_specs=pl.BlockSpec((None, 3, MOBA_BLOCK, MOBA_BLOCK), lambda h: (h, 0, 0, 0)),
        out_shape=jax.ShapeDtypeStruct((n_heads, 3, MOBA_BLOCK, MOBA_BLOCK), F32),
        compiler_params=_params(("arbitrary",), 6 * MOBA_BLOCK * MOBA_BLOCK * 4),
        name="bias_tiles",
    )(rel_bias, buckets)


def _block_penalty_t(gate_t, n_valid):
    row = lax.broadcasted_iota(jnp.int32, gate_t.shape, 0)
    valid = row < n_valid
    pen = jnp.zeros(gate_t.shape, F32)
    for j in range(n_valid):
        gj = gate_t[j:j + 1, :]
        beats = ((gate_t > gj) | ((gate_t == gj) & (row < j))) & valid
        rank = jnp.sum(beats.astype(F32), axis=0, keepdims=True)
        pen = jnp.where(row == j, jnp.where(rank < MOBA_TOPK, 0.0, NEG_INF), pen)
    return pen


def _prompt_attn_kernel(q_ref, k_ref, v_ref, bias_ref, o_ref, kx_ref, pen_ref):
    t, dh = q_ref.shape
    blk = MOBA_BLOCK
    nblk = t // blk
    row = lax.broadcasted_iota(jnp.int32, (t, dh), 0)
    col = lax.broadcasted_iota(jnp.int32, (t, dh), 1)
    kx_ref[:, :dh] = k_ref[...]
    kx_ref[:, dh:] = (col == lax.shift_right_logical(row, blk.bit_length() - 1)).astype(BF16)
    in_block = (lax.shift_right_logical(lax.broadcasted_iota(jnp.int32, (nblk, t), 1), blk.bit_length() - 1)
                == lax.broadcasted_iota(jnp.int32, (nblk, t), 0))
    km = _dot(jnp.where(in_block, 1.0 / blk, 0.0).astype(BF16), k_ref[...])
    km_hi = km.astype(BF16)
    km_lo = (km - km_hi.astype(F32)).astype(BF16)
    eye = (lax.broadcasted_iota(jnp.int32, (blk, blk), 0) == lax.broadcasted_iota(jnp.int32, (blk, blk), 1)).astype(BF16)
    pen_ref[...] = jnp.zeros(pen_ref.shape, F32)
    def masked_logits(i):
        qi = q_ref[i * blk:(i + 1) * blk, :]
        nk = (i + 1) * blk
        if i <= MOBA_TOPK:
            return _dot_nt(qi, k_ref[0:nk, :])
        gate_t = _dot_nt(km_hi, qi) + _dot_nt(km_lo, qi)
        pen_ref[0:nblk, :] = _block_penalty_t(gate_t, i)
        pen = _dot_nt(eye, pen_ref[...].astype(BF16)).astype(BF16)
        return _dot_nt(jnp.concatenate([qi, pen], axis=1), kx_ref[0:nk, :])

    s_next = masked_logits(0)
    for i in range(nblk):
        nk = (i + 1) * blk
        s = s_next
        if i + 1 < nblk:
            s_next = masked_logits(i + 1)
        bias = [bias_ref[2]] * (i - 1) + ([bias_ref[1]] if i > 0 else []) + [bias_ref[0]]
        s = s + (jnp.concatenate(bias, axis=1) if len(bias) > 1 else bias[0])
        m = jnp.max(s, axis=1, keepdims=True)
        p = jnp.exp2(s - m)
        l = jnp.sum(p, axis=1, keepdims=True)
        o = _dot(p.astype(BF16), v_ref[0:nk, :])
        o_ref[i * blk:(i + 1) * blk, :] = (o / l).astype(o_ref.dtype)


def _prompt_attention(q, k, v, bias_tiles, n_heads):
    n, t, width = q.shape
    dh = width // n_heads
    nblk = t // MOBA_BLOCK
    assert dh == V7X_LANES and t % MOBA_BLOCK == 0 and nblk <= V7X_LANES and MOBA_BLOCK & (MOBA_BLOCK - 1) == 0
    seq_spec = pl.BlockSpec((None, t, dh), lambda h, b: (b, 0, h))
    blocks = 4 * t * dh * 2 + 3 * MOBA_BLOCK * MOBA_BLOCK * 4 + t * 2 * dh * 2 + 6 * MOBA_BLOCK * t * 4
    return pl.pallas_call(
        _prompt_attn_kernel,
        grid=(n_heads, n),
        in_specs=[seq_spec, seq_spec, seq_spec,
                  pl.BlockSpec((None, 3, MOBA_BLOCK, MOBA_BLOCK), lambda h, b: (h, 0, 0, 0))],
        out_specs=seq_spec,
        out_shape=jax.ShapeDtypeStruct((n, t, width), BF16),
        scratch_shapes=[pltpu.VMEM((t, 2 * dh), BF16), pltpu.VMEM((dh, MOBA_BLOCK), F32)],
        compiler_params=_params(("parallel", "parallel"), blocks),
        name="prompt_attn",
    )(q, k, v, bias_tiles)


DECODE_ROWS = 32


def _lane_sums(xs):
    dh = xs[0].shape[-1]
    r = lax.broadcasted_iota(jnp.int32, (2 * dh, 2 * dh), 0)
    c = lax.broadcasted_iota(jnp.int32, (2 * dh, 2 * dh), 1)
    pair_ones = ((r < dh) == (c < dh)).astype(BF16)
    outs = []
    for i in range(0, len(xs) - 1, 2):
        y = _dot(jnp.concatenate([xs[i].astype(BF16), xs[i + 1].astype(BF16)], axis=1), pair_ones)
        outs += [y[:, :dh], y[:, dh:]]
    if len(xs) % 2:
        outs.append(_dot(xs[-1].astype(BF16), jnp.ones((dh, dh), BF16)))
    return outs


def _page_bias(relb_ref, page_buckets):
    if len(set(page_buckets)) == 1:
        return relb_ref[page_buckets[0]][None] * LOG2_E
    return jnp.stack([relb_ref[b] for b in page_buckets]) * LOG2_E


def _decode_pages(part, q_ref, kn_ref, vn_ref, relb_ref, k_refs, v_refs, o_ref, m_ref, l_ref, g_ref, acc_ref, *,
                  n_pages, pages_per_block, buckets, other_work=()):
    pps = len(k_refs)
    parts = n_pages // pps
    page, h, dh = k_refs[0].shape
    q = q_ref[...]
    rows = min(DECODE_ROWS, page)
    steps_per_page = -(-page // (2 * rows))
    interleave = {}
    for c, job in enumerate(other_work):
        interleave.setdefault(c * pps * steps_per_page // len(other_work), []).append(job)
    for p in range(pps):
        page_buckets = [buckets[(a * pps + p) * page:(a * pps + p + 1) * page] for a in range(parts)]
        shared = set(b for pb in page_buckets for b in pb)
        shared = shared.pop() if len(shared) == 1 else None
        m = jnp.full((h, dh), NEG_INF, F32)
        l = acc = gsum = jnp.zeros((h, dh), F32)
        for r0 in range(0, page, 2 * rows):
            for job in interleave.get(p * steps_per_page + r0 // (2 * rows), ()):
                job()
            groups = [slice(r, r + rows) for r in range(r0, min(r0 + 2 * rows, page), rows)]
            sums = _lane_sums([(k_refs[p][g] * q[None]).reshape(rows * h, dh) for g in groups])
            for g, s in zip(groups, sums):
                s = s.reshape(rows, h, dh)
                gsum = gsum + jnp.sum(s, axis=0)
                if shared is None:
                    bias = _page_bias(relb_ref, page_buckets[0][g])
                    for a in range(1, parts):
                        if page_buckets[a][g] != page_buckets[0][g]:
                            bias = jnp.where(part == a, _page_bias(relb_ref, page_buckets[a][g]), bias)
                    s = s + bias
                m_new = jnp.maximum(m, jnp.max(s, axis=0))
                scale = jnp.exp2(m - m_new)
                e = jnp.exp2(s - m_new[None])
                l = l * scale + jnp.sum(e, axis=0)
                acc = acc * scale + jnp.sum(e * v_refs[p][g], axis=0)
                m = m_new
        idx = part * pps + p
        m_ref[idx] = m if shared is None else m + relb_ref[shared] * LOG2_E
        l_ref[idx] = l
        g_ref[idx] = gsum
        acc_ref[idx] = acc

    def merge():
        n_blocks = n_pages // pages_per_block
        gate = [sum(g_ref[p] for p in range(j * pages_per_block, (j + 1) * pages_per_block)) for j in range(n_blocks)]
        s_own = _lane_sums([q * kn_ref[...]])[0] + relb_ref[0] * LOG2_E
        m_tot = s_own
        sel = []
        for j in range(n_blocks):
            rank = jnp.zeros((h, dh), F32)
            for j2 in range(n_blocks):
                if j2 != j:
                    beats = (gate[j2] > gate[j]) | ((gate[j2] == gate[j]) & (j2 < j))
                    rank = rank + beats.astype(F32)
            sel.append(rank < MOBA_TOPK)
            for p in range(j * pages_per_block, (j + 1) * pages_per_block):
                m_tot = jnp.maximum(m_tot, jnp.where(sel[j], m_ref[p], NEG_INF))
        w_own = jnp.exp2(s_own - m_tot)
        num = w_own * vn_ref[...]
        den = w_own
        for j in range(n_blocks):
            for p in range(j * pages_per_block, (j + 1) * pages_per_block):
                w = jnp.where(sel[j], jnp.exp2(m_ref[p] - m_tot), 0.0)
                num = num + w * acc_ref[p]
                den = den + w * l_ref[p]
        o_ref[...] = (num / den).astype(o_ref.dtype)

    if parts == 1:
        merge()
    else:
        pl.when(part == parts - 1)(merge)


def _decode_attn_kernel(pt_ref, q_ref, kn_ref, vn_ref, relb_ref, *refs, n_pages, pages_per_block, buckets):
    del pt_ref
    _decode_pages(0, q_ref, kn_ref, vn_ref, relb_ref, refs[:n_pages], refs[n_pages:2 * n_pages], *refs[2 * n_pages:],
                  n_pages=n_pages, pages_per_block=pages_per_block, buckets=buckets)


def _decode_specs(page_table, cache_k, rel_bias, h, dh, pages_per_step, seq_of, part_of):
    page = cache_k.shape[1]
    n_pages = page_table.shape[1]
    past = n_pages * page
    num_buckets = rel_bias.shape[0]
    assert dh == V7X_LANES and MOBA_BLOCK % page == 0 and past % MOBA_BLOCK == 0 and n_pages % pages_per_step == 0
    buckets = tuple(int(b) for b in _rel_bucket_np(past - np.arange(past), num_buckets))
    relb = jnp.broadcast_to(rel_bias[:, :, None], (num_buckets, h, dh))
    tok_spec = pl.BlockSpec((None, h, dh), lambda *g: (seq_of(*g[:-1]), 0, 0))
    relb_spec = pl.BlockSpec((num_buckets, h, dh), lambda *g: (0, 0, 0))
    page_specs = [pl.BlockSpec((None, page, h, dh),
                               lambda *g, p=p: (g[-1][seq_of(*g[:-1]), part_of(*g[:-1]) * pages_per_step + p], 0, 0, 0))
                  for p in range(pages_per_step)]
    scratch = [pltpu.VMEM((n_pages, h, dh), F32)] * 4
    return buckets, relb, tok_spec, relb_spec, page_specs, scratch


def _decode_attention(q, k_new, v_new, cache_k, cache_v, page_table, rel_bias):
    s, h, dh = q.shape
    page = cache_k.shape[1]
    n_pages = page_table.shape[1]
    buckets, relb, tok_spec, relb_spec, page_specs, scratch = _decode_specs(
        page_table, cache_k, rel_bias, h, dh, n_pages, seq_of=lambda b: b, part_of=lambda b: 0)
    blocks = 2 * n_pages * page * h * dh * 4 + relb.size * 4
    kernel = functools.partial(_decode_attn_kernel, n_pages=n_pages, pages_per_block=MOBA_BLOCK // page,
                               buckets=buckets)
    return pl.pallas_call(
        kernel,
        grid_spec=pltpu.PrefetchScalarGridSpec(
            num_scalar_prefetch=1,
            grid=(s,),
            in_specs=[tok_spec, tok_spec, tok_spec, relb_spec] + page_specs + page_specs,
            out_specs=tok_spec,
            scratch_shapes=scratch),
        out_shape=jax.ShapeDtypeStruct((s, h, dh), BF16),
        compiler_params=_params(("parallel",), blocks),
        name="decode_attn",
    )(page_table, q, k_new, v_new, relb, *([cache_k] * n_pages), *([cache_v] * n_pages))


CONV_HALO = 32
CONV_ROWS = 64
GLU_CHUNK = 256


def _causal_conv_columns(ext_ref, w_ref, conv_ref, c0, tt):
    kw = w_ref.shape[0]
    first = CONV_HALO - (kw - 1)
    rows = min(CONV_ROWS, tt)
    span = rows + CONV_HALO
    cs = slice(c0, c0 + V7X_LANES)
    for r0 in range(0, tt, rows):
        x = ext_ref[r0:r0 + span, cs]
        acc = jnp.zeros((rows, V7X_LANES), F32)
        for b in range(V7X_SUBLANES):
            taps = [s for s in range(first, first + kw) if s % V7X_SUBLANES == b]
            assert all(s + rows <= span for s in taps)
            xb = x if b == 0 else pltpu.roll(x, span - b, axis=0)
            for s in taps:
                acc = acc + xb[s - b:s - b + rows] * w_ref[s - first:s - first + 1, cs]
        conv_ref[r0:r0 + rows, cs] = acc


def _glu_conv_kernel(x_ref, wv_ref, wg_ref, w_ref, cb_ref, g_ref, b_ref, o_ref, tail_ref, ext_ref, conv_ref):
    tt = x_ref.shape[0]
    ch = wv_ref.shape[1]
    i = pl.program_id(1)

    @pl.when(i == 0)
    def _():
        ext_ref[0:CONV_HALO, :] = jnp.zeros((CONV_HALO, ch), F32)

    @pl.when(i > 0)
    def _():
        ext_ref[0:CONV_HALO, :] = ext_ref[tt:tt + CONV_HALO, :]

    x = x_ref[...]
    chunk = min(GLU_CHUNK, ch)
    for c0 in range(0, ch, chunk):
        cols = slice(c0, c0 + chunk)
        ext_ref[CONV_HALO:CONV_HALO + tt, cols] = _dot(x, wv_ref[:, cols]) * _sigmoid(_dot(x, wg_ref[:, cols]))
        for c in range(c0, c0 + chunk, V7X_LANES):
            _causal_conv_columns(ext_ref, w_ref, conv_ref, c, tt)
    y = _layer_norm(conv_ref[...] + cb_ref[...], g_ref[...], b_ref[...])
    o_ref[...] = (y * _sigmoid(y)).astype(o_ref.dtype)

    @pl.when(i == pl.num_programs(1) - 1)
    def _():
        tail_ref[...] = ext_ref[tt:tt + CONV_HALO, :]


def _prompt_glu_conv(xb, w, col_val, col_gate, conv_w, conv_b, ln_g, ln_b):
    n, t, d = xb.shape
    kw, ch = conv_w.shape
    tt = _tile(t, 256)
    assert kw - 1 <= CONV_HALO <= tt and tt % V7X_SUBLANES == 0 and ch % min(GLU_CHUNK, ch) == 0
    assert col_val % ch == 0 and col_gate % ch == 0
    vec = pl.BlockSpec((1, ch), lambda b, i: (0, 0))
    wspec = lambda col: pl.BlockSpec((d, ch), lambda b, i: (0, col // ch), pipeline_mode=pl.Buffered(1))
    blocks = tt * d * 2 + d * ch * 2 + (2 * tt + 2 * CONV_HALO) * ch * 4 + tt * ch * 2
    return pl.pallas_call(
        _glu_conv_kernel,
        grid=(n, t // tt),
        in_specs=[pl.BlockSpec((None, tt, d), lambda b, i: (b, i, 0)), wspec(col_val), wspec(col_gate),
                  pl.BlockSpec((kw, ch), lambda b, i: (0, 0)), vec, vec, vec],
        out_specs=[pl.BlockSpec((None, tt, ch), lambda b, i: (b, i, 0)),
                   pl.BlockSpec((None, CONV_HALO, ch), lambda b, i: (b, 0, 0))],
        out_shape=[jax.ShapeDtypeStruct((n, t, ch), BF16), jax.ShapeDtypeStruct((n, CONV_HALO, ch), F32)],
        scratch_shapes=[pltpu.VMEM((CONV_HALO + tt, ch), F32), pltpu.VMEM((tt, ch), F32)],
        compiler_params=_params(("parallel", "arbitrary"), blocks),
        name="glu_conv",
    )(xb, w, w, conv_w, conv_b.reshape(1, ch), ln_g.reshape(1, ch), ln_b.reshape(1, ch))


def _decode_conv_kernel(state_ref, u_ref, w_ref, cb_ref, g_ref, b_ref, o_ref, new_state_ref):
    kw = w_ref.shape[0]
    w_hist = w_ref[0:kw - 1, :]
    w_last = w_ref[kw - 1:kw, :]
    for b in range(state_ref.shape[0]):
        u = u_ref[b]
        conv = jnp.sum(state_ref[b] * w_hist, axis=0, keepdims=True) + u * w_last
        y = _layer_norm(conv + cb_ref[...], g_ref[...], b_ref[...])
        o_ref[b] = (y * _sigmoid(y)).astype(o_ref.dtype)
        new_state_ref[b, 0:kw - 2, :] = state_ref[b, 1:kw - 1, :]
        new_state_ref[b, kw - 2:kw - 1, :] = u


def _decode_conv(state, u, conv_w, conv_b, ln_g, ln_b):
    s, hist, ch = state.shape
    kw = conv_w.shape[0]
    ts = _tile(s, 16)
    vec = pl.BlockSpec((1, ch), lambda i: (0, 0))
    tok = pl.BlockSpec((ts, 1, ch), lambda i: (i, 0, 0))
    hist_spec = pl.BlockSpec((ts, hist, ch), lambda i: (i, 0, 0))
    blocks = 2 * ts * 32 * ch * 4 + 2 * ts * V7X_SUBLANES * ch * 4
    return pl.pallas_call(
        _decode_conv_kernel,
        grid=(s // ts,),
        in_specs=[hist_spec, tok, pl.BlockSpec((kw, ch), lambda i: (0, 0)), vec, vec, vec],
        out_specs=[tok, hist_spec],
        out_shape=[jax.ShapeDtypeStruct((s, 1, ch), F32), jax.ShapeDtypeStruct((s, hist, ch), F32)],
        compiler_params=_params(("parallel",), blocks),
        name="decode_conv",
    )(state, u, conv_w, conv_b.reshape(1, ch), ln_g.reshape(1, ch), ln_b.reshape(1, ch))


def _merge_kernel(x_ref, a_ref, c_ref, wga_ref, wgc_ref, wao_ref, wco_ref, o_ref):
    x = x_ref[...]
    mixed = (_sigmoid(_dot(x, wga_ref[...])) * _dot(a_ref[...], wao_ref[...])
             + _sigmoid(_dot(x, wgc_ref[...])) * _dot(c_ref[...], wco_ref[...]))
    o_ref[...] = mixed.astype(o_ref.dtype)


def _merge(xb, attn, cn, w_in, col_ga, col_gc, w_ao, w_co, tm):
    m, d = xb.shape
    wa = attn.shape[1]
    wc = cn.shape[1]
    tn = _tile(d, 512)
    tm = _tile(m, tm)
    blocks = tm * (d + wa + wc + tn) * 2 + (2 * d + wa + wc) * tn * 2
    return pl.pallas_call(
        _merge_kernel,
        grid=(m // tm, d // tn),
        in_specs=[pl.BlockSpec((tm, d), lambda i, j: (i, 0)),
                  pl.BlockSpec((tm, wa), lambda i, j: (i, 0)),
                  pl.BlockSpec((tm, wc), lambda i, j: (i, 0)),
                  pl.BlockSpec((d, tn), lambda i, j: (0, col_ga // tn + j)),
                  pl.BlockSpec((d, tn), lambda i, j: (0, col_gc // tn + j)),
                  pl.BlockSpec((wa, tn), lambda i, j: (0, j)),
                  pl.BlockSpec((wc, tn), lambda i, j: (0, j))],
        out_specs=pl.BlockSpec((tm, tn), lambda i, j: (i, j)),
        out_shape=jax.ShapeDtypeStruct((m, d), BF16),
        compiler_params=_params(("parallel", "parallel"), blocks),
        name="merge",
    )(xb, attn, cn, w_in, w_in, w_ao, w_co)


def _out_ln_kernel(x_ref, mixed_ref, w_ref, g_ref, b_ref, of_ref, *maybe_ob_ref, alpha):
    y = _layer_norm(alpha * x_ref[...] + _dot(mixed_ref[...], w_ref[...]), g_ref[...], b_ref[...])
    of_ref[...] = y
    for ob_ref in maybe_ob_ref:
        ob_ref[...] = y.astype(ob_ref.dtype)


def _out_ln(x, mixed, w_out, g, b, alpha, tm, emit_bf16):
    m, d = x.shape
    tm = _tile(m, tm)
    vec = pl.BlockSpec((1, d), lambda i: (0, 0))
    row = pl.BlockSpec((tm, d), lambda i: (i, 0))
    blocks = tm * d * (4 + 2 + 4 + 2) + d * d * 2
    return pl.pallas_call(
        functools.partial(_out_ln_kernel, alpha=alpha),
        grid=(m // tm,),
        in_specs=[row, row, pl.BlockSpec((d, d), lambda i: (0, 0)), vec, vec],
        out_specs=[row, row] if emit_bf16 else [row],
        out_shape=[jax.ShapeDtypeStruct((m, d), F32)] + ([jax.ShapeDtypeStruct((m, d), BF16)] if emit_bf16 else []),
        compiler_params=_params(("parallel",), blocks),
        name="out_ln",
    )(x, mixed, w_out, g.reshape(1, d), b.reshape(1, d))


def _mlp_ln_kernel(xf_ref, xb_ref, w1_ref, w2_ref, g_ref, b_ref, o_ref, *, alpha):
    f = pl.program_id(1)

    @pl.when(f == 0)
    def _():
        o_ref[...] = alpha * xf_ref[...]

    hid = jnp.maximum(_dot(xb_ref[...], w1_ref[...]), 0.0)
    o_ref[...] += _dot((hid * hid).astype(BF16), w2_ref[...])

    @pl.when(f == pl.num_programs(1) - 1)
    def _():
        o_ref[...] = _layer_norm(o_ref[...], g_ref[...], b_ref[...])


def _mlp_ln(xf, xb, w1, w2, g, b, alpha, tm):
    m, d = xf.shape
    dff = w1.shape[1]
    tf = _tile(dff, 1024)
    tm = _tile(m, tm)
    vec = pl.BlockSpec((1, d), lambda i, f: (0, 0))
    row = pl.BlockSpec((tm, d), lambda i, f: (i, 0))
    blocks = tm * d * (4 + 2 + 4) + 2 * d * tf * 2 + tm * tf * 4
    return pl.pallas_call(
        functools.partial(_mlp_ln_kernel, alpha=alpha),
        grid=(m // tm, dff // tf),
        in_specs=[row, row, pl.BlockSpec((d, tf), lambda i, f: (0, f)), pl.BlockSpec((tf, d), lambda i, f: (f, 0)),
                  vec, vec],
        out_specs=row,
        out_shape=jax.ShapeDtypeStruct((m, d), F32),
        compiler_params=_params(("parallel", "arbitrary"), blocks),
        name="mlp_ln",
    )(xf, xb, w1, w2, g.reshape(1, d), b.reshape(1, d))


def _mlp_ln_decode_kernel(pt_ref, xf_ref, w1_ref, w2_ref, g_ref, b_ref, q_ref, kn_ref, vn_ref, relb_ref, *refs,
                          alpha, pages_per_step, n_pages, pages_per_block, buckets):
    del pt_ref
    k_refs, v_refs = refs[:pages_per_step], refs[pages_per_step:2 * pages_per_step]
    o_ref, ao_ref, xb_ref, hid_ref, m_ref, l_ref, gs_ref, acc_ref = refs[2 * pages_per_step:]
    f = pl.program_id(1)
    nf = pl.num_programs(1)

    @pl.when(f == 0)
    def _():
        x = xf_ref[...]
        o_ref[...] = alpha * x
        xb_ref[...] = x.astype(BF16)

    tf, d = w2_ref.shape
    wh = math.gcd(tf, MLP_CHUNK)
    wo = math.gcd(d, MLP_CHUNK)

    def hid_chunk(c):
        hid = jnp.maximum(_dot(xb_ref[...], w1_ref[:, c * wh:(c + 1) * wh]), 0.0)
        hid_ref[:, c * wh:(c + 1) * wh] = (hid * hid).astype(BF16)

    def out_chunk(c):
        o_ref[:, c * wo:(c + 1) * wo] += _dot(hid_ref[...], w2_ref[:, c * wo:(c + 1) * wo])

    chunks = ([functools.partial(hid_chunk, c) for c in range(tf // wh)]
              + [functools.partial(out_chunk, c) for c in range(d // wo)])
    part = lax.rem(pl.program_id(0) * nf + f, n_pages // pages_per_step)
    _decode_pages(part, q_ref, kn_ref, vn_ref, relb_ref, k_refs, v_refs, ao_ref, m_ref, l_ref, gs_ref, acc_ref,
                  n_pages=n_pages, pages_per_block=pages_per_block, buckets=buckets, other_work=chunks)

    @pl.when(f == nf - 1)
    def _():
        o_ref[...] = _layer_norm(o_ref[...], g_ref[...], b_ref[...])


MLP_CHUNK = 256
FUSED_MLP_ROWS = 512
FUSED_MLP_COLS = 1024


def _fused_pages_per_step(m, dff, s, n_pages):
    steps = (m // _tile(m, FUSED_MLP_ROWS)) * (dff // _tile(dff, FUSED_MLP_COLS))
    if (s * n_pages) % steps or n_pages % ((s * n_pages) // steps):
        return None
    return (s * n_pages) // steps


def _mlp_ln_decode(xf, w1, w2, g, b, alpha, q, k_new, v_new, cache_k, cache_v, page_table, rel_bias, pps):
    m, d = xf.shape
    dff = w1.shape[1]
    tm, tf = _tile(m, FUSED_MLP_ROWS), _tile(dff, FUSED_MLP_COLS)
    nf = dff // tf
    s, h, dh = q.shape
    page = cache_k.shape[1]
    n_pages = page_table.shape[1]
    parts = n_pages // pps
    assert (m // tm) * nf == s * parts
    buckets, relb, tok_spec, relb_spec, page_specs, scratch = _decode_specs(
        page_table, cache_k, rel_bias, h, dh, pps,
        seq_of=lambda i, f: (i * nf + f) // parts, part_of=lambda i, f: (i * nf + f) % parts)
    vec = pl.BlockSpec((1, d), lambda i, f, pt: (0, 0))
    row = pl.BlockSpec((tm, d), lambda i, f, pt: (i, 0))
    blocks = tm * d * 9 + 2 * d * tf * 2 + tm * tf * 4 + 2 * pps * page * h * dh * 4 + relb.size * 4
    kernel = functools.partial(_mlp_ln_decode_kernel, alpha=alpha, pages_per_step=pps, n_pages=n_pages,
                               pages_per_block=MOBA_BLOCK // page, buckets=buckets)
    return pl.pallas_call(
        kernel,
        grid_spec=pltpu.PrefetchScalarGridSpec(
            num_scalar_prefetch=1,
            grid=(m // tm, nf),
            in_specs=[row, pl.BlockSpec((d, tf), lambda i, f, pt: (0, f)), pl.BlockSpec((tf, d), lambda i, f, pt: (f, 0)),
                      vec, vec, tok_spec, tok_spec, tok_spec, relb_spec] + page_specs + page_specs,
            out_specs=[row, tok_spec],
            scratch_shapes=[pltpu.VMEM((tm, d), BF16), pltpu.VMEM((tm, tf), BF16)] + scratch),
        out_shape=[jax.ShapeDtypeStruct((m, d), F32), jax.ShapeDtypeStruct((s, h, dh), BF16)],
        compiler_params=_params(("arbitrary", "arbitrary"), blocks),
        name="mlp_ln_decode_attn",
    )(page_table, xf, w1, w2, g.reshape(1, d), b.reshape(1, d), q, k_new, v_new, relb,
      *([cache_k] * pps), *([cache_v] * pps))


def _merge_out(x2d, xb, attn, cn, lw, alpha, tm, emit_bf16):
    mixed = _merge(xb, attn, cn, lw["w_in"], lw["col_ga"], lw["col_gc"], lw["w_attn_out"], lw["w_conv_out"], tm=2 * tm)
    return _out_ln(x2d, mixed, lw["w_out"], lw["ln1_g"], lw["ln1_b"], alpha, tm, emit_bf16)


def kernel(x_prompt, x_sample, cache_k, cache_v, state_conv, page_table, rel_bias, w_in, w_attn_out, conv_w, conv_b,
           conv_ln_g, conv_ln_b, w_conv_out, w_out, ln1_g, ln1_b, w_ff1, w_ff2, ln2_g, ln2_b):
    depth = w_in.shape[0]
    n, t, d = x_prompt.shape
    s, ts, _ = x_sample.shape
    n_heads, dh = cache_k.shape[-2:]
    aw = n_heads * dh
    ch = conv_w.shape[-1]
    alpha = (2 * depth) ** 0.25
    scale = dh ** -0.5 * LOG2_E
    assert ts == 1 and w_in.shape[-1] == 3 * aw + 2 * ch + 2 * d
    col_val, col_gate = 3 * aw, 3 * aw + ch
    col_ga, col_gc = 3 * aw + 2 * ch, 3 * aw + 2 * ch + d

    bias_tiles = _prompt_bias_tiles(rel_bias)
    hp = x_prompt.reshape(n * t, d)
    hs = x_sample.reshape(s, d)
    outs = [[] for _ in range(6)]
    for l in range(depth):
        lw = dict(w_in=w_in[l].astype(BF16), w_attn_out=w_attn_out[l].astype(BF16),
                  w_conv_out=w_conv_out[l].astype(BF16), w_out=w_out[l].astype(BF16),
                  w_ff1=w_ff1[l].astype(BF16), w_ff2=w_ff2[l].astype(BF16),
                  ln1_g=ln1_g[l], ln1_b=ln1_b[l], ln2_g=ln2_g[l], ln2_b=ln2_b[l], col_ga=col_ga, col_gc=col_gc)
        wi = lw["w_in"]

        xb, q, kf, kb, vf, vb = _proj_qkv(hp, wi, aw, scale, tm=512, q_dtype=BF16)
        attn = _prompt_attention(q.reshape(n, t, aw), kb.reshape(n, t, aw), vb.reshape(n, t, aw), bias_tiles, n_heads)
        cn, u_tail = _prompt_glu_conv(xb.reshape(n, t, d), wi, col_val, col_gate, conv_w[l], conv_b[l],
                                      conv_ln_g[l], conv_ln_b[l])
        outs[0].append(kf.reshape(n, t, n_heads, dh))
        outs[1].append(vf.reshape(n, t, n_heads, dh))
        outs[2].append(u_tail[:, CONV_HALO - (conv_w.shape[1] - 1):, :])

        xsb, qs, ksf, _, vsf, _ = _proj_qkv(hs, wi, aw, scale, tm=s, q_dtype=F32)
        us = _proj_glu(xsb, wi, col_val, col_gate, ch, tm=s)
        dec = (qs.reshape(s, n_heads, dh), ksf.reshape(s, n_heads, dh), vsf.reshape(s, n_heads, dh),
               cache_k[l], cache_v[l], page_table, rel_bias)

        mlp_w = (lw["w_ff1"], lw["w_ff2"], lw["ln2_g"], lw["ln2_b"], alpha)
        pps = _fused_pages_per_step(n * t, w_ff1.shape[-1], s, page_table.shape[1])
        x1 = _merge_out(hp, xb, attn.reshape(n * t, aw), cn.reshape(n * t, ch), lw, alpha, 512, emit_bf16=pps is None)
        if pps is None:
            hp_next = _mlp_ln(x1[0], x1[1], *mlp_w, tm=512)
            attn_s = _decode_attention(*dec)
        else:
            hp_next, attn_s = _mlp_ln_decode(x1[0], *mlp_w, *dec, pps)

        cn_s, new_state = _decode_conv(state_conv[l], us.reshape(s, 1, ch), conv_w[l], conv_b[l], conv_ln_g[l],
                                       conv_ln_b[l])
        x1s = _merge_out(hs, xsb, attn_s.reshape(s, aw), cn_s.reshape(s, ch).astype(BF16), lw, alpha, s, emit_bf16=True)
        hs_next = _mlp_ln(x1s[0], x1s[1], *mlp_w, tm=s)
        outs[3].append(ksf.reshape(s, 1, n_heads, dh))
        outs[4].append(vsf.reshape(s, 1, n_heads, dh))
        outs[5].append(new_state)
        hp, hs = hp_next, hs_next

    return (hp.reshape(n, t, d), hs.reshape(s, 1, d)) + tuple(jnp.stack(o) for o in outs)
```

```python
import functools
import math

import numpy as np
import jax
import jax.numpy as jnp
from jax import lax
from jax.experimental import pallas as pl
from jax.experimental.pallas import tpu as pltpu

MOBA_BLOCK = 256
MOBA_TOPK = 3
MAX_DISTANCE = 128
LN_EPS = 1e-5
NEG_INF = -1e30
LOG2_E = math.log2(math.e)

V7X_VMEM_BYTES = 64 * 1024 * 1024
V7X_LANES = 128
V7X_SUBLANES = 8

F32 = jnp.float32
BF16 = jnp.bfloat16


def _vmem_limit(block_bytes):
    return int(min(max(2 * block_bytes + (16 << 20), 32 << 20), V7X_VMEM_BYTES - (4 << 20)))


def _params(semantics, block_bytes):
    return pltpu.CompilerParams(dimension_semantics=semantics, vmem_limit_bytes=_vmem_limit(block_bytes))


def _tile(n, want):
    t = min(n, want)
    while n % t:
        t -= 1
    return t


def _sigmoid(x):
    return 1.0 / (1.0 + jnp.exp(-x))


def _layer_norm(y, g, b):
    mu = jnp.mean(y, axis=-1, keepdims=True)
    d = y - mu
    var = jnp.mean(d * d, axis=-1, keepdims=True)
    return d * lax.rsqrt(var + LN_EPS) * g + b


def _dot(a, b):
    return jnp.dot(a, b, preferred_element_type=F32)


def _dot_nt(a, b):
    return lax.dot_general(a, b, (((1,), (1,)), ((), ())), preferred_element_type=F32)


def _proj_qkv_kernel(x_ref, w_ref, xb_ref, q_ref, kf_ref, kb_ref, vf_ref, vb_ref, *, scale):
    aw = q_ref.shape[1]
    xb = x_ref[...].astype(BF16)
    xb_ref[...] = xb
    q_ref[...] = (_dot(xb, w_ref[:, 0:aw]) * scale).astype(q_ref.dtype)
    for c, (f_ref, b_ref) in enumerate(((kf_ref, kb_ref), (vf_ref, vb_ref)), start=1):
        acc = _dot(xb, w_ref[:, c * aw:(c + 1) * aw])
        f_ref[...] = acc
        b_ref[...] = acc.astype(b_ref.dtype)


def _proj_glu_kernel(x_ref, wv_ref, wg_ref, u_ref):
    x = x_ref[...]
    u_ref[...] = _dot(x, wv_ref[...]) * _sigmoid(_dot(x, wg_ref[...]))


def _proj_qkv(x, w, aw, scale, tm, q_dtype):
    m, d = x.shape
    tm = _tile(m, tm)
    row = lambda width: pl.BlockSpec((tm, width), lambda i: (i, 0))
    blocks = tm * d * 6 + tm * aw * 16 + d * 3 * aw
    return pl.pallas_call(
        functools.partial(_proj_qkv_kernel, scale=scale),
        grid=(m // tm,),
        in_specs=[row(d), pl.BlockSpec((d, 3 * aw), lambda i: (0, 0), pipeline_mode=pl.Buffered(1))],
        out_specs=[row(d)] + [row(aw)] * 5,
        out_shape=[jax.ShapeDtypeStruct((m, d), BF16), jax.ShapeDtypeStruct((m, aw), q_dtype)]
        + [jax.ShapeDtypeStruct((m, aw), dt) for dt in (F32, BF16, F32, BF16)],
        compiler_params=_params(("parallel",), blocks),
        name="proj_qkv",
    )(x, w)


def _proj_glu(xb, w, col_val, col_gate, ncols, tm):
    m, d = xb.shape
    tn = _tile(ncols, 512)
    tm = _tile(m, tm)
    blocks = tm * d * 2 + 2 * d * tn * 2 + tm * tn * 4
    return pl.pallas_call(
        _proj_glu_kernel,
        grid=(m // tm, ncols // tn),
        in_specs=[pl.BlockSpec((tm, d), lambda i, j: (i, 0)),
                  pl.BlockSpec((d, tn), lambda i, j: (0, col_val // tn + j)),
                  pl.BlockSpec((d, tn), lambda i, j: (0, col_gate // tn + j))],
        out_specs=pl.BlockSpec((tm, tn), lambda i, j: (i, j)),
        out_shape=jax.ShapeDtypeStruct((m, ncols), F32),
        compiler_params=_params(("parallel", "parallel"), blocks),
        name="proj_glu",
    )(xb, w, w)


def _rel_bucket_np(dist, num_buckets):
    n = np.maximum(dist, 0)
    max_exact = num_buckets // 2
    nf = np.maximum(n, 1).astype(np.float32)
    large = max_exact + (np.log(nf / np.float32(max_exact)) / np.float32(math.log(MAX_DISTANCE / max_exact))
                         * np.float32(num_buckets - max_exact)).astype(np.int32)
    large = np.minimum(large, num_buckets - 1)
    return np.where(n < max_exact, n, large).astype(np.int32)


def _bias_tiles_kernel(rb_ref, bucket_ref, o_ref, *, num_buckets):
    h = pl.program_id(0)
    for t in range(bucket_ref.shape[0]):
        bucket = bucket_ref[t]
        acc = jnp.full(bucket.shape, NEG_INF, F32)
        for b in range(num_buckets):
            acc = jnp.where(bucket == b, rb_ref[b, h] * LOG2_E, acc)
        o_ref[t] = acc


def _prompt_bias_tiles(rel_bias):
    num_buckets, n_heads = rel_bias.shape
    assert MAX_DISTANCE <= MOBA_BLOCK + 1
    r = np.arange(MOBA_BLOCK)[:, None]
    c = np.arange(MOBA_BLOCK)[None, :]
    diag = np.where(r - c >= 0, _rel_bucket_np(r - c, num_buckets), -1)
    prev = _rel_bucket_np(r - c + MOBA_BLOCK, num_buckets)
    far = _rel_bucket_np(r - c + 2 * MOBA_BLOCK, num_buckets)
    buckets = jnp.asarray(np.stack([diag, prev, far]).astype(np.int32))
    return pl.pallas_call(
        functools.partial(_bias_tiles_kernel, num_buckets=num_buckets),
        grid=(n_heads,),
        in_specs=[pl.BlockSpec(memory_space=pltpu.SMEM),
                  pl.BlockSpec((3, MOBA_BLOCK, MOBA_BLOCK), lambda h: (0, 0, 0))],
        out_specs=pl.BlockSpec((None, 3, MOBA_BLOCK, MOBA_BLOCK), lambda h: (h, 0, 0, 0)),
        out_shape=jax.ShapeDtypeStruct((n_heads, 3, MOBA_BLOCK, MOBA_BLOCK), F32),
        compiler_params=_params(("arbitrary",), 6 * MOBA_BLOCK * MOBA_BLOCK * 4),
        name="bias_tiles",
    )(rel_bias, buckets)


def _block_penalty_t(gate_t, n_valid):
    row = lax.broadcasted_iota(jnp.int32, gate_t.shape, 0)
    valid = row < n_valid
    pen = jnp.zeros(gate_t.shape, F32)
    for j in range(n_valid):
        gj = gate_t[j:j + 1, :]
        beats = ((gate_t > gj) | ((gate_t == gj) & (row < j))) & valid
        rank = jnp.sum(beats.astype(F32), axis=0, keepdims=True)
        pen = jnp.where(row == j, jnp.where(rank < MOBA_TOPK, 0.0, NEG_INF), pen)
    return pen


def _prompt_attn_kernel(q_ref, k_ref, v_ref, bias_ref, o_ref, kx_ref, pen_ref):
    t, dh = q_ref.shape
    blk = MOBA_BLOCK
    nblk = t // blk
    row = lax.broadcasted_iota(jnp.int32, (t, dh), 0)
    col = lax.broadcasted_iota(jnp.int32, (t, dh), 1)
    kx_ref[:, :dh] = k_ref[...]
    kx_ref[:, dh:] = (col == lax.shift_right_logical(row, blk.bit_length() - 1)).astype(BF16)
    in_block = (lax.shift_right_logical(lax.broadcasted_iota(jnp.int32, (nblk, t), 1), blk.bit_length() - 1)
                == lax.broadcasted_iota(jnp.int32, (nblk, t), 0))
    km = _dot(jnp.where(in_block, 1.0 / blk, 0.0).astype(BF16), k_ref[...])
    km_hi = km.astype(BF16)
    km_lo = (km - km_hi.astype(F32)).astype(BF16)
    eye = (lax.broadcasted_iota(jnp.int32, (blk, blk), 0) == lax.broadcasted_iota(jnp.int32, (blk, blk), 1)).astype(BF16)
    pen_ref[...] = jnp.zeros(pen_ref.shape, F32)
    def masked_logits(i):
        qi = q_ref[i * blk:(i + 1) * blk, :]
        nk = (i + 1) * blk
        if i <= MOBA_TOPK:
            return _dot_nt(qi, k_ref[0:nk, :])
        gate_t = _dot_nt(km_hi, qi) + _dot_nt(km_lo, qi)
        pen_ref[0:nblk, :] = _block_penalty_t(gate_t, i)
        pen = _dot_nt(eye, pen_ref[...].astype(BF16)).astype(BF16)
        return _dot_nt(jnp.concatenate([qi, pen], axis=1), kx_ref[0:nk, :])

    s_next = masked_logits(0)
    for i in range(nblk):
        nk = (i + 1) * blk
        s = s_next
        if i + 1 < nblk:
            s_next = masked_logits(i + 1)
        bias = [bias_ref[2]] * (i - 1) + ([bias_ref[1]] if i > 0 else []) + [bias_ref[0]]
        s = s + (jnp.concatenate(bias, axis=1) if len(bias) > 1 else bias[0])
        m = jnp.max(s, axis=1, keepdims=True)
        p = jnp.exp2(s - m)
        l = jnp.sum(p, axis=1, keepdims=True)
        o = _dot(p.astype(BF16), v_ref[0:nk, :])
        o_ref[i * blk:(i + 1) * blk, :] = (o / l).astype(o_ref.dtype)


def _prompt_attention(q, k, v, bias_tiles, n_heads):
    n, t, width = q.shape
    dh = width // n_heads
    nblk = t // MOBA_BLOCK
    assert dh == V7X_LANES and t % MOBA_BLOCK == 0 and nblk <= V7X_LANES and MOBA_BLOCK & (MOBA_BLOCK - 1) == 0
    seq_spec = pl.BlockSpec((None, t, dh), lambda h, b: (b, 0, h))
    blocks = 4 * t * dh * 2 + 3 * MOBA_BLOCK * MOBA_BLOCK * 4 + t * 2 * dh * 2 + 6 * MOBA_BLOCK * t * 4
    return pl.pallas_call(
        _prompt_attn_kernel,
        grid=(n_heads, n),
        in_specs=[seq_spec, seq_spec, seq_spec,
                  pl.BlockSpec((None, 3, MOBA_BLOCK, MOBA_BLOCK), lambda h, b: (h, 0, 0, 0))],
        out_specs=seq_spec,
        out_shape=jax.ShapeDtypeStruct((n, t, width), BF16),
        scratch_shapes=[pltpu.VMEM((t, 2 * dh), BF16), pltpu.VMEM((dh, MOBA_BLOCK), F32)],
        compiler_params=_params(("parallel", "parallel"), blocks),
        name="prompt_attn",
    )(q, k, v, bias_tiles)


DECODE_ROWS = 32


def _lane_sum(x):
    return jnp.broadcast_to(jnp.sum(x, axis=-1, keepdims=True), x.shape)


def _page_bias(relb_ref, page_buckets):
    if len(set(page_buckets)) == 1:
        return relb_ref[page_buckets[0]][None] * LOG2_E
    return jnp.stack([relb_ref[b] for b in page_buckets]) * LOG2_E


def _decode_pages(part, q_ref, kn_ref, vn_ref, relb_ref, k_refs, v_refs, o_ref, m_ref, l_ref, g_ref, acc_ref, *,
                  n_pages, pages_per_block, buckets, other_work=()):
    pps = len(k_refs)
    parts = n_pages // pps
    page, h, dh = k_refs[0].shape
    q = q_ref[...]
    rows = min(DECODE_ROWS, page)
    steps_per_page = -(-page // (2 * rows))
    interleave = {}
    for c, job in enumerate(other_work):
        interleave.setdefault(c * pps * steps_per_page // len(other_work), []).append(job)
    for p in range(pps):
        page_buckets = [buckets[(a * pps + p) * page:(a * pps + p + 1) * page] for a in range(parts)]
        shared = set(b for pb in page_buckets for b in pb)
        shared = shared.pop() if len(shared) == 1 else None
        m = jnp.full((h, dh), NEG_INF, F32)
        l = acc = gsum = jnp.zeros((h, dh), F32)
        for r0 in range(0, page, 2 * rows):
            for job in interleave.get(p * steps_per_page + r0 // (2 * rows), ()):
                job()
            groups = [slice(r, r + rows) for r in range(r0, min(r0 + 2 * rows, page), rows)]
            for g in groups:
                s = _lane_sum(k_refs[p][g] * q[None])
                gsum = gsum + jnp.sum(s, axis=0)
                if shared is None:
                    bias = _page_bias(relb_ref, page_buckets[0][g])
                    for a in range(1, parts):
                        if page_buckets[a][g] != page_buckets[0][g]:
                            bias = jnp.where(part == a, _page_bias(relb_ref, page_buckets[a][g]), bias)
                    s = s + bias
                m_new = jnp.maximum(m, jnp.max(s, axis=0))
                scale = jnp.exp2(m - m_new)
                e = jnp.exp2(s - m_new[None])
                l = l * scale + jnp.sum(e, axis=0)
                acc = acc * scale + jnp.sum(e * v_refs[p][g], axis=0)
                m = m_new
        idx = part * pps + p
        m_ref[idx] = m if shared is None else m + relb_ref[shared] * LOG2_E
        l_ref[idx] = l
        g_ref[idx] = gsum
        acc_ref[idx] = acc

    def merge():
        n_blocks = n_pages // pages_per_block
        gate = [sum(g_ref[p] for p in range(j * pages_per_block, (j + 1) * pages_per_block)) for j in range(n_blocks)]
        s_own = _lane_sum(q * kn_ref[...]) + relb_ref[0] * LOG2_E
        m_tot = s_own
        sel = []
        for j in range(n_blocks):
            rank = jnp.zeros((h, dh), F32)
            for j2 in range(n_blocks):
                if j2 != j:
                    beats = (gate[j2] > gate[j]) | ((gate[j2] == gate[j]) & (j2 < j))
                    rank = rank + beats.astype(F32)
            sel.append(rank < MOBA_TOPK)
            for p in range(j * pages_per_block, (j + 1) * pages_per_block):
                m_tot = jnp.maximum(m_tot, jnp.where(sel[j], m_ref[p], NEG_INF))
        w_own = jnp.exp2(s_own - m_tot)
        num = w_own * vn_ref[...]
        den = w_own
        for j in range(n_blocks):
            for p in range(j * pages_per_block, (j + 1) * pages_per_block):
                w = jnp.where(sel[j], jnp.exp2(m_ref[p] - m_tot), 0.0)
                num = num + w * acc_ref[p]
                den = den + w * l_ref[p]
        o_ref[...] = (num / den).astype(o_ref.dtype)

    if parts == 1:
        merge()
    else:
        pl.when(part == parts - 1)(merge)


def _decode_attn_kernel(pt_ref, q_ref, kn_ref, vn_ref, relb_ref, *refs, n_pages, pages_per_block, buckets):
    del pt_ref
    _decode_pages(0, q_ref, kn_ref, vn_ref, relb_ref, refs[:n_pages], refs[n_pages:2 * n_pages], *refs[2 * n_pages:],
                  n_pages=n_pages, pages_per_block=pages_per_block, buckets=buckets)


def _decode_specs(page_table, cache_k, rel_bias, h, dh, pages_per_step, seq_of, part_of):
    page = cache_k.shape[1]
    n_pages = page_table.shape[1]
    past = n_pages * page
    num_buckets = rel_bias.shape[0]
    assert dh == V7X_LANES and MOBA_BLOCK % page == 0 and past % MOBA_BLOCK == 0 and n_pages % pages_per_step == 0
    buckets = tuple(int(b) for b in _rel_bucket_np(past - np.arange(past), num_buckets))
    relb = jnp.broadcast_to(rel_bias[:, :, None], (num_buckets, h, dh))
    tok_spec = pl.BlockSpec((None, h, dh), lambda *g: (seq_of(*g[:-1]), 0, 0))
    relb_spec = pl.BlockSpec((num_buckets, h, dh), lambda *g: (0, 0, 0))
    page_specs = [pl.BlockSpec((None, page, h, dh),
                               lambda *g, p=p: (g[-1][seq_of(*g[:-1]), part_of(*g[:-1]) * pages_per_step + p], 0, 0, 0))
                  for p in range(pages_per_step)]
    scratch = [pltpu.VMEM((n_pages, h, dh), F32)] * 4
    return buckets, relb, tok_spec, relb_spec, page_specs, scratch


def _decode_attention(q, k_new, v_new, cache_k, cache_v, page_table, rel_bias):
    s, h, dh = q.shape
    page = cache_k.shape[1]
    n_pages = page_table.shape[1]
    buckets, relb, tok_spec, relb_spec, page_specs, scratch = _decode_specs(
        page_table, cache_k, rel_bias, h, dh, n_pages, seq_of=lambda b: b, part_of=lambda b: 0)
    blocks = 2 * n_pages * page * h * dh * 4 + relb.size * 4
    kernel = functools.partial(_decode_attn_kernel, n_pages=n_pages, pages_per_block=MOBA_BLOCK // page,
                               buckets=buckets)
    return pl.pallas_call(
        kernel,
        grid_spec=pltpu.PrefetchScalarGridSpec(
            num_scalar_prefetch=1,
            grid=(s,),
            in_specs=[tok_spec, tok_spec, tok_spec, relb_spec] + page_specs + page_specs,
            out_specs=tok_spec,
            scratch_shapes=scratch),
        out_shape=jax.ShapeDtypeStruct((s, h, dh), BF16),
        compiler_params=_params(("parallel",), blocks),
        name="decode_attn",
    )(page_table, q, k_new, v_new, relb, *([cache_k] * n_pages), *([cache_v] * n_pages))


CONV_HALO = 32
CONV_ROWS = 64
GLU_CHUNK = 256


def _causal_conv_columns(ext_ref, w_ref, conv_ref, c0, tt):
    kw = w_ref.shape[0]
    first = CONV_HALO - (kw - 1)
    rows = min(CONV_ROWS, tt)
    span = rows + CONV_HALO
    cs = slice(c0, c0 + V7X_LANES)
    for r0 in range(0, tt, rows):
        x = ext_ref[r0:r0 + span, cs]
        acc = jnp.zeros((rows, V7X_LANES), F32)
        for b in range(V7X_SUBLANES):
            taps = [s for s in range(first, first + kw) if s % V7X_SUBLANES == b]
            assert all(s + rows <= span for s in taps)
            xb = x if b == 0 else pltpu.roll(x, span - b, axis=0)
            for s in taps:
                acc = acc + xb[s - b:s - b + rows] * w_ref[s - first:s - first + 1, cs]
        conv_ref[r0:r0 + rows, cs] = acc


def _glu_conv_kernel(x_ref, wv_ref, wg_ref, w_ref, cb_ref, g_ref, b_ref, o_ref, tail_ref, ext_ref, conv_ref):
    tt = x_ref.shape[0]
    ch = wv_ref.shape[1]
    i = pl.program_id(1)

    @pl.when(i == 0)
    def _():
        ext_ref[0:CONV_HALO, :] = jnp.zeros((CONV_HALO, ch), F32)

    @pl.when(i > 0)
    def _():
        ext_ref[0:CONV_HALO, :] = ext_ref[tt:tt + CONV_HALO, :]

    x = x_ref[...]
    chunk = min(GLU_CHUNK, ch)
    for c0 in range(0, ch, chunk):
        cols = slice(c0, c0 + chunk)
        ext_ref[CONV_HALO:CONV_HALO + tt, cols] = _dot(x, wv_ref[:, cols]) * _sigmoid(_dot(x, wg_ref[:, cols]))
        for c in range(c0, c0 + chunk, V7X_LANES):
            _causal_conv_columns(ext_ref, w_ref, conv_ref, c, tt)
    y = _layer_norm(conv_ref[...] + cb_ref[...], g_ref[...], b_ref[...])
    o_ref[...] = (y * _sigmoid(y)).astype(o_ref.dtype)

    @pl.when(i == pl.num_programs(1) - 1)
    def _():
        tail_ref[...] = ext_ref[tt:tt + CONV_HALO, :]


def _prompt_glu_conv(xb, w, col_val, col_gate, conv_w, conv_b, ln_g, ln_b):
    n, t, d = xb.shape
    kw, ch = conv_w.shape
    tt = _tile(t, 256)
    assert kw - 1 <= CONV_HALO <= tt and tt % V7X_SUBLANES == 0 and ch % min(GLU_CHUNK, ch) == 0
    assert col_val % ch == 0 and col_gate % ch == 0
    vec = pl.BlockSpec((1, ch), lambda b, i: (0, 0))
    wspec = lambda col: pl.BlockSpec((d, ch), lambda b, i: (0, col // ch), pipeline_mode=pl.Buffered(1))
    blocks = tt * d * 2 + d * ch * 2 + (2 * tt + 2 * CONV_HALO) * ch * 4 + tt * ch * 2
    return pl.pallas_call(
        _glu_conv_kernel,
        grid=(n, t // tt),
        in_specs=[pl.BlockSpec((None, tt, d), lambda b, i: (b, i, 0)), wspec(col_val), wspec(col_gate),
                  pl.BlockSpec((kw, ch), lambda b, i: (0, 0)), vec, vec, vec],
        out_specs=[pl.BlockSpec((None, tt, ch), lambda b, i: (b, i, 0)),
                   pl.BlockSpec((None, CONV_HALO, ch), lambda b, i: (b, 0, 0))],
        out_shape=[jax.ShapeDtypeStruct((n, t, ch), BF16), jax.ShapeDtypeStruct((n, CONV_HALO, ch), F32)],
        scratch_shapes=[pltpu.VMEM((CONV_HALO + tt, ch), F32), pltpu.VMEM((tt, ch), F32)],
        compiler_params=_params(("parallel", "arbitrary"), blocks),
        name="glu_conv",
    )(xb, w, w, conv_w, conv_b.reshape(1, ch), ln_g.reshape(1, ch), ln_b.reshape(1, ch))


def _decode_conv_kernel(state_ref, u_ref, w_ref, cb_ref, g_ref, b_ref, o_ref, new_state_ref):
    kw = w_ref.shape[0]
    w_hist = w_ref[0:kw - 1, :]
    w_last = w_ref[kw - 1:kw, :]
    for b in range(state_ref.shape[0]):
        u = u_ref[b]
        conv = jnp.sum(state_ref[b] * w_hist, axis=0, keepdims=True) + u * w_last
        y = _layer_norm(conv + cb_ref[...], g_ref[...], b_ref[...])
        o_ref[b] = (y * _sigmoid(y)).astype(o_ref.dtype)
        new_state_ref[b, 0:kw - 2, :] = state_ref[b, 1:kw - 1, :]
        new_state_ref[b, kw - 2:kw - 1, :] = u


def _decode_conv(state, u, conv_w, conv_b, ln_g, ln_b):
    s, hist, ch = state.shape
    kw = conv_w.shape[0]
    ts = _tile(s, 16)
    vec = pl.BlockSpec((1, ch), lambda i: (0, 0))
    tok = pl.BlockSpec((ts, 1, ch), lambda i: (i, 0, 0))
    hist_spec = pl.BlockSpec((ts, hist, ch), lambda i: (i, 0, 0))
    blocks = 2 * ts * 32 * ch * 4 + 2 * ts * V7X_SUBLANES * ch * 4
    return pl.pallas_call(
        _decode_conv_kernel,
        grid=(s // ts,),
        in_specs=[hist_spec, tok, pl.BlockSpec((kw, ch), lambda i: (0, 0)), vec, vec, vec],
        out_specs=[tok, hist_spec],
        out_shape=[jax.ShapeDtypeStruct((s, 1, ch), F32), jax.ShapeDtypeStruct((s, hist, ch), F32)],
        compiler_params=_params(("parallel",), blocks),
        name="decode_conv",
    )(state, u, conv_w, conv_b.reshape(1, ch), ln_g.reshape(1, ch), ln_b.reshape(1, ch))


def _merge_kernel(x_ref, a_ref, c_ref, wga_ref, wgc_ref, wao_ref, wco_ref, o_ref):
    x = x_ref[...]
    mixed = (_sigmoid(_dot(x, wga_ref[...])) * _dot(a_ref[...], wao_ref[...])
             + _sigmoid(_dot(x, wgc_ref[...])) * _dot(c_ref[...], wco_ref[...]))
    o_ref[...] = mixed.astype(o_ref.dtype)


def _merge(xb, attn, cn, w_in, col_ga, col_gc, w_ao, w_co, tm):
    m, d = xb.shape
    wa = attn.shape[1]
    wc = cn.shape[1]
    tn = _tile(d, 512)
    tm = _tile(m, tm)
    blocks = tm * (d + wa + wc + tn) * 2 + (2 * d + wa + wc) * tn * 2
    return pl.pallas_call(
        _merge_kernel,
        grid=(m // tm, d // tn),
        in_specs=[pl.BlockSpec((tm, d), lambda i, j: (i, 0)),
                  pl.BlockSpec((tm, wa), lambda i, j: (i, 0)),
                  pl.BlockSpec((tm, wc), lambda i, j: (i, 0)),
                  pl.BlockSpec((d, tn), lambda i, j: (0, col_ga // tn + j)),
                  pl.BlockSpec((d, tn), lambda i, j: (0, col_gc // tn + j)),
                  pl.BlockSpec((wa, tn), lambda i, j: (0, j)),
                  pl.BlockSpec((wc, tn), lambda i, j: (0, j))],
        out_specs=pl.BlockSpec((tm, tn), lambda i, j: (i, j)),
        out_shape=jax.ShapeDtypeStruct((m, d), BF16),
        compiler_params=_params(("parallel", "parallel"), blocks),
        name="merge",
    )(xb, attn, cn, w_in, w_in, w_ao, w_co)


def _out_ln_kernel(x_ref, mixed_ref, w_ref, g_ref, b_ref, of_ref, *maybe_ob_ref, alpha):
    y = _layer_norm(alpha * x_ref[...] + _dot(mixed_ref[...], w_ref[...]), g_ref[...], b_ref[...])
    of_ref[...] = y
    for ob_ref in maybe_ob_ref:
        ob_ref[...] = y.astype(ob_ref.dtype)


def _out_ln(x, mixed, w_out, g, b, alpha, tm, emit_bf16):
    m, d = x.shape
    tm = _tile(m, tm)
    vec = pl.BlockSpec((1, d), lambda i: (0, 0))
    row = pl.BlockSpec((tm, d), lambda i: (i, 0))
    blocks = tm * d * (4 + 2 + 4 + 2) + d * d * 2
    return pl.pallas_call(
        functools.partial(_out_ln_kernel, alpha=alpha),
        grid=(m // tm,),
        in_specs=[row, row, pl.BlockSpec((d, d), lambda i: (0, 0)), vec, vec],
        out_specs=[row, row] if emit_bf16 else [row],
        out_shape=[jax.ShapeDtypeStruct((m, d), F32)] + ([jax.ShapeDtypeStruct((m, d), BF16)] if emit_bf16 else []),
        compiler_params=_params(("parallel",), blocks),
        name="out_ln",
    )(x, mixed, w_out, g.reshape(1, d), b.reshape(1, d))


def _mlp_ln_kernel(xf_ref, xb_ref, w1_ref, w2_ref, g_ref, b_ref, o_ref, *, alpha):
    f = pl.program_id(1)

    @pl.when(f == 0)
    def _():
        o_ref[...] = alpha * xf_ref[...]

    hid = jnp.maximum(_dot(xb_ref[...], w1_ref[...]), 0.0)
    o_ref[...] += _dot((hid * hid).astype(BF16), w2_ref[...])

    @pl.when(f == pl.num_programs(1) - 1)
    def _():
        o_ref[...] = _layer_norm(o_ref[...], g_ref[...], b_ref[...])


def _mlp_ln(xf, xb, w1, w2, g, b, alpha, tm):
    m, d = xf.shape
    dff = w1.shape[1]
    tf = _tile(dff, 1024)
    tm = _tile(m, tm)
    vec = pl.BlockSpec((1, d), lambda i, f: (0, 0))
    row = pl.BlockSpec((tm, d), lambda i, f: (i, 0))
    blocks = tm * d * (4 + 2 + 4) + 2 * d * tf * 2 + tm * tf * 4
    return pl.pallas_call(
        functools.partial(_mlp_ln_kernel, alpha=alpha),
        grid=(m // tm, dff // tf),
        in_specs=[row, row, pl.BlockSpec((d, tf), lambda i, f: (0, f)), pl.BlockSpec((tf, d), lambda i, f: (f, 0)),
                  vec, vec],
        out_specs=row,
        out_shape=jax.ShapeDtypeStruct((m, d), F32),
        compiler_params=_params(("parallel", "arbitrary"), blocks),
        name="mlp_ln",
    )(xf, xb, w1, w2, g.reshape(1, d), b.reshape(1, d))


def _mlp_ln_decode_kernel(pt_ref, xf_ref, w1_ref, w2_ref, g_ref, b_ref, q_ref, kn_ref, vn_ref, relb_ref, *refs,
                          alpha, pages_per_step, n_pages, pages_per_block, buckets):
    del pt_ref
    k_refs, v_refs = refs[:pages_per_step], refs[pages_per_step:2 * pages_per_step]
    o_ref, ao_ref, xb_ref, hid_ref, m_ref, l_ref, gs_ref, acc_ref = refs[2 * pages_per_step:]
    f = pl.program_id(1)
    nf = pl.num_programs(1)

    @pl.when(f == 0)
    def _():
        x = xf_ref[...]
        o_ref[...] = alpha * x
        xb_ref[...] = x.astype(BF16)

    tf, d = w2_ref.shape
    wh = math.gcd(tf, MLP_CHUNK)
    wo = math.gcd(d, MLP_CHUNK)

    def hid_chunk(c):
        hid = jnp.maximum(_dot(xb_ref[...], w1_ref[:, c * wh:(c + 1) * wh]), 0.0)
        hid_ref[:, c * wh:(c + 1) * wh] = (hid * hid).astype(BF16)

    def out_chunk(c):
        o_ref[:, c * wo:(c + 1) * wo] += _dot(hid_ref[...], w2_ref[:, c * wo:(c + 1) * wo])

    chunks = ([functools.partial(hid_chunk, c) for c in range(tf // wh)]
              + [functools.partial(out_chunk, c) for c in range(d // wo)])
    part = lax.rem(pl.program_id(0) * nf + f, n_pages // pages_per_step)
    _decode_pages(part, q_ref, kn_ref, vn_ref, relb_ref, k_refs, v_refs, ao_ref, m_ref, l_ref, gs_ref, acc_ref,
                  n_pages=n_pages, pages_per_block=pages_per_block, buckets=buckets, other_work=chunks)

    @pl.when(f == nf - 1)
    def _():
        o_ref[...] = _layer_norm(o_ref[...], g_ref[...], b_ref[...])


MLP_CHUNK = 256
FUSED_MLP_ROWS = 512
FUSED_MLP_COLS = 1024


def _fused_pages_per_step(m, dff, s, n_pages):
    steps = (m // _tile(m, FUSED_MLP_ROWS)) * (dff // _tile(dff, FUSED_MLP_COLS))
    if (s * n_pages) % steps or n_pages % ((s * n_pages) // steps):
        return None
    return (s * n_pages) // steps


def _mlp_ln_decode(xf, w1, w2, g, b, alpha, q, k_new, v_new, cache_k, cache_v, page_table, rel_bias, pps):
    m, d = xf.shape
    dff = w1.shape[1]
    tm, tf = _tile(m, FUSED_MLP_ROWS), _tile(dff, FUSED_MLP_COLS)
    nf = dff // tf
    s, h, dh = q.shape
    page = cache_k.shape[1]
    n_pages = page_table.shape[1]
    parts = n_pages // pps
    assert (m // tm) * nf == s * parts
    buckets, relb, tok_spec, relb_spec, page_specs, scratch = _decode_specs(
        page_table, cache_k, rel_bias, h, dh, pps,
        seq_of=lambda i, f: (i * nf + f) // parts, part_of=lambda i, f: (i * nf + f) % parts)
    vec = pl.BlockSpec((1, d), lambda i, f, pt: (0, 0))
    row = pl.BlockSpec((tm, d), lambda i, f, pt: (i, 0))
    blocks = tm * d * 9 + 2 * d * tf * 2 + tm * tf * 4 + 2 * pps * page * h * dh * 4 + relb.size * 4
    kernel = functools.partial(_mlp_ln_decode_kernel, alpha=alpha, pages_per_step=pps, n_pages=n_pages,
                               pages_per_block=MOBA_BLOCK // page, buckets=buckets)
    return pl.pallas_call(
        kernel,
        grid_spec=pltpu.PrefetchScalarGridSpec(
            num_scalar_prefetch=1,
            grid=(m // tm, nf),
            in_specs=[row, pl.BlockSpec((d, tf), lambda i, f, pt: (0, f)), pl.BlockSpec((tf, d), lambda i, f, pt: (f, 0)),
                      vec, vec, tok_spec, tok_spec, tok_spec, relb_spec] + page_specs + page_specs,
            out_specs=[row, tok_spec],
            scratch_shapes=[pltpu.VMEM((tm, d), BF16), pltpu.VMEM((tm, tf), BF16)] + scratch),
        out_shape=[jax.ShapeDtypeStruct((m, d), F32), jax.ShapeDtypeStruct((s, h, dh), BF16)],
        compiler_params=_params(("arbitrary", "arbitrary"), blocks),
        name="mlp_ln_decode_attn",
    )(page_table, xf, w1, w2, g.reshape(1, d), b.reshape(1, d), q, k_new, v_new, relb,
      *([cache_k] * pps), *([cache_v] * pps))


def _merge_out(x2d, xb, attn, cn, lw, alpha, tm, emit_bf16):
    mixed = _merge(xb, attn, cn, lw["w_in"], lw["col_ga"], lw["col_gc"], lw["w_attn_out"], lw["w_conv_out"], tm=2 * tm)
    return _out_ln(x2d, mixed, lw["w_out"], lw["ln1_g"], lw["ln1_b"], alpha, tm, emit_bf16)


def kernel(x_prompt, x_sample, cache_k, cache_v, state_conv, page_table, rel_bias, w_in, w_attn_out, conv_w, conv_b,
           conv_ln_g, conv_ln_b, w_conv_out, w_out, ln1_g, ln1_b, w_ff1, w_ff2, ln2_g, ln2_b):
    depth = w_in.shape[0]
    n, t, d = x_prompt.shape
    s, ts, _ = x_sample.shape
    n_heads, dh = cache_k.shape[-2:]
    aw = n_heads * dh
    ch = conv_w.shape[-1]
    alpha = (2 * depth) ** 0.25
    scale = dh ** -0.5 * LOG2_E
    assert ts == 1 and w_in.shape[-1] == 3 * aw + 2 * ch + 2 * d
    col_val, col_gate = 3 * aw, 3 * aw + ch
    col_ga, col_gc = 3 * aw + 2 * ch, 3 * aw + 2 * ch + d

    bias_tiles = _prompt_bias_tiles(rel_bias)
    hp = x_prompt.reshape(n * t, d)
    hs = x_sample.reshape(s, d)
    outs = [[] for _ in range(6)]
    for l in range(depth):
        lw = dict(w_in=w_in[l].astype(BF16), w_attn_out=w_attn_out[l].astype(BF16),
                  w_conv_out=w_conv_out[l].astype(BF16), w_out=w_out[l].astype(BF16),
                  w_ff1=w_ff1[l].astype(BF16), w_ff2=w_ff2[l].astype(BF16),
                  ln1_g=ln1_g[l], ln1_b=ln1_b[l], ln2_g=ln2_g[l], ln2_b=ln2_b[l], col_ga=col_ga, col_gc=col_gc)
        wi = lw["w_in"]

        xb, q, kf, kb, vf, vb = _proj_qkv(hp, wi, aw, scale, tm=512, q_dtype=BF16)
        attn = _prompt_attention(q.reshape(n, t, aw), kb.reshape(n, t, aw), vb.reshape(n, t, aw), bias_tiles, n_heads)
        cn, u_tail = _prompt_glu_conv(xb.reshape(n, t, d), wi, col_val, col_gate, conv_w[l], conv_b[l],
                                      conv_ln_g[l], conv_ln_b[l])
        outs[0].append(kf.reshape(n, t, n_heads, dh))
        outs[1].append(vf.reshape(n, t, n_heads, dh))
        outs[2].append(u_tail[:, CONV_HALO - (conv_w.shape[1] - 1):, :])

        xsb, qs, ksf, _, vsf, _ = _proj_qkv(hs, wi, aw, scale, tm=s, q_dtype=F32)
        us = _proj_glu(xsb, wi, col_val, col_gate, ch, tm=s)
        dec = (qs.reshape(s, n_heads, dh), ksf.reshape(s, n_heads, dh), vsf.reshape(s, n_heads, dh),
               cache_k[l], cache_v[l], page_table, rel_bias)

        mlp_w = (lw["w_ff1"], lw["w_ff2"], lw["ln2_g"], lw["ln2_b"], alpha)
        pps = _fused_pages_per_step(n * t, w_ff1.shape[-1], s, page_table.shape[1])
        x1 = _merge_out(hp, xb, attn.reshape(n * t, aw), cn.reshape(n * t, ch), lw, alpha, 512, emit_bf16=pps is None)
        if pps is None:
            hp_next = _mlp_ln(x1[0], x1[1], *mlp_w, tm=512)
            attn_s = _decode_attention(*dec)
        else:
            hp_next, attn_s = _mlp_ln_decode(x1[0], *mlp_w, *dec, pps)

        cn_s, new_state = _decode_conv(state_conv[l], us.reshape(s, 1, ch), conv_w[l], conv_b[l], conv_ln_g[l],
                                       conv_ln_b[l])
        x1s = _merge_out(hs, xsb, attn_s.reshape(s, aw), cn_s.reshape(s, ch).astype(BF16), lw, alpha, s, emit_bf16=True)
        hs_next = _mlp_ln(x1s[0], x1s[1], *mlp_w, tm=s)
        outs[3].append(ksf.reshape(s, 1, n_heads, dh))
        outs[4].append(vsf.reshape(s, 1, n_heads, dh))
        outs[5].append(new_state)
        hp, hs = hp_next, hs_next

    return (hp.reshape(n, t, d), hs.reshape(s, 1, d)) + tuple(jnp.stack(o) for o in outs)
```

```python
import functools
import math

import numpy as np
import jax
import jax.numpy as jnp
from jax import lax
from jax.experimental import pallas as pl
from jax.experimental.pallas import tpu as pltpu

MOBA_BLOCK = 256
MOBA_TOPK = 3
MAX_DISTANCE = 128
LN_EPS = 1e-5
NEG_INF = -1e30
LOG2_E = math.log2(math.e)

V7X_VMEM_BYTES = 64 * 1024 * 1024
V7X_LANES = 128
V7X_SUBLANES = 8

F32 = jnp.float32
BF16 = jnp.bfloat16


def _vmem_limit(block_bytes):
    return int(min(max(2 * block_bytes + (16 << 20), 32 << 20), V7X_VMEM_BYTES - (4 << 20)))


def _params(semantics, block_bytes):
    return pltpu.CompilerParams(dimension_semantics=semantics, vmem_limit_bytes=_vmem_limit(block_bytes))


def _tile(n, want):
    t = min(n, want)
    while n % t:
        t -= 1
    return t


def _sigmoid(x):
    return 1.0 / (1.0 + jnp.exp(-x))


def _layer_norm(y, g, b):
    mu = jnp.mean(y, axis=-1, keepdims=True)
    d = y - mu
    var = jnp.mean(d * d, axis=-1, keepdims=True)
    return d * lax.rsqrt(var + LN_EPS) * g + b


def _dot(a, b):
    return jnp.dot(a, b, preferred_element_type=F32)


def _dot_nt(a, b):
    return lax.dot_general(a, b, (((1,), (1,)), ((), ())), preferred_element_type=F32)


def _proj_qkv_kernel(x_ref, w_ref, xb_ref, q_ref, kf_ref, kb_ref, vf_ref, vb_ref, *, scale):
    aw = q_ref.shape[1]
    xb = x_ref[...].astype(BF16)
    xb_ref[...] = xb
    q_ref[...] = (_dot(xb, w_ref[:, 0:aw]) * scale).astype(q_ref.dtype)
    for c, (f_ref, b_ref) in enumerate(((kf_ref, kb_ref), (vf_ref, vb_ref)), start=1):
        acc = _dot(xb, w_ref[:, c * aw:(c + 1) * aw])
        f_ref[...] = acc
        b_ref[...] = acc.astype(b_ref.dtype)


def _proj_glu_kernel(x_ref, wv_ref, wg_ref, u_ref):
    x = x_ref[...]
    u_ref[...] = _dot(x, wv_ref[...]) * _sigmoid(_dot(x, wg_ref[...]))


def _proj_qkv(x, w, aw, scale, tm, q_dtype):
    m, d = x.shape
    tm = _tile(m, tm)
    row = lambda width: pl.BlockSpec((tm, width), lambda i: (i, 0))
    blocks = tm * d * 6 + tm * aw * 16 + d * 3 * aw
    return pl.pallas_call(
        functools.partial(_proj_qkv_kernel, scale=scale),
        grid=(m // tm,),
        in_specs=[row(d), pl.BlockSpec((d, 3 * aw), lambda i: (0, 0), pipeline_mode=pl.Buffered(1))],
        out_specs=[row(d)] + [row(aw)] * 5,
        out_shape=[jax.ShapeDtypeStruct((m, d), BF16), jax.ShapeDtypeStruct((m, aw), q_dtype)]
        + [jax.ShapeDtypeStruct((m, aw), dt) for dt in (F32, BF16, F32, BF16)],
        compiler_params=_params(("parallel",), blocks),
        name="proj_qkv",
    )(x, w)


def _proj_glu(xb, w, col_val, col_gate, ncols, tm):
    m, d = xb.shape
    tn = _tile(ncols, 512)
    tm = _tile(m, tm)
    assert col_val % tn == 0 and col_gate % tn == 0
    blocks = tm * d * 2 + 2 * d * tn * 2 + tm * tn * 4
    return pl.pallas_call(
        _proj_glu_kernel,
        grid=(m // tm, ncols // tn),
        in_specs=[pl.BlockSpec((tm, d), lambda i, j: (i, 0)),
                  pl.BlockSpec((d, tn), lambda i, j: (0, col_val // tn + j)),
                  pl.BlockSpec((d, tn), lambda i, j: (0, col_gate // tn + j))],
        out_specs=pl.BlockSpec((tm, tn), lambda i, j: (i, j)),
        out_shape=jax.ShapeDtypeStruct((m, ncols), F32),
        compiler_params=_params(("parallel", "parallel"), blocks),
        name="proj_glu",
    )(xb, w, w)


def _rel_bucket_np(dist, num_buckets):
    n = np.maximum(dist, 0)
    max_exact = num_buckets // 2
    nf = np.maximum(n, 1).astype(np.float32)
    large = max_exact + (np.log(nf / np.float32(max_exact)) / np.float32(math.log(MAX_DISTANCE / max_exact))
                         * np.float32(num_buckets - max_exact)).astype(np.int32)
    large = np.minimum(large, num_buckets - 1)
    return np.where(n < max_exact, n, large).astype(np.int32)


def _bias_tiles_kernel(rb_ref, bucket_ref, o_ref, *, num_buckets):
    h = pl.program_id(0)
    for t in range(bucket_ref.shape[0]):
        bucket = bucket_ref[t]
        acc = jnp.full(bucket.shape, NEG_INF, F32)
        for b in range(num_buckets):
            acc = jnp.where(bucket == b, rb_ref[b, h] * LOG2_E, acc)
        o_ref[t] = acc


def _prompt_bias_tiles(rel_bias):
    num_buckets, n_heads = rel_bias.shape
    assert MAX_DISTANCE <= MOBA_BLOCK + 1
    r = np.arange(MOBA_BLOCK)[:, None]
    c = np.arange(MOBA_BLOCK)[None, :]
    diag = np.where(r - c >= 0, _rel_bucket_np(r - c, num_buckets), -1)
    prev = _rel_bucket_np(r - c + MOBA_BLOCK, num_buckets)
    far = _rel_bucket_np(r - c + 2 * MOBA_BLOCK, num_buckets)
    buckets = jnp.asarray(np.stack([diag, prev, far]).astype(np.int32))
    return pl.pallas_call(
        functools.partial(_bias_tiles_kernel, num_buckets=num_buckets),
        grid=(n_heads,),
        in_specs=[pl.BlockSpec(memory_space=pltpu.SMEM),
                  pl.BlockSpec((3, MOBA_BLOCK, MOBA_BLOCK), lambda h: (0, 0, 0))],
        out_specs=pl.BlockSpec((None, 3, MOBA_BLOCK, MOBA_BLOCK), lambda h: (h, 0, 0, 0)),
        out_shape=jax.ShapeDtypeStruct((n_heads, 3, MOBA_BLOCK, MOBA_BLOCK), F32),
        compiler_params=_params(("arbitrary",), 6 * MOBA_BLOCK * MOBA_BLOCK * 4),
        name="bias_tiles",
    )(rel_bias, buckets)


def _block_penalty_t(gate_t, n_valid):
    row = lax.broadcasted_iota(jnp.int32, gate_t.shape, 0)
    valid = row < n_valid
    pen = jnp.zeros(gate_t.shape, F32)
    for j in range(n_valid):
        gj = gate_t[j:j + 1, :]
        beats = ((gate_t > gj) | ((gate_t == gj) & (row < j))) & valid
        rank = jnp.sum(beats.astype(F32), axis=0, keepdims=True)
        pen = jnp.where(row == j, jnp.where(rank < MOBA_TOPK, 0.0, NEG_INF), pen)
    return pen


def _prompt_attn_kernel(q_ref, k_ref, v_ref, bias_ref, o_ref, kx_ref, pen_ref):
    t, dh = q_ref.shape
    blk = MOBA_BLOCK
    nblk = t // blk
    row = lax.broadcasted_iota(jnp.int32, (t, dh), 0)
    col = lax.broadcasted_iota(jnp.int32, (t, dh), 1)
    kx_ref[:, :dh] = k_ref[...]
    kx_ref[:, dh:] = (col == lax.shift_right_logical(row, blk.bit_length() - 1)).astype(BF16)
    in_block = (lax.shift_right_logical(lax.broadcasted_iota(jnp.int32, (nblk, t), 1), blk.bit_length() - 1)
                == lax.broadcasted_iota(jnp.int32, (nblk, t), 0))
    km = _dot(jnp.where(in_block, 1.0 / blk, 0.0).astype(BF16), k_ref[...])
    km_hi = km.astype(BF16)
    km_lo = (km - km_hi.astype(F32)).astype(BF16)
    eye = (lax.broadcasted_iota(jnp.int32, (blk, blk), 0) == lax.broadcasted_iota(jnp.int32, (blk, blk), 1)).astype(BF16)
    pen_ref[...] = jnp.zeros(pen_ref.shape, F32)
    def masked_logits(i):
        qi = q_ref[i * blk:(i + 1) * blk, :]
        nk = (i + 1) * blk
        if i <= MOBA_TOPK:
            return _dot_nt(qi, k_ref[0:nk, :])
        gate_t = _dot_nt(km_hi, qi) + _dot_nt(km_lo, qi)
        pen_ref[0:nblk, :] = _block_penalty_t(gate_t, i)
        pen = _dot_nt(eye, pen_ref[...].astype(BF16)).astype(BF16)
        return _dot_nt(jnp.concatenate([qi, pen], axis=1), kx_ref[0:nk, :])

    s_next = masked_logits(0)
    for i in range(nblk):
        nk = (i + 1) * blk
        s = s_next
        if i + 1 < nblk:
            s_next = masked_logits(i + 1)
        bias = [bias_ref[2]] * (i - 1) + ([bias_ref[1]] if i > 0 else []) + [bias_ref[0]]
        s = s + (jnp.concatenate(bias, axis=1) if len(bias) > 1 else bias[0])
        m = jnp.max(s, axis=1, keepdims=True)
        p = jnp.exp2(s - m)
        l = jnp.sum(p, axis=1, keepdims=True)
        o = _dot(p.astype(BF16), v_ref[0:nk, :])
        o_ref[i * blk:(i + 1) * blk, :] = (o / l).astype(o_ref.dtype)


def _prompt_attention(q, k, v, bias_tiles, n_heads):
    n, t, width = q.shape
    dh = width // n_heads
    nblk = t // MOBA_BLOCK
    assert dh == V7X_LANES and t % MOBA_BLOCK == 0 and nblk <= V7X_LANES and MOBA_BLOCK & (MOBA_BLOCK - 1) == 0
    seq_spec = pl.BlockSpec((None, t, dh), lambda h, b: (b, 0, h))
    blocks = 4 * t * dh * 2 + 3 * MOBA_BLOCK * MOBA_BLOCK * 4 + t * 2 * dh * 2 + 6 * MOBA_BLOCK * t * 4
    return pl.pallas_call(
        _prompt_attn_kernel,
        grid=(n_heads, n),
        in_specs=[seq_spec, seq_spec, seq_spec,
                  pl.BlockSpec((None, 3, MOBA_BLOCK, MOBA_BLOCK), lambda h, b: (h, 0, 0, 0))],
        out_specs=seq_spec,
        out_shape=jax.ShapeDtypeStruct((n, t, width), BF16),
        scratch_shapes=[pltpu.VMEM((t, 2 * dh), BF16), pltpu.VMEM((dh, MOBA_BLOCK), F32)],
        compiler_params=_params(("parallel", "parallel"), blocks),
        name="prompt_attn",
    )(q, k, v, bias_tiles)


DECODE_ROWS = 32


def _lane_sum(x):
    return jnp.broadcast_to(jnp.sum(x, axis=-1, keepdims=True), x.shape)


def _page_bias(relb_ref, page_buckets):
    if len(set(page_buckets)) == 1:
        return relb_ref[page_buckets[0]][None] * LOG2_E
    return jnp.stack([relb_ref[b] for b in page_buckets]) * LOG2_E


def _decode_pages(part, q_ref, kn_ref, vn_ref, relb_ref, k_refs, v_refs, o_ref, m_ref, l_ref, g_ref, acc_ref, *,
                  n_pages, pages_per_block, buckets, other_work=()):
    pps = len(k_refs)
    parts = n_pages // pps
    page, h, dh = k_refs[0].shape
    q = q_ref[...]
    rows = min(DECODE_ROWS, page)
    steps_per_page = -(-page // (2 * rows))
    interleave = {}
    for c, job in enumerate(other_work):
        interleave.setdefault(c * pps * steps_per_page // len(other_work), []).append(job)
    for p in range(pps):
        page_buckets = [buckets[(a * pps + p) * page:(a * pps + p + 1) * page] for a in range(parts)]
        shared = set(b for pb in page_buckets for b in pb)
        shared = shared.pop() if len(shared) == 1 else None
        m = jnp.full((h, dh), NEG_INF, F32)
        l = acc = gsum = jnp.zeros((h, dh), F32)
        for r0 in range(0, page, 2 * rows):
            for job in interleave.get(p * steps_per_page + r0 // (2 * rows), ()):
                job()
            groups = [slice(r, r + rows) for r in range(r0, min(r0 + 2 * rows, page), rows)]
            for g in groups:
                s = _lane_sum(k_refs[p][g] * q[None])
                gsum = gsum + jnp.sum(s, axis=0)
                if shared is None:
                    bias = _page_bias(relb_ref, page_buckets[0][g])
                    for a in range(1, parts):
                        if page_buckets[a][g] != page_buckets[0][g]:
                            bias = jnp.where(part == a, _page_bias(relb_ref, page_buckets[a][g]), bias)
                    s = s + bias
                m_new = jnp.maximum(m, jnp.max(s, axis=0))
                scale = jnp.exp2(m - m_new)
                e = jnp.exp2(s - m_new[None])
                l = l * scale + jnp.sum(e, axis=0)
                acc = acc * scale + jnp.sum(e * v_refs[p][g], axis=0)
                m = m_new
        idx = part * pps + p
        m_ref[idx] = m if shared is None else m + relb_ref[shared] * LOG2_E
        l_ref[idx] = l
        g_ref[idx] = gsum
        acc_ref[idx] = acc

    def merge():
        n_blocks = n_pages // pages_per_block
        gate = [sum(g_ref[p] for p in range(j * pages_per_block, (j + 1) * pages_per_block)) for j in range(n_blocks)]
        s_own = _lane_sum(q * kn_ref[...]) + relb_ref[0] * LOG2_E
        m_tot = s_own
        sel = []
        for j in range(n_blocks):
            rank = jnp.zeros((h, dh), F32)
            for j2 in range(n_blocks):
                if j2 != j:
                    beats = (gate[j2] > gate[j]) | ((gate[j2] == gate[j]) & (j2 < j))
                    rank = rank + beats.astype(F32)
            sel.append(rank < MOBA_TOPK)
            for p in range(j * pages_per_block, (j + 1) * pages_per_block):
                m_tot = jnp.maximum(m_tot, jnp.where(sel[j], m_ref[p], NEG_INF))
        w_own = jnp.exp2(s_own - m_tot)
        num = w_own * vn_ref[...]
        den = w_own
        for j in range(n_blocks):
            for p in range(j * pages_per_block, (j + 1) * pages_per_block):
                w = jnp.where(sel[j], jnp.exp2(m_ref[p] - m_tot), 0.0)
                num = num + w * acc_ref[p]
                den = den + w * l_ref[p]
        o_ref[...] = (num / den).astype(o_ref.dtype)

    if parts == 1:
        merge()
    else:
        pl.when(part == parts - 1)(merge)


def _decode_attn_kernel(pt_ref, q_ref, kn_ref, vn_ref, relb_ref, *refs, n_pages, pages_per_block, buckets):
    del pt_ref
    _decode_pages(0, q_ref, kn_ref, vn_ref, relb_ref, refs[:n_pages], refs[n_pages:2 * n_pages], *refs[2 * n_pages:],
                  n_pages=n_pages, pages_per_block=pages_per_block, buckets=buckets)


def _decode_specs(page_table, cache_k, rel_bias, h, dh, pages_per_step, seq_of, part_of):
    page = cache_k.shape[1]
    n_pages = page_table.shape[1]
    past = n_pages * page
    num_buckets = rel_bias.shape[0]
    assert dh == V7X_LANES and MOBA_BLOCK % page == 0 and past % MOBA_BLOCK == 0 and n_pages % pages_per_step == 0
    buckets = tuple(int(b) for b in _rel_bucket_np(past - np.arange(past), num_buckets))
    relb = jnp.broadcast_to(rel_bias[:, :, None], (num_buckets, h, dh))
    tok_spec = pl.BlockSpec((None, h, dh), lambda *g: (seq_of(*g[:-1]), 0, 0))
    relb_spec = pl.BlockSpec((num_buckets, h, dh), lambda *g: (0, 0, 0))
    page_specs = [pl.BlockSpec((None, page, h, dh),
                               lambda *g, p=p: (g[-1][seq_of(*g[:-1]), part_of(*g[:-1]) * pages_per_step + p], 0, 0, 0))
                  for p in range(pages_per_step)]
    scratch = [pltpu.VMEM((n_pages, h, dh), F32)] * 4
    return buckets, relb, tok_spec, relb_spec, page_specs, scratch


def _decode_attention(q, k_new, v_new, cache_k, cache_v, page_table, rel_bias):
    s, h, dh = q.shape
    page = cache_k.shape[1]
    n_pages = page_table.shape[1]
    buckets, relb, tok_spec, relb_spec, page_specs, scratch = _decode_specs(
        page_table, cache_k, rel_bias, h, dh, n_pages, seq_of=lambda b: b, part_of=lambda b: 0)
    blocks = 2 * n_pages * page * h * dh * 4 + relb.size * 4
    kernel = functools.partial(_decode_attn_kernel, n_pages=n_pages, pages_per_block=MOBA_BLOCK // page,
                               buckets=buckets)
    return pl.pallas_call(
        kernel,
        grid_spec=pltpu.PrefetchScalarGridSpec(
            num_scalar_prefetch=1,
            grid=(s,),
            in_specs=[tok_spec, tok_spec, tok_spec, relb_spec] + page_specs + page_specs,
            out_specs=tok_spec,
            scratch_shapes=scratch),
        out_shape=jax.ShapeDtypeStruct((s, h, dh), BF16),
        compiler_params=_params(("parallel",), blocks),
        name="decode_attn",
    )(page_table, q, k_new, v_new, relb, *([cache_k] * n_pages), *([cache_v] * n_pages))


CONV_HALO = 32
CONV_ROWS = 64
GLU_CHUNK = 256


def _causal_conv_columns(ext_ref, w_ref, conv_ref, c0, tt):
    kw = w_ref.shape[0]
    first = CONV_HALO - (kw - 1)
    rows = min(CONV_ROWS, tt)
    span = rows + CONV_HALO
    cs = slice(c0, c0 + V7X_LANES)
    for r0 in range(0, tt, rows):
        x = ext_ref[r0:r0 + span, cs]
        acc = jnp.zeros((rows, V7X_LANES), F32)
        for b in range(V7X_SUBLANES):
            taps = [s for s in range(first, first + kw) if s % V7X_SUBLANES == b]
            assert all(s + rows <= span for s in taps)
            xb = x if b == 0 else pltpu.roll(x, span - b, axis=0)
            for s in taps:
                acc = acc + xb[s - b:s - b + rows] * w_ref[s - first:s - first + 1, cs]
        conv_ref[r0:r0 + rows, cs] = acc


def _glu_conv_kernel(x_ref, wv_ref, wg_ref, w_ref, cb_ref, g_ref, b_ref, *refs):
    n_hosted = (len(refs) - 4) // 2
    o_ref, tail_ref = refs[n_hosted:n_hosted + 2]
    ext_ref, conv_ref = refs[-2:]
    for src_ref, dst_ref in zip(refs[:n_hosted], refs[n_hosted + 2:-2]):
        dst_ref[...] = src_ref[...].astype(dst_ref.dtype)
    tt = x_ref.shape[0]
    ch = wv_ref.shape[1]
    i = pl.program_id(1)

    @pl.when(i == 0)
    def _():
        ext_ref[0:CONV_HALO, :] = jnp.zeros((CONV_HALO, ch), F32)

    @pl.when(i > 0)
    def _():
        ext_ref[0:CONV_HALO, :] = ext_ref[tt:tt + CONV_HALO, :]

    x = x_ref[...]
    chunk = min(GLU_CHUNK, ch)
    for c0 in range(0, ch, chunk):
        cols = slice(c0, c0 + chunk)
        ext_ref[CONV_HALO:CONV_HALO + tt, cols] = _dot(x, wv_ref[:, cols]) * _sigmoid(_dot(x, wg_ref[:, cols]))
        for c in range(c0, c0 + chunk, V7X_LANES):
            _causal_conv_columns(ext_ref, w_ref, conv_ref, c, tt)
    y = _layer_norm(conv_ref[...] + cb_ref[...], g_ref[...], b_ref[...])
    o_ref[...] = (y * _sigmoid(y)).astype(o_ref.dtype)

    @pl.when(i == pl.num_programs(1) - 1)
    def _():
        tail_ref[...] = ext_ref[tt:tt + CONV_HALO, :]


def _prompt_glu_conv(xb, w, col_val, col_gate, conv_w, conv_b, ln_g, ln_b, to_bf16):
    n, t, d = xb.shape
    kw, ch = conv_w.shape
    tt = _tile(t, 256)
    nt = t // tt
    assert kw - 1 <= CONV_HALO <= tt and tt % V7X_SUBLANES == 0 and ch % min(GLU_CHUNK, ch) == 0
    assert col_val % ch == 0 and col_gate % ch == 0
    hosted = to_bf16 if all(a.shape[0] % (n * nt * 2 * V7X_SUBLANES) == 0 for a in to_bf16) else ()
    slab = lambda a: pl.BlockSpec((a.shape[0] // (n * nt), a.shape[1]), lambda b, i: (b * nt + i, 0))
    vec = pl.BlockSpec((1, ch), lambda b, i: (0, 0))
    wspec = lambda col: pl.BlockSpec((d, ch), lambda b, i: (0, col // ch), pipeline_mode=pl.Buffered(1))
    blocks = (tt * d * 2 + d * ch * 2 + (2 * tt + 2 * CONV_HALO) * ch * 4 + tt * ch * 2
              + sum(a.size * 6 // (n * nt) for a in hosted))
    cn, tail, *copies = pl.pallas_call(
        _glu_conv_kernel,
        grid=(n, nt),
        in_specs=[pl.BlockSpec((None, tt, d), lambda b, i: (b, i, 0)), wspec(col_val), wspec(col_gate),
                  pl.BlockSpec((kw, ch), lambda b, i: (0, 0)), vec, vec, vec] + [slab(a) for a in hosted],
        out_specs=[pl.BlockSpec((None, tt, ch), lambda b, i: (b, i, 0)),
                   pl.BlockSpec((None, CONV_HALO, ch), lambda b, i: (b, 0, 0))] + [slab(a) for a in hosted],
        out_shape=[jax.ShapeDtypeStruct((n, t, ch), BF16), jax.ShapeDtypeStruct((n, CONV_HALO, ch), F32)]
        + [jax.ShapeDtypeStruct(a.shape, BF16) for a in hosted],
        scratch_shapes=[pltpu.VMEM((CONV_HALO + tt, ch), F32), pltpu.VMEM((tt, ch), F32)],
        compiler_params=_params(("arbitrary", "arbitrary"), blocks),
        name="glu_conv",
    )(xb, w, w, conv_w, conv_b.reshape(1, ch), ln_g.reshape(1, ch), ln_b.reshape(1, ch), *hosted)
    return cn, tail, (copies if hosted else [a.astype(BF16) for a in to_bf16])


def _decode_conv_kernel(state_ref, u_ref, w_ref, cb_ref, g_ref, b_ref, o_ref, new_state_ref):
    kw = w_ref.shape[0]
    w_hist = w_ref[0:kw - 1, :]
    w_last = w_ref[kw - 1:kw, :]
    for b in range(state_ref.shape[0]):
        u = u_ref[b]
        conv = jnp.sum(state_ref[b] * w_hist, axis=0, keepdims=True) + u * w_last
        y = _layer_norm(conv + cb_ref[...], g_ref[...], b_ref[...])
        o_ref[b] = (y * _sigmoid(y)).astype(o_ref.dtype)
        new_state_ref[b, 0:kw - 2, :] = state_ref[b, 1:kw - 1, :]
        new_state_ref[b, kw - 2:kw - 1, :] = u


def _decode_conv(state, u, conv_w, conv_b, ln_g, ln_b):
    s, hist, ch = state.shape
    kw = conv_w.shape[0]
    ts = _tile(s, 16)
    vec = pl.BlockSpec((1, ch), lambda i: (0, 0))
    tok = pl.BlockSpec((ts, 1, ch), lambda i: (i, 0, 0))
    hist_spec = pl.BlockSpec((ts, hist, ch), lambda i: (i, 0, 0))
    blocks = 2 * ts * 32 * ch * 4 + 2 * ts * V7X_SUBLANES * ch * 4
    return pl.pallas_call(
        _decode_conv_kernel,
        grid=(s // ts,),
        in_specs=[hist_spec, tok, pl.BlockSpec((kw, ch), lambda i: (0, 0)), vec, vec, vec],
        out_specs=[tok, hist_spec],
        out_shape=[jax.ShapeDtypeStruct((s, 1, ch), F32), jax.ShapeDtypeStruct((s, hist, ch), F32)],
        compiler_params=_params(("parallel",), blocks),
        name="decode_conv",
    )(state, u, conv_w, conv_b.reshape(1, ch), ln_g.reshape(1, ch), ln_b.reshape(1, ch))


def _merge_kernel(x_ref, a_ref, c_ref, wga_ref, wgc_ref, wao_ref, wco_ref, o_ref):
    x = x_ref[...]
    mixed = (_sigmoid(_dot(x, wga_ref[...])) * _dot(a_ref[...], wao_ref[...])
             + _sigmoid(_dot(x, wgc_ref[...])) * _dot(c_ref[...], wco_ref[...]))
    o_ref[...] = mixed.astype(o_ref.dtype)


def _merge(xb, attn, cn, w_in, col_ga, col_gc, w_ao, w_co, tm):
    m, d = xb.shape
    wa = attn.shape[1]
    wc = cn.shape[1]
    tn = _tile(d, 512)
    tm = _tile(m, tm)
    assert col_ga % tn == 0 and col_gc % tn == 0
    blocks = tm * (d + wa + wc + tn) * 2 + (2 * d + wa + wc) * tn * 2
    return pl.pallas_call(
        _merge_kernel,
        grid=(m // tm, d // tn),
        in_specs=[pl.BlockSpec((tm, d), lambda i, j: (i, 0)),
                  pl.BlockSpec((tm, wa), lambda i, j: (i, 0)),
                  pl.BlockSpec((tm, wc), lambda i, j: (i, 0)),
                  pl.BlockSpec((d, tn), lambda i, j: (0, col_ga // tn + j)),
                  pl.BlockSpec((d, tn), lambda i, j: (0, col_gc // tn + j)),
                  pl.BlockSpec((wa, tn), lambda i, j: (0, j)),
                  pl.BlockSpec((wc, tn), lambda i, j: (0, j))],
        out_specs=pl.BlockSpec((tm, tn), lambda i, j: (i, j)),
        out_shape=jax.ShapeDtypeStruct((m, d), BF16),
        compiler_params=_params(("parallel", "parallel"), blocks),
        name="merge",
    )(xb, attn, cn, w_in, w_in, w_ao, w_co)


def _out_ln_kernel(x_ref, mixed_ref, w_ref, g_ref, b_ref, of_ref, *maybe_ob_ref, alpha):
    y = _layer_norm(alpha * x_ref[...] + _dot(mixed_ref[...], w_ref[...]), g_ref[...], b_ref[...])
    of_ref[...] = y
    for ob_ref in maybe_ob_ref:
        ob_ref[...] = y.astype(ob_ref.dtype)


def _out_ln(x, mixed, w_out, g, b, alpha, tm, emit_bf16):
    m, d = x.shape
    tm = _tile(m, tm)
    vec = pl.BlockSpec((1, d), lambda i: (0, 0))
    row = pl.BlockSpec((tm, d), lambda i: (i, 0))
    blocks = tm * d * (4 + 2 + 4 + 2) + d * d * 2
    return pl.pallas_call(
        functools.partial(_out_ln_kernel, alpha=alpha),
        grid=(m // tm,),
        in_specs=[row, row, pl.BlockSpec((d, d), lambda i: (0, 0)), vec, vec],
        out_specs=[row, row] if emit_bf16 else [row],
        out_shape=[jax.ShapeDtypeStruct((m, d), F32)] + ([jax.ShapeDtypeStruct((m, d), BF16)] if emit_bf16 else []),
        compiler_params=_params(("parallel",), blocks),
        name="out_ln",
    )(x, mixed, w_out, g.reshape(1, d), b.reshape(1, d))


def _mlp_ln_kernel(xf_ref, xb_ref, w1_ref, w2_ref, g_ref, b_ref, o_ref, *, alpha):
    f = pl.program_id(1)

    @pl.when(f == 0)
    def _():
        o_ref[...] = alpha * xf_ref[...]

    hid = jnp.maximum(_dot(xb_ref[...], w1_ref[...]), 0.0)
    o_ref[...] += _dot((hid * hid).astype(BF16), w2_ref[...])

    @pl.when(f == pl.num_programs(1) - 1)
    def _():
        o_ref[...] = _layer_norm(o_ref[...], g_ref[...], b_ref[...])


def _mlp_ln(xf, xb, w1, w2, g, b, alpha, tm):
    m, d = xf.shape
    dff = w1.shape[1]
    tf = _tile(dff, 1024)
    tm = _tile(m, tm)
    vec = pl.BlockSpec((1, d), lambda i, f: (0, 0))
    row = pl.BlockSpec((tm, d), lambda i, f: (i, 0))
    blocks = tm * d * (4 + 2 + 4) + 2 * d * tf * 2 + tm * tf * 4
    return pl.pallas_call(
        functools.partial(_mlp_ln_kernel, alpha=alpha),
        grid=(m // tm, dff // tf),
        in_specs=[row, row, pl.BlockSpec((d, tf), lambda i, f: (0, f)), pl.BlockSpec((tf, d), lambda i, f: (f, 0)),
                  vec, vec],
        out_specs=row,
        out_shape=jax.ShapeDtypeStruct((m, d), F32),
        compiler_params=_params(("parallel", "arbitrary"), blocks),
        name="mlp_ln",
    )(xf, xb, w1, w2, g.reshape(1, d), b.reshape(1, d))


def _mlp_ln_decode_kernel(pt_ref, xf_ref, w1_ref, w2_ref, g_ref, b_ref, q_ref, kn_ref, vn_ref, relb_ref, *refs,
                          alpha, pages_per_step, n_pages, pages_per_block, buckets):
    del pt_ref
    k_refs, v_refs = refs[:pages_per_step], refs[pages_per_step:2 * pages_per_step]
    o_ref, ao_ref, xb_ref, hid_ref, m_ref, l_ref, gs_ref, acc_ref = refs[2 * pages_per_step:]
    f = pl.program_id(1)
    nf = pl.num_programs(1)

    @pl.when(f == 0)
    def _():
        x = xf_ref[...]
        o_ref[...] = alpha * x
        xb_ref[...] = x.astype(BF16)

    tf, d = w2_ref.shape
    wh = math.gcd(tf, MLP_CHUNK)
    wo = math.gcd(d, MLP_CHUNK)

    def hid_chunk(c):
        hid = jnp.maximum(_dot(xb_ref[...], w1_ref[:, c * wh:(c + 1) * wh]), 0.0)
        hid_ref[:, c * wh:(c + 1) * wh] = (hid * hid).astype(BF16)

    def out_chunk(c):
        o_ref[:, c * wo:(c + 1) * wo] += _dot(hid_ref[...], w2_ref[:, c * wo:(c + 1) * wo])

    chunks = ([functools.partial(hid_chunk, c) for c in range(tf // wh)]
              + [functools.partial(out_chunk, c) for c in range(d // wo)])
    part = lax.rem(pl.program_id(0) * nf + f, n_pages // pages_per_step)
    _decode_pages(part, q_ref, kn_ref, vn_ref, relb_ref, k_refs, v_refs, ao_ref, m_ref, l_ref, gs_ref, acc_ref,
                  n_pages=n_pages, pages_per_block=pages_per_block, buckets=buckets, other_work=chunks)

    @pl.when(f == nf - 1)
    def _():
        o_ref[...] = _layer_norm(o_ref[...], g_ref[...], b_ref[...])


MLP_CHUNK = 256
FUSED_MLP_ROWS = 512
FUSED_MLP_COLS = 1024


def _fused_pages_per_step(m, dff, s, n_pages):
    steps = (m // _tile(m, FUSED_MLP_ROWS)) * (dff // _tile(dff, FUSED_MLP_COLS))
    if (s * n_pages) % steps or n_pages % ((s * n_pages) // steps):
        return None
    return (s * n_pages) // steps


def _mlp_ln_decode(xf, w1, w2, g, b, alpha, q, k_new, v_new, cache_k, cache_v, page_table, rel_bias, pps):
    m, d = xf.shape
    dff = w1.shape[1]
    tm, tf = _tile(m, FUSED_MLP_ROWS), _tile(dff, FUSED_MLP_COLS)
    nf = dff // tf
    s, h, dh = q.shape
    page = cache_k.shape[1]
    n_pages = page_table.shape[1]
    parts = n_pages // pps
    assert (m // tm) * nf == s * parts
    buckets, relb, tok_spec, relb_spec, page_specs, scratch = _decode_specs(
        page_table, cache_k, rel_bias, h, dh, pps,
        seq_of=lambda i, f: (i * nf + f) // parts, part_of=lambda i, f: (i * nf + f) % parts)
    vec = pl.BlockSpec((1, d), lambda i, f, pt: (0, 0))
    row = pl.BlockSpec((tm, d), lambda i, f, pt: (i, 0))
    blocks = tm * d * 9 + 2 * d * tf * 2 + tm * tf * 4 + 2 * pps * page * h * dh * 4 + relb.size * 4
    kernel = functools.partial(_mlp_ln_decode_kernel, alpha=alpha, pages_per_step=pps, n_pages=n_pages,
                               pages_per_block=MOBA_BLOCK // page, buckets=buckets)
    return pl.pallas_call(
        kernel,
        grid_spec=pltpu.PrefetchScalarGridSpec(
            num_scalar_prefetch=1,
            grid=(m // tm, nf),
            in_specs=[row, pl.BlockSpec((d, tf), lambda i, f, pt: (0, f)), pl.BlockSpec((tf, d), lambda i, f, pt: (f, 0)),
                      vec, vec, tok_spec, tok_spec, tok_spec, relb_spec] + page_specs + page_specs,
            out_specs=[row, tok_spec],
            scratch_shapes=[pltpu.VMEM((tm, d), BF16), pltpu.VMEM((tm, tf), BF16)] + scratch),
        out_shape=[jax.ShapeDtypeStruct((m, d), F32), jax.ShapeDtypeStruct((s, h, dh), BF16)],
        compiler_params=_params(("arbitrary", "arbitrary"), blocks),
        name="mlp_ln_decode_attn",
    )(page_table, xf, w1, w2, g.reshape(1, d), b.reshape(1, d), q, k_new, v_new, relb,
      *([cache_k] * pps), *([cache_v] * pps))


def _merge_out(x2d, xb, attn, cn, lw, alpha, tm, emit_bf16):
    mixed = _merge(xb, attn, cn, lw["w_in"], lw["col_ga"], lw["col_gc"], lw["w_attn_out"], lw["w_conv_out"], tm=2 * tm)
    return _out_ln(x2d, mixed, lw["w_out"], lw["ln1_g"], lw["ln1_b"], alpha, tm, emit_bf16)


def kernel(x_prompt, x_sample, cache_k, cache_v, state_conv, page_table, rel_bias, w_in, w_attn_out, conv_w, conv_b,
           conv_ln_g, conv_ln_b, w_conv_out, w_out, ln1_g, ln1_b, w_ff1, w_ff2, ln2_g, ln2_b):
    depth = w_in.shape[0]
    n, t, d = x_prompt.shape
    s, ts, _ = x_sample.shape
    n_heads, dh = cache_k.shape[-2:]
    aw = n_heads * dh
    ch = conv_w.shape[-1]
    alpha = (2 * depth) ** 0.25
    scale = dh ** -0.5 * LOG2_E
    assert ts == 1 and w_in.shape[-1] == 3 * aw + 2 * ch + 2 * d
    col_val, col_gate = 3 * aw, 3 * aw + ch
    col_ga, col_gc = 3 * aw + 2 * ch, 3 * aw + 2 * ch + d

    bias_tiles = _prompt_bias_tiles(rel_bias)
    hp = x_prompt.reshape(n * t, d)
    hs = x_sample.reshape(s, d)
    outs = [[] for _ in range(6)]
    for l in range(depth):
        wi = w_in[l].astype(BF16)

        xb, q, kf, kb, vf, vb = _proj_qkv(hp, wi, aw, scale, tm=512, q_dtype=BF16)
        attn = _prompt_attention(q.reshape(n, t, aw), kb.reshape(n, t, aw), vb.reshape(n, t, aw), bias_tiles, n_heads)
        late = dict(w_attn_out=w_attn_out[l], w_conv_out=w_conv_out[l], w_out=w_out[l], w_ff1=w_ff1[l], w_ff2=w_ff2[l])
        cn, u_tail, late_bf16 = _prompt_glu_conv(xb.reshape(n, t, d), wi, col_val, col_gate, conv_w[l], conv_b[l],
                                                 conv_ln_g[l], conv_ln_b[l], to_bf16=tuple(late.values()))
        lw = dict(zip(late, late_bf16), w_in=wi, ln1_g=ln1_g[l], ln1_b=ln1_b[l], ln2_g=ln2_g[l], ln2_b=ln2_b[l],
                  col_ga=col_ga, col_gc=col_gc)
        outs[0].append(kf.reshape(n, t, n_heads, dh))
        outs[1].append(vf.reshape(n, t, n_heads, dh))
        outs[2].append(u_tail[:, CONV_HALO - (conv_w.shape[1] - 1):, :])

        xsb, qs, ksf, _, vsf, _ = _proj_qkv(hs, wi, aw, scale, tm=s, q_dtype=F32)
        us = _proj_glu(xsb, wi, col_val, col_gate, ch, tm=s)
        dec = (qs.reshape(s, n_heads, dh), ksf.reshape(s, n_heads, dh), vsf.reshape(s, n_heads, dh),
               cache_k[l], cache_v[l], page_table, rel_bias)

        mlp_w = (lw["w_ff1"], lw["w_ff2"], lw["ln2_g"], lw["ln2_b"], alpha)
        pps = _fused_pages_per_step(n * t, w_ff1.shape[-1], s, page_table.shape[1])
        x1 = _merge_out(hp, xb, attn.reshape(n * t, aw), cn.reshape(n * t, ch), lw, alpha, 512, emit_bf16=pps is None)
        if pps is None:
            hp_next = _mlp_ln(x1[0], x1[1], *mlp_w, tm=512)
            attn_s = _decode_attention(*dec)
        else:
            hp_next, attn_s = _mlp_ln_decode(x1[0], *mlp_w, *dec, pps)

        cn_s, new_state = _decode_conv(state_conv[l], us.reshape(s, 1, ch), conv_w[l], conv_b[l], conv_ln_g[l],
                                       conv_ln_b[l])
        x1s = _merge_out(hs, xsb, attn_s.reshape(s, aw), cn_s.reshape(s, ch).astype(BF16), lw, alpha, s, emit_bf16=True)
        hs_next = _mlp_ln(x1s[0], x1s[1], *mlp_w, tm=s)
        outs[3].append(ksf.reshape(s, 1, n_heads, dh))
        outs[4].append(vsf.reshape(s, 1, n_heads, dh))
        outs[5].append(new_state)
        hp, hs = hp_next, hs_next

    return (hp.reshape(n, t, d), hs.reshape(s, 1, d)) + tuple(jnp.stack(o) for o in outs)
```

```python
import functools
import math

import numpy as np
import jax
import jax.numpy as jnp
from jax import lax
from jax.experimental import pallas as pl
from jax.experimental.pallas import tpu as pltpu

MOBA_BLOCK = 256
MOBA_TOPK = 3
MAX_DISTANCE = 128
LN_EPS = 1e-5
NEG_INF = -1e30
LOG2_E = math.log2(math.e)

V7X_VMEM_BYTES = 64 * 1024 * 1024
V7X_LANES = 128
V7X_SUBLANES = 8

F32 = jnp.float32
BF16 = jnp.bfloat16


def _vmem_limit(block_bytes):
    return int(min(max(2 * block_bytes + (16 << 20), 32 << 20), V7X_VMEM_BYTES - (4 << 20)))


def _params(semantics, block_bytes):
    return pltpu.CompilerParams(dimension_semantics=semantics, vmem_limit_bytes=_vmem_limit(block_bytes))


def _tile(n, want):
    t = min(n, want)
    while n % t:
        t -= 1
    return t


def _sigmoid(x):
    return 1.0 / (1.0 + jnp.exp(-x))


def _layer_norm(y, g, b):
    mu = jnp.mean(y, axis=-1, keepdims=True)
    d = y - mu
    var = jnp.mean(d * d, axis=-1, keepdims=True)
    return d * lax.rsqrt(var + LN_EPS) * g + b


def _dot(a, b):
    return jnp.dot(a, b, preferred_element_type=F32)


def _dot_nt(a, b):
    return lax.dot_general(a, b, (((1,), (1,)), ((), ())), preferred_element_type=F32)


def _proj_qkv_kernel(x_ref, w_ref, xb_ref, q_ref, kf_ref, kb_ref, vf_ref, vb_ref, *, scale):
    aw = q_ref.shape[1]
    xb = x_ref[...].astype(BF16)
    xb_ref[...] = xb
    q_ref[...] = (_dot(xb, w_ref[:, 0:aw]) * scale).astype(q_ref.dtype)
    for c, (f_ref, b_ref) in enumerate(((kf_ref, kb_ref), (vf_ref, vb_ref)), start=1):
        acc = _dot(xb, w_ref[:, c * aw:(c + 1) * aw])
        f_ref[...] = acc
        b_ref[...] = acc.astype(b_ref.dtype)


def _proj_glu_kernel(x_ref, wv_ref, wg_ref, u_ref):
    x = x_ref[...]
    u_ref[...] = _dot(x, wv_ref[...]) * _sigmoid(_dot(x, wg_ref[...]))


def _proj_qkv(x, w, aw, scale, tm, q_dtype):
    m, d = x.shape
    tm = _tile(m, tm)
    row = lambda width: pl.BlockSpec((tm, width), lambda i: (i, 0))
    blocks = tm * d * 6 + tm * aw * 16 + d * 3 * aw
    return pl.pallas_call(
        functools.partial(_proj_qkv_kernel, scale=scale),
        grid=(m // tm,),
        in_specs=[row(d), pl.BlockSpec((d, 3 * aw), lambda i: (0, 0), pipeline_mode=pl.Buffered(1))],
        out_specs=[row(d)] + [row(aw)] * 5,
        out_shape=[jax.ShapeDtypeStruct((m, d), BF16), jax.ShapeDtypeStruct((m, aw), q_dtype)]
        + [jax.ShapeDtypeStruct((m, aw), dt) for dt in (F32, BF16, F32, BF16)],
        compiler_params=_params(("parallel",), blocks),
        name="proj_qkv",
    )(x, w)


def _proj_glu(xb, w, col_val, col_gate, ncols, tm):
    m, d = xb.shape
    tn = _tile(ncols, 512)
    tm = _tile(m, tm)
    assert col_val % tn == 0 and col_gate % tn == 0
    blocks = tm * d * 2 + 2 * d * tn * 2 + tm * tn * 4
    return pl.pallas_call(
        _proj_glu_kernel,
        grid=(m // tm, ncols // tn),
        in_specs=[pl.BlockSpec((tm, d), lambda i, j: (i, 0)),
                  pl.BlockSpec((d, tn), lambda i, j: (0, col_val // tn + j)),
                  pl.BlockSpec((d, tn), lambda i, j: (0, col_gate // tn + j))],
        out_specs=pl.BlockSpec((tm, tn), lambda i, j: (i, j)),
        out_shape=jax.ShapeDtypeStruct((m, ncols), F32),
        compiler_params=_params(("parallel", "parallel"), blocks),
        name="proj_glu",
    )(xb, w, w)


def _rel_bucket_np(dist, num_buckets):
    n = np.maximum(dist, 0)
    max_exact = num_buckets // 2
    nf = np.maximum(n, 1).astype(np.float32)
    large = max_exact + (np.log(nf / np.float32(max_exact)) / np.float32(math.log(MAX_DISTANCE / max_exact))
                         * np.float32(num_buckets - max_exact)).astype(np.int32)
    large = np.minimum(large, num_buckets - 1)
    return np.where(n < max_exact, n, large).astype(np.int32)


def _bias_tiles_kernel(rb_ref, bucket_ref, o_ref, *, num_buckets):
    h = pl.program_id(0)
    far = rb_ref[num_buckets - 1, h]
    bucket = bucket_ref[...]
    acc = jnp.full(bucket.shape, NEG_INF, F32)
    for b in range(num_buckets):
        acc = jnp.where(bucket == b, (rb_ref[b, h] - far) * LOG2_E, acc)
    o_ref[...] = acc


def _prompt_bias_tiles(rel_bias):
    num_buckets, n_heads = rel_bias.shape
    assert MAX_DISTANCE <= MOBA_BLOCK + 1
    r = np.arange(MOBA_BLOCK)[:, None]
    c = np.arange(MOBA_BLOCK)[None, :]
    diag = np.where(r - c >= 0, _rel_bucket_np(r - c, num_buckets), -1)
    prev = _rel_bucket_np(r - c + MOBA_BLOCK, num_buckets)
    buckets = jnp.asarray(np.concatenate([prev, diag], axis=1).astype(np.int32))
    return pl.pallas_call(
        functools.partial(_bias_tiles_kernel, num_buckets=num_buckets),
        grid=(n_heads,),
        in_specs=[pl.BlockSpec(memory_space=pltpu.SMEM),
                  pl.BlockSpec((MOBA_BLOCK, 2 * MOBA_BLOCK), lambda h: (0, 0))],
        out_specs=pl.BlockSpec((None, MOBA_BLOCK, 2 * MOBA_BLOCK), lambda h: (h, 0, 0)),
        out_shape=jax.ShapeDtypeStruct((n_heads, MOBA_BLOCK, 2 * MOBA_BLOCK), F32),
        compiler_params=_params(("arbitrary",), 4 * MOBA_BLOCK * MOBA_BLOCK * 4),
        name="bias_tiles",
    )(rel_bias, buckets)


def _block_penalty_t(gate_t, n_valid):
    row = lax.broadcasted_iota(jnp.int32, gate_t.shape, 0)
    valid = row < n_valid
    pen = jnp.zeros(gate_t.shape, F32)
    for j in range(n_valid):
        gj = gate_t[j:j + 1, :]
        beats = ((gate_t > gj) | ((gate_t == gj) & (row < j))) & valid
        rank = jnp.sum(beats.astype(F32), axis=0, keepdims=True)
        pen = jnp.where(row == j, jnp.where(rank < MOBA_TOPK, 0.0, NEG_INF), pen)
    return pen


def _prompt_attn_kernel(q_ref, k_ref, v_ref, bias_ref, o_ref, kx_ref, pen_ref):
    t, dh = q_ref.shape
    blk = MOBA_BLOCK
    nblk = t // blk
    row = lax.broadcasted_iota(jnp.int32, (t, dh), 0)
    col = lax.broadcasted_iota(jnp.int32, (t, dh), 1)
    kx_ref[:, :dh] = k_ref[...]
    kx_ref[:, dh:] = (col == lax.shift_right_logical(row, blk.bit_length() - 1)).astype(BF16)
    in_block = (lax.shift_right_logical(lax.broadcasted_iota(jnp.int32, (nblk, t), 1), blk.bit_length() - 1)
                == lax.broadcasted_iota(jnp.int32, (nblk, t), 0))
    km = _dot(jnp.where(in_block, 1.0 / blk, 0.0).astype(BF16), k_ref[...])
    km_hi = km.astype(BF16)
    km_lo = (km - km_hi.astype(F32)).astype(BF16)
    eye = (lax.broadcasted_iota(jnp.int32, (blk, blk), 0) == lax.broadcasted_iota(jnp.int32, (blk, blk), 1)).astype(BF16)
    pen_ref[...] = jnp.zeros(pen_ref.shape, F32)
    def masked_logits(i):
        qi = q_ref[i * blk:(i + 1) * blk, :]
        nk = (i + 1) * blk
        if i <= MOBA_TOPK:
            return _dot_nt(qi, k_ref[0:nk, :])
        gate_t = _dot_nt(km_hi, qi) + _dot_nt(km_lo, qi)
        pen_ref[0:nblk, :] = _block_penalty_t(gate_t, i)
        pen = _dot_nt(eye, pen_ref[...].astype(BF16)).astype(BF16)
        return _dot_nt(jnp.concatenate([qi, pen], axis=1), kx_ref[0:nk, :])

    s_next = masked_logits(0)
    for i in range(nblk):
        nk = (i + 1) * blk
        s = s_next
        if i + 1 < nblk:
            s_next = masked_logits(i + 1)
        near = min(nk, 2 * blk)
        s_near = s[:, nk - near:] + bias_ref[:, 2 * blk - near:]
        s = s_near if near == nk else jnp.concatenate([s[:, :nk - near], s_near], axis=1)
        m = jnp.max(s, axis=1, keepdims=True)
        p = jnp.exp2(s - m)
        l = jnp.sum(p, axis=1, keepdims=True)
        o = _dot(p.astype(BF16), v_ref[0:nk, :])
        o_ref[i * blk:(i + 1) * blk, :] = (o / l).astype(o_ref.dtype)


def _prompt_attention(q, k, v, bias_tiles, n_heads):
    n, t, width = q.shape
    dh = width // n_heads
    nblk = t // MOBA_BLOCK
    assert dh == V7X_LANES and t % MOBA_BLOCK == 0 and nblk <= V7X_LANES and MOBA_BLOCK & (MOBA_BLOCK - 1) == 0
    seq_spec = pl.BlockSpec((None, t, dh), lambda h, b: (b, 0, h))
    blocks = 4 * t * dh * 2 + 2 * MOBA_BLOCK * MOBA_BLOCK * 4 + t * 2 * dh * 2 + 6 * MOBA_BLOCK * t * 4
    return pl.pallas_call(
        _prompt_attn_kernel,
        grid=(n_heads, n),
        in_specs=[seq_spec, seq_spec, seq_spec,
                  pl.BlockSpec((None, MOBA_BLOCK, 2 * MOBA_BLOCK), lambda h, b: (h, 0, 0))],
        out_specs=seq_spec,
        out_shape=jax.ShapeDtypeStruct((n, t, width), BF16),
        scratch_shapes=[pltpu.VMEM((t, 2 * dh), BF16), pltpu.VMEM((dh, MOBA_BLOCK), F32)],
        compiler_params=_params(("parallel", "parallel"), blocks),
        name="prompt_attn",
    )(q, k, v, bias_tiles)


DECODE_ROWS = 16


def _lane_sum(x):
    return jnp.broadcast_to(jnp.sum(x, axis=-1, keepdims=True), x.shape)


def _page_bias(relb_ref, page_buckets):
    if len(set(page_buckets)) == 1:
        return relb_ref[page_buckets[0]][None] * LOG2_E
    return jnp.stack([relb_ref[b] for b in page_buckets]) * LOG2_E


def _decode_pages(part, q_ref, kn_ref, vn_ref, relb_ref, k_refs, v_refs, o_ref, m_ref, l_ref, g_ref, acc_ref, *,
                  n_pages, pages_per_block, buckets, other_work=()):
    pps = len(k_refs)
    parts = n_pages // pps
    page, h, dh = k_refs[0].shape
    q = q_ref[...]
    rows = min(DECODE_ROWS, page)
    steps_per_page = -(-page // (2 * rows))
    interleave = {}
    for c, job in enumerate(other_work):
        interleave.setdefault(c * pps * steps_per_page // len(other_work), []).append(job)
    for p in range(pps):
        page_buckets = [buckets[(a * pps + p) * page:(a * pps + p + 1) * page] for a in range(parts)]
        shared = set(b for pb in page_buckets for b in pb)
        shared = shared.pop() if len(shared) == 1 else None
        m = jnp.full((h, dh), NEG_INF, F32)
        l = acc = gsum = jnp.zeros((h, dh), F32)
        for r0 in range(0, page, 2 * rows):
            for job in interleave.get(p * steps_per_page + r0 // (2 * rows), ()):
                job()
            groups = [slice(r, r + rows) for r in range(r0, min(r0 + 2 * rows, page), rows)]
            for g in groups:
                s = _lane_sum(k_refs[p][g] * q[None])
                gsum = gsum + jnp.sum(s, axis=0)
                if shared is None:
                    bias = _page_bias(relb_ref, page_buckets[0][g])
                    for a in range(1, parts):
                        if page_buckets[a][g] != page_buckets[0][g]:
                            bias = jnp.where(part == a, _page_bias(relb_ref, page_buckets[a][g]), bias)
                    s = s + bias
                m_new = jnp.maximum(m, jnp.max(s, axis=0))
                scale = jnp.exp2(m - m_new)
                e = jnp.exp2(s - m_new[None])
                l = l * scale + jnp.sum(e, axis=0)
                acc = acc * scale + jnp.sum(e * v_refs[p][g], axis=0)
                m = m_new
        idx = part * pps + p
        m_ref[idx] = m if shared is None else m + relb_ref[shared] * LOG2_E
        l_ref[idx] = l
        g_ref[idx] = gsum
        acc_ref[idx] = acc

    def merge():
        n_blocks = n_pages // pages_per_block
        gate = [sum(g_ref[p] for p in range(j * pages_per_block, (j + 1) * pages_per_block)) for j in range(n_blocks)]
        s_own = _lane_sum(q * kn_ref[...]) + relb_ref[0] * LOG2_E
        m_tot = s_own
        sel = []
        for j in range(n_blocks):
            rank = jnp.zeros((h, dh), F32)
            for j2 in range(n_blocks):
                if j2 != j:
                    beats = (gate[j2] > gate[j]) | ((gate[j2] == gate[j]) & (j2 < j))
                    rank = rank + beats.astype(F32)
            sel.append(rank < MOBA_TOPK)
            for p in range(j * pages_per_block, (j + 1) * pages_per_block):
                m_tot = jnp.maximum(m_tot, jnp.where(sel[j], m_ref[p], NEG_INF))
        w_own = jnp.exp2(s_own - m_tot)
        num = w_own * vn_ref[...]
        den = w_own
        for j in range(n_blocks):
            for p in range(j * pages_per_block, (j + 1) * pages_per_block):
                w = jnp.where(sel[j], jnp.exp2(m_ref[p] - m_tot), 0.0)
                num = num + w * acc_ref[p]
                den = den + w * l_ref[p]
        o_ref[...] = (num / den).astype(o_ref.dtype)

    if parts == 1:
        merge()
    else:
        pl.when(part == parts - 1)(merge)


def _decode_attn_kernel(pt_ref, q_ref, kn_ref, vn_ref, relb_ref, *refs, n_pages, pages_per_block, buckets):
    del pt_ref
    _decode_pages(0, q_ref, kn_ref, vn_ref, relb_ref, refs[:n_pages], refs[n_pages:2 * n_pages], *refs[2 * n_pages:],
                  n_pages=n_pages, pages_per_block=pages_per_block, buckets=buckets)


def _decode_specs(page_table, cache_k, rel_bias, h, dh, pages_per_step, seq_of, part_of):
    page = cache_k.shape[1]
    n_pages = page_table.shape[1]
    past = n_pages * page
    num_buckets = rel_bias.shape[0]
    assert dh == V7X_LANES and MOBA_BLOCK % page == 0 and past % MOBA_BLOCK == 0 and n_pages % pages_per_step == 0
    buckets = tuple(int(b) for b in _rel_bucket_np(past - np.arange(past), num_buckets))
    relb = jnp.broadcast_to(rel_bias[:, :, None], (num_buckets, h, dh))
    tok_spec = pl.BlockSpec((None, h, dh), lambda *g: (seq_of(*g[:-1]), 0, 0))
    relb_spec = pl.BlockSpec((num_buckets, h, dh), lambda *g: (0, 0, 0))
    page_specs = [pl.BlockSpec((None, page, h, dh),
                               lambda *g, p=p: (g[-1][seq_of(*g[:-1]), part_of(*g[:-1]) * pages_per_step + p], 0, 0, 0))
                  for p in range(pages_per_step)]
    scratch = [pltpu.VMEM((n_pages, h, dh), F32)] * 4
    return buckets, relb, tok_spec, relb_spec, page_specs, scratch


def _decode_attention(q, k_new, v_new, cache_k, cache_v, page_table, rel_bias):
    s, h, dh = q.shape
    page = cache_k.shape[1]
    n_pages = page_table.shape[1]
    buckets, relb, tok_spec, relb_spec, page_specs, scratch = _decode_specs(
        page_table, cache_k, rel_bias, h, dh, n_pages, seq_of=lambda b: b, part_of=lambda b: 0)
    blocks = 2 * n_pages * page * h * dh * 4 + relb.size * 4
    kernel = functools.partial(_decode_attn_kernel, n_pages=n_pages, pages_per_block=MOBA_BLOCK // page,
                               buckets=buckets)
    return pl.pallas_call(
        kernel,
        grid_spec=pltpu.PrefetchScalarGridSpec(
            num_scalar_prefetch=1,
            grid=(s,),
            in_specs=[tok_spec, tok_spec, tok_spec, relb_spec] + page_specs + page_specs,
            out_specs=tok_spec,
            scratch_shapes=scratch),
        out_shape=jax.ShapeDtypeStruct((s, h, dh), BF16),
        compiler_params=_params(("parallel",), blocks),
        name="decode_attn",
    )(page_table, q, k_new, v_new, relb, *([cache_k] * n_pages), *([cache_v] * n_pages))


CONV_HALO = 32
CONV_ROWS = 256
GLU_CHUNK = 256


def _causal_conv_columns(ext_ref, w_ref, conv_ref, c0, tt):
    kw = w_ref.shape[0]
    first = CONV_HALO - (kw - 1)
    rows = min(CONV_ROWS, tt)
    span = rows + CONV_HALO
    cs = slice(c0, c0 + V7X_LANES)
    for r0 in range(0, tt, rows):
        x = ext_ref[r0:r0 + span, cs]
        acc = jnp.zeros((rows, V7X_LANES), F32)
        for b in range(V7X_SUBLANES):
            taps = [s for s in range(first, first + kw) if s % V7X_SUBLANES == b]
            assert all(s + rows <= span for s in taps)
            xb = x if b == 0 else pltpu.roll(x, span - b, axis=0)
            for s in taps:
                acc = acc + xb[s - b:s - b + rows] * w_ref[s - first:s - first + 1, cs]
        conv_ref[r0:r0 + rows, cs] = acc


def _glu_conv_kernel(x_ref, wv_ref, wg_ref, w_ref, cb_ref, g_ref, b_ref, *refs):
    n_hosted = (len(refs) - 4) // 2
    o_ref, tail_ref = refs[n_hosted:n_hosted + 2]
    ext_ref, conv_ref = refs[-2:]
    for src_ref, dst_ref in zip(refs[:n_hosted], refs[n_hosted + 2:-2]):
        dst_ref[...] = src_ref[...].astype(dst_ref.dtype)
    tt = x_ref.shape[0]
    ch = wv_ref.shape[1]
    i = pl.program_id(1)

    @pl.when(i == 0)
    def _():
        ext_ref[0:CONV_HALO, :] = jnp.zeros((CONV_HALO, ch), F32)

    @pl.when(i > 0)
    def _():
        ext_ref[0:CONV_HALO, :] = ext_ref[tt:tt + CONV_HALO, :]

    x = x_ref[...]
    chunk = min(GLU_CHUNK, ch)
    for c0 in range(0, ch, chunk):
        cols = slice(c0, c0 + chunk)
        ext_ref[CONV_HALO:CONV_HALO + tt, cols] = _dot(x, wv_ref[:, cols]) * _sigmoid(_dot(x, wg_ref[:, cols]))
        for c in range(c0, c0 + chunk, V7X_LANES):
            _causal_conv_columns(ext_ref, w_ref, conv_ref, c, tt)
    y = _layer_norm(conv_ref[...] + cb_ref[...], g_ref[...], b_ref[...])
    o_ref[...] = (y * _sigmoid(y)).astype(o_ref.dtype)

    @pl.when(i == pl.num_programs(1) - 1)
    def _():
        tail_ref[...] = ext_ref[tt:tt + CONV_HALO, :]


def _prompt_glu_conv(xb, w, col_val, col_gate, conv_w, conv_b, ln_g, ln_b, to_bf16):
    n, t, d = xb.shape
    kw, ch = conv_w.shape
    tt = _tile(t, 256)
    nt = t // tt
    assert kw - 1 <= CONV_HALO <= tt and tt % V7X_SUBLANES == 0 and ch % min(GLU_CHUNK, ch) == 0
    assert col_val % ch == 0 and col_gate % ch == 0
    hosted = to_bf16 if all(a.shape[0] % (n * nt * 2 * V7X_SUBLANES) == 0 for a in to_bf16) else ()
    slab = lambda a: pl.BlockSpec((a.shape[0] // (n * nt), a.shape[1]), lambda b, i: (b * nt + i, 0))
    vec = pl.BlockSpec((1, ch), lambda b, i: (0, 0))
    wspec = lambda col: pl.BlockSpec((d, ch), lambda b, i: (0, col // ch), pipeline_mode=pl.Buffered(1))
    blocks = (tt * d * 2 + d * ch * 2 + (2 * tt + 2 * CONV_HALO) * ch * 4 + tt * ch * 2
              + sum(a.size * 6 // (n * nt) for a in hosted))
    cn, tail, *copies = pl.pallas_call(
        _glu_conv_kernel,
        grid=(n, nt),
        in_specs=[pl.BlockSpec((None, tt, d), lambda b, i: (b, i, 0)), wspec(col_val), wspec(col_gate),
                  pl.BlockSpec((kw, ch), lambda b, i: (0, 0)), vec, vec, vec] + [slab(a) for a in hosted],
        out_specs=[pl.BlockSpec((None, tt, ch), lambda b, i: (b, i, 0)),
                   pl.BlockSpec((None, CONV_HALO, ch), lambda b, i: (b, 0, 0))] + [slab(a) for a in hosted],
        out_shape=[jax.ShapeDtypeStruct((n, t, ch), BF16), jax.ShapeDtypeStruct((n, CONV_HALO, ch), F32)]
        + [jax.ShapeDtypeStruct(a.shape, BF16) for a in hosted],
        scratch_shapes=[pltpu.VMEM((CONV_HALO + tt, ch), F32), pltpu.VMEM((tt, ch), F32)],
        compiler_params=_params(("arbitrary", "arbitrary"), blocks),
        name="glu_conv",
    )(xb, w, w, conv_w, conv_b.reshape(1, ch), ln_g.reshape(1, ch), ln_b.reshape(1, ch), *hosted)
    return cn, tail, (copies if hosted else [a.astype(BF16) for a in to_bf16])


def _decode_conv_kernel(state_ref, u_ref, w_ref, cb_ref, g_ref, b_ref, o_ref, new_state_ref):
    kw = w_ref.shape[0]
    w_hist = w_ref[0:kw - 1, :]
    w_last = w_ref[kw - 1:kw, :]
    for b in range(state_ref.shape[0]):
        u = u_ref[b]
        conv = jnp.sum(state_ref[b] * w_hist, axis=0, keepdims=True) + u * w_last
        y = _layer_norm(conv + cb_ref[...], g_ref[...], b_ref[...])
        o_ref[b] = (y * _sigmoid(y)).astype(o_ref.dtype)
        new_state_ref[b, 0:kw - 2, :] = state_ref[b, 1:kw - 1, :]
        new_state_ref[b, kw - 2:kw - 1, :] = u


def _decode_conv(state, u, conv_w, conv_b, ln_g, ln_b):
    s, hist, ch = state.shape
    kw = conv_w.shape[0]
    ts = _tile(s, 16)
    vec = pl.BlockSpec((1, ch), lambda i: (0, 0))
    tok = pl.BlockSpec((ts, 1, ch), lambda i: (i, 0, 0))
    hist_spec = pl.BlockSpec((ts, hist, ch), lambda i: (i, 0, 0))
    blocks = 2 * ts * 32 * ch * 4 + 2 * ts * V7X_SUBLANES * ch * 4
    return pl.pallas_call(
        _decode_conv_kernel,
        grid=(s // ts,),
        in_specs=[hist_spec, tok, pl.BlockSpec((kw, ch), lambda i: (0, 0)), vec, vec, vec],
        out_specs=[tok, hist_spec],
        out_shape=[jax.ShapeDtypeStruct((s, 1, ch), F32), jax.ShapeDtypeStruct((s, hist, ch), F32)],
        compiler_params=_params(("parallel",), blocks),
        name="decode_conv",
    )(state, u, conv_w, conv_b.reshape(1, ch), ln_g.reshape(1, ch), ln_b.reshape(1, ch))


def _merge_kernel(x_ref, a_ref, c_ref, wga_ref, wgc_ref, wao_ref, wco_ref, o_ref):
    x = x_ref[...]
    mixed = (_sigmoid(_dot(x, wga_ref[...])) * _dot(a_ref[...], wao_ref[...])
             + _sigmoid(_dot(x, wgc_ref[...])) * _dot(c_ref[...], wco_ref[...]))
    o_ref[...] = mixed.astype(o_ref.dtype)


def _merge(xb, attn, cn, w_in, col_ga, col_gc, w_ao, w_co, tm):
    m, d = xb.shape
    wa = attn.shape[1]
    wc = cn.shape[1]
    tn = _tile(d, 512)
    tm = _tile(m, tm)
    assert col_ga % tn == 0 and col_gc % tn == 0
    blocks = tm * (d + wa + wc + tn) * 2 + (2 * d + wa + wc) * tn * 2
    return pl.pallas_call(
        _merge_kernel,
        grid=(m // tm, d // tn),
        in_specs=[pl.BlockSpec((tm, d), lambda i, j: (i, 0)),
                  pl.BlockSpec((tm, wa), lambda i, j: (i, 0)),
                  pl.BlockSpec((tm, wc), lambda i, j: (i, 0)),
                  pl.BlockSpec((d, tn), lambda i, j: (0, col_ga // tn + j)),
                  pl.BlockSpec((d, tn), lambda i, j: (0, col_gc // tn + j)),
                  pl.BlockSpec((wa, tn), lambda i, j: (0, j)),
                  pl.BlockSpec((wc, tn), lambda i, j: (0, j))],
        out_specs=pl.BlockSpec((tm, tn), lambda i, j: (i, j)),
        out_shape=jax.ShapeDtypeStruct((m, d), BF16),
        compiler_params=_params(("parallel", "parallel"), blocks),
        name="merge",
    )(xb, attn, cn, w_in, w_in, w_ao, w_co)


def _out_ln_kernel(x_ref, mixed_ref, w_ref, g_ref, b_ref, of_ref, *maybe_ob_ref, alpha):
    y = _layer_norm(alpha * x_ref[...] + _dot(mixed_ref[...], w_ref[...]), g_ref[...], b_ref[...])
    of_ref[...] = y
    for ob_ref in maybe_ob_ref:
        ob_ref[...] = y.astype(ob_ref.dtype)


def _out_ln(x, mixed, w_out, g, b, alpha, tm, emit_bf16):
    m, d = x.shape
    tm = _tile(m, tm)
    vec = pl.BlockSpec((1, d), lambda i: (0, 0))
    row = pl.BlockSpec((tm, d), lambda i: (i, 0))
    blocks = tm * d * (4 + 2 + 4 + 2) + d * d * 2
    return pl.pallas_call(
        functools.partial(_out_ln_kernel, alpha=alpha),
        grid=(m // tm,),
        in_specs=[row, row, pl.BlockSpec((d, d), lambda i: (0, 0)), vec, vec],
        out_specs=[row, row] if emit_bf16 else [row],
        out_shape=[jax.ShapeDtypeStruct((m, d), F32)] + ([jax.ShapeDtypeStruct((m, d), BF16)] if emit_bf16 else []),
        compiler_params=_params(("parallel",), blocks),
        name="out_ln",
    )(x, mixed, w_out, g.reshape(1, d), b.reshape(1, d))


def _mlp_ln_kernel(xf_ref, xb_ref, w1_ref, w2_ref, g_ref, b_ref, o_ref, *, alpha):
    f = pl.program_id(1)

    @pl.when(f == 0)
    def _():
        o_ref[...] = alpha * xf_ref[...]

    hid = jnp.maximum(_dot(xb_ref[...], w1_ref[...]), 0.0)
    o_ref[...] += _dot((hid * hid).astype(BF16), w2_ref[...])

    @pl.when(f == pl.num_programs(1) - 1)
    def _():
        o_ref[...] = _layer_norm(o_ref[...], g_ref[...], b_ref[...])


def _mlp_ln(xf, xb, w1, w2, g, b, alpha, tm):
    m, d = xf.shape
    dff = w1.shape[1]
    tf = _tile(dff, 1024)
    tm = _tile(m, tm)
    vec = pl.BlockSpec((1, d), lambda i, f: (0, 0))
    row = pl.BlockSpec((tm, d), lambda i, f: (i, 0))
    blocks = tm * d * (4 + 2 + 4) + 2 * d * tf * 2 + tm * tf * 4
    return pl.pallas_call(
        functools.partial(_mlp_ln_kernel, alpha=alpha),
        grid=(m // tm, dff // tf),
        in_specs=[row, row, pl.BlockSpec((d, tf), lambda i, f: (0, f)), pl.BlockSpec((tf, d), lambda i, f: (f, 0)),
                  vec, vec],
        out_specs=row,
        out_shape=jax.ShapeDtypeStruct((m, d), F32),
        compiler_params=_params(("parallel", "arbitrary"), blocks),
        name="mlp_ln",
    )(xf, xb, w1, w2, g.reshape(1, d), b.reshape(1, d))


def _mlp_ln_decode_kernel(pt_ref, xf_ref, w1_ref, w2_ref, g_ref, b_ref, q_ref, kn_ref, vn_ref, relb_ref, *refs,
                          alpha, pages_per_step, n_pages, pages_per_block, buckets):
    del pt_ref
    k_refs, v_refs = refs[:pages_per_step], refs[pages_per_step:2 * pages_per_step]
    o_ref, ao_ref, xb_ref, hid_ref, m_ref, l_ref, gs_ref, acc_ref = refs[2 * pages_per_step:]
    f = pl.program_id(1)
    nf = pl.num_programs(1)

    @pl.when(f == 0)
    def _():
        x = xf_ref[...]
        o_ref[...] = alpha * x
        xb_ref[...] = x.astype(BF16)

    tf, d = w2_ref.shape
    wh = math.gcd(tf, MLP_CHUNK)
    wo = math.gcd(d, MLP_CHUNK)

    def hid_chunk(c):
        hid = jnp.maximum(_dot(xb_ref[...], w1_ref[:, c * wh:(c + 1) * wh]), 0.0)
        hid_ref[:, c * wh:(c + 1) * wh] = (hid * hid).astype(BF16)

    def out_chunk(c):
        o_ref[:, c * wo:(c + 1) * wo] += _dot(hid_ref[...], w2_ref[:, c * wo:(c + 1) * wo])

    chunks = ([functools.partial(hid_chunk, c) for c in range(tf // wh)]
              + [functools.partial(out_chunk, c) for c in range(d // wo)])
    part = lax.rem(pl.program_id(0) * nf + f, n_pages // pages_per_step)
    _decode_pages(part, q_ref, kn_ref, vn_ref, relb_ref, k_refs, v_refs, ao_ref, m_ref, l_ref, gs_ref, acc_ref,
                  n_pages=n_pages, pages_per_block=pages_per_block, buckets=buckets, other_work=chunks)

    @pl.when(f == nf - 1)
    def _():
        o_ref[...] = _layer_norm(o_ref[...], g_ref[...], b_ref[...])


MLP_CHUNK = 256
FUSED_MLP_ROWS = 512
FUSED_MLP_COLS = 1024


def _fused_pages_per_step(m, dff, s, n_pages):
    steps = (m // _tile(m, FUSED_MLP_ROWS)) * (dff // _tile(dff, FUSED_MLP_COLS))
    if (s * n_pages) % steps or n_pages % ((s * n_pages) // steps):
        return None
    return (s * n_pages) // steps


def _mlp_ln_decode(xf, w1, w2, g, b, alpha, q, k_new, v_new, cache_k, cache_v, page_table, rel_bias, pps):
    m, d = xf.shape
    dff = w1.shape[1]
    tm, tf = _tile(m, FUSED_MLP_ROWS), _tile(dff, FUSED_MLP_COLS)
    nf = dff // tf
    s, h, dh = q.shape
    page = cache_k.shape[1]
    n_pages = page_table.shape[1]
    parts = n_pages // pps
    assert (m // tm) * nf == s * parts
    buckets, relb, tok_spec, relb_spec, page_specs, scratch = _decode_specs(
        page_table, cache_k, rel_bias, h, dh, pps,
        seq_of=lambda i, f: (i * nf + f) // parts, part_of=lambda i, f: (i * nf + f) % parts)
    vec = pl.BlockSpec((1, d), lambda i, f, pt: (0, 0))
    row = pl.BlockSpec((tm, d), lambda i, f, pt: (i, 0))
    blocks = tm * d * 9 + 2 * d * tf * 2 + tm * tf * 4 + 2 * pps * page * h * dh * 4 + relb.size * 4
    kernel = functools.partial(_mlp_ln_decode_kernel, alpha=alpha, pages_per_step=pps, n_pages=n_pages,
                               pages_per_block=MOBA_BLOCK // page, buckets=buckets)
    return pl.pallas_call(
        kernel,
        grid_spec=pltpu.PrefetchScalarGridSpec(
            num_scalar_prefetch=1,
            grid=(m // tm, nf),
            in_specs=[row, pl.BlockSpec((d, tf), lambda i, f, pt: (0, f)), pl.BlockSpec((tf, d), lambda i, f, pt: (f, 0)),
                      vec, vec, tok_spec, tok_spec, tok_spec, relb_spec] + page_specs + page_specs,
            out_specs=[row, tok_spec],
            scratch_shapes=[pltpu.VMEM((tm, d), BF16), pltpu.VMEM((tm, tf), BF16)] + scratch),
        out_shape=[jax.ShapeDtypeStruct((m, d), F32), jax.ShapeDtypeStruct((s, h, dh), BF16)],
        compiler_params=_params(("arbitrary", "arbitrary"), blocks),
        name="mlp_ln_decode_attn",
    )(page_table, xf, w1, w2, g.reshape(1, d), b.reshape(1, d), q, k_new, v_new, relb,
      *([cache_k] * pps), *([cache_v] * pps))


def _merge_out(x2d, xb, attn, cn, lw, alpha, tm, emit_bf16):
    mixed = _merge(xb, attn, cn, lw["w_in"], lw["col_ga"], lw["col_gc"], lw["w_attn_out"], lw["w_conv_out"], tm=2 * tm)
    return _out_ln(x2d, mixed, lw["w_out"], lw["ln1_g"], lw["ln1_b"], alpha, tm, emit_bf16)


def kernel(x_prompt, x_sample, cache_k, cache_v, state_conv, page_table, rel_bias, w_in, w_attn_out, conv_w, conv_b,
           conv_ln_g, conv_ln_b, w_conv_out, w_out, ln1_g, ln1_b, w_ff1, w_ff2, ln2_g, ln2_b):
    depth = w_in.shape[0]
    n, t, d = x_prompt.shape
    s, ts, _ = x_sample.shape
    n_heads, dh = cache_k.shape[-2:]
    aw = n_heads * dh
    ch = conv_w.shape[-1]
    alpha = (2 * depth) ** 0.25
    scale = dh ** -0.5 * LOG2_E
    assert ts == 1 and w_in.shape[-1] == 3 * aw + 2 * ch + 2 * d
    col_val, col_gate = 3 * aw, 3 * aw + ch
    col_ga, col_gc = 3 * aw + 2 * ch, 3 * aw + 2 * ch + d

    bias_tiles = _prompt_bias_tiles(rel_bias)
    hp = x_prompt.reshape(n * t, d)
    hs = x_sample.reshape(s, d)
    outs = [[] for _ in range(6)]
    for l in range(depth):
        wi = w_in[l].astype(BF16)

        xb, q, kf, kb, vf, vb = _proj_qkv(hp, wi, aw, scale, tm=512, q_dtype=BF16)
        attn = _prompt_attention(q.reshape(n, t, aw), kb.reshape(n, t, aw), vb.reshape(n, t, aw), bias_tiles, n_heads)
        late = dict(w_attn_out=w_attn_out[l], w_conv_out=w_conv_out[l], w_out=w_out[l], w_ff1=w_ff1[l], w_ff2=w_ff2[l])
        cn, u_tail, late_bf16 = _prompt_glu_conv(xb.reshape(n, t, d), wi, col_val, col_gate, conv_w[l], conv_b[l],
                                                 conv_ln_g[l], conv_ln_b[l], to_bf16=tuple(late.values()))
        lw = dict(zip(late, late_bf16), w_in=wi, ln1_g=ln1_g[l], ln1_b=ln1_b[l], ln2_g=ln2_g[l], ln2_b=ln2_b[l],
                  col_ga=col_ga, col_gc=col_gc)
        outs[0].append(kf.reshape(n, t, n_heads, dh))
        outs[1].append(vf.reshape(n, t, n_heads, dh))
        outs[2].append(u_tail[:, CONV_HALO - (conv_w.shape[1] - 1):, :])

        xsb, qs, ksf, _, vsf, _ = _proj_qkv(hs, wi, aw, scale, tm=s, q_dtype=F32)
        us = _proj_glu(xsb, wi, col_val, col_gate, ch, tm=s)
        dec = (qs.reshape(s, n_heads, dh), ksf.reshape(s, n_heads, dh), vsf.reshape(s, n_heads, dh),
               cache_k[l], cache_v[l], page_table, rel_bias)

        mlp_w = (lw["w_ff1"], lw["w_ff2"], lw["ln2_g"], lw["ln2_b"], alpha)
        pps = _fused_pages_per_step(n * t, w_ff1.shape[-1], s, page_table.shape[1])
        x1 = _merge_out(hp, xb, attn.reshape(n * t, aw), cn.reshape(n * t, ch), lw, alpha, 512, emit_bf16=pps is None)
        if pps is None:
            hp_next = _mlp_ln(x1[0], x1[1], *mlp_w, tm=512)
            attn_s = _decode_attention(*dec)
        else:
            hp_next, attn_s = _mlp_ln_decode(x1[0], *mlp_w, *dec, pps)

        cn_s, new_state = _decode_conv(state_conv[l], us.reshape(s, 1, ch), conv_w[l], conv_b[l], conv_ln_g[l],
                                       conv_ln_b[l])
        x1s = _merge_out(hs, xsb, attn_s.reshape(s, aw), cn_s.reshape(s, ch).astype(BF16), lw, alpha, s, emit_bf16=True)
        hs_next = _mlp_ln(x1s[0], x1s[1], *mlp_w, tm=s)
        outs[3].append(ksf.reshape(s, 1, n_heads, dh))
        outs[4].append(vsf.reshape(s, 1, n_heads, dh))
        outs[5].append(new_state)
        hp, hs = hp_next, hs_next

    return (hp.reshape(n, t, d), hs.reshape(s, 1, d)) + tuple(jnp.stack(o) for o in outs)
```

```python
import functools
import math

import numpy as np
import jax
import jax.numpy as jnp
from jax import lax
from jax.experimental import pallas as pl
from jax.experimental.pallas import tpu as pltpu

MOBA_BLOCK = 256
MOBA_TOPK = 3
MAX_DISTANCE = 128
LN_EPS = 1e-5
NEG_INF = -1e30
LOG2_E = math.log2(math.e)

V7X_VMEM_BYTES = 64 * 1024 * 1024
V7X_LANES = 128
V7X_SUBLANES = 8

F32 = jnp.float32
BF16 = jnp.bfloat16


def _vmem_limit(block_bytes):
    return int(min(max(2 * block_bytes + (16 << 20), 32 << 20), V7X_VMEM_BYTES - (4 << 20)))


def _params(semantics, block_bytes):
    return pltpu.CompilerParams(dimension_semantics=semantics, vmem_limit_bytes=_vmem_limit(block_bytes))


def _tile(n, want):
    t = min(n, want)
    while n % t:
        t -= 1
    return t


def _sigmoid(x):
    return 1.0 / (1.0 + jnp.exp(-x))


def _layer_norm(y, g, b):
    mu = jnp.mean(y, axis=-1, keepdims=True)
    d = y - mu
    var = jnp.mean(d * d, axis=-1, keepdims=True)
    return d * lax.rsqrt(var + LN_EPS) * g + b


def _dot(a, b):
    return jnp.dot(a, b, preferred_element_type=F32)


def _dot_nt(a, b):
    return lax.dot_general(a, b, (((1,), (1,)), ((), ())), preferred_element_type=F32)


def _proj_qkv_kernel(x_ref, w_ref, xb_ref, q_ref, kf_ref, kb_ref, vf_ref, vb_ref, *, scale):
    aw = q_ref.shape[1]
    xb = x_ref[...].astype(BF16)
    xb_ref[...] = xb
    q_ref[...] = (_dot(xb, w_ref[:, 0:aw]) * scale).astype(q_ref.dtype)
    for c, (f_ref, b_ref) in enumerate(((kf_ref, kb_ref), (vf_ref, vb_ref)), start=1):
        acc = _dot(xb, w_ref[:, c * aw:(c + 1) * aw])
        f_ref[...] = acc
        b_ref[...] = acc.astype(b_ref.dtype)


def _proj_glu_kernel(x_ref, wv_ref, wg_ref, u_ref):
    x = x_ref[...]
    u_ref[...] = _dot(x, wv_ref[...]) * _sigmoid(_dot(x, wg_ref[...]))


def _proj_qkv(x, w, aw, scale, tm, q_dtype):
    m, d = x.shape
    tm = _tile(m, tm)
    row = lambda width: pl.BlockSpec((tm, width), lambda i: (i, 0))
    blocks = tm * d * 6 + tm * aw * 16 + d * 3 * aw
    return pl.pallas_call(
        functools.partial(_proj_qkv_kernel, scale=scale),
        grid=(m // tm,),
        in_specs=[row(d), pl.BlockSpec((d, 3 * aw), lambda i: (0, 0), pipeline_mode=pl.Buffered(1))],
        out_specs=[row(d)] + [row(aw)] * 5,
        out_shape=[jax.ShapeDtypeStruct((m, d), BF16), jax.ShapeDtypeStruct((m, aw), q_dtype)]
        + [jax.ShapeDtypeStruct((m, aw), dt) for dt in (F32, BF16, F32, BF16)],
        compiler_params=_params(("parallel",), blocks),
        name="proj_qkv",
    )(x, w)


def _proj_glu(xb, w, col_val, col_gate, ncols, tm):
    m, d = xb.shape
    tn = _tile(ncols, 512)
    tm = _tile(m, tm)
    assert col_val % tn == 0 and col_gate % tn == 0
    blocks = tm * d * 2 + 2 * d * tn * 2 + tm * tn * 4
    return pl.pallas_call(
        _proj_glu_kernel,
        grid=(m // tm, ncols // tn),
        in_specs=[pl.BlockSpec((tm, d), lambda i, j: (i, 0)),
                  pl.BlockSpec((d, tn), lambda i, j: (0, col_val // tn + j)),
                  pl.BlockSpec((d, tn), lambda i, j: (0, col_gate // tn + j))],
        out_specs=pl.BlockSpec((tm, tn), lambda i, j: (i, j)),
        out_shape=jax.ShapeDtypeStruct((m, ncols), F32),
        compiler_params=_params(("parallel", "parallel"), blocks),
        name="proj_glu",
    )(xb, w, w)


def _rel_bucket_np(dist, num_buckets):
    n = np.maximum(dist, 0)
    max_exact = num_buckets // 2
    nf = np.maximum(n, 1).astype(np.float32)
    large = max_exact + (np.log(nf / np.float32(max_exact)) / np.float32(math.log(MAX_DISTANCE / max_exact))
                         * np.float32(num_buckets - max_exact)).astype(np.int32)
    large = np.minimum(large, num_buckets - 1)
    return np.where(n < max_exact, n, large).astype(np.int32)


def _bias_tiles_kernel(rb_ref, bucket_ref, o_ref, *, num_buckets):
    h = pl.program_id(0)
    far = rb_ref[num_buckets - 1, h]
    bucket = bucket_ref[...]
    acc = jnp.full(bucket.shape, NEG_INF, F32)
    for b in range(num_buckets):
        acc = jnp.where(bucket == b, (rb_ref[b, h] - far) * LOG2_E, acc)
    o_ref[...] = acc


def _prompt_bias_tiles(rel_bias):
    num_buckets, n_heads = rel_bias.shape
    assert MAX_DISTANCE <= MOBA_BLOCK + 1
    r = np.arange(MOBA_BLOCK)[:, None]
    c = np.arange(MOBA_BLOCK)[None, :]
    diag = np.where(r - c >= 0, _rel_bucket_np(r - c, num_buckets), -1)
    prev = _rel_bucket_np(r - c + MOBA_BLOCK, num_buckets)
    buckets = jnp.asarray(np.concatenate([prev, diag], axis=1).astype(np.int32))
    return pl.pallas_call(
        functools.partial(_bias_tiles_kernel, num_buckets=num_buckets),
        grid=(n_heads,),
        in_specs=[pl.BlockSpec(memory_space=pltpu.SMEM),
                  pl.BlockSpec((MOBA_BLOCK, 2 * MOBA_BLOCK), lambda h: (0, 0))],
        out_specs=pl.BlockSpec((None, MOBA_BLOCK, 2 * MOBA_BLOCK), lambda h: (h, 0, 0)),
        out_shape=jax.ShapeDtypeStruct((n_heads, MOBA_BLOCK, 2 * MOBA_BLOCK), F32),
        compiler_params=_params(("arbitrary",), 4 * MOBA_BLOCK * MOBA_BLOCK * 4),
        name="bias_tiles",
    )(rel_bias, buckets)


def _block_penalty_t(gate_t, n_valid):
    row = lax.broadcasted_iota(jnp.int32, gate_t.shape, 0)
    valid = row < n_valid
    pen = jnp.zeros(gate_t.shape, F32)
    for j in range(n_valid):
        gj = gate_t[j:j + 1, :]
        beats = ((gate_t > gj) | ((gate_t == gj) & (row < j))) & valid
        rank = jnp.sum(beats.astype(F32), axis=0, keepdims=True)
        pen = jnp.where(row == j, jnp.where(rank < MOBA_TOPK, 0.0, NEG_INF), pen)
    return pen


def _prompt_attn_kernel(q_ref, k_ref, v_ref, bias_ref, o_ref, kx_ref, pen_ref):
    t, dh = q_ref.shape
    blk = MOBA_BLOCK
    nblk = t // blk
    row = lax.broadcasted_iota(jnp.int32, (t, dh), 0)
    col = lax.broadcasted_iota(jnp.int32, (t, dh), 1)
    kx_ref[:, :dh] = k_ref[...]
    kx_ref[:, dh:] = (col == lax.shift_right_logical(row, blk.bit_length() - 1)).astype(BF16)
    in_block = (lax.shift_right_logical(lax.broadcasted_iota(jnp.int32, (nblk, t), 1), blk.bit_length() - 1)
                == lax.broadcasted_iota(jnp.int32, (nblk, t), 0))
    km = _dot(jnp.where(in_block, 1.0 / blk, 0.0).astype(BF16), k_ref[...])
    km_hi = km.astype(BF16)
    km_lo = (km - km_hi.astype(F32)).astype(BF16)
    eye = (lax.broadcasted_iota(jnp.int32, (blk, blk), 0) == lax.broadcasted_iota(jnp.int32, (blk, blk), 1)).astype(BF16)
    pen_ref[...] = jnp.zeros(pen_ref.shape, F32)
    def masked_logits(i):
        qi = q_ref[i * blk:(i + 1) * blk, :]
        nk = (i + 1) * blk
        if i <= MOBA_TOPK:
            return _dot_nt(qi, k_ref[0:nk, :])
        gate_t = _dot_nt(km_hi, qi) + _dot_nt(km_lo, qi)
        pen_ref[0:nblk, :] = _block_penalty_t(gate_t, i)
        pen = _dot_nt(eye, pen_ref[...].astype(BF16)).astype(BF16)
        return _dot_nt(jnp.concatenate([qi, pen], axis=1), kx_ref[0:nk, :])

    s_next = masked_logits(0)
    for i in range(nblk):
        nk = (i + 1) * blk
        s = s_next
        if i + 1 < nblk:
            s_next = masked_logits(i + 1)
        near = min(nk, 2 * blk)
        s_near = s[:, nk - near:] + bias_ref[:, 2 * blk - near:]
        s = s_near if near == nk else jnp.concatenate([s[:, :nk - near], s_near], axis=1)
        m = jnp.max(s, axis=1, keepdims=True)
        p = jnp.exp2(s - m)
        l = jnp.sum(p, axis=1, keepdims=True)
        o = _dot(p.astype(BF16), v_ref[0:nk, :])
        o_ref[i * blk:(i + 1) * blk, :] = (o / l).astype(o_ref.dtype)


def _prompt_attention(q, k, v, bias_tiles, n_heads):
    n, t, width = q.shape
    dh = width // n_heads
    nblk = t // MOBA_BLOCK
    assert dh == V7X_LANES and t % MOBA_BLOCK == 0 and nblk <= V7X_LANES and MOBA_BLOCK & (MOBA_BLOCK - 1) == 0
    seq_spec = pl.BlockSpec((None, t, dh), lambda h, b: (b, 0, h))
    blocks = 4 * t * dh * 2 + 2 * MOBA_BLOCK * MOBA_BLOCK * 4 + t * 2 * dh * 2 + 6 * MOBA_BLOCK * t * 4
    return pl.pallas_call(
        _prompt_attn_kernel,
        grid=(n_heads, n),
        in_specs=[seq_spec, seq_spec, seq_spec,
                  pl.BlockSpec((None, MOBA_BLOCK, 2 * MOBA_BLOCK), lambda h, b: (h, 0, 0))],
        out_specs=seq_spec,
        out_shape=jax.ShapeDtypeStruct((n, t, width), BF16),
        scratch_shapes=[pltpu.VMEM((t, 2 * dh), BF16), pltpu.VMEM((dh, MOBA_BLOCK), F32)],
        compiler_params=_params(("parallel", "parallel"), blocks),
        name="prompt_attn",
    )(q, k, v, bias_tiles)


DECODE_ROWS = 16


def _lane_sum(x):
    return jnp.broadcast_to(jnp.sum(x, axis=-1, keepdims=True), x.shape)


def _page_bias(relb_ref, page_buckets):
    if len(set(page_buckets)) == 1:
        return relb_ref[page_buckets[0]][None] * LOG2_E
    return jnp.stack([relb_ref[b] for b in page_buckets]) * LOG2_E


def _decode_pages(part, q_ref, kn_ref, vn_ref, relb_ref, k_refs, v_refs, o_ref, m_ref, l_ref, g_ref, acc_ref, *,
                  n_pages, pages_per_block, buckets, other_work=()):
    pps = len(k_refs)
    parts = n_pages // pps
    page, h, dh = k_refs[0].shape
    q = q_ref[...]
    rows = min(DECODE_ROWS, page)
    steps_per_page = -(-page // (2 * rows))
    interleave = {}
    for c, job in enumerate(other_work):
        interleave.setdefault(c * pps * steps_per_page // len(other_work), []).append(job)
    for p in range(pps):
        page_buckets = [buckets[(a * pps + p) * page:(a * pps + p + 1) * page] for a in range(parts)]
        shared = set(b for pb in page_buckets for b in pb)
        shared = shared.pop() if len(shared) == 1 else None
        m = jnp.full((h, dh), NEG_INF, F32)
        l = acc = gsum = jnp.zeros((h, dh), F32)
        for r0 in range(0, page, 2 * rows):
            for job in interleave.get(p * steps_per_page + r0 // (2 * rows), ()):
                job()
            groups = [slice(r, r + rows) for r in range(r0, min(r0 + 2 * rows, page), rows)]
            for g in groups:
                s = _lane_sum(k_refs[p][g] * q[None])
                gsum = gsum + jnp.sum(s, axis=0)
                if shared is None:
                    bias = _page_bias(relb_ref, page_buckets[0][g])
                    for a in range(1, parts):
                        if page_buckets[a][g] != page_buckets[0][g]:
                            bias = jnp.where(part == a, _page_bias(relb_ref, page_buckets[a][g]), bias)
                    s = s + bias
                m_new = jnp.maximum(m, jnp.max(s, axis=0))
                scale = jnp.exp2(m - m_new)
                e = jnp.exp2(s - m_new[None])
                l = l * scale + jnp.sum(e, axis=0)
                acc = acc * scale + jnp.sum(e * v_refs[p][g], axis=0)
                m = m_new
        idx = part * pps + p
        m_ref[idx] = m if shared is None else m + relb_ref[shared] * LOG2_E
        l_ref[idx] = l
        g_ref[idx] = gsum
        acc_ref[idx] = acc

    def merge():
        n_blocks = n_pages // pages_per_block
        gate = [sum(g_ref[p] for p in range(j * pages_per_block, (j + 1) * pages_per_block)) for j in range(n_blocks)]
        s_own = _lane_sum(q * kn_ref[...]) + relb_ref[0] * LOG2_E
        m_tot = s_own
        sel = []
        for j in range(n_blocks):
            rank = jnp.zeros((h, dh), F32)
            for j2 in range(n_blocks):
                if j2 != j:
                    beats = (gate[j2] > gate[j]) | ((gate[j2] == gate[j]) & (j2 < j))
                    rank = rank + beats.astype(F32)
            sel.append(rank < MOBA_TOPK)
            for p in range(j * pages_per_block, (j + 1) * pages_per_block):
                m_tot = jnp.maximum(m_tot, jnp.where(sel[j], m_ref[p], NEG_INF))
        w_own = jnp.exp2(s_own - m_tot)
        num = w_own * vn_ref[...]
        den = w_own
        for j in range(n_blocks):
            for p in range(j * pages_per_block, (j + 1) * pages_per_block):
                w = jnp.where(sel[j], jnp.exp2(m_ref[p] - m_tot), 0.0)
                num = num + w * acc_ref[p]
                den = den + w * l_ref[p]
        o_ref[...] = (num / den).astype(o_ref.dtype)

    if parts == 1:
        merge()
    else:
        pl.when(part == parts - 1)(merge)


def _decode_attn_kernel(pt_ref, q_ref, kn_ref, vn_ref, relb_ref, *refs, n_pages, pages_per_block, buckets):
    del pt_ref
    _decode_pages(0, q_ref, kn_ref, vn_ref, relb_ref, refs[:n_pages], refs[n_pages:2 * n_pages], *refs[2 * n_pages:],
                  n_pages=n_pages, pages_per_block=pages_per_block, buckets=buckets)


def _decode_specs(page_table, cache_k, rel_bias, h, dh, pages_per_step, seq_of, part_of):
    page = cache_k.shape[1]
    n_pages = page_table.shape[1]
    past = n_pages * page
    num_buckets = rel_bias.shape[0]
    assert dh == V7X_LANES and MOBA_BLOCK % page == 0 and past % MOBA_BLOCK == 0 and n_pages % pages_per_step == 0
    buckets = tuple(int(b) for b in _rel_bucket_np(past - np.arange(past), num_buckets))
    relb = jnp.broadcast_to(rel_bias[:, :, None], (num_buckets, h, dh))
    tok_spec = pl.BlockSpec((None, h, dh), lambda *g: (seq_of(*g[:-1]), 0, 0))
    relb_spec = pl.BlockSpec((num_buckets, h, dh), lambda *g: (0, 0, 0))
    page_specs = [pl.BlockSpec((None, page, h, dh),
                               lambda *g, p=p: (g[-1][seq_of(*g[:-1]), part_of(*g[:-1]) * pages_per_step + p], 0, 0, 0))
                  for p in range(pages_per_step)]
    scratch = [pltpu.VMEM((n_pages, h, dh), F32)] * 4
    return buckets, relb, tok_spec, relb_spec, page_specs, scratch


def _decode_attention(q, k_new, v_new, cache_k, cache_v, page_table, rel_bias):
    s, h, dh = q.shape
    page = cache_k.shape[1]
    n_pages = page_table.shape[1]
    buckets, relb, tok_spec, relb_spec, page_specs, scratch = _decode_specs(
        page_table, cache_k, rel_bias, h, dh, n_pages, seq_of=lambda b: b, part_of=lambda b: 0)
    blocks = 2 * n_pages * page * h * dh * 4 + relb.size * 4
    kernel = functools.partial(_decode_attn_kernel, n_pages=n_pages, pages_per_block=MOBA_BLOCK // page,
                               buckets=buckets)
    return pl.pallas_call(
        kernel,
        grid_spec=pltpu.PrefetchScalarGridSpec(
            num_scalar_prefetch=1,
            grid=(s,),
            in_specs=[tok_spec, tok_spec, tok_spec, relb_spec] + page_specs + page_specs,
            out_specs=tok_spec,
            scratch_shapes=scratch),
        out_shape=jax.ShapeDtypeStruct((s, h, dh), BF16),
        compiler_params=_params(("parallel",), blocks),
        name="decode_attn",
    )(page_table, q, k_new, v_new, relb, *([cache_k] * n_pages), *([cache_v] * n_pages))


CONV_HALO = 32
CONV_ROWS = 256
GLU_CHUNK = 256


def _causal_conv_columns(ext_ref, w_ref, conv_ref, c0, tt):
    kw = w_ref.shape[0]
    first = CONV_HALO - (kw - 1)
    rows = min(CONV_ROWS, tt)
    span = rows + CONV_HALO
    cs = slice(c0, c0 + V7X_LANES)
    for r0 in range(0, tt, rows):
        x = ext_ref[r0:r0 + span, cs]
        acc = jnp.zeros((rows, V7X_LANES), F32)
        for b in range(V7X_SUBLANES):
            taps = [s for s in range(first, first + kw) if s % V7X_SUBLANES == b]
            assert all(s + rows <= span for s in taps)
            xb = x if b == 0 else pltpu.roll(x, span - b, axis=0)
            for s in taps:
                acc = acc + xb[s - b:s - b + rows] * w_ref[s - first:s - first + 1, cs]
        conv_ref[r0:r0 + rows, cs] = acc


def _glu_conv_kernel(x_ref, wv_ref, wg_ref, w_ref, cb_ref, g_ref, b_ref, *refs):
    n_hosted = (len(refs) - 4) // 2
    o_ref, tail_ref = refs[n_hosted:n_hosted + 2]
    ext_ref, conv_ref = refs[-2:]
    for src_ref, dst_ref in zip(refs[:n_hosted], refs[n_hosted + 2:-2]):
        dst_ref[...] = src_ref[...].astype(dst_ref.dtype)
    tt = x_ref.shape[0]
    ch = wv_ref.shape[1]
    i = pl.program_id(1)

    @pl.when(i == 0)
    def _():
        ext_ref[0:CONV_HALO, :] = jnp.zeros((CONV_HALO, ch), F32)

    @pl.when(i > 0)
    def _():
        ext_ref[0:CONV_HALO, :] = ext_ref[tt:tt + CONV_HALO, :]

    x = x_ref[...]
    chunk = min(GLU_CHUNK, ch)
    for c0 in range(0, ch, chunk):
        cols = slice(c0, c0 + chunk)
        ext_ref[CONV_HALO:CONV_HALO + tt, cols] = _dot(x, wv_ref[:, cols]) * _sigmoid(_dot(x, wg_ref[:, cols]))
        for c in range(c0, c0 + chunk, V7X_LANES):
            _causal_conv_columns(ext_ref, w_ref, conv_ref, c, tt)
    y = _layer_norm(conv_ref[...] + cb_ref[...], g_ref[...], b_ref[...])
    o_ref[...] = (y * _sigmoid(y)).astype(o_ref.dtype)

    @pl.when(i == pl.num_programs(1) - 1)
    def _():
        tail_ref[...] = ext_ref[tt:tt + CONV_HALO, :]


def _prompt_glu_conv(xb, w, col_val, col_gate, conv_w, conv_b, ln_g, ln_b, to_bf16):
    n, t, d = xb.shape
    kw, ch = conv_w.shape
    tt = _tile(t, 256)
    nt = t // tt
    assert kw - 1 <= CONV_HALO <= tt and tt % V7X_SUBLANES == 0 and ch % min(GLU_CHUNK, ch) == 0
    assert col_val % ch == 0 and col_gate % ch == 0
    hosted = to_bf16 if all(a.shape[0] % (n * nt * 2 * V7X_SUBLANES) == 0 for a in to_bf16) else ()
    slab = lambda a: pl.BlockSpec((a.shape[0] // (n * nt), a.shape[1]), lambda b, i: (b * nt + i, 0))
    vec = pl.BlockSpec((1, ch), lambda b, i: (0, 0))
    wspec = lambda col: pl.BlockSpec((d, ch), lambda b, i: (0, col // ch), pipeline_mode=pl.Buffered(1))
    blocks = (tt * d * 2 + d * ch * 2 + (2 * tt + 2 * CONV_HALO) * ch * 4 + tt * ch * 2
              + sum(a.size * 6 // (n * nt) for a in hosted))
    cn, tail, *copies = pl.pallas_call(
        _glu_conv_kernel,
        grid=(n, nt),
        in_specs=[pl.BlockSpec((None, tt, d), lambda b, i: (b, i, 0)), wspec(col_val), wspec(col_gate),
                  pl.BlockSpec((kw, ch), lambda b, i: (0, 0)), vec, vec, vec] + [slab(a) for a in hosted],
        out_specs=[pl.BlockSpec((None, tt, ch), lambda b, i: (b, i, 0)),
                   pl.BlockSpec((None, CONV_HALO, ch), lambda b, i: (b, 0, 0))] + [slab(a) for a in hosted],
        out_shape=[jax.ShapeDtypeStruct((n, t, ch), BF16), jax.ShapeDtypeStruct((n, CONV_HALO, ch), F32)]
        + [jax.ShapeDtypeStruct(a.shape, BF16) for a in hosted],
        scratch_shapes=[pltpu.VMEM((CONV_HALO + tt, ch), F32), pltpu.VMEM((tt, ch), F32)],
        compiler_params=_params(("arbitrary", "arbitrary"), blocks),
        name="glu_conv",
    )(xb, w, w, conv_w, conv_b.reshape(1, ch), ln_g.reshape(1, ch), ln_b.reshape(1, ch), *hosted)
    return cn, tail, (copies if hosted else [a.astype(BF16) for a in to_bf16])


def _decode_conv_kernel(state_ref, u_ref, w_ref, cb_ref, g_ref, b_ref, o_ref, new_state_ref):
    kw = w_ref.shape[0]
    w_hist = w_ref[0:kw - 1, :]
    w_last = w_ref[kw - 1:kw, :]
    for b in range(state_ref.shape[0]):
        u = u_ref[b]
        conv = jnp.sum(state_ref[b] * w_hist, axis=0, keepdims=True) + u * w_last
        y = _layer_norm(conv + cb_ref[...], g_ref[...], b_ref[...])
        o_ref[b] = (y * _sigmoid(y)).astype(o_ref.dtype)
        new_state_ref[b, 0:kw - 2, :] = state_ref[b, 1:kw - 1, :]
        new_state_ref[b, kw - 2:kw - 1, :] = u


def _decode_conv(state, u, conv_w, conv_b, ln_g, ln_b):
    s, hist, ch = state.shape
    kw = conv_w.shape[0]
    ts = _tile(s, 16)
    vec = pl.BlockSpec((1, ch), lambda i: (0, 0))
    tok = pl.BlockSpec((ts, 1, ch), lambda i: (i, 0, 0))
    hist_spec = pl.BlockSpec((ts, hist, ch), lambda i: (i, 0, 0))
    blocks = 2 * ts * 32 * ch * 4 + 2 * ts * V7X_SUBLANES * ch * 4
    return pl.pallas_call(
        _decode_conv_kernel,
        grid=(s // ts,),
        in_specs=[hist_spec, tok, pl.BlockSpec((kw, ch), lambda i: (0, 0)), vec, vec, vec],
        out_specs=[tok, hist_spec],
        out_shape=[jax.ShapeDtypeStruct((s, 1, ch), F32), jax.ShapeDtypeStruct((s, hist, ch), F32)],
        compiler_params=_params(("parallel",), blocks),
        name="decode_conv",
    )(state, u, conv_w, conv_b.reshape(1, ch), ln_g.reshape(1, ch), ln_b.reshape(1, ch))


def _merge_kernel(x_ref, a_ref, c_ref, wga_ref, wgc_ref, wao_ref, wco_ref, o_ref):
    x = x_ref[...]
    mixed = (_sigmoid(_dot(x, wga_ref[...])) * _dot(a_ref[...], wao_ref[...])
             + _sigmoid(_dot(x, wgc_ref[...])) * _dot(c_ref[...], wco_ref[...]))
    o_ref[...] = mixed.astype(o_ref.dtype)


def _merge(xb, attn, cn, w_in, col_ga, col_gc, w_ao, w_co, tm):
    m, d = xb.shape
    wa = attn.shape[1]
    wc = cn.shape[1]
    tn = _tile(d, 512)
    tm = _tile(m, tm)
    assert col_ga % tn == 0 and col_gc % tn == 0
    blocks = tm * (d + wa + wc + tn) * 2 + (2 * d + wa + wc) * tn * 2
    return pl.pallas_call(
        _merge_kernel,
        grid=(m // tm, d // tn),
        in_specs=[pl.BlockSpec((tm, d), lambda i, j: (i, 0)),
                  pl.BlockSpec((tm, wa), lambda i, j: (i, 0)),
                  pl.BlockSpec((tm, wc), lambda i, j: (i, 0)),
                  pl.BlockSpec((d, tn), lambda i, j: (0, col_ga // tn + j)),
                  pl.BlockSpec((d, tn), lambda i, j: (0, col_gc // tn + j)),
                  pl.BlockSpec((wa, tn), lambda i, j: (0, j)),
                  pl.BlockSpec((wc, tn), lambda i, j: (0, j))],
        out_specs=pl.BlockSpec((tm, tn), lambda i, j: (i, j)),
        out_shape=jax.ShapeDtypeStruct((m, d), BF16),
        compiler_params=_params(("parallel", "parallel"), blocks),
        name="merge",
    )(xb, attn, cn, w_in, w_in, w_ao, w_co)


def _out_ln_kernel(x_ref, mixed_ref, w_ref, g_ref, b_ref, of_ref, *maybe_ob_ref, alpha):
    tm = x_ref.shape[0]
    halves = [slice(0, tm // 2), slice(tm // 2, tm)] if tm % (4 * V7X_SUBLANES) == 0 else [slice(0, tm)]
    acc = [_dot(mixed_ref[rows, :], w_ref[...]) for rows in halves]
    for rows, a in zip(halves, acc):
        y = _layer_norm(alpha * x_ref[rows, :] + a, g_ref[...], b_ref[...])
        of_ref[rows, :] = y
        for ob_ref in maybe_ob_ref:
            ob_ref[rows, :] = y.astype(ob_ref.dtype)


def _out_ln(x, mixed, w_out, g, b, alpha, tm, emit_bf16):
    m, d = x.shape
    tm = _tile(m, tm)
    vec = pl.BlockSpec((1, d), lambda i: (0, 0))
    row = pl.BlockSpec((tm, d), lambda i: (i, 0))
    blocks = tm * d * (4 + 2 + 4 + 2) + d * d * 2
    return pl.pallas_call(
        functools.partial(_out_ln_kernel, alpha=alpha),
        grid=(m // tm,),
        in_specs=[row, row, pl.BlockSpec((d, d), lambda i: (0, 0)), vec, vec],
        out_specs=[row, row] if emit_bf16 else [row],
        out_shape=[jax.ShapeDtypeStruct((m, d), F32)] + ([jax.ShapeDtypeStruct((m, d), BF16)] if emit_bf16 else []),
        compiler_params=_params(("parallel",), blocks),
        name="out_ln",
    )(x, mixed, w_out, g.reshape(1, d), b.reshape(1, d))


def _mlp_ln_kernel(xf_ref, xb_ref, w1_ref, w2_ref, g_ref, b_ref, o_ref, *, alpha):
    f = pl.program_id(1)

    @pl.when(f == 0)
    def _():
        o_ref[...] = alpha * xf_ref[...]

    hid = jnp.maximum(_dot(xb_ref[...], w1_ref[...]), 0.0)
    o_ref[...] += _dot((hid * hid).astype(BF16), w2_ref[...])

    @pl.when(f == pl.num_programs(1) - 1)
    def _():
        o_ref[...] = _layer_norm(o_ref[...], g_ref[...], b_ref[...])


def _mlp_ln(xf, xb, w1, w2, g, b, alpha, tm):
    m, d = xf.shape
    dff = w1.shape[1]
    tf = _tile(dff, 1024)
    tm = _tile(m, tm)
    vec = pl.BlockSpec((1, d), lambda i, f: (0, 0))
    row = pl.BlockSpec((tm, d), lambda i, f: (i, 0))
    blocks = tm * d * (4 + 2 + 4) + 2 * d * tf * 2 + tm * tf * 4
    return pl.pallas_call(
        functools.partial(_mlp_ln_kernel, alpha=alpha),
        grid=(m // tm, dff // tf),
        in_specs=[row, row, pl.BlockSpec((d, tf), lambda i, f: (0, f)), pl.BlockSpec((tf, d), lambda i, f: (f, 0)),
                  vec, vec],
        out_specs=row,
        out_shape=jax.ShapeDtypeStruct((m, d), F32),
        compiler_params=_params(("parallel", "arbitrary"), blocks),
        name="mlp_ln",
    )(xf, xb, w1, w2, g.reshape(1, d), b.reshape(1, d))


def _mlp_ln_decode_kernel(pt_ref, xf_ref, w1_ref, w2_ref, g_ref, b_ref, q_ref, kn_ref, vn_ref, relb_ref, *refs,
                          alpha, pages_per_step, n_pages, pages_per_block, buckets):
    del pt_ref
    k_refs, v_refs = refs[:pages_per_step], refs[pages_per_step:2 * pages_per_step]
    o_ref, ao_ref, xb_ref, hid_ref, m_ref, l_ref, gs_ref, acc_ref = refs[2 * pages_per_step:]
    part = pl.program_id(2)
    f = pl.program_id(1) * pl.num_programs(2) + part
    nf = pl.num_programs(1) * pl.num_programs(2)

    @pl.when(f == 0)
    def _():
        x = xf_ref[...]
        o_ref[...] = alpha * x
        xb_ref[...] = x.astype(BF16)

    tf, d = w2_ref.shape
    wh = math.gcd(tf, MLP_CHUNK)
    wo = math.gcd(d, MLP_CHUNK)

    def hid_chunk(c):
        hid = jnp.maximum(_dot(xb_ref[...], w1_ref[:, c * wh:(c + 1) * wh]), 0.0)
        hid_ref[:, c * wh:(c + 1) * wh] = (hid * hid).astype(BF16)

    def out_chunk(c):
        o_ref[:, c * wo:(c + 1) * wo] += _dot(hid_ref[...], w2_ref[:, c * wo:(c + 1) * wo])

    chunks = ([functools.partial(hid_chunk, c) for c in range(tf // wh)]
              + [functools.partial(out_chunk, c) for c in range(d // wo)])
    _decode_pages(part, q_ref, kn_ref, vn_ref, relb_ref, k_refs, v_refs, ao_ref, m_ref, l_ref, gs_ref, acc_ref,
                  n_pages=n_pages, pages_per_block=pages_per_block, buckets=buckets, other_work=chunks)

    @pl.when(f == nf - 1)
    def _():
        o_ref[...] = _layer_norm(o_ref[...], g_ref[...], b_ref[...])


MLP_CHUNK = 256
FUSED_MLP_ROWS = 512
FUSED_MLP_COLS = 1024


def _fused_pages_per_step(m, dff, s, n_pages):
    nf = dff // _tile(dff, FUSED_MLP_COLS)
    steps = (m // _tile(m, FUSED_MLP_ROWS)) * nf
    if (s * n_pages) % steps or n_pages % ((s * n_pages) // steps) or nf % (n_pages // ((s * n_pages) // steps)):
        return None
    return (s * n_pages) // steps


def _mlp_ln_decode(xf, w1, w2, g, b, alpha, q, k_new, v_new, cache_k, cache_v, page_table, rel_bias, pps):
    m, d = xf.shape
    dff = w1.shape[1]
    tm, tf = _tile(m, FUSED_MLP_ROWS), _tile(dff, FUSED_MLP_COLS)
    nf = dff // tf
    s, h, dh = q.shape
    page = cache_k.shape[1]
    n_pages = page_table.shape[1]
    parts = n_pages // pps
    spr = nf // parts
    assert (m // tm) * nf == s * parts and nf == spr * parts
    buckets, relb, tok_spec, relb_spec, page_specs, scratch = _decode_specs(
        page_table, cache_k, rel_bias, h, dh, pps,
        seq_of=lambda i, j, a: i * spr + j, part_of=lambda i, j, a: a)
    vec = pl.BlockSpec((1, d), lambda i, j, a, pt: (0, 0))
    row = pl.BlockSpec((tm, d), lambda i, j, a, pt: (i, 0))
    blocks = tm * d * 9 + 2 * d * tf * 2 + tm * tf * 4 + 2 * pps * page * h * dh * 4 + relb.size * 4
    kernel = functools.partial(_mlp_ln_decode_kernel, alpha=alpha, pages_per_step=pps, n_pages=n_pages,
                               pages_per_block=MOBA_BLOCK // page, buckets=buckets)
    return pl.pallas_call(
        kernel,
        grid_spec=pltpu.PrefetchScalarGridSpec(
            num_scalar_prefetch=1,
            grid=(m // tm, spr, parts),
            in_specs=[row, pl.BlockSpec((d, tf), lambda i, j, a, pt: (0, j * parts + a)),
                      pl.BlockSpec((tf, d), lambda i, j, a, pt: (j * parts + a, 0)),
                      vec, vec, tok_spec, tok_spec, tok_spec, relb_spec] + page_specs + page_specs,
            out_specs=[row, tok_spec],
            scratch_shapes=[pltpu.VMEM((tm, d), BF16), pltpu.VMEM((tm, tf), BF16)] + scratch),
        out_shape=[jax.ShapeDtypeStruct((m, d), F32), jax.ShapeDtypeStruct((s, h, dh), BF16)],
        compiler_params=_params(("arbitrary", "arbitrary", "arbitrary"), blocks),
        name="mlp_ln_decode_attn",
    )(page_table, xf, w1, w2, g.reshape(1, d), b.reshape(1, d), q, k_new, v_new, relb,
      *([cache_k] * pps), *([cache_v] * pps))


def _merge_out(x2d, xb, attn, cn, lw, alpha, tm, emit_bf16):
    mixed = _merge(xb, attn, cn, lw["w_in"], lw["col_ga"], lw["col_gc"], lw["w_attn_out"], lw["w_conv_out"], tm=2 * tm)
    return _out_ln(x2d, mixed, lw["w_out"], lw["ln1_g"], lw["ln1_b"], alpha, tm, emit_bf16)


def kernel(x_prompt, x_sample, cache_k, cache_v, state_conv, page_table, rel_bias, w_in, w_attn_out, conv_w, conv_b,
           conv_ln_g, conv_ln_b, w_conv_out, w_out, ln1_g, ln1_b, w_ff1, w_ff2, ln2_g, ln2_b):
    depth = w_in.shape[0]
    n, t, d = x_prompt.shape
    s, ts, _ = x_sample.shape
    n_heads, dh = cache_k.shape[-2:]
    aw = n_heads * dh
    ch = conv_w.shape[-1]
    alpha = (2 * depth) ** 0.25
    scale = dh ** -0.5 * LOG2_E
    assert ts == 1 and w_in.shape[-1] == 3 * aw + 2 * ch + 2 * d
    col_val, col_gate = 3 * aw, 3 * aw + ch
    col_ga, col_gc = 3 * aw + 2 * ch, 3 * aw + 2 * ch + d

    bias_tiles = _prompt_bias_tiles(rel_bias)
    hp = x_prompt.reshape(n * t, d)
    hs = x_sample.reshape(s, d)
    outs = [[] for _ in range(6)]
    for l in range(depth):
        wi = w_in[l].astype(BF16)

        xb, q, kf, kb, vf, vb = _proj_qkv(hp, wi, aw, scale, tm=512, q_dtype=BF16)
        attn = _prompt_attention(q.reshape(n, t, aw), kb.reshape(n, t, aw), vb.reshape(n, t, aw), bias_tiles, n_heads)
        late = dict(w_attn_out=w_attn_out[l], w_conv_out=w_conv_out[l], w_out=w_out[l], w_ff1=w_ff1[l], w_ff2=w_ff2[l])
        cn, u_tail, late_bf16 = _prompt_glu_conv(xb.reshape(n, t, d), wi, col_val, col_gate, conv_w[l], conv_b[l],
                                                 conv_ln_g[l], conv_ln_b[l], to_bf16=tuple(late.values()))
        lw = dict(zip(late, late_bf16), w_in=wi, ln1_g=ln1_g[l], ln1_b=ln1_b[l], ln2_g=ln2_g[l], ln2_b=ln2_b[l],
                  col_ga=col_ga, col_gc=col_gc)
        outs[0].append(kf.reshape(n, t, n_heads, dh))
        outs[1].append(vf.reshape(n, t, n_heads, dh))
        outs[2].append(u_tail[:, CONV_HALO - (conv_w.shape[1] - 1):, :])

        xsb, qs, ksf, _, vsf, _ = _proj_qkv(hs, wi, aw, scale, tm=s, q_dtype=F32)
        us = _proj_glu(xsb, wi, col_val, col_gate, ch, tm=s)
        dec = (qs.reshape(s, n_heads, dh), ksf.reshape(s, n_heads, dh), vsf.reshape(s, n_heads, dh),
               cache_k[l], cache_v[l], page_table, rel_bias)

        mlp_w = (lw["w_ff1"], lw["w_ff2"], lw["ln2_g"], lw["ln2_b"], alpha)
        pps = _fused_pages_per_step(n * t, w_ff1.shape[-1], s, page_table.shape[1])
        x1 = _merge_out(hp, xb, attn.reshape(n * t, aw), cn.reshape(n * t, ch), lw, alpha, 512, emit_bf16=pps is None)
        if pps is None:
            hp_next = _mlp_ln(x1[0], x1[1], *mlp_w, tm=512)
            attn_s = _decode_attention(*dec)
        else:
            hp_next, attn_s = _mlp_ln_decode(x1[0], *mlp_w, *dec, pps)

        cn_s, new_state = _decode_conv(state_conv[l], us.reshape(s, 1, ch), conv_w[l], conv_b[l], conv_ln_g[l],
                                       conv_ln_b[l])
        x1s = _merge_out(hs, xsb, attn_s.reshape(s, aw), cn_s.reshape(s, ch).astype(BF16), lw, alpha, s, emit_bf16=True)
        hs_next = _mlp_ln(x1s[0], x1s[1], *mlp_w, tm=s)
        outs[3].append(ksf.reshape(s, 1, n_heads, dh))
        outs[4].append(vsf.reshape(s, 1, n_heads, dh))
        outs[5].append(new_state)
        hp, hs = hp_next, hs_next

    return (hp.reshape(n, t, d), hs.reshape(s, 1, d)) + tuple(jnp.stack(o) for o in outs)
```

```python
import functools
import math

import numpy as np
import jax
import jax.numpy as jnp
from jax import lax
from jax.experimental import pallas as pl
from jax.experimental.pallas import tpu as pltpu

MOBA_BLOCK = 256
MOBA_TOPK = 3
MAX_DISTANCE = 128
LN_EPS = 1e-5
NEG_INF = -1e30
LOG2_E = math.log2(math.e)

V7X_VMEM_BYTES = 64 * 1024 * 1024
V7X_LANES = 128
V7X_SUBLANES = 8

F32 = jnp.float32
BF16 = jnp.bfloat16


def _vmem_limit(block_bytes):
    return int(min(max(2 * block_bytes + (16 << 20), 32 << 20), V7X_VMEM_BYTES - (4 << 20)))


def _params(semantics, block_bytes):
    return pltpu.CompilerParams(dimension_semantics=semantics, vmem_limit_bytes=_vmem_limit(block_bytes))


def _tile(n, want):
    t = min(n, want)
    while n % t:
        t -= 1
    return t


def _sigmoid(x):
    return 1.0 / (1.0 + jnp.exp(-x))


def _layer_norm(y, g, b):
    mu = jnp.mean(y, axis=-1, keepdims=True)
    d = y - mu
    var = jnp.mean(d * d, axis=-1, keepdims=True)
    return d * lax.rsqrt(var + LN_EPS) * g + b


def _dot(a, b):
    return jnp.dot(a, b, preferred_element_type=F32)


def _dot_nt(a, b):
    return lax.dot_general(a, b, (((1,), (1,)), ((), ())), preferred_element_type=F32)


def _proj_qkv_kernel(x_ref, w_ref, xb_ref, q_ref, kf_ref, kb_ref, vf_ref, vb_ref, *, scale):
    aw = q_ref.shape[1]
    xb = x_ref[...].astype(BF16)
    xb_ref[...] = xb
    q_ref[...] = (_dot(xb, w_ref[:, 0:aw]) * scale).astype(q_ref.dtype)
    for c, (f_ref, b_ref) in enumerate(((kf_ref, kb_ref), (vf_ref, vb_ref)), start=1):
        acc = _dot(xb, w_ref[:, c * aw:(c + 1) * aw])
        f_ref[...] = acc
        b_ref[...] = acc.astype(b_ref.dtype)


def _proj_glu_kernel(x_ref, wv_ref, wg_ref, u_ref):
    x = x_ref[...]
    u_ref[...] = _dot(x, wv_ref[...]) * _sigmoid(_dot(x, wg_ref[...]))


def _proj_qkv(x, w, aw, scale, tm, q_dtype):
    m, d = x.shape
    tm = _tile(m, tm)
    row = lambda width: pl.BlockSpec((tm, width), lambda i: (i, 0))
    blocks = tm * d * 6 + tm * aw * 16 + d * 3 * aw
    return pl.pallas_call(
        functools.partial(_proj_qkv_kernel, scale=scale),
        grid=(m // tm,),
        in_specs=[row(d), pl.BlockSpec((d, 3 * aw), lambda i: (0, 0), pipeline_mode=pl.Buffered(1))],
        out_specs=[row(d)] + [row(aw)] * 5,
        out_shape=[jax.ShapeDtypeStruct((m, d), BF16), jax.ShapeDtypeStruct((m, aw), q_dtype)]
        + [jax.ShapeDtypeStruct((m, aw), dt) for dt in (F32, BF16, F32, BF16)],
        compiler_params=_params(("parallel",), blocks),
        name="proj_qkv",
    )(x, w)


def _proj_glu(xb, w, col_val, col_gate, ncols, tm):
    m, d = xb.shape
    tn = _tile(ncols, 512)
    tm = _tile(m, tm)
    assert col_val % tn == 0 and col_gate % tn == 0
    blocks = tm * d * 2 + 2 * d * tn * 2 + tm * tn * 4
    return pl.pallas_call(
        _proj_glu_kernel,
        grid=(m // tm, ncols // tn),
        in_specs=[pl.BlockSpec((tm, d), lambda i, j: (i, 0)),
                  pl.BlockSpec((d, tn), lambda i, j: (0, col_val // tn + j)),
                  pl.BlockSpec((d, tn), lambda i, j: (0, col_gate // tn + j))],
        out_specs=pl.BlockSpec((tm, tn), lambda i, j: (i, j)),
        out_shape=jax.ShapeDtypeStruct((m, ncols), F32),
        compiler_params=_params(("parallel", "parallel"), blocks),
        name="proj_glu",
    )(xb, w, w)


def _rel_bucket_np(dist, num_buckets):
    n = np.maximum(dist, 0)
    max_exact = num_buckets // 2
    nf = np.maximum(n, 1).astype(np.float32)
    large = max_exact + (np.log(nf / np.float32(max_exact)) / np.float32(math.log(MAX_DISTANCE / max_exact))
                         * np.float32(num_buckets - max_exact)).astype(np.int32)
    large = np.minimum(large, num_buckets - 1)
    return np.where(n < max_exact, n, large).astype(np.int32)


def _bias_tiles_kernel(rb_ref, bucket_ref, o_ref, *, num_buckets):
    h = pl.program_id(0)
    far = rb_ref[num_buckets - 1, h]
    bucket = bucket_ref[...]
    acc = jnp.full(bucket.shape, NEG_INF, F32)
    for b in range(num_buckets):
        acc = jnp.where(bucket == b, (rb_ref[b, h] - far) * LOG2_E, acc)
    o_ref[...] = acc


def _prompt_bias_tiles(rel_bias):
    num_buckets, n_heads = rel_bias.shape
    assert MAX_DISTANCE <= MOBA_BLOCK + 1
    r = np.arange(MOBA_BLOCK)[:, None]
    c = np.arange(MOBA_BLOCK)[None, :]
    diag = np.where(r - c >= 0, _rel_bucket_np(r - c, num_buckets), -1)
    prev = _rel_bucket_np(r - c + MOBA_BLOCK, num_buckets)
    buckets = jnp.asarray(np.concatenate([prev, diag], axis=1).astype(np.int32))
    return pl.pallas_call(
        functools.partial(_bias_tiles_kernel, num_buckets=num_buckets),
        grid=(n_heads,),
        in_specs=[pl.BlockSpec(memory_space=pltpu.SMEM),
                  pl.BlockSpec((MOBA_BLOCK, 2 * MOBA_BLOCK), lambda h: (0, 0))],
        out_specs=pl.BlockSpec((None, MOBA_BLOCK, 2 * MOBA_BLOCK), lambda h: (h, 0, 0)),
        out_shape=jax.ShapeDtypeStruct((n_heads, MOBA_BLOCK, 2 * MOBA_BLOCK), F32),
        compiler_params=_params(("arbitrary",), 4 * MOBA_BLOCK * MOBA_BLOCK * 4),
        name="bias_tiles",
    )(rel_bias, buckets)


def _block_penalty_t(gate_t, n_valid):
    row = lax.broadcasted_iota(jnp.int32, gate_t.shape, 0)
    valid = row < n_valid
    pen = jnp.zeros(gate_t.shape, F32)
    for j in range(n_valid):
        gj = gate_t[j:j + 1, :]
        beats = ((gate_t > gj) | ((gate_t == gj) & (row < j))) & valid
        rank = jnp.sum(beats.astype(F32), axis=0, keepdims=True)
        pen = jnp.where(row == j, jnp.where(rank < MOBA_TOPK, 0.0, NEG_INF), pen)
    return pen


def _prompt_attn_kernel(q_ref, k_ref, v_ref, bias_ref, o_ref, kx_ref, pen_ref):
    t = q_ref.shape[0]
    dh = V7X_LANES
    heads = [slice(c, c + dh) for c in range(0, q_ref.shape[1], dh)]
    blk = MOBA_BLOCK
    nblk = t // blk
    row = lax.broadcasted_iota(jnp.int32, (t, dh), 0)
    col = lax.broadcasted_iota(jnp.int32, (t, dh), 1)
    membership = (col == lax.shift_right_logical(row, blk.bit_length() - 1)).astype(BF16)
    in_block = (lax.shift_right_logical(lax.broadcasted_iota(jnp.int32, (nblk, t), 1), blk.bit_length() - 1)
                == lax.broadcasted_iota(jnp.int32, (nblk, t), 0))
    averaging = jnp.where(in_block, 1.0 / blk, 0.0).astype(BF16)
    eye = (lax.broadcasted_iota(jnp.int32, (blk, blk), 0) == lax.broadcasted_iota(jnp.int32, (blk, blk), 1)).astype(BF16)
    km_hi, km_lo = [], []
    for hh, cs in enumerate(heads):
        kx_ref[hh, :, :dh] = k_ref[:, cs]
        kx_ref[hh, :, dh:] = membership
        km = _dot(averaging, k_ref[:, cs])
        km_hi.append(km.astype(BF16))
        km_lo.append((km - km_hi[hh].astype(F32)).astype(BF16))
    pen_ref[...] = jnp.zeros(pen_ref.shape, F32)

    def masked_logits(hh, i):
        qi = q_ref[i * blk:(i + 1) * blk, heads[hh]]
        nk = (i + 1) * blk
        if i <= MOBA_TOPK:
            return _dot_nt(qi, k_ref[0:nk, heads[hh]])
        gate_t = _dot_nt(km_hi[hh], qi) + _dot_nt(km_lo[hh], qi)
        pen_ref[hh, 0:nblk, :] = _block_penalty_t(gate_t, i)
        pen = _dot_nt(eye, pen_ref[hh].astype(BF16)).astype(BF16)
        return _dot_nt(jnp.concatenate([qi, pen], axis=1), kx_ref[hh, 0:nk, :])

    s_next = [masked_logits(hh, 0) for hh in range(len(heads))]
    for i in range(nblk):
        nk = (i + 1) * blk
        for hh, cs in enumerate(heads):
            s = s_next[hh]
            if i + 1 < nblk:
                s_next[hh] = masked_logits(hh, i + 1)
            near = min(nk, 2 * blk)
            s_near = s[:, nk - near:] + bias_ref[hh, :, 2 * blk - near:]
            s = s_near if near == nk else jnp.concatenate([s[:, :nk - near], s_near], axis=1)
            m = jnp.max(s, axis=1, keepdims=True)
            p = jnp.exp2(s - m)
            l = jnp.sum(p, axis=1, keepdims=True)
            o = _dot(p.astype(BF16), v_ref[0:nk, cs])
            o_ref[i * blk:(i + 1) * blk, cs] = (o / l).astype(o_ref.dtype)


ATTN_HEADS_PER_STEP = 2


def _prompt_attention(q, k, v, bias_tiles, n_heads):
    n, t, width = q.shape
    dh = width // n_heads
    nblk = t // MOBA_BLOCK
    assert dh == V7X_LANES and t % MOBA_BLOCK == 0 and nblk <= V7X_LANES and MOBA_BLOCK & (MOBA_BLOCK - 1) == 0
    hp = ATTN_HEADS_PER_STEP if n_heads % ATTN_HEADS_PER_STEP == 0 else 1
    seq_spec = pl.BlockSpec((None, t, hp * dh), lambda h, b: (b, 0, h))
    blocks = hp * (4 * t * dh * 2 + 2 * MOBA_BLOCK * MOBA_BLOCK * 4 + t * 2 * dh * 2 + 6 * MOBA_BLOCK * t * 4)
    return pl.pallas_call(
        _prompt_attn_kernel,
        grid=(n_heads // hp, n),
        in_specs=[seq_spec, seq_spec, seq_spec,
                  pl.BlockSpec((hp, MOBA_BLOCK, 2 * MOBA_BLOCK), lambda h, b: (h, 0, 0))],
        out_specs=seq_spec,
        out_shape=jax.ShapeDtypeStruct((n, t, width), BF16),
        scratch_shapes=[pltpu.VMEM((hp, t, 2 * dh), BF16), pltpu.VMEM((hp, dh, MOBA_BLOCK), F32)],
        compiler_params=_params(("parallel", "parallel"), blocks),
        name="prompt_attn",
    )(q, k, v, bias_tiles)


DECODE_ROWS = 16


def _lane_sum(x):
    return jnp.broadcast_to(jnp.sum(x, axis=-1, keepdims=True), x.shape)


def _page_bias(relb_ref, page_buckets):
    if len(set(page_buckets)) == 1:
        return relb_ref[page_buckets[0]][None] * LOG2_E
    return jnp.stack([relb_ref[b] for b in page_buckets]) * LOG2_E


def _decode_pages(part, q_ref, kn_ref, vn_ref, relb_ref, k_refs, v_refs, o_ref, m_ref, l_ref, g_ref, acc_ref, *,
                  n_pages, pages_per_block, buckets, other_work=()):
    pps = len(k_refs)
    parts = n_pages // pps
    page, h, dh = k_refs[0].shape
    q = q_ref[...]
    rows = min(DECODE_ROWS, page)
    steps_per_page = -(-page // (2 * rows))
    interleave = {}
    for c, job in enumerate(other_work):
        interleave.setdefault(c * pps * steps_per_page // len(other_work), []).append(job)
    for p in range(pps):
        page_buckets = [buckets[(a * pps + p) * page:(a * pps + p + 1) * page] for a in range(parts)]
        shared = set(b for pb in page_buckets for b in pb)
        shared = shared.pop() if len(shared) == 1 else None
        m = jnp.full((h, dh), NEG_INF, F32)
        l = acc = gsum = jnp.zeros((h, dh), F32)
        for r0 in range(0, page, 2 * rows):
            for job in interleave.get(p * steps_per_page + r0 // (2 * rows), ()):
                job()
            groups = [slice(r, r + rows) for r in range(r0, min(r0 + 2 * rows, page), rows)]
            for g in groups:
                s = _lane_sum(k_refs[p][g] * q[None])
                gsum = gsum + jnp.sum(s, axis=0)
                if shared is None:
                    bias = _page_bias(relb_ref, page_buckets[0][g])
                    for a in range(1, parts):
                        if page_buckets[a][g] != page_buckets[0][g]:
                            bias = jnp.where(part == a, _page_bias(relb_ref, page_buckets[a][g]), bias)
                    s = s + bias
                m_new = jnp.maximum(m, jnp.max(s, axis=0))
                scale = jnp.exp2(m - m_new)
                e = jnp.exp2(s - m_new[None])
                l = l * scale + jnp.sum(e, axis=0)
                acc = acc * scale + jnp.sum(e * v_refs[p][g], axis=0)
                m = m_new
        idx = part * pps + p
        m_ref[idx] = m if shared is None else m + relb_ref[shared] * LOG2_E
        l_ref[idx] = l
        g_ref[idx] = gsum
        acc_ref[idx] = acc

    def merge():
        n_blocks = n_pages // pages_per_block
        gate = [sum(g_ref[p] for p in range(j * pages_per_block, (j + 1) * pages_per_block)) for j in range(n_blocks)]
        s_own = _lane_sum(q * kn_ref[...]) + relb_ref[0] * LOG2_E
        m_tot = s_own
        sel = []
        for j in range(n_blocks):
            rank = jnp.zeros((h, dh), F32)
            for j2 in range(n_blocks):
                if j2 != j:
                    beats = (gate[j2] > gate[j]) | ((gate[j2] == gate[j]) & (j2 < j))
                    rank = rank + beats.astype(F32)
            sel.append(rank < MOBA_TOPK)
            for p in range(j * pages_per_block, (j + 1) * pages_per_block):
                m_tot = jnp.maximum(m_tot, jnp.where(sel[j], m_ref[p], NEG_INF))
        w_own = jnp.exp2(s_own - m_tot)
        num = w_own * vn_ref[...]
        den = w_own
        for j in range(n_blocks):
            for p in range(j * pages_per_block, (j + 1) * pages_per_block):
                w = jnp.where(sel[j], jnp.exp2(m_ref[p] - m_tot), 0.0)
                num = num + w * acc_ref[p]
                den = den + w * l_ref[p]
        o_ref[...] = (num / den).astype(o_ref.dtype)

    if parts == 1:
        merge()
    else:
        pl.when(part == parts - 1)(merge)


def _decode_attn_kernel(pt_ref, q_ref, kn_ref, vn_ref, relb_ref, *refs, n_pages, pages_per_block, buckets):
    del pt_ref
    _decode_pages(0, q_ref, kn_ref, vn_ref, relb_ref, refs[:n_pages], refs[n_pages:2 * n_pages], *refs[2 * n_pages:],
                  n_pages=n_pages, pages_per_block=pages_per_block, buckets=buckets)


def _decode_specs(page_table, cache_k, rel_bias, h, dh, pages_per_step, seq_of, part_of):
    page = cache_k.shape[1]
    n_pages = page_table.shape[1]
    past = n_pages * page
    num_buckets = rel_bias.shape[0]
    assert dh == V7X_LANES and MOBA_BLOCK % page == 0 and past % MOBA_BLOCK == 0 and n_pages % pages_per_step == 0
    buckets = tuple(int(b) for b in _rel_bucket_np(past - np.arange(past), num_buckets))
    relb = jnp.broadcast_to(rel_bias[:, :, None], (num_buckets, h, dh))
    tok_spec = pl.BlockSpec((None, h, dh), lambda *g: (seq_of(*g[:-1]), 0, 0))
    relb_spec = pl.BlockSpec((num_buckets, h, dh), lambda *g: (0, 0, 0))
    page_specs = [pl.BlockSpec((None, page, h, dh),
                               lambda *g, p=p: (g[-1][seq_of(*g[:-1]), part_of(*g[:-1]) * pages_per_step + p], 0, 0, 0))
                  for p in range(pages_per_step)]
    scratch = [pltpu.VMEM((n_pages, h, dh), F32)] * 4
    return buckets, relb, tok_spec, relb_spec, page_specs, scratch


def _decode_attention(q, k_new, v_new, cache_k, cache_v, page_table, rel_bias):
    s, h, dh = q.shape
    page = cache_k.shape[1]
    n_pages = page_table.shape[1]
    buckets, relb, tok_spec, relb_spec, page_specs, scratch = _decode_specs(
        page_table, cache_k, rel_bias, h, dh, n_pages, seq_of=lambda b: b, part_of=lambda b: 0)
    blocks = 2 * n_pages * page * h * dh * 4 + relb.size * 4
    kernel = functools.partial(_decode_attn_kernel, n_pages=n_pages, pages_per_block=MOBA_BLOCK // page,
                               buckets=buckets)
    return pl.pallas_call(
        kernel,
        grid_spec=pltpu.PrefetchScalarGridSpec(
            num_scalar_prefetch=1,
            grid=(s,),
            in_specs=[tok_spec, tok_spec, tok_spec, relb_spec] + page_specs + page_specs,
            out_specs=tok_spec,
            scratch_shapes=scratch),
        out_shape=jax.ShapeDtypeStruct((s, h, dh), BF16),
        compiler_params=_params(("parallel",), blocks),
        name="decode_attn",
    )(page_table, q, k_new, v_new, relb, *([cache_k] * n_pages), *([cache_v] * n_pages))


CONV_HALO = 32
CONV_ROWS = 256
GLU_CHUNK = 256


def _causal_conv_columns(ext_ref, w_ref, conv_ref, c0, tt):
    kw = w_ref.shape[0]
    first = CONV_HALO - (kw - 1)
    rows = min(CONV_ROWS, tt)
    span = rows + CONV_HALO
    cs = slice(c0, c0 + V7X_LANES)
    for r0 in range(0, tt, rows):
        x = ext_ref[r0:r0 + span, cs]
        acc = jnp.zeros((rows, V7X_LANES), F32)
        for b in range(V7X_SUBLANES):
            taps = [s for s in range(first, first + kw) if s % V7X_SUBLANES == b]
            assert all(s + rows <= span for s in taps)
            xb = x if b == 0 else pltpu.roll(x, span - b, axis=0)
            for s in taps:
                acc = acc + xb[s - b:s - b + rows] * w_ref[s - first:s - first + 1, cs]
        conv_ref[r0:r0 + rows, cs] = acc


def _glu_conv_kernel(x_ref, wv_ref, wg_ref, w_ref, cb_ref, g_ref, b_ref, *refs):
    n_hosted = (len(refs) - 4) // 2
    o_ref, tail_ref = refs[n_hosted:n_hosted + 2]
    ext_ref, conv_ref = refs[-2:]
    for src_ref, dst_ref in zip(refs[:n_hosted], refs[n_hosted + 2:-2]):
        dst_ref[...] = src_ref[...].astype(dst_ref.dtype)
    tt = x_ref.shape[0]
    ch = wv_ref.shape[1]
    i = pl.program_id(1)

    @pl.when(i == 0)
    def _():
        ext_ref[0:CONV_HALO, :] = jnp.zeros((CONV_HALO, ch), F32)

    @pl.when(i > 0)
    def _():
        ext_ref[0:CONV_HALO, :] = ext_ref[tt:tt + CONV_HALO, :]

    x = x_ref[...]
    chunk = min(GLU_CHUNK, ch)
    for c0 in range(0, ch, chunk):
        cols = slice(c0, c0 + chunk)
        ext_ref[CONV_HALO:CONV_HALO + tt, cols] = _dot(x, wv_ref[:, cols]) * _sigmoid(_dot(x, wg_ref[:, cols]))
        for c in range(c0, c0 + chunk, V7X_LANES):
            _causal_conv_columns(ext_ref, w_ref, conv_ref, c, tt)
    y = _layer_norm(conv_ref[...] + cb_ref[...], g_ref[...], b_ref[...])
    o_ref[...] = (y * _sigmoid(y)).astype(o_ref.dtype)

    @pl.when(i == pl.num_programs(1) - 1)
    def _():
        tail_ref[...] = ext_ref[tt:tt + CONV_HALO, :]


def _prompt_glu_conv(xb, w, col_val, col_gate, conv_w, conv_b, ln_g, ln_b, to_bf16):
    n, t, d = xb.shape
    kw, ch = conv_w.shape
    tt = _tile(t, 256)
    nt = t // tt
    assert kw - 1 <= CONV_HALO <= tt and tt % V7X_SUBLANES == 0 and ch % min(GLU_CHUNK, ch) == 0
    assert col_val % ch == 0 and col_gate % ch == 0
    hosted = to_bf16 if all(a.shape[0] % (n * nt * 2 * V7X_SUBLANES) == 0 for a in to_bf16) else ()
    slab = lambda a: pl.BlockSpec((a.shape[0] // (n * nt), a.shape[1]), lambda b, i: (b * nt + i, 0))
    vec = pl.BlockSpec((1, ch), lambda b, i: (0, 0))
    wspec = lambda col: pl.BlockSpec((d, ch), lambda b, i: (0, col // ch), pipeline_mode=pl.Buffered(1))
    blocks = (tt * d * 2 + d * ch * 2 + (2 * tt + 2 * CONV_HALO) * ch * 4 + tt * ch * 2
              + sum(a.size * 6 // (n * nt) for a in hosted))
    cn, tail, *copies = pl.pallas_call(
        _glu_conv_kernel,
        grid=(n, nt),
        in_specs=[pl.BlockSpec((None, tt, d), lambda b, i: (b, i, 0)), wspec(col_val), wspec(col_gate),
                  pl.BlockSpec((kw, ch), lambda b, i: (0, 0)), vec, vec, vec] + [slab(a) for a in hosted],
        out_specs=[pl.BlockSpec((None, tt, ch), lambda b, i: (b, i, 0)),
                   pl.BlockSpec((None, CONV_HALO, ch), lambda b, i: (b, 0, 0))] + [slab(a) for a in hosted],
        out_shape=[jax.ShapeDtypeStruct((n, t, ch), BF16), jax.ShapeDtypeStruct((n, CONV_HALO, ch), F32)]
        + [jax.ShapeDtypeStruct(a.shape, BF16) for a in hosted],
        scratch_shapes=[pltpu.VMEM((CONV_HALO + tt, ch), F32), pltpu.VMEM((tt, ch), F32)],
        compiler_params=_params(("arbitrary", "arbitrary"), blocks),
        name="glu_conv",
    )(xb, w, w, conv_w, conv_b.reshape(1, ch), ln_g.reshape(1, ch), ln_b.reshape(1, ch), *hosted)
    return cn, tail, (copies if hosted else [a.astype(BF16) for a in to_bf16])


def _decode_conv_kernel(state_ref, u_ref, w_ref, cb_ref, g_ref, b_ref, o_ref, new_state_ref):
    kw = w_ref.shape[0]
    w_hist = w_ref[0:kw - 1, :]
    w_last = w_ref[kw - 1:kw, :]
    for b in range(state_ref.shape[0]):
        u = u_ref[b]
        conv = jnp.sum(state_ref[b] * w_hist, axis=0, keepdims=True) + u * w_last
        y = _layer_norm(conv + cb_ref[...], g_ref[...], b_ref[...])
        o_ref[b] = (y * _sigmoid(y)).astype(o_ref.dtype)
        new_state_ref[b, 0:kw - 2, :] = state_ref[b, 1:kw - 1, :]
        new_state_ref[b, kw - 2:kw - 1, :] = u


def _decode_conv(state, u, conv_w, conv_b, ln_g, ln_b):
    s, hist, ch = state.shape
    kw = conv_w.shape[0]
    ts = _tile(s, 16)
    vec = pl.BlockSpec((1, ch), lambda i: (0, 0))
    tok = pl.BlockSpec((ts, 1, ch), lambda i: (i, 0, 0))
    hist_spec = pl.BlockSpec((ts, hist, ch), lambda i: (i, 0, 0))
    blocks = 2 * ts * 32 * ch * 4 + 2 * ts * V7X_SUBLANES * ch * 4
    return pl.pallas_call(
        _decode_conv_kernel,
        grid=(s // ts,),
        in_specs=[hist_spec, tok, pl.BlockSpec((kw, ch), lambda i: (0, 0)), vec, vec, vec],
        out_specs=[tok, hist_spec],
        out_shape=[jax.ShapeDtypeStruct((s, 1, ch), F32), jax.ShapeDtypeStruct((s, hist, ch), F32)],
        compiler_params=_params(("parallel",), blocks),
        name="decode_conv",
    )(state, u, conv_w, conv_b.reshape(1, ch), ln_g.reshape(1, ch), ln_b.reshape(1, ch))


def _merge_kernel(x_ref, a_ref, c_ref, wga_ref, wgc_ref, wao_ref, wco_ref, o_ref):
    x = x_ref[...]
    mixed = (_sigmoid(_dot(x, wga_ref[...])) * _dot(a_ref[...], wao_ref[...])
             + _sigmoid(_dot(x, wgc_ref[...])) * _dot(c_ref[...], wco_ref[...]))
    o_ref[...] = mixed.astype(o_ref.dtype)


def _merge(xb, attn, cn, w_in, col_ga, col_gc, w_ao, w_co, tm):
    m, d = xb.shape
    wa = attn.shape[1]
    wc = cn.shape[1]
    tn = _tile(d, 512)
    tm = _tile(m, tm)
    assert col_ga % tn == 0 and col_gc % tn == 0
    blocks = tm * (d + wa + wc + tn) * 2 + (2 * d + wa + wc) * tn * 2
    return pl.pallas_call(
        _merge_kernel,
        grid=(m // tm, d // tn),
        in_specs=[pl.BlockSpec((tm, d), lambda i, j: (i, 0)),
                  pl.BlockSpec((tm, wa), lambda i, j: (i, 0)),
                  pl.BlockSpec((tm, wc), lambda i, j: (i, 0)),
                  pl.BlockSpec((d, tn), lambda i, j: (0, col_ga // tn + j)),
                  pl.BlockSpec((d, tn), lambda i, j: (0, col_gc // tn + j)),
                  pl.BlockSpec((wa, tn), lambda i, j: (0, j)),
                  pl.BlockSpec((wc, tn), lambda i, j: (0, j))],
        out_specs=pl.BlockSpec((tm, tn), lambda i, j: (i, j)),
        out_shape=jax.ShapeDtypeStruct((m, d), BF16),
        compiler_params=_params(("parallel", "parallel"), blocks),
        name="merge",
    )(xb, attn, cn, w_in, w_in, w_ao, w_co)


def _out_ln_kernel(x_ref, mixed_ref, w_ref, g_ref, b_ref, of_ref, *maybe_ob_ref, alpha):
    tm = x_ref.shape[0]
    halves = [slice(0, tm // 2), slice(tm // 2, tm)] if tm % (4 * V7X_SUBLANES) == 0 else [slice(0, tm)]
    acc = [_dot(mixed_ref[rows, :], w_ref[...]) for rows in halves]
    for rows, a in zip(halves, acc):
        y = _layer_norm(alpha * x_ref[rows, :] + a, g_ref[...], b_ref[...])
        of_ref[rows, :] = y
        for ob_ref in maybe_ob_ref:
            ob_ref[rows, :] = y.astype(ob_ref.dtype)


def _out_ln(x, mixed, w_out, g, b, alpha, tm, emit_bf16):
    m, d = x.shape
    tm = _tile(m, tm)
    vec = pl.BlockSpec((1, d), lambda i: (0, 0))
    row = pl.BlockSpec((tm, d), lambda i: (i, 0))
    blocks = tm * d * (4 + 2 + 4 + 2) + d * d * 2
    return pl.pallas_call(
        functools.partial(_out_ln_kernel, alpha=alpha),
        grid=(m // tm,),
        in_specs=[row, row, pl.BlockSpec((d, d), lambda i: (0, 0)), vec, vec],
        out_specs=[row, row] if emit_bf16 else [row],
        out_shape=[jax.ShapeDtypeStruct((m, d), F32)] + ([jax.ShapeDtypeStruct((m, d), BF16)] if emit_bf16 else []),
        compiler_params=_params(("parallel",), blocks),
        name="out_ln",
    )(x, mixed, w_out, g.reshape(1, d), b.reshape(1, d))


def _mlp_ln_kernel(xf_ref, xb_ref, w1_ref, w2_ref, g_ref, b_ref, o_ref, *, alpha):
    f = pl.program_id(1)

    @pl.when(f == 0)
    def _():
        o_ref[...] = alpha * xf_ref[...]

    hid = jnp.maximum(_dot(xb_ref[...], w1_ref[...]), 0.0)
    o_ref[...] += _dot((hid * hid).astype(BF16), w2_ref[...])

    @pl.when(f == pl.num_programs(1) - 1)
    def _():
        o_ref[...] = _layer_norm(o_ref[...], g_ref[...], b_ref[...])


def _mlp_ln(xf, xb, w1, w2, g, b, alpha, tm):
    m, d = xf.shape
    dff = w1.shape[1]
    tf = _tile(dff, 1024)
    tm = _tile(m, tm)
    vec = pl.BlockSpec((1, d), lambda i, f: (0, 0))
    row = pl.BlockSpec((tm, d), lambda i, f: (i, 0))
    blocks = tm * d * (4 + 2 + 4) + 2 * d * tf * 2 + tm * tf * 4
    return pl.pallas_call(
        functools.partial(_mlp_ln_kernel, alpha=alpha),
        grid=(m // tm, dff // tf),
        in_specs=[row, row, pl.BlockSpec((d, tf), lambda i, f: (0, f)), pl.BlockSpec((tf, d), lambda i, f: (f, 0)),
                  vec, vec],
        out_specs=row,
        out_shape=jax.ShapeDtypeStruct((m, d), F32),
        compiler_params=_params(("parallel", "arbitrary"), blocks),
        name="mlp_ln",
    )(xf, xb, w1, w2, g.reshape(1, d), b.reshape(1, d))


def _mlp_ln_decode_kernel(pt_ref, xf_ref, w1_ref, w2_ref, g_ref, b_ref, q_ref, kn_ref, vn_ref, relb_ref, *refs,
                          alpha, pages_per_step, n_pages, pages_per_block, buckets):
    del pt_ref
    k_refs, v_refs = refs[:pages_per_step], refs[pages_per_step:2 * pages_per_step]
    o_ref, ao_ref, xb_ref, hid_ref, m_ref, l_ref, gs_ref, acc_ref = refs[2 * pages_per_step:]
    part = pl.program_id(2)
    f = pl.program_id(1) * pl.num_programs(2) + part
    nf = pl.num_programs(1) * pl.num_programs(2)

    @pl.when(f == 0)
    def _():
        x = xf_ref[...]
        o_ref[...] = alpha * x
        xb_ref[...] = x.astype(BF16)

    tf, d = w2_ref.shape
    wh = math.gcd(tf, MLP_CHUNK)
    wo = math.gcd(d, MLP_CHUNK)

    def hid_chunk(c):
        hid = jnp.maximum(_dot(xb_ref[...], w1_ref[:, c * wh:(c + 1) * wh]), 0.0)
        hid_ref[:, c * wh:(c + 1) * wh] = (hid * hid).astype(BF16)

    def out_chunk(c):
        o_ref[:, c * wo:(c + 1) * wo] += _dot(hid_ref[...], w2_ref[:, c * wo:(c + 1) * wo])

    chunks = ([functools.partial(hid_chunk, c) for c in range(tf // wh)]
              + [functools.partial(out_chunk, c) for c in range(d // wo)])
    _decode_pages(part, q_ref, kn_ref, vn_ref, relb_ref, k_refs, v_refs, ao_ref, m_ref, l_ref, gs_ref, acc_ref,
                  n_pages=n_pages, pages_per_block=pages_per_block, buckets=buckets, other_work=chunks)

    @pl.when(f == nf - 1)
    def _():
        o_ref[...] = _layer_norm(o_ref[...], g_ref[...], b_ref[...])


MLP_CHUNK = 256
FUSED_MLP_ROWS = 512
FUSED_MLP_COLS = 1024


def _fused_pages_per_step(m, dff, s, n_pages):
    nf = dff // _tile(dff, FUSED_MLP_COLS)
    steps = (m // _tile(m, FUSED_MLP_ROWS)) * nf
    if (s * n_pages) % steps or n_pages % ((s * n_pages) // steps) or nf % (n_pages // ((s * n_pages) // steps)):
        return None
    return (s * n_pages) // steps


def _mlp_ln_decode(xf, w1, w2, g, b, alpha, q, k_new, v_new, cache_k, cache_v, page_table, rel_bias, pps):
    m, d = xf.shape
    dff = w1.shape[1]
    tm, tf = _tile(m, FUSED_MLP_ROWS), _tile(dff, FUSED_MLP_COLS)
    nf = dff // tf
    s, h, dh = q.shape
    page = cache_k.shape[1]
    n_pages = page_table.shape[1]
    parts = n_pages // pps
    spr = nf // parts
    assert (m // tm) * nf == s * parts and nf == spr * parts
    buckets, relb, tok_spec, relb_spec, page_specs, scratch = _decode_specs(
        page_table, cache_k, rel_bias, h, dh, pps,
        seq_of=lambda i, j, a: i * spr + j, part_of=lambda i, j, a: a)
    vec = pl.BlockSpec((1, d), lambda i, j, a, pt: (0, 0))
    row = pl.BlockSpec((tm, d), lambda i, j, a, pt: (i, 0))
    blocks = tm * d * 9 + 2 * d * tf * 2 + tm * tf * 4 + 2 * pps * page * h * dh * 4 + relb.size * 4
    kernel = functools.partial(_mlp_ln_decode_kernel, alpha=alpha, pages_per_step=pps, n_pages=n_pages,
                               pages_per_block=MOBA_BLOCK // page, buckets=buckets)
    return pl.pallas_call(
        kernel,
        grid_spec=pltpu.PrefetchScalarGridSpec(
            num_scalar_prefetch=1,
            grid=(m // tm, spr, parts),
            in_specs=[row, pl.BlockSpec((d, tf), lambda i, j, a, pt: (0, j * parts + a)),
                      pl.BlockSpec((tf, d), lambda i, j, a, pt: (j * parts + a, 0)),
                      vec, vec, tok_spec, tok_spec, tok_spec, relb_spec] + page_specs + page_specs,
            out_specs=[row, tok_spec],
            scratch_shapes=[pltpu.VMEM((tm, d), BF16), pltpu.VMEM((tm, tf), BF16)] + scratch),
        out_shape=[jax.ShapeDtypeStruct((m, d), F32), jax.ShapeDtypeStruct((s, h, dh), BF16)],
        compiler_params=_params(("arbitrary", "arbitrary", "arbitrary"), blocks),
        name="mlp_ln_decode_attn",
    )(page_table, xf, w1, w2, g.reshape(1, d), b.reshape(1, d), q, k_new, v_new, relb,
      *([cache_k] * pps), *([cache_v] * pps))


def _merge_out(x2d, xb, attn, cn, lw, alpha, tm, emit_bf16):
    mixed = _merge(xb, attn, cn, lw["w_in"], lw["col_ga"], lw["col_gc"], lw["w_attn_out"], lw["w_conv_out"], tm=2 * tm)
    return _out_ln(x2d, mixed, lw["w_out"], lw["ln1_g"], lw["ln1_b"], alpha, tm, emit_bf16)


def kernel(x_prompt, x_sample, cache_k, cache_v, state_conv, page_table, rel_bias, w_in, w_attn_out, conv_w, conv_b,
           conv_ln_g, conv_ln_b, w_conv_out, w_out, ln1_g, ln1_b, w_ff1, w_ff2, ln2_g, ln2_b):
    depth = w_in.shape[0]
    n, t, d = x_prompt.shape
    s, ts, _ = x_sample.shape
    n_heads, dh = cache_k.shape[-2:]
    aw = n_heads * dh
    ch = conv_w.shape[-1]
    alpha = (2 * depth) ** 0.25
    scale = dh ** -0.5 * LOG2_E
    assert ts == 1 and w_in.shape[-1] == 3 * aw + 2 * ch + 2 * d
    col_val, col_gate = 3 * aw, 3 * aw + ch
    col_ga, col_gc = 3 * aw + 2 * ch, 3 * aw + 2 * ch + d

    bias_tiles = _prompt_bias_tiles(rel_bias)
    hp = x_prompt.reshape(n * t, d)
    hs = x_sample.reshape(s, d)
    outs = [[] for _ in range(6)]
    for l in range(depth):
        wi = w_in[l].astype(BF16)

        xb, q, kf, kb, vf, vb = _proj_qkv(hp, wi, aw, scale, tm=512, q_dtype=BF16)
        attn = _prompt_attention(q.reshape(n, t, aw), kb.reshape(n, t, aw), vb.reshape(n, t, aw), bias_tiles, n_heads)
        late = dict(w_attn_out=w_attn_out[l], w_conv_out=w_conv_out[l], w_out=w_out[l], w_ff1=w_ff1[l], w_ff2=w_ff2[l])
        cn, u_tail, late_bf16 = _prompt_glu_conv(xb.reshape(n, t, d), wi, col_val, col_gate, conv_w[l], conv_b[l],
                                                 conv_ln_g[l], conv_ln_b[l], to_bf16=tuple(late.values()))
        lw = dict(zip(late, late_bf16), w_in=wi, ln1_g=ln1_g[l], ln1_b=ln1_b[l], ln2_g=ln2_g[l], ln2_b=ln2_b[l],
                  col_ga=col_ga, col_gc=col_gc)
        outs[0].append(kf.reshape(n, t, n_heads, dh))
        outs[1].append(vf.reshape(n, t, n_heads, dh))
        outs[2].append(u_tail[:, CONV_HALO - (conv_w.shape[1] - 1):, :])

        xsb, qs, ksf, _, vsf, _ = _proj_qkv(hs, wi, aw, scale, tm=s, q_dtype=F32)
        us = _proj_glu(xsb, wi, col_val, col_gate, ch, tm=s)
        dec = (qs.reshape(s, n_heads, dh), ksf.reshape(s, n_heads, dh), vsf.reshape(s, n_heads, dh),
               cache_k[l], cache_v[l], page_table, rel_bias)

        mlp_w = (lw["w_ff1"], lw["w_ff2"], lw["ln2_g"], lw["ln2_b"], alpha)
        pps = _fused_pages_per_step(n * t, w_ff1.shape[-1], s, page_table.shape[1])
        x1 = _merge_out(hp, xb, attn.reshape(n * t, aw), cn.reshape(n * t, ch), lw, alpha, 512, emit_bf16=pps is None)
        if pps is None:
            hp_next = _mlp_ln(x1[0], x1[1], *mlp_w, tm=512)
            attn_s = _decode_attention(*dec)
        else:
            hp_next, attn_s = _mlp_ln_decode(x1[0], *mlp_w, *dec, pps)

        cn_s, new_state = _decode_conv(state_conv[l], us.reshape(s, 1, ch), conv_w[l], conv_b[l], conv_ln_g[l],
                                       conv_ln_b[l])
        x1s = _merge_out(hs, xsb, attn_s.reshape(s, aw), cn_s.reshape(s, ch).astype(BF16), lw, alpha, s, emit_bf16=True)
        hs_next = _mlp_ln(x1s[0], x1s[1], *mlp_w, tm=s)
        outs[3].append(ksf.reshape(s, 1, n_heads, dh))
        outs[4].append(vsf.reshape(s, 1, n_heads, dh))
        outs[5].append(new_state)
        hp, hs = hp_next, hs_next

    return (hp.reshape(n, t, d), hs.reshape(s, 1, d)) + tuple(jnp.stack(o) for o in outs)
```

```python
import functools
import math

import numpy as np
import jax
import jax.numpy as jnp
from jax import lax
from jax.experimental import pallas as pl
from jax.experimental.pallas import tpu as pltpu

MOBA_BLOCK = 256
MOBA_TOPK = 3
MAX_DISTANCE = 128
LN_EPS = 1e-5
NEG_INF = -1e30
LOG2_E = math.log2(math.e)

V7X_VMEM_BYTES = 64 * 1024 * 1024
V7X_LANES = 128
V7X_SUBLANES = 8

F32 = jnp.float32
BF16 = jnp.bfloat16


def _vmem_limit(block_bytes):
    return int(min(max(2 * block_bytes + (16 << 20), 32 << 20), V7X_VMEM_BYTES - (4 << 20)))


def _params(semantics, block_bytes):
    return pltpu.CompilerParams(dimension_semantics=semantics, vmem_limit_bytes=_vmem_limit(block_bytes))


def _tile(n, want):
    t = min(n, want)
    while n % t:
        t -= 1
    return t


def _sigmoid(x):
    return 1.0 / (1.0 + jnp.exp(-x))


def _layer_norm(y, g, b):
    mu = jnp.mean(y, axis=-1, keepdims=True)
    d = y - mu
    var = jnp.mean(d * d, axis=-1, keepdims=True)
    return d * lax.rsqrt(var + LN_EPS) * g + b


def _dot(a, b):
    return jnp.dot(a, b, preferred_element_type=F32)


def _dot_nt(a, b):
    return lax.dot_general(a, b, (((1,), (1,)), ((), ())), preferred_element_type=F32)


def _proj_qkv_kernel(x_ref, w_ref, xb_ref, q_ref, kf_ref, kb_ref, vf_ref, vb_ref, *, scale):
    aw = q_ref.shape[1]
    xb = x_ref[...].astype(BF16)
    xb_ref[...] = xb
    q_ref[...] = (_dot(xb, w_ref[:, 0:aw]) * scale).astype(q_ref.dtype)
    for c, (f_ref, b_ref) in enumerate(((kf_ref, kb_ref), (vf_ref, vb_ref)), start=1):
        acc = _dot(xb, w_ref[:, c * aw:(c + 1) * aw])
        f_ref[...] = acc
        b_ref[...] = acc.astype(b_ref.dtype)


def _proj_glu_kernel(x_ref, wv_ref, wg_ref, u_ref):
    x = x_ref[...]
    u_ref[...] = _dot(x, wv_ref[...]) * _sigmoid(_dot(x, wg_ref[...]))


def _proj_qkv(x, w, aw, scale, tm, q_dtype):
    m, d = x.shape
    tm = _tile(m, tm)
    row = lambda width: pl.BlockSpec((tm, width), lambda i: (i, 0))
    blocks = tm * d * 6 + tm * aw * 16 + d * 3 * aw
    return pl.pallas_call(
        functools.partial(_proj_qkv_kernel, scale=scale),
        grid=(m // tm,),
        in_specs=[row(d), pl.BlockSpec((d, 3 * aw), lambda i: (0, 0), pipeline_mode=pl.Buffered(1))],
        out_specs=[row(d)] + [row(aw)] * 5,
        out_shape=[jax.ShapeDtypeStruct((m, d), BF16), jax.ShapeDtypeStruct((m, aw), q_dtype)]
        + [jax.ShapeDtypeStruct((m, aw), dt) for dt in (F32, BF16, F32, BF16)],
        compiler_params=_params(("parallel",), blocks),
        name="proj_qkv",
    )(x, w)


def _proj_glu(xb, w, col_val, col_gate, ncols, tm):
    m, d = xb.shape
    tn = _tile(ncols, 512)
    tm = _tile(m, tm)
    assert col_val % tn == 0 and col_gate % tn == 0
    blocks = tm * d * 2 + 2 * d * tn * 2 + tm * tn * 4
    return pl.pallas_call(
        _proj_glu_kernel,
        grid=(m // tm, ncols // tn),
        in_specs=[pl.BlockSpec((tm, d), lambda i, j: (i, 0)),
                  pl.BlockSpec((d, tn), lambda i, j: (0, col_val // tn + j)),
                  pl.BlockSpec((d, tn), lambda i, j: (0, col_gate // tn + j))],
        out_specs=pl.BlockSpec((tm, tn), lambda i, j: (i, j)),
        out_shape=jax.ShapeDtypeStruct((m, ncols), F32),
        compiler_params=_params(("parallel", "parallel"), blocks),
        name="proj_glu",
    )(xb, w, w)


def _rel_bucket_np(dist, num_buckets):
    n = np.maximum(dist, 0)
    max_exact = num_buckets // 2
    nf = np.maximum(n, 1).astype(np.float32)
    large = max_exact + (np.log(nf / np.float32(max_exact)) / np.float32(math.log(MAX_DISTANCE / max_exact))
                         * np.float32(num_buckets - max_exact)).astype(np.int32)
    large = np.minimum(large, num_buckets - 1)
    return np.where(n < max_exact, n, large).astype(np.int32)


def _bias_tiles_kernel(rb_ref, bucket_ref, o_ref, *, num_buckets):
    h = pl.program_id(0)
    far = rb_ref[num_buckets - 1, h]
    bucket = bucket_ref[...]
    acc = jnp.full(bucket.shape, NEG_INF, F32)
    for b in range(num_buckets):
        acc = jnp.where(bucket == b, (rb_ref[b, h] - far) * LOG2_E, acc)
    o_ref[...] = acc


def _prompt_bias_tiles(rel_bias):
    num_buckets, n_heads = rel_bias.shape
    assert MAX_DISTANCE <= MOBA_BLOCK + 1
    r = np.arange(MOBA_BLOCK)[:, None]
    c = np.arange(MOBA_BLOCK)[None, :]
    diag = np.where(r - c >= 0, _rel_bucket_np(r - c, num_buckets), -1)
    prev = _rel_bucket_np(r - c + MOBA_BLOCK, num_buckets)
    buckets = jnp.asarray(np.concatenate([prev, diag], axis=1).astype(np.int32))
    return pl.pallas_call(
        functools.partial(_bias_tiles_kernel, num_buckets=num_buckets),
        grid=(n_heads,),
        in_specs=[pl.BlockSpec(memory_space=pltpu.SMEM),
                  pl.BlockSpec((MOBA_BLOCK, 2 * MOBA_BLOCK), lambda h: (0, 0))],
        out_specs=pl.BlockSpec((None, MOBA_BLOCK, 2 * MOBA_BLOCK), lambda h: (h, 0, 0)),
        out_shape=jax.ShapeDtypeStruct((n_heads, MOBA_BLOCK, 2 * MOBA_BLOCK), F32),
        compiler_params=_params(("arbitrary",), 4 * MOBA_BLOCK * MOBA_BLOCK * 4),
        name="bias_tiles",
    )(rel_bias, buckets)


def _block_penalty_t(gate_t, n_valid):
    row = lax.broadcasted_iota(jnp.int32, gate_t.shape, 0)
    valid = row < n_valid
    pen = jnp.zeros(gate_t.shape, F32)
    for j in range(n_valid):
        gj = gate_t[j:j + 1, :]
        beats = ((gate_t > gj) | ((gate_t == gj) & (row < j))) & valid
        rank = jnp.sum(beats.astype(F32), axis=0, keepdims=True)
        pen = jnp.where(row == j, jnp.where(rank < MOBA_TOPK, 0.0, NEG_INF), pen)
    return pen


def _prompt_attn_kernel(q_ref, k_ref, v_ref, bias_ref, o_ref, kx_ref, pen_ref):
    t = q_ref.shape[0]
    dh = V7X_LANES
    heads = [slice(c, c + dh) for c in range(0, q_ref.shape[1], dh)]
    blk = MOBA_BLOCK
    nblk = t // blk
    row = lax.broadcasted_iota(jnp.int32, (t, dh), 0)
    col = lax.broadcasted_iota(jnp.int32, (t, dh), 1)
    membership = (col == lax.shift_right_logical(row, blk.bit_length() - 1)).astype(BF16)
    in_block = (lax.shift_right_logical(lax.broadcasted_iota(jnp.int32, (nblk, t), 1), blk.bit_length() - 1)
                == lax.broadcasted_iota(jnp.int32, (nblk, t), 0))
    averaging = jnp.where(in_block, 1.0 / blk, 0.0).astype(BF16)
    eye = (lax.broadcasted_iota(jnp.int32, (blk, blk), 0) == lax.broadcasted_iota(jnp.int32, (blk, blk), 1)).astype(BF16)
    km_hi, km_lo = [], []
    for hh, cs in enumerate(heads):
        kx_ref[hh, :, :dh] = k_ref[:, cs]
        kx_ref[hh, :, dh:] = membership
        km = _dot(averaging, k_ref[:, cs])
        km_hi.append(km.astype(BF16))
        km_lo.append((km - km_hi[hh].astype(F32)).astype(BF16))
    pen_ref[...] = jnp.zeros(pen_ref.shape, F32)

    def masked_logits(hh, i):
        qi = q_ref[i * blk:(i + 1) * blk, heads[hh]]
        nk = (i + 1) * blk
        if i <= MOBA_TOPK:
            return _dot_nt(qi, k_ref[0:nk, heads[hh]])
        gate_t = _dot_nt(km_hi[hh], qi) + _dot_nt(km_lo[hh], qi)
        pen_ref[hh, 0:nblk, :] = _block_penalty_t(gate_t, i)
        pen = _dot_nt(eye, pen_ref[hh].astype(BF16)).astype(BF16)
        return _dot_nt(jnp.concatenate([qi, pen], axis=1), kx_ref[hh, 0:nk, :])

    s_next = [masked_logits(hh, 0) for hh in range(len(heads))]
    for i in range(nblk):
        nk = (i + 1) * blk
        for hh, cs in enumerate(heads):
            s = s_next[hh]
            if i + 1 < nblk:
                s_next[hh] = masked_logits(hh, i + 1)
            near = min(nk, 2 * blk)
            s_near = s[:, nk - near:] + bias_ref[hh, :, 2 * blk - near:]
            s = s_near if near == nk else jnp.concatenate([s[:, :nk - near], s_near], axis=1)
            m = jnp.max(s, axis=1, keepdims=True)
            p = jnp.exp2(s - m)
            l = jnp.sum(p, axis=1, keepdims=True)
            o = _dot(p.astype(BF16), v_ref[0:nk, cs])
            o_ref[i * blk:(i + 1) * blk, cs] = (o / l).astype(o_ref.dtype)


ATTN_HEADS_PER_STEP = 2


def _prompt_attention(q, k, v, bias_tiles, n_heads):
    n, t, width = q.shape
    dh = width // n_heads
    nblk = t // MOBA_BLOCK
    assert dh == V7X_LANES and t % MOBA_BLOCK == 0 and nblk <= V7X_LANES and MOBA_BLOCK & (MOBA_BLOCK - 1) == 0
    hp = ATTN_HEADS_PER_STEP if n_heads % ATTN_HEADS_PER_STEP == 0 else 1
    seq_spec = pl.BlockSpec((None, t, hp * dh), lambda h, b: (b, 0, h))
    blocks = hp * (4 * t * dh * 2 + 2 * MOBA_BLOCK * MOBA_BLOCK * 4 + t * 2 * dh * 2 + 6 * MOBA_BLOCK * t * 4)
    return pl.pallas_call(
        _prompt_attn_kernel,
        grid=(n_heads // hp, n),
        in_specs=[seq_spec, seq_spec, seq_spec,
                  pl.BlockSpec((hp, MOBA_BLOCK, 2 * MOBA_BLOCK), lambda h, b: (h, 0, 0))],
        out_specs=seq_spec,
        out_shape=jax.ShapeDtypeStruct((n, t, width), BF16),
        scratch_shapes=[pltpu.VMEM((hp, t, 2 * dh), BF16), pltpu.VMEM((hp, dh, MOBA_BLOCK), F32)],
        compiler_params=_params(("parallel", "parallel"), blocks),
        name="prompt_attn",
    )(q, k, v, bias_tiles)


DECODE_ROWS = 16


def _lane_sum(x):
    return jnp.broadcast_to(jnp.sum(x, axis=-1, keepdims=True), x.shape)


def _page_bias(relb_ref, page_buckets):
    if len(set(page_buckets)) == 1:
        return relb_ref[page_buckets[0]][None] * LOG2_E
    return jnp.stack([relb_ref[b] for b in page_buckets]) * LOG2_E


def _decode_pages(part, q_ref, kn_ref, vn_ref, relb_ref, k_refs, v_refs, o_ref, m_ref, l_ref, g_ref, acc_ref, *,
                  n_pages, pages_per_block, buckets, other_work=()):
    pps = len(k_refs)
    parts = n_pages // pps
    page, h, dh = k_refs[0].shape
    q = q_ref[...]
    rows = min(DECODE_ROWS, page)
    steps_per_page = -(-page // (2 * rows))
    interleave = {}
    for c, job in enumerate(other_work):
        interleave.setdefault(c * pps * steps_per_page // len(other_work), []).append(job)
    for p in range(pps):
        page_buckets = [buckets[(a * pps + p) * page:(a * pps + p + 1) * page] for a in range(parts)]
        shared = set(b for pb in page_buckets for b in pb)
        shared = shared.pop() if len(shared) == 1 else None
        m = jnp.full((h, dh), NEG_INF, F32)
        l = acc = gsum = jnp.zeros((h, dh), F32)
        for r0 in range(0, page, 2 * rows):
            for job in interleave.get(p * steps_per_page + r0 // (2 * rows), ()):
                job()
            groups = [slice(r, r + rows) for r in range(r0, min(r0 + 2 * rows, page), rows)]
            for g in groups:
                s = _lane_sum(k_refs[p][g] * q[None])
                gsum = gsum + jnp.sum(s, axis=0)
                if shared is None:
                    bias = _page_bias(relb_ref, page_buckets[0][g])
                    for a in range(1, parts):
                        if page_buckets[a][g] != page_buckets[0][g]:
                            bias = jnp.where(part == a, _page_bias(relb_ref, page_buckets[a][g]), bias)
                    s = s + bias
                m_new = jnp.maximum(m, jnp.max(s, axis=0))
                scale = jnp.exp2(m - m_new)
                e = jnp.exp2(s - m_new[None])
                l = l * scale + jnp.sum(e, axis=0)
                acc = acc * scale + jnp.sum(e * v_refs[p][g], axis=0)
                m = m_new
        idx = part * pps + p
        m_ref[idx] = m if shared is None else m + relb_ref[shared] * LOG2_E
        l_ref[idx] = l
        g_ref[idx] = gsum
        acc_ref[idx] = acc

    def merge():
        n_blocks = n_pages // pages_per_block
        gate = [sum(g_ref[p] for p in range(j * pages_per_block, (j + 1) * pages_per_block)) for j in range(n_blocks)]
        s_own = _lane_sum(q * kn_ref[...]) + relb_ref[0] * LOG2_E
        m_tot = s_own
        sel = []
        for j in range(n_blocks):
            rank = jnp.zeros((h, dh), F32)
            for j2 in range(n_blocks):
                if j2 != j:
                    beats = (gate[j2] > gate[j]) | ((gate[j2] == gate[j]) & (j2 < j))
                    rank = rank + beats.astype(F32)
            sel.append(rank < MOBA_TOPK)
            for p in range(j * pages_per_block, (j + 1) * pages_per_block):
                m_tot = jnp.maximum(m_tot, jnp.where(sel[j], m_ref[p], NEG_INF))
        w_own = jnp.exp2(s_own - m_tot)
        num = w_own * vn_ref[...]
        den = w_own
        for j in range(n_blocks):
            for p in range(j * pages_per_block, (j + 1) * pages_per_block):
                w = jnp.where(sel[j], jnp.exp2(m_ref[p] - m_tot), 0.0)
                num = num + w * acc_ref[p]
                den = den + w * l_ref[p]
        o_ref[...] = (num / den).astype(o_ref.dtype)

    if parts == 1:
        merge()
    else:
        pl.when(part == parts - 1)(merge)


def _decode_attn_kernel(pt_ref, q_ref, kn_ref, vn_ref, relb_ref, *refs, n_pages, pages_per_block, buckets):
    del pt_ref
    _decode_pages(0, q_ref, kn_ref, vn_ref, relb_ref, refs[:n_pages], refs[n_pages:2 * n_pages], *refs[2 * n_pages:],
                  n_pages=n_pages, pages_per_block=pages_per_block, buckets=buckets)


def _decode_specs(page_table, cache_k, rel_bias, h, dh, pages_per_step, seq_of, part_of):
    page = cache_k.shape[1]
    n_pages = page_table.shape[1]
    past = n_pages * page
    num_buckets = rel_bias.shape[0]
    assert dh == V7X_LANES and MOBA_BLOCK % page == 0 and past % MOBA_BLOCK == 0 and n_pages % pages_per_step == 0
    buckets = tuple(int(b) for b in _rel_bucket_np(past - np.arange(past), num_buckets))
    relb = jnp.broadcast_to(rel_bias[:, :, None], (num_buckets, h, dh))
    tok_spec = pl.BlockSpec((None, h, dh), lambda *g: (seq_of(*g[:-1]), 0, 0))
    relb_spec = pl.BlockSpec((num_buckets, h, dh), lambda *g: (0, 0, 0))
    page_specs = [pl.BlockSpec((None, page, h, dh),
                               lambda *g, p=p: (g[-1][seq_of(*g[:-1]), part_of(*g[:-1]) * pages_per_step + p], 0, 0, 0))
                  for p in range(pages_per_step)]
    scratch = [pltpu.VMEM((n_pages, h, dh), F32)] * 4
    return buckets, relb, tok_spec, relb_spec, page_specs, scratch


def _decode_attention(q, k_new, v_new, cache_k, cache_v, page_table, rel_bias):
    s, h, dh = q.shape
    page = cache_k.shape[1]
    n_pages = page_table.shape[1]
    buckets, relb, tok_spec, relb_spec, page_specs, scratch = _decode_specs(
        page_table, cache_k, rel_bias, h, dh, n_pages, seq_of=lambda b: b, part_of=lambda b: 0)
    blocks = 2 * n_pages * page * h * dh * 4 + relb.size * 4
    kernel = functools.partial(_decode_attn_kernel, n_pages=n_pages, pages_per_block=MOBA_BLOCK // page,
                               buckets=buckets)
    return pl.pallas_call(
        kernel,
        grid_spec=pltpu.PrefetchScalarGridSpec(
            num_scalar_prefetch=1,
            grid=(s,),
            in_specs=[tok_spec, tok_spec, tok_spec, relb_spec] + page_specs + page_specs,
            out_specs=tok_spec,
            scratch_shapes=scratch),
        out_shape=jax.ShapeDtypeStruct((s, h, dh), BF16),
        compiler_params=_params(("parallel",), blocks),
        name="decode_attn",
    )(page_table, q, k_new, v_new, relb, *([cache_k] * n_pages), *([cache_v] * n_pages))


CONV_HALO = 32
CONV_ROWS = 256
GLU_CHUNK = 256


def _causal_conv_columns(ext_ref, w_ref, conv_ref, c0, tt):
    kw = w_ref.shape[0]
    first = CONV_HALO - (kw - 1)
    rows = min(CONV_ROWS, tt)
    span = rows + CONV_HALO
    cs = slice(c0, c0 + V7X_LANES)
    for r0 in range(0, tt, rows):
        x = ext_ref[r0:r0 + span, cs]
        acc = jnp.zeros((rows, V7X_LANES), F32)
        for b in range(V7X_SUBLANES):
            taps = [s for s in range(first, first + kw) if s % V7X_SUBLANES == b]
            assert all(s + rows <= span for s in taps)
            xb = x if b == 0 else pltpu.roll(x, span - b, axis=0)
            for s in taps:
                acc = acc + xb[s - b:s - b + rows] * w_ref[s - first:s - first + 1, cs]
        conv_ref[r0:r0 + rows, cs] = acc


def _glu_conv_kernel(x_ref, wv_ref, wg_ref, w_ref, cb_ref, g_ref, b_ref, *refs):
    n_hosted = (len(refs) - 4) // 2
    o_ref, tail_ref = refs[n_hosted:n_hosted + 2]
    ext_ref, conv_ref = refs[-2:]
    for src_ref, dst_ref in zip(refs[:n_hosted], refs[n_hosted + 2:-2]):
        dst_ref[...] = src_ref[...].astype(dst_ref.dtype)
    tt = x_ref.shape[0]
    ch = wv_ref.shape[1]
    i = pl.program_id(1)

    @pl.when(i == 0)
    def _():
        ext_ref[0:CONV_HALO, :] = jnp.zeros((CONV_HALO, ch), F32)

    @pl.when(i > 0)
    def _():
        ext_ref[0:CONV_HALO, :] = ext_ref[tt:tt + CONV_HALO, :]

    x = x_ref[...]
    chunk = min(GLU_CHUNK, ch)
    for c0 in range(0, ch, chunk):
        cols = slice(c0, c0 + chunk)
        ext_ref[CONV_HALO:CONV_HALO + tt, cols] = _dot(x, wv_ref[:, cols]) * _sigmoid(_dot(x, wg_ref[:, cols]))
        for c in range(c0, c0 + chunk, V7X_LANES):
            _causal_conv_columns(ext_ref, w_ref, conv_ref, c, tt)
    y = _layer_norm(conv_ref[...] + cb_ref[...], g_ref[...], b_ref[...])
    o_ref[...] = (y * _sigmoid(y)).astype(o_ref.dtype)

    @pl.when(i == pl.num_programs(1) - 1)
    def _():
        tail_ref[...] = ext_ref[tt:tt + CONV_HALO, :]


def _prompt_glu_conv(xb, w, col_val, col_gate, conv_w, conv_b, ln_g, ln_b, to_bf16):
    n, t, d = xb.shape
    kw, ch = conv_w.shape
    tt = _tile(t, 256)
    nt = t // tt
    assert kw - 1 <= CONV_HALO <= tt and tt % V7X_SUBLANES == 0 and ch % min(GLU_CHUNK, ch) == 0
    assert col_val % ch == 0 and col_gate % ch == 0
    hosted = to_bf16 if all(a.shape[0] % (n * nt * 2 * V7X_SUBLANES) == 0 for a in to_bf16) else ()
    slab = lambda a: pl.BlockSpec((a.shape[0] // (n * nt), a.shape[1]), lambda b, i: (b * nt + i, 0))
    vec = pl.BlockSpec((1, ch), lambda b, i: (0, 0))
    wspec = lambda col: pl.BlockSpec((d, ch), lambda b, i: (0, col // ch), pipeline_mode=pl.Buffered(1))
    blocks = (tt * d * 2 + d * ch * 2 + (2 * tt + 2 * CONV_HALO) * ch * 4 + tt * ch * 2
              + sum(a.size * 6 // (n * nt) for a in hosted))
    cn, tail, *copies = pl.pallas_call(
        _glu_conv_kernel,
        grid=(n, nt),
        in_specs=[pl.BlockSpec((None, tt, d), lambda b, i: (b, i, 0)), wspec(col_val), wspec(col_gate),
                  pl.BlockSpec((kw, ch), lambda b, i: (0, 0)), vec, vec, vec] + [slab(a) for a in hosted],
        out_specs=[pl.BlockSpec((None, tt, ch), lambda b, i: (b, i, 0)),
                   pl.BlockSpec((None, CONV_HALO, ch), lambda b, i: (b, 0, 0))] + [slab(a) for a in hosted],
        out_shape=[jax.ShapeDtypeStruct((n, t, ch), BF16), jax.ShapeDtypeStruct((n, CONV_HALO, ch), F32)]
        + [jax.ShapeDtypeStruct(a.shape, BF16) for a in hosted],
        scratch_shapes=[pltpu.VMEM((CONV_HALO + tt, ch), F32), pltpu.VMEM((tt, ch), F32)],
        compiler_params=_params(("arbitrary", "arbitrary"), blocks),
        name="glu_conv",
    )(xb, w, w, conv_w, conv_b.reshape(1, ch), ln_g.reshape(1, ch), ln_b.reshape(1, ch), *hosted)
    return cn, tail, (copies if hosted else [a.astype(BF16) for a in to_bf16])


def _decode_conv_kernel(state_ref, u_ref, w_ref, cb_ref, g_ref, b_ref, o_ref, new_state_ref):
    kw = w_ref.shape[0]
    u = u_ref[...]
    conv = u * w_ref[kw - 1:kw, :]
    for k in range(kw - 1):
        conv = conv + state_ref[k] * w_ref[k:k + 1, :]
    y = _layer_norm(conv + cb_ref[...], g_ref[...], b_ref[...])
    o_ref[...] = (y * _sigmoid(y)).astype(o_ref.dtype)
    for k in range(kw - 2):
        new_state_ref[k] = state_ref[k + 1]
    new_state_ref[kw - 2] = u


def _decode_conv(states, layer, u, conv_w, conv_b, ln_g, ln_b):
    _, s, hist, ch = states.shape
    kw = conv_w.shape[0]
    ts = _tile(s, 16)
    vec = pl.BlockSpec((1, ch), lambda i: (0, 0))
    tok = pl.BlockSpec((ts, ch), lambda i: (i, 0))
    blocks = 2 * hist * ts * ch * 4 + ts * ch * 6
    out, new_state = pl.pallas_call(
        _decode_conv_kernel,
        grid=(s // ts,),
        in_specs=[pl.BlockSpec((None, hist, ts, ch), lambda i: (layer, 0, i, 0)), tok,
                  pl.BlockSpec((kw, ch), lambda i: (0, 0)), vec, vec, vec],
        out_specs=[tok, pl.BlockSpec((hist, ts, ch), lambda i: (0, i, 0))],
        out_shape=[jax.ShapeDtypeStruct((s, ch), BF16), jax.ShapeDtypeStruct((hist, s, ch), F32)],
        compiler_params=_params(("parallel",), blocks),
        name="decode_conv",
    )(jnp.transpose(states, (0, 2, 1, 3)), u, conv_w, conv_b.reshape(1, ch), ln_g.reshape(1, ch), ln_b.reshape(1, ch))
    return out, jnp.transpose(new_state, (1, 0, 2))


def _merge_kernel(x_ref, a_ref, c_ref, wga_ref, wgc_ref, wao_ref, wco_ref, o_ref):
    x = x_ref[...]
    mixed = (_sigmoid(_dot(x, wga_ref[...])) * _dot(a_ref[...], wao_ref[...])
             + _sigmoid(_dot(x, wgc_ref[...])) * _dot(c_ref[...], wco_ref[...]))
    o_ref[...] = mixed.astype(o_ref.dtype)


def _merge(xb, attn, cn, w_in, col_ga, col_gc, w_ao, w_co, tm):
    m, d = xb.shape
    wa = attn.shape[1]
    wc = cn.shape[1]
    tn = _tile(d, 512)
    tm = _tile(m, tm)
    assert col_ga % tn == 0 and col_gc % tn == 0
    blocks = tm * (d + wa + wc + tn) * 2 + (2 * d + wa + wc) * tn * 2
    return pl.pallas_call(
        _merge_kernel,
        grid=(m // tm, d // tn),
        in_specs=[pl.BlockSpec((tm, d), lambda i, j: (i, 0)),
                  pl.BlockSpec((tm, wa), lambda i, j: (i, 0)),
                  pl.BlockSpec((tm, wc), lambda i, j: (i, 0)),
                  pl.BlockSpec((d, tn), lambda i, j: (0, col_ga // tn + j)),
                  pl.BlockSpec((d, tn), lambda i, j: (0, col_gc // tn + j)),
                  pl.BlockSpec((wa, tn), lambda i, j: (0, j)),
                  pl.BlockSpec((wc, tn), lambda i, j: (0, j))],
        out_specs=pl.BlockSpec((tm, tn), lambda i, j: (i, j)),
        out_shape=jax.ShapeDtypeStruct((m, d), BF16),
        compiler_params=_params(("parallel", "parallel"), blocks),
        name="merge",
    )(xb, attn, cn, w_in, w_in, w_ao, w_co)


def _out_ln_kernel(x_ref, mixed_ref, w_ref, g_ref, b_ref, of_ref, *maybe_ob_ref, alpha):
    tm = x_ref.shape[0]
    halves = [slice(0, tm // 2), slice(tm // 2, tm)] if tm % (4 * V7X_SUBLANES) == 0 else [slice(0, tm)]
    acc = [_dot(mixed_ref[rows, :], w_ref[...]) for rows in halves]
    for rows, a in zip(halves, acc):
        y = _layer_norm(alpha * x_ref[rows, :] + a, g_ref[...], b_ref[...])
        of_ref[rows, :] = y
        for ob_ref in maybe_ob_ref:
            ob_ref[rows, :] = y.astype(ob_ref.dtype)


def _out_ln(x, mixed, w_out, g, b, alpha, tm, emit_bf16):
    m, d = x.shape
    tm = _tile(m, tm)
    vec = pl.BlockSpec((1, d), lambda i: (0, 0))
    row = pl.BlockSpec((tm, d), lambda i: (i, 0))
    blocks = tm * d * (4 + 2 + 4 + 2) + d * d * 2
    return pl.pallas_call(
        functools.partial(_out_ln_kernel, alpha=alpha),
        grid=(m // tm,),
        in_specs=[row, row, pl.BlockSpec((d, d), lambda i: (0, 0)), vec, vec],
        out_specs=[row, row] if emit_bf16 else [row],
        out_shape=[jax.ShapeDtypeStruct((m, d), F32)] + ([jax.ShapeDtypeStruct((m, d), BF16)] if emit_bf16 else []),
        compiler_params=_params(("parallel",), blocks),
        name="out_ln",
    )(x, mixed, w_out, g.reshape(1, d), b.reshape(1, d))


def _mlp_ln_kernel(xf_ref, xb_ref, w1_ref, w2_ref, g_ref, b_ref, o_ref, *, alpha):
    f = pl.program_id(1)

    @pl.when(f == 0)
    def _():
        o_ref[...] = alpha * xf_ref[...]

    hid = jnp.maximum(_dot(xb_ref[...], w1_ref[...]), 0.0)
    o_ref[...] += _dot((hid * hid).astype(BF16), w2_ref[...])

    @pl.when(f == pl.num_programs(1) - 1)
    def _():
        o_ref[...] = _layer_norm(o_ref[...], g_ref[...], b_ref[...])


def _mlp_ln(xf, xb, w1, w2, g, b, alpha, tm):
    m, d = xf.shape
    dff = w1.shape[1]
    tf = _tile(dff, 1024)
    tm = _tile(m, tm)
    vec = pl.BlockSpec((1, d), lambda i, f: (0, 0))
    row = pl.BlockSpec((tm, d), lambda i, f: (i, 0))
    blocks = tm * d * (4 + 2 + 4) + 2 * d * tf * 2 + tm * tf * 4
    return pl.pallas_call(
        functools.partial(_mlp_ln_kernel, alpha=alpha),
        grid=(m // tm, dff // tf),
        in_specs=[row, row, pl.BlockSpec((d, tf), lambda i, f: (0, f)), pl.BlockSpec((tf, d), lambda i, f: (f, 0)),
                  vec, vec],
        out_specs=row,
        out_shape=jax.ShapeDtypeStruct((m, d), F32),
        compiler_params=_params(("parallel", "arbitrary"), blocks),
        name="mlp_ln",
    )(xf, xb, w1, w2, g.reshape(1, d), b.reshape(1, d))


def _mlp_ln_decode_kernel(pt_ref, xf_ref, w1_ref, w2_ref, g_ref, b_ref, q_ref, kn_ref, vn_ref, relb_ref, *refs,
                          alpha, pages_per_step, n_pages, pages_per_block, buckets):
    del pt_ref
    k_refs, v_refs = refs[:pages_per_step], refs[pages_per_step:2 * pages_per_step]
    o_ref, ao_ref, xb_ref, hid_ref, m_ref, l_ref, gs_ref, acc_ref = refs[2 * pages_per_step:]
    part = pl.program_id(2)
    f = pl.program_id(1) * pl.num_programs(2) + part
    nf = pl.num_programs(1) * pl.num_programs(2)

    @pl.when(f == 0)
    def _():
        x = xf_ref[...]
        o_ref[...] = alpha * x
        xb_ref[...] = x.astype(BF16)

    tf, d = w2_ref.shape
    wh = math.gcd(tf, MLP_CHUNK)
    wo = math.gcd(d, MLP_CHUNK)

    def hid_chunk(c):
        hid = jnp.maximum(_dot(xb_ref[...], w1_ref[:, c * wh:(c + 1) * wh]), 0.0)
        hid_ref[:, c * wh:(c + 1) * wh] = (hid * hid).astype(BF16)

    def out_chunk(c):
        o_ref[:, c * wo:(c + 1) * wo] += _dot(hid_ref[...], w2_ref[:, c * wo:(c + 1) * wo])

    chunks = ([functools.partial(hid_chunk, c) for c in range(tf // wh)]
              + [functools.partial(out_chunk, c) for c in range(d // wo)])
    _decode_pages(part, q_ref, kn_ref, vn_ref, relb_ref, k_refs, v_refs, ao_ref, m_ref, l_ref, gs_ref, acc_ref,
                  n_pages=n_pages, pages_per_block=pages_per_block, buckets=buckets, other_work=chunks)

    @pl.when(f == nf - 1)
    def _():
        o_ref[...] = _layer_norm(o_ref[...], g_ref[...], b_ref[...])


MLP_CHUNK = 256
FUSED_MLP_ROWS = 512
FUSED_MLP_COLS = 1024


def _fused_pages_per_step(m, dff, s, n_pages):
    nf = dff // _tile(dff, FUSED_MLP_COLS)
    steps = (m // _tile(m, FUSED_MLP_ROWS)) * nf
    if (s * n_pages) % steps or n_pages % ((s * n_pages) // steps) or nf % (n_pages // ((s * n_pages) // steps)):
        return None
    return (s * n_pages) // steps


def _mlp_ln_decode(xf, w1, w2, g, b, alpha, q, k_new, v_new, cache_k, cache_v, page_table, rel_bias, pps):
    m, d = xf.shape
    dff = w1.shape[1]
    tm, tf = _tile(m, FUSED_MLP_ROWS), _tile(dff, FUSED_MLP_COLS)
    nf = dff // tf
    s, h, dh = q.shape
    page = cache_k.shape[1]
    n_pages = page_table.shape[1]
    parts = n_pages // pps
    spr = nf // parts
    assert (m // tm) * nf == s * parts and nf == spr * parts
    buckets, relb, tok_spec, relb_spec, page_specs, scratch = _decode_specs(
        page_table, cache_k, rel_bias, h, dh, pps,
        seq_of=lambda i, j, a: i * spr + j, part_of=lambda i, j, a: a)
    vec = pl.BlockSpec((1, d), lambda i, j, a, pt: (0, 0))
    row = pl.BlockSpec((tm, d), lambda i, j, a, pt: (i, 0))
    blocks = tm * d * 9 + 2 * d * tf * 2 + tm * tf * 4 + 2 * pps * page * h * dh * 4 + relb.size * 4
    kernel = functools.partial(_mlp_ln_decode_kernel, alpha=alpha, pages_per_step=pps, n_pages=n_pages,
                               pages_per_block=MOBA_BLOCK // page, buckets=buckets)
    return pl.pallas_call(
        kernel,
        grid_spec=pltpu.PrefetchScalarGridSpec(
            num_scalar_prefetch=1,
            grid=(m // tm, spr, parts),
            in_specs=[row, pl.BlockSpec((d, tf), lambda i, j, a, pt: (0, j * parts + a)),
                      pl.BlockSpec((tf, d), lambda i, j, a, pt: (j * parts + a, 0)),
                      vec, vec, tok_spec, tok_spec, tok_spec, relb_spec] + page_specs + page_specs,
            out_specs=[row, tok_spec],
            scratch_shapes=[pltpu.VMEM((tm, d), BF16), pltpu.VMEM((tm, tf), BF16)] + scratch),
        out_shape=[jax.ShapeDtypeStruct((m, d), F32), jax.ShapeDtypeStruct((s, h, dh), BF16)],
        compiler_params=_params(("arbitrary", "arbitrary", "arbitrary"), blocks),
        name="mlp_ln_decode_attn",
    )(page_table, xf, w1, w2, g.reshape(1, d), b.reshape(1, d), q, k_new, v_new, relb,
      *([cache_k] * pps), *([cache_v] * pps))


def _merge_out(x2d, xb, attn, cn, lw, alpha, tm, emit_bf16):
    mixed = _merge(xb, attn, cn, lw["w_in"], lw["col_ga"], lw["col_gc"], lw["w_attn_out"], lw["w_conv_out"], tm=2 * tm)
    return _out_ln(x2d, mixed, lw["w_out"], lw["ln1_g"], lw["ln1_b"], alpha, tm, emit_bf16)


def kernel(x_prompt, x_sample, cache_k, cache_v, state_conv, page_table, rel_bias, w_in, w_attn_out, conv_w, conv_b,
           conv_ln_g, conv_ln_b, w_conv_out, w_out, ln1_g, ln1_b, w_ff1, w_ff2, ln2_g, ln2_b):
    depth = w_in.shape[0]
    n, t, d = x_prompt.shape
    s, ts, _ = x_sample.shape
    n_heads, dh = cache_k.shape[-2:]
    aw = n_heads * dh
    ch = conv_w.shape[-1]
    alpha = (2 * depth) ** 0.25
    scale = dh ** -0.5 * LOG2_E
    assert ts == 1 and w_in.shape[-1] == 3 * aw + 2 * ch + 2 * d
    col_val, col_gate = 3 * aw, 3 * aw + ch
    col_ga, col_gc = 3 * aw + 2 * ch, 3 * aw + 2 * ch + d

    bias_tiles = _prompt_bias_tiles(rel_bias)
    hp = x_prompt.reshape(n * t, d)
    hs = x_sample.reshape(s, d)
    outs = [[] for _ in range(6)]
    for l in range(depth):
        wi = w_in[l].astype(BF16)

        xb, q, kf, kb, vf, vb = _proj_qkv(hp, wi, aw, scale, tm=512, q_dtype=BF16)
        attn = _prompt_attention(q.reshape(n, t, aw), kb.reshape(n, t, aw), vb.reshape(n, t, aw), bias_tiles, n_heads)
        late = dict(w_attn_out=w_attn_out[l], w_conv_out=w_conv_out[l], w_out=w_out[l], w_ff1=w_ff1[l], w_ff2=w_ff2[l])
        cn, u_tail, late_bf16 = _prompt_glu_conv(xb.reshape(n, t, d), wi, col_val, col_gate, conv_w[l], conv_b[l],
                                                 conv_ln_g[l], conv_ln_b[l], to_bf16=tuple(late.values()))
        lw = dict(zip(late, late_bf16), w_in=wi, ln1_g=ln1_g[l], ln1_b=ln1_b[l], ln2_g=ln2_g[l], ln2_b=ln2_b[l],
                  col_ga=col_ga, col_gc=col_gc)
        outs[0].append(kf.reshape(n, t, n_heads, dh))
        outs[1].append(vf.reshape(n, t, n_heads, dh))
        outs[2].append(u_tail[:, CONV_HALO - (conv_w.shape[1] - 1):, :])

        xsb, qs, ksf, _, vsf, _ = _proj_qkv(hs, wi, aw, scale, tm=s, q_dtype=F32)
        us = _proj_glu(xsb, wi, col_val, col_gate, ch, tm=s)
        dec = (qs.reshape(s, n_heads, dh), ksf.reshape(s, n_heads, dh), vsf.reshape(s, n_heads, dh),
               cache_k[l], cache_v[l], page_table, rel_bias)

        mlp_w = (lw["w_ff1"], lw["w_ff2"], lw["ln2_g"], lw["ln2_b"], alpha)
        pps = _fused_pages_per_step(n * t, w_ff1.shape[-1], s, page_table.shape[1])
        x1 = _merge_out(hp, xb, attn.reshape(n * t, aw), cn.reshape(n * t, ch), lw, alpha, 512, emit_bf16=pps is None)
        if pps is None:
            hp_next = _mlp_ln(x1[0], x1[1], *mlp_w, tm=512)
            attn_s = _decode_attention(*dec)
        else:
            hp_next, attn_s = _mlp_ln_decode(x1[0], *mlp_w, *dec, pps)

        cn_s, new_state = _decode_conv(state_conv, l, us, conv_w[l], conv_b[l], conv_ln_g[l], conv_ln_b[l])
        x1s = _merge_out(hs, xsb, attn_s.reshape(s, aw), cn_s, lw, alpha, s, emit_bf16=True)
        hs_next = _mlp_ln(x1s[0], x1s[1], *mlp_w, tm=s)
        outs[3].append(ksf.reshape(s, 1, n_heads, dh))
        outs[4].append(vsf.reshape(s, 1, n_heads, dh))
        outs[5].append(new_state)
        hp, hs = hp_next, hs_next

    return (hp.reshape(n, t, d), hs.reshape(s, 1, d)) + tuple(jnp.stack(o) for o in outs)
```

```python
import functools
import math

import numpy as np
import jax
import jax.numpy as jnp
from jax import lax
from jax.experimental import pallas as pl
from jax.experimental.pallas import tpu as pltpu

MOBA_BLOCK = 256
MOBA_TOPK = 3
MAX_DISTANCE = 128
LN_EPS = 1e-5
NEG_INF = -1e30
LOG2_E = math.log2(math.e)

V7X_VMEM_BYTES = 64 * 1024 * 1024
V7X_LANES = 128
V7X_SUBLANES = 8

F32 = jnp.float32
BF16 = jnp.bfloat16


def _vmem_limit(block_bytes):
    return int(min(max(2 * block_bytes + (16 << 20), 32 << 20), V7X_VMEM_BYTES - (4 << 20)))


def _params(semantics, block_bytes):
    return pltpu.CompilerParams(dimension_semantics=semantics, vmem_limit_bytes=_vmem_limit(block_bytes))


def _tile(n, want):
    t = min(n, want)
    while n % t:
        t -= 1
    return t


def _sigmoid(x):
    return 1.0 / (1.0 + jnp.exp(-x))


def _layer_norm(y, g, b):
    mu = jnp.mean(y, axis=-1, keepdims=True)
    d = y - mu
    var = jnp.mean(d * d, axis=-1, keepdims=True)
    return d * lax.rsqrt(var + LN_EPS) * g + b


def _dot(a, b):
    return jnp.dot(a, b, preferred_element_type=F32)


def _dot_nt(a, b):
    return lax.dot_general(a, b, (((1,), (1,)), ((), ())), preferred_element_type=F32)


def _proj_qkv_kernel(x_ref, w_ref, xb_ref, q_ref, kf_ref, kb_ref, vf_ref, vb_ref, *, scale):
    aw = q_ref.shape[1]
    xb = x_ref[...].astype(BF16)
    xb_ref[...] = xb
    q_ref[...] = (_dot(xb, w_ref[:, 0:aw]) * scale).astype(q_ref.dtype)
    for c, (f_ref, b_ref) in enumerate(((kf_ref, kb_ref), (vf_ref, vb_ref)), start=1):
        acc = _dot(xb, w_ref[:, c * aw:(c + 1) * aw])
        f_ref[...] = acc
        b_ref[...] = acc.astype(b_ref.dtype)


def _proj_glu_kernel(x_ref, wv_ref, wg_ref, u_ref):
    x = x_ref[...]
    u_ref[...] = _dot(x, wv_ref[...]) * _sigmoid(_dot(x, wg_ref[...]))


def _proj_qkv(x, w, aw, scale, tm, q_dtype):
    m, d = x.shape
    tm = _tile(m, tm)
    row = lambda width: pl.BlockSpec((tm, width), lambda i: (i, 0))
    blocks = tm * d * 6 + tm * aw * 16 + d * 3 * aw
    return pl.pallas_call(
        functools.partial(_proj_qkv_kernel, scale=scale),
        grid=(m // tm,),
        in_specs=[row(d), pl.BlockSpec((d, 3 * aw), lambda i: (0, 0), pipeline_mode=pl.Buffered(1))],
        out_specs=[row(d)] + [row(aw)] * 5,
        out_shape=[jax.ShapeDtypeStruct((m, d), BF16), jax.ShapeDtypeStruct((m, aw), q_dtype)]
        + [jax.ShapeDtypeStruct((m, aw), dt) for dt in (F32, BF16, F32, BF16)],
        compiler_params=_params(("parallel",), blocks),
        name="proj_qkv",
    )(x, w)


def _proj_glu(xb, w, col_val, col_gate, ncols, tm):
    m, d = xb.shape
    tn = _tile(ncols, 512)
    tm = _tile(m, tm)
    assert col_val % tn == 0 and col_gate % tn == 0
    blocks = tm * d * 2 + 2 * d * tn * 2 + tm * tn * 4
    return pl.pallas_call(
        _proj_glu_kernel,
        grid=(m // tm, ncols // tn),
        in_specs=[pl.BlockSpec((tm, d), lambda i, j: (i, 0)),
                  pl.BlockSpec((d, tn), lambda i, j: (0, col_val // tn + j)),
                  pl.BlockSpec((d, tn), lambda i, j: (0, col_gate // tn + j))],
        out_specs=pl.BlockSpec((tm, tn), lambda i, j: (i, j)),
        out_shape=jax.ShapeDtypeStruct((m, ncols), F32),
        compiler_params=_params(("parallel", "parallel"), blocks),
        name="proj_glu",
    )(xb, w, w)


def _rel_bucket_np(dist, num_buckets):
    n = np.maximum(dist, 0)
    max_exact = num_buckets // 2
    nf = np.maximum(n, 1).astype(np.float32)
    large = max_exact + (np.log(nf / np.float32(max_exact)) / np.float32(math.log(MAX_DISTANCE / max_exact))
                         * np.float32(num_buckets - max_exact)).astype(np.int32)
    large = np.minimum(large, num_buckets - 1)
    return np.where(n < max_exact, n, large).astype(np.int32)


def _bias_tiles_kernel(rb_ref, bucket_ref, o_ref, *, num_buckets):
    h = pl.program_id(0)
    far = rb_ref[num_buckets - 1, h]
    bucket = bucket_ref[...]
    acc = jnp.full(bucket.shape, NEG_INF, F32)
    for b in range(num_buckets):
        acc = jnp.where(bucket == b, (rb_ref[b, h] - far) * LOG2_E, acc)
    o_ref[...] = acc


def _prompt_bias_tiles(rel_bias):
    num_buckets, n_heads = rel_bias.shape
    assert MAX_DISTANCE <= MOBA_BLOCK + 1
    r = np.arange(MOBA_BLOCK)[:, None]
    c = np.arange(MOBA_BLOCK)[None, :]
    diag = np.where(r - c >= 0, _rel_bucket_np(r - c, num_buckets), -1)
    prev = _rel_bucket_np(r - c + MOBA_BLOCK, num_buckets)
    buckets = jnp.asarray(np.concatenate([prev, diag], axis=1).astype(np.int32))
    return pl.pallas_call(
        functools.partial(_bias_tiles_kernel, num_buckets=num_buckets),
        grid=(n_heads,),
        in_specs=[pl.BlockSpec(memory_space=pltpu.SMEM),
                  pl.BlockSpec((MOBA_BLOCK, 2 * MOBA_BLOCK), lambda h: (0, 0))],
        out_specs=pl.BlockSpec((None, MOBA_BLOCK, 2 * MOBA_BLOCK), lambda h: (h, 0, 0)),
        out_shape=jax.ShapeDtypeStruct((n_heads, MOBA_BLOCK, 2 * MOBA_BLOCK), F32),
        compiler_params=_params(("arbitrary",), 4 * MOBA_BLOCK * MOBA_BLOCK * 4),
        name="bias_tiles",
    )(rel_bias, buckets)


def _block_penalty_t(gate_t, n_valid):
    row = lax.broadcasted_iota(jnp.int32, gate_t.shape, 0)
    valid = row < n_valid
    pen = jnp.zeros(gate_t.shape, F32)
    for j in range(n_valid):
        gj = gate_t[j:j + 1, :]
        beats = ((gate_t > gj) | ((gate_t == gj) & (row < j))) & valid
        rank = jnp.sum(beats.astype(F32), axis=0, keepdims=True)
        pen = jnp.where(row == j, jnp.where(rank < MOBA_TOPK, 0.0, NEG_INF), pen)
    return pen


def _prompt_attn_kernel(q_ref, k_ref, v_ref, bias_ref, o_ref, kx_ref, pen_ref):
    t = q_ref.shape[0]
    dh = V7X_LANES
    heads = [slice(c, c + dh) for c in range(0, q_ref.shape[1], dh)]
    blk = MOBA_BLOCK
    nblk = t // blk
    row = lax.broadcasted_iota(jnp.int32, (t, dh), 0)
    col = lax.broadcasted_iota(jnp.int32, (t, dh), 1)
    membership = (col == lax.shift_right_logical(row, blk.bit_length() - 1)).astype(BF16)
    in_block = (lax.shift_right_logical(lax.broadcasted_iota(jnp.int32, (nblk, t), 1), blk.bit_length() - 1)
                == lax.broadcasted_iota(jnp.int32, (nblk, t), 0))
    averaging = jnp.where(in_block, 1.0 / blk, 0.0).astype(BF16)
    eye = (lax.broadcasted_iota(jnp.int32, (blk, blk), 0) == lax.broadcasted_iota(jnp.int32, (blk, blk), 1)).astype(BF16)
    km_hi, km_lo = [], []
    for hh, cs in enumerate(heads):
        kx_ref[hh, :, :dh] = k_ref[:, cs]
        kx_ref[hh, :, dh:] = membership
        km = _dot(averaging, k_ref[:, cs])
        km_hi.append(km.astype(BF16))
        km_lo.append((km - km_hi[hh].astype(F32)).astype(BF16))
    pen_ref[...] = jnp.zeros(pen_ref.shape, F32)

    def masked_logits(hh, i):
        qi = q_ref[i * blk:(i + 1) * blk, heads[hh]]
        nk = (i + 1) * blk
        if i <= MOBA_TOPK:
            return _dot_nt(qi, k_ref[0:nk, heads[hh]])
        gate_t = _dot_nt(km_hi[hh], qi) + _dot_nt(km_lo[hh], qi)
        pen_ref[hh, 0:nblk, :] = _block_penalty_t(gate_t, i)
        pen = _dot_nt(eye, pen_ref[hh].astype(BF16)).astype(BF16)
        return _dot_nt(jnp.concatenate([qi, pen], axis=1), kx_ref[hh, 0:nk, :])

    s_next = [masked_logits(hh, 0) for hh in range(len(heads))]
    for i in range(nblk):
        nk = (i + 1) * blk
        for hh, cs in enumerate(heads):
            s = s_next[hh]
            if i + 1 < nblk:
                s_next[hh] = masked_logits(hh, i + 1)
            near = min(nk, 2 * blk)
            s_near = s[:, nk - near:] + bias_ref[hh, :, 2 * blk - near:]
            s = s_near if near == nk else jnp.concatenate([s[:, :nk - near], s_near], axis=1)
            m = jnp.max(s, axis=1, keepdims=True)
            p = jnp.exp2(s - m)
            l = jnp.sum(p, axis=1, keepdims=True)
            o = _dot(p.astype(BF16), v_ref[0:nk, cs])
            o_ref[i * blk:(i + 1) * blk, cs] = (o / l).astype(o_ref.dtype)


ATTN_HEADS_PER_STEP = 4


def _prompt_attention(q, k, v, bias_tiles, n_heads):
    n, t, width = q.shape
    dh = width // n_heads
    nblk = t // MOBA_BLOCK
    assert dh == V7X_LANES and t % MOBA_BLOCK == 0 and nblk <= V7X_LANES and MOBA_BLOCK & (MOBA_BLOCK - 1) == 0
    hp = ATTN_HEADS_PER_STEP if n_heads % ATTN_HEADS_PER_STEP == 0 else 1
    seq_spec = pl.BlockSpec((None, t, hp * dh), lambda h, b: (b, 0, h))
    blocks = hp * (4 * t * dh * 2 + 2 * MOBA_BLOCK * MOBA_BLOCK * 4 + t * 2 * dh * 2 + 6 * MOBA_BLOCK * t * 4)
    return pl.pallas_call(
        _prompt_attn_kernel,
        grid=(n_heads // hp, n),
        in_specs=[seq_spec, seq_spec, seq_spec,
                  pl.BlockSpec((hp, MOBA_BLOCK, 2 * MOBA_BLOCK), lambda h, b: (h, 0, 0))],
        out_specs=seq_spec,
        out_shape=jax.ShapeDtypeStruct((n, t, width), BF16),
        scratch_shapes=[pltpu.VMEM((hp, t, 2 * dh), BF16), pltpu.VMEM((hp, dh, MOBA_BLOCK), F32)],
        compiler_params=_params(("parallel", "parallel"), blocks),
        name="prompt_attn",
    )(q, k, v, bias_tiles)


DECODE_ROWS = 16


def _lane_sum(x):
    return jnp.broadcast_to(jnp.sum(x, axis=-1, keepdims=True), x.shape)


def _page_bias(relb_ref, page_buckets):
    if len(set(page_buckets)) == 1:
        return relb_ref[page_buckets[0]][None] * LOG2_E
    return jnp.stack([relb_ref[b] for b in page_buckets]) * LOG2_E


def _decode_pages(part, q_ref, kn_ref, vn_ref, relb_ref, k_refs, v_refs, o_ref, m_ref, l_ref, g_ref, acc_ref, *,
                  n_pages, pages_per_block, buckets, other_work=()):
    pps = len(k_refs)
    parts = n_pages // pps
    page, h, dh = k_refs[0].shape
    q = q_ref[...]
    rows = min(DECODE_ROWS, page)
    steps_per_page = -(-page // (2 * rows))
    interleave = {}
    for c, job in enumerate(other_work):
        interleave.setdefault(c * pps * steps_per_page // len(other_work), []).append(job)
    for p in range(pps):
        page_buckets = [buckets[(a * pps + p) * page:(a * pps + p + 1) * page] for a in range(parts)]
        shared = set(b for pb in page_buckets for b in pb)
        shared = shared.pop() if len(shared) == 1 else None
        m = jnp.full((h, dh), NEG_INF, F32)
        l = acc = gsum = jnp.zeros((h, dh), F32)
        for r0 in range(0, page, 2 * rows):
            for job in interleave.get(p * steps_per_page + r0 // (2 * rows), ()):
                job()
            groups = [slice(r, r + rows) for r in range(r0, min(r0 + 2 * rows, page), rows)]
            for g in groups:
                s = _lane_sum(k_refs[p][g] * q[None])
                gsum = gsum + jnp.sum(s, axis=0)
                if shared is None:
                    bias = _page_bias(relb_ref, page_buckets[0][g])
                    for a in range(1, parts):
                        if page_buckets[a][g] != page_buckets[0][g]:
                            bias = jnp.where(part == a, _page_bias(relb_ref, page_buckets[a][g]), bias)
                    s = s + bias
                m_new = jnp.maximum(m, jnp.max(s, axis=0))
                scale = jnp.exp2(m - m_new)
                e = jnp.exp2(s - m_new[None])
                l = l * scale + jnp.sum(e, axis=0)
                acc = acc * scale + jnp.sum(e * v_refs[p][g], axis=0)
                m = m_new
        idx = part * pps + p
        m_ref[idx] = m if shared is None else m + relb_ref[shared] * LOG2_E
        l_ref[idx] = l
        g_ref[idx] = gsum
        acc_ref[idx] = acc

    def merge():
        n_blocks = n_pages // pages_per_block
        gate = [sum(g_ref[p] for p in range(j * pages_per_block, (j + 1) * pages_per_block)) for j in range(n_blocks)]
        s_own = _lane_sum(q * kn_ref[...]) + relb_ref[0] * LOG2_E
        m_tot = s_own
        sel = []
        for j in range(n_blocks):
            rank = jnp.zeros((h, dh), F32)
            for j2 in range(n_blocks):
                if j2 != j:
                    beats = (gate[j2] > gate[j]) | ((gate[j2] == gate[j]) & (j2 < j))
                    rank = rank + beats.astype(F32)
            sel.append(rank < MOBA_TOPK)
            for p in range(j * pages_per_block, (j + 1) * pages_per_block):
                m_tot = jnp.maximum(m_tot, jnp.where(sel[j], m_ref[p], NEG_INF))
        w_own = jnp.exp2(s_own - m_tot)
        num = w_own * vn_ref[...]
        den = w_own
        for j in range(n_blocks):
            for p in range(j * pages_per_block, (j + 1) * pages_per_block):
                w = jnp.where(sel[j], jnp.exp2(m_ref[p] - m_tot), 0.0)
                num = num + w * acc_ref[p]
                den = den + w * l_ref[p]
        o_ref[...] = (num / den).astype(o_ref.dtype)

    if parts == 1:
        merge()
    else:
        pl.when(part == parts - 1)(merge)


def _decode_attn_kernel(pt_ref, q_ref, kn_ref, vn_ref, relb_ref, *refs, n_pages, pages_per_block, buckets):
    del pt_ref
    _decode_pages(0, q_ref, kn_ref, vn_ref, relb_ref, refs[:n_pages], refs[n_pages:2 * n_pages], *refs[2 * n_pages:],
                  n_pages=n_pages, pages_per_block=pages_per_block, buckets=buckets)


def _decode_specs(page_table, cache_k, rel_bias, h, dh, pages_per_step, seq_of, part_of):
    page = cache_k.shape[1]
    n_pages = page_table.shape[1]
    past = n_pages * page
    num_buckets = rel_bias.shape[0]
    assert dh == V7X_LANES and MOBA_BLOCK % page == 0 and past % MOBA_BLOCK == 0 and n_pages % pages_per_step == 0
    buckets = tuple(int(b) for b in _rel_bucket_np(past - np.arange(past), num_buckets))
    relb = jnp.broadcast_to(rel_bias[:, :, None], (num_buckets, h, dh))
    tok_spec = pl.BlockSpec((None, h, dh), lambda *g: (seq_of(*g[:-1]), 0, 0))
    relb_spec = pl.BlockSpec((num_buckets, h, dh), lambda *g: (0, 0, 0))
    page_specs = [pl.BlockSpec((None, page, h, dh),
                               lambda *g, p=p: (g[-1][seq_of(*g[:-1]), part_of(*g[:-1]) * pages_per_step + p], 0, 0, 0))
                  for p in range(pages_per_step)]
    scratch = [pltpu.VMEM((n_pages, h, dh), F32)] * 4
    return buckets, relb, tok_spec, relb_spec, page_specs, scratch


def _decode_attention(q, k_new, v_new, cache_k, cache_v, page_table, rel_bias):
    s, h, dh = q.shape
    page = cache_k.shape[1]
    n_pages = page_table.shape[1]
    buckets, relb, tok_spec, relb_spec, page_specs, scratch = _decode_specs(
        page_table, cache_k, rel_bias, h, dh, n_pages, seq_of=lambda b: b, part_of=lambda b: 0)
    blocks = 2 * n_pages * page * h * dh * 4 + relb.size * 4
    kernel = functools.partial(_decode_attn_kernel, n_pages=n_pages, pages_per_block=MOBA_BLOCK // page,
                               buckets=buckets)
    return pl.pallas_call(
        kernel,
        grid_spec=pltpu.PrefetchScalarGridSpec(
            num_scalar_prefetch=1,
            grid=(s,),
            in_specs=[tok_spec, tok_spec, tok_spec, relb_spec] + page_specs + page_specs,
            out_specs=tok_spec,
            scratch_shapes=scratch),
        out_shape=jax.ShapeDtypeStruct((s, h, dh), BF16),
        compiler_params=_params(("parallel",), blocks),
        name="decode_attn",
    )(page_table, q, k_new, v_new, relb, *([cache_k] * n_pages), *([cache_v] * n_pages))


CONV_HALO = 32
CONV_ROWS = 256
GLU_CHUNK = 256


def _causal_conv_columns(ext_ref, w_ref, conv_ref, c0, tt):
    kw = w_ref.shape[0]
    first = CONV_HALO - (kw - 1)
    rows = min(CONV_ROWS, tt)
    span = rows + CONV_HALO
    cs = slice(c0, c0 + V7X_LANES)
    for r0 in range(0, tt, rows):
        x = ext_ref[r0:r0 + span, cs]
        acc = jnp.zeros((rows, V7X_LANES), F32)
        for b in range(V7X_SUBLANES):
            taps = [s for s in range(first, first + kw) if s % V7X_SUBLANES == b]
            assert all(s + rows <= span for s in taps)
            xb = x if b == 0 else pltpu.roll(x, span - b, axis=0)
            for s in taps:
                acc = acc + xb[s - b:s - b + rows] * w_ref[s - first:s - first + 1, cs]
        conv_ref[r0:r0 + rows, cs] = acc


def _glu_conv_kernel(x_ref, wv_ref, wg_ref, w_ref, cb_ref, g_ref, b_ref, *refs):
    n_hosted = (len(refs) - 4) // 2
    o_ref, tail_ref = refs[n_hosted:n_hosted + 2]
    ext_ref, conv_ref = refs[-2:]
    for src_ref, dst_ref in zip(refs[:n_hosted], refs[n_hosted + 2:-2]):
        dst_ref[...] = src_ref[...].astype(dst_ref.dtype)
    tt = x_ref.shape[0]
    ch = wv_ref.shape[1]
    i = pl.program_id(1)

    @pl.when(i == 0)
    def _():
        ext_ref[0:CONV_HALO, :] = jnp.zeros((CONV_HALO, ch), F32)

    @pl.when(i > 0)
    def _():
        ext_ref[0:CONV_HALO, :] = ext_ref[tt:tt + CONV_HALO, :]

    x = x_ref[...]
    chunk = min(GLU_CHUNK, ch)
    for c0 in range(0, ch, chunk):
        cols = slice(c0, c0 + chunk)
        ext_ref[CONV_HALO:CONV_HALO + tt, cols] = _dot(x, wv_ref[:, cols]) * _sigmoid(_dot(x, wg_ref[:, cols]))
        for c in range(c0, c0 + chunk, V7X_LANES):
            _causal_conv_columns(ext_ref, w_ref, conv_ref, c, tt)
    y = _layer_norm(conv_ref[...] + cb_ref[...], g_ref[...], b_ref[...])
    o_ref[...] = (y * _sigmoid(y)).astype(o_ref.dtype)

    @pl.when(i == pl.num_programs(1) - 1)
    def _():
        tail_ref[...] = ext_ref[tt:tt + CONV_HALO, :]


def _prompt_glu_conv(xb, w, col_val, col_gate, conv_w, conv_b, ln_g, ln_b, to_bf16):
    n, t, d = xb.shape
    kw, ch = conv_w.shape
    tt = _tile(t, 256)
    nt = t // tt
    assert kw - 1 <= CONV_HALO <= tt and tt % V7X_SUBLANES == 0 and ch % min(GLU_CHUNK, ch) == 0
    assert col_val % ch == 0 and col_gate % ch == 0
    hosted = to_bf16 if all(a.shape[0] % (n * nt * 2 * V7X_SUBLANES) == 0 for a in to_bf16) else ()
    slab = lambda a: pl.BlockSpec((a.shape[0] // (n * nt), a.shape[1]), lambda b, i: (b * nt + i, 0))
    vec = pl.BlockSpec((1, ch), lambda b, i: (0, 0))
    wspec = lambda col: pl.BlockSpec((d, ch), lambda b, i: (0, col // ch), pipeline_mode=pl.Buffered(1))
    blocks = (tt * d * 2 + d * ch * 2 + (2 * tt + 2 * CONV_HALO) * ch * 4 + tt * ch * 2
              + sum(a.size * 6 // (n * nt) for a in hosted))
    cn, tail, *copies = pl.pallas_call(
        _glu_conv_kernel,
        grid=(n, nt),
        in_specs=[pl.BlockSpec((None, tt, d), lambda b, i: (b, i, 0)), wspec(col_val), wspec(col_gate),
                  pl.BlockSpec((kw, ch), lambda b, i: (0, 0)), vec, vec, vec] + [slab(a) for a in hosted],
        out_specs=[pl.BlockSpec((None, tt, ch), lambda b, i: (b, i, 0)),
                   pl.BlockSpec((None, CONV_HALO, ch), lambda b, i: (b, 0, 0))] + [slab(a) for a in hosted],
        out_shape=[jax.ShapeDtypeStruct((n, t, ch), BF16), jax.ShapeDtypeStruct((n, CONV_HALO, ch), F32)]
        + [jax.ShapeDtypeStruct(a.shape, BF16) for a in hosted],
        scratch_shapes=[pltpu.VMEM((CONV_HALO + tt, ch), F32), pltpu.VMEM((tt, ch), F32)],
        compiler_params=_params(("arbitrary", "arbitrary"), blocks),
        name="glu_conv",
    )(xb, w, w, conv_w, conv_b.reshape(1, ch), ln_g.reshape(1, ch), ln_b.reshape(1, ch), *hosted)
    return cn, tail, (copies if hosted else [a.astype(BF16) for a in to_bf16])


def _decode_conv_kernel(state_ref, u_ref, w_ref, cb_ref, g_ref, b_ref, o_ref, new_state_ref):
    kw = w_ref.shape[0]
    u = u_ref[...]
    conv = u * w_ref[kw - 1:kw, :]
    for k in range(kw - 1):
        conv = conv + state_ref[k] * w_ref[k:k + 1, :]
    y = _layer_norm(conv + cb_ref[...], g_ref[...], b_ref[...])
    o_ref[...] = (y * _sigmoid(y)).astype(o_ref.dtype)
    for k in range(kw - 2):
        new_state_ref[k] = state_ref[k + 1]
    new_state_ref[kw - 2] = u


def _decode_conv(states, layer, u, conv_w, conv_b, ln_g, ln_b):
    _, s, hist, ch = states.shape
    kw = conv_w.shape[0]
    ts = _tile(s, 16)
    vec = pl.BlockSpec((1, ch), lambda i: (0, 0))
    tok = pl.BlockSpec((ts, ch), lambda i: (i, 0))
    blocks = 2 * hist * ts * ch * 4 + ts * ch * 6
    out, new_state = pl.pallas_call(
        _decode_conv_kernel,
        grid=(s // ts,),
        in_specs=[pl.BlockSpec((None, hist, ts, ch), lambda i: (layer, 0, i, 0)), tok,
                  pl.BlockSpec((kw, ch), lambda i: (0, 0)), vec, vec, vec],
        out_specs=[tok, pl.BlockSpec((hist, ts, ch), lambda i: (0, i, 0))],
        out_shape=[jax.ShapeDtypeStruct((s, ch), BF16), jax.ShapeDtypeStruct((hist, s, ch), F32)],
        compiler_params=_params(("parallel",), blocks),
        name="decode_conv",
    )(jnp.transpose(states, (0, 2, 1, 3)), u, conv_w, conv_b.reshape(1, ch), ln_g.reshape(1, ch), ln_b.reshape(1, ch))
    return out, jnp.transpose(new_state, (1, 0, 2))


def _merge_kernel(x_ref, a_ref, c_ref, wga_ref, wgc_ref, wao_ref, wco_ref, o_ref):
    x = x_ref[...]
    mixed = (_sigmoid(_dot(x, wga_ref[...])) * _dot(a_ref[...], wao_ref[...])
             + _sigmoid(_dot(x, wgc_ref[...])) * _dot(c_ref[...], wco_ref[...]))
    o_ref[...] = mixed.astype(o_ref.dtype)


def _merge(xb, attn, cn, w_in, col_ga, col_gc, w_ao, w_co, tm):
    m, d = xb.shape
    wa = attn.shape[1]
    wc = cn.shape[1]
    tn = _tile(d, 512)
    tm = _tile(m, tm)
    assert col_ga % tn == 0 and col_gc % tn == 0
    blocks = tm * (d + wa + wc + tn) * 2 + (2 * d + wa + wc) * tn * 2
    return pl.pallas_call(
        _merge_kernel,
        grid=(m // tm, d // tn),
        in_specs=[pl.BlockSpec((tm, d), lambda i, j: (i, 0)),
                  pl.BlockSpec((tm, wa), lambda i, j: (i, 0)),
                  pl.BlockSpec((tm, wc), lambda i, j: (i, 0)),
                  pl.BlockSpec((d, tn), lambda i, j: (0, col_ga // tn + j)),
                  pl.BlockSpec((d, tn), lambda i, j: (0, col_gc // tn + j)),
                  pl.BlockSpec((wa, tn), lambda i, j: (0, j)),
                  pl.BlockSpec((wc, tn), lambda i, j: (0, j))],
        out_specs=pl.BlockSpec((tm, tn), lambda i, j: (i, j)),
        out_shape=jax.ShapeDtypeStruct((m, d), BF16),
        compiler_params=_params(("parallel", "parallel"), blocks),
        name="merge",
    )(xb, attn, cn, w_in, w_in, w_ao, w_co)


def _out_ln_kernel(x_ref, mixed_ref, w_ref, g_ref, b_ref, of_ref, *maybe_ob_ref, alpha):
    tm = x_ref.shape[0]
    halves = [slice(0, tm // 2), slice(tm // 2, tm)] if tm % (4 * V7X_SUBLANES) == 0 else [slice(0, tm)]
    acc = [_dot(mixed_ref[rows, :], w_ref[...]) for rows in halves]
    for rows, a in zip(halves, acc):
        y = _layer_norm(alpha * x_ref[rows, :] + a, g_ref[...], b_ref[...])
        of_ref[rows, :] = y
        for ob_ref in maybe_ob_ref:
            ob_ref[rows, :] = y.astype(ob_ref.dtype)


def _out_ln(x, mixed, w_out, g, b, alpha, tm, emit_bf16):
    m, d = x.shape
    tm = _tile(m, tm)
    vec = pl.BlockSpec((1, d), lambda i: (0, 0))
    row = pl.BlockSpec((tm, d), lambda i: (i, 0))
    blocks = tm * d * (4 + 2 + 4 + 2) + d * d * 2
    return pl.pallas_call(
        functools.partial(_out_ln_kernel, alpha=alpha),
        grid=(m // tm,),
        in_specs=[row, row, pl.BlockSpec((d, d), lambda i: (0, 0)), vec, vec],
        out_specs=[row, row] if emit_bf16 else [row],
        out_shape=[jax.ShapeDtypeStruct((m, d), F32)] + ([jax.ShapeDtypeStruct((m, d), BF16)] if emit_bf16 else []),
        compiler_params=_params(("parallel",), blocks),
        name="out_ln",
    )(x, mixed, w_out, g.reshape(1, d), b.reshape(1, d))


def _mlp_ln_kernel(xf_ref, xb_ref, w1_ref, w2_ref, g_ref, b_ref, o_ref, *, alpha):
    f = pl.program_id(1)

    @pl.when(f == 0)
    def _():
        o_ref[...] = alpha * xf_ref[...]

    hid = jnp.maximum(_dot(xb_ref[...], w1_ref[...]), 0.0)
    o_ref[...] += _dot((hid * hid).astype(BF16), w2_ref[...])

    @pl.when(f == pl.num_programs(1) - 1)
    def _():
        o_ref[...] = _layer_norm(o_ref[...], g_ref[...], b_ref[...])


def _mlp_ln(xf, xb, w1, w2, g, b, alpha, tm):
    m, d = xf.shape
    dff = w1.shape[1]
    tf = _tile(dff, 1024)
    tm = _tile(m, tm)
    vec = pl.BlockSpec((1, d), lambda i, f: (0, 0))
    row = pl.BlockSpec((tm, d), lambda i, f: (i, 0))
    blocks = tm * d * (4 + 2 + 4) + 2 * d * tf * 2 + tm * tf * 4
    return pl.pallas_call(
        functools.partial(_mlp_ln_kernel, alpha=alpha),
        grid=(m // tm, dff // tf),
        in_specs=[row, row, pl.BlockSpec((d, tf), lambda i, f: (0, f)), pl.BlockSpec((tf, d), lambda i, f: (f, 0)),
                  vec, vec],
        out_specs=row,
        out_shape=jax.ShapeDtypeStruct((m, d), F32),
        compiler_params=_params(("parallel", "arbitrary"), blocks),
        name="mlp_ln",
    )(xf, xb, w1, w2, g.reshape(1, d), b.reshape(1, d))


def _mlp_ln_decode_kernel(pt_ref, xf_ref, w1_ref, w2_ref, g_ref, b_ref, q_ref, kn_ref, vn_ref, relb_ref, *refs,
                          alpha, pages_per_step, n_pages, pages_per_block, buckets):
    del pt_ref
    k_refs, v_refs = refs[:pages_per_step], refs[pages_per_step:2 * pages_per_step]
    o_ref, ao_ref, xb_ref, hid_ref, m_ref, l_ref, gs_ref, acc_ref = refs[2 * pages_per_step:]
    part = pl.program_id(2)
    f = pl.program_id(1) * pl.num_programs(2) + part
    nf = pl.num_programs(1) * pl.num_programs(2)

    @pl.when(f == 0)
    def _():
        x = xf_ref[...]
        o_ref[...] = alpha * x
        xb_ref[...] = x.astype(BF16)

    tf, d = w2_ref.shape
    wh = math.gcd(tf, MLP_CHUNK)
    wo = math.gcd(d, MLP_CHUNK)

    def hid_chunk(c):
        hid = jnp.maximum(_dot(xb_ref[...], w1_ref[:, c * wh:(c + 1) * wh]), 0.0)
        hid_ref[:, c * wh:(c + 1) * wh] = (hid * hid).astype(BF16)

    def out_chunk(c):
        o_ref[:, c * wo:(c + 1) * wo] += _dot(hid_ref[...], w2_ref[:, c * wo:(c + 1) * wo])

    chunks = ([functools.partial(hid_chunk, c) for c in range(tf // wh)]
              + [functools.partial(out_chunk, c) for c in range(d // wo)])
    _decode_pages(part, q_ref, kn_ref, vn_ref, relb_ref, k_refs, v_refs, ao_ref, m_ref, l_ref, gs_ref, acc_ref,
                  n_pages=n_pages, pages_per_block=pages_per_block, buckets=buckets, other_work=chunks)

    @pl.when(f == nf - 1)
    def _():
        o_ref[...] = _layer_norm(o_ref[...], g_ref[...], b_ref[...])


MLP_CHUNK = 256
FUSED_MLP_ROWS = 512
FUSED_MLP_COLS = 1024


def _fused_pages_per_step(m, dff, s, n_pages):
    nf = dff // _tile(dff, FUSED_MLP_COLS)
    steps = (m // _tile(m, FUSED_MLP_ROWS)) * nf
    if (s * n_pages) % steps or n_pages % ((s * n_pages) // steps) or nf % (n_pages // ((s * n_pages) // steps)):
        return None
    return (s * n_pages) // steps


def _mlp_ln_decode(xf, w1, w2, g, b, alpha, q, k_new, v_new, cache_k, cache_v, page_table, rel_bias, pps):
    m, d = xf.shape
    dff = w1.shape[1]
    tm, tf = _tile(m, FUSED_MLP_ROWS), _tile(dff, FUSED_MLP_COLS)
    nf = dff // tf
    s, h, dh = q.shape
    page = cache_k.shape[1]
    n_pages = page_table.shape[1]
    parts = n_pages // pps
    spr = nf // parts
    assert (m // tm) * nf == s * parts and nf == spr * parts
    buckets, relb, tok_spec, relb_spec, page_specs, scratch = _decode_specs(
        page_table, cache_k, rel_bias, h, dh, pps,
        seq_of=lambda i, j, a: i * spr + j, part_of=lambda i, j, a: a)
    vec = pl.BlockSpec((1, d), lambda i, j, a, pt: (0, 0))
    row = pl.BlockSpec((tm, d), lambda i, j, a, pt: (i, 0))
    blocks = tm * d * 9 + 2 * d * tf * 2 + tm * tf * 4 + 2 * pps * page * h * dh * 4 + relb.size * 4
    kernel = functools.partial(_mlp_ln_decode_kernel, alpha=alpha, pages_per_step=pps, n_pages=n_pages,
                               pages_per_block=MOBA_BLOCK // page, buckets=buckets)
    return pl.pallas_call(
        kernel,
        grid_spec=pltpu.PrefetchScalarGridSpec(
            num_scalar_prefetch=1,
            grid=(m // tm, spr, parts),
            in_specs=[row, pl.BlockSpec((d, tf), lambda i, j, a, pt: (0, j * parts + a)),
                      pl.BlockSpec((tf, d), lambda i, j, a, pt: (j * parts + a, 0)),
                      vec, vec, tok_spec, tok_spec, tok_spec, relb_spec] + page_specs + page_specs,
            out_specs=[row, tok_spec],
            scratch_shapes=[pltpu.VMEM((tm, d), BF16), pltpu.VMEM((tm, tf), BF16)] + scratch),
        out_shape=[jax.ShapeDtypeStruct((m, d), F32), jax.ShapeDtypeStruct((s, h, dh), BF16)],
        compiler_params=_params(("arbitrary", "arbitrary", "arbitrary"), blocks),
        name="mlp_ln_decode_attn",
    )(page_table, xf, w1, w2, g.reshape(1, d), b.reshape(1, d), q, k_new, v_new, relb,
      *([cache_k] * pps), *([cache_v] * pps))


def _merge_out(x2d, xb, attn, cn, lw, alpha, tm, emit_bf16):
    mixed = _merge(xb, attn, cn, lw["w_in"], lw["col_ga"], lw["col_gc"], lw["w_attn_out"], lw["w_conv_out"], tm=2 * tm)
    return _out_ln(x2d, mixed, lw["w_out"], lw["ln1_g"], lw["ln1_b"], alpha, tm, emit_bf16)


def kernel(x_prompt, x_sample, cache_k, cache_v, state_conv, page_table, rel_bias, w_in, w_attn_out, conv_w, conv_b,
           conv_ln_g, conv_ln_b, w_conv_out, w_out, ln1_g, ln1_b, w_ff1, w_ff2, ln2_g, ln2_b):
    depth = w_in.shape[0]
    n, t, d = x_prompt.shape
    s, ts, _ = x_sample.shape
    n_heads, dh = cache_k.shape[-2:]
    aw = n_heads * dh
    ch = conv_w.shape[-1]
    alpha = (2 * depth) ** 0.25
    scale = dh ** -0.5 * LOG2_E
    assert ts == 1 and w_in.shape[-1] == 3 * aw + 2 * ch + 2 * d
    col_val, col_gate = 3 * aw, 3 * aw + ch
    col_ga, col_gc = 3 * aw + 2 * ch, 3 * aw + 2 * ch + d

    bias_tiles = _prompt_bias_tiles(rel_bias)
    hp = x_prompt.reshape(n * t, d)
    hs = x_sample.reshape(s, d)
    outs = [[] for _ in range(6)]
    for l in range(depth):
        wi = w_in[l].astype(BF16)

        xb, q, kf, kb, vf, vb = _proj_qkv(hp, wi, aw, scale, tm=512, q_dtype=BF16)
        attn = _prompt_attention(q.reshape(n, t, aw), kb.reshape(n, t, aw), vb.reshape(n, t, aw), bias_tiles, n_heads)
        late = dict(w_attn_out=w_attn_out[l], w_conv_out=w_conv_out[l], w_out=w_out[l], w_ff1=w_ff1[l], w_ff2=w_ff2[l])
        cn, u_tail, late_bf16 = _prompt_glu_conv(xb.reshape(n, t, d), wi, col_val, col_gate, conv_w[l], conv_b[l],
                                                 conv_ln_g[l], conv_ln_b[l], to_bf16=tuple(late.values()))
        lw = dict(zip(late, late_bf16), w_in=wi, ln1_g=ln1_g[l], ln1_b=ln1_b[l], ln2_g=ln2_g[l], ln2_b=ln2_b[l],
                  col_ga=col_ga, col_gc=col_gc)
        outs[0].append(kf.reshape(n, t, n_heads, dh))
        outs[1].append(vf.reshape(n, t, n_heads, dh))
        outs[2].append(u_tail[:, CONV_HALO - (conv_w.shape[1] - 1):, :])

        xsb, qs, ksf, _, vsf, _ = _proj_qkv(hs, wi, aw, scale, tm=s, q_dtype=F32)
        us = _proj_glu(xsb, wi, col_val, col_gate, ch, tm=s)
        dec = (qs.reshape(s, n_heads, dh), ksf.reshape(s, n_heads, dh), vsf.reshape(s, n_heads, dh),
               cache_k[l], cache_v[l], page_table, rel_bias)

        mlp_w = (lw["w_ff1"], lw["w_ff2"], lw["ln2_g"], lw["ln2_b"], alpha)
        pps = _fused_pages_per_step(n * t, w_ff1.shape[-1], s, page_table.shape[1])
        x1 = _merge_out(hp, xb, attn.reshape(n * t, aw), cn.reshape(n * t, ch), lw, alpha, 512, emit_bf16=pps is None)
        if pps is None:
            hp_next = _mlp_ln(x1[0], x1[1], *mlp_w, tm=512)
            attn_s = _decode_attention(*dec)
        else:
            hp_next, attn_s = _mlp_ln_decode(x1[0], *mlp_w, *dec, pps)

        cn_s, new_state = _decode_conv(state_conv, l, us, conv_w[l], conv_b[l], conv_ln_g[l], conv_ln_b[l])
        x1s = _merge_out(hs, xsb, attn_s.reshape(s, aw), cn_s, lw, alpha, s, emit_bf16=True)
        hs_next = _mlp_ln(x1s[0], x1s[1], *mlp_w, tm=s)
        outs[3].append(ksf.reshape(s, 1, n_heads, dh))
        outs[4].append(vsf.reshape(s, 1, n_heads, dh))
        outs[5].append(new_state)
        hp, hs = hp_next, hs_next

    return (hp.reshape(n, t, d), hs.reshape(s, 1, d)) + tuple(jnp.stack(o) for o in outs)
```

```python
import functools
import math

import numpy as np
import jax
import jax.numpy as jnp
from jax import lax
from jax.experimental import pallas as pl
from jax.experimental.pallas import tpu as pltpu

MOBA_BLOCK = 256
MOBA_TOPK = 3
MAX_DISTANCE = 128
LN_EPS = 1e-5
NEG_INF = -1e30
LOG2_E = math.log2(math.e)

V7X_VMEM_BYTES = 64 * 1024 * 1024
V7X_LANES = 128
V7X_SUBLANES = 8

F32 = jnp.float32
BF16 = jnp.bfloat16


def _vmem_limit(block_bytes):
    return int(min(max(2 * block_bytes + (16 << 20), 32 << 20), V7X_VMEM_BYTES - (4 << 20)))


def _params(semantics, block_bytes):
    return pltpu.CompilerParams(dimension_semantics=semantics, vmem_limit_bytes=_vmem_limit(block_bytes))


def _tile(n, want):
    t = min(n, want)
    while n % t:
        t -= 1
    return t


def _sigmoid(x):
    return 1.0 / (1.0 + jnp.exp(-x))


def _layer_norm(y, g, b):
    mu = jnp.mean(y, axis=-1, keepdims=True)
    d = y - mu
    var = jnp.mean(d * d, axis=-1, keepdims=True)
    return d * lax.rsqrt(var + LN_EPS) * g + b


def _dot(a, b):
    return jnp.dot(a, b, preferred_element_type=F32)


def _dot_nt(a, b):
    return lax.dot_general(a, b, (((1,), (1,)), ((), ())), preferred_element_type=F32)


def _proj_qkv_kernel(x_ref, w_ref, xb_ref, q_ref, kf_ref, kb_ref, vf_ref, vb_ref, *, scale):
    aw = q_ref.shape[1]
    xb = x_ref[...].astype(BF16)
    xb_ref[...] = xb
    q_ref[...] = (_dot(xb, w_ref[:, 0:aw]) * scale).astype(q_ref.dtype)
    for c, (f_ref, b_ref) in enumerate(((kf_ref, kb_ref), (vf_ref, vb_ref)), start=1):
        acc = _dot(xb, w_ref[:, c * aw:(c + 1) * aw])
        f_ref[...] = acc
        b_ref[...] = acc.astype(b_ref.dtype)


def _proj_glu_kernel(x_ref, wv_ref, wg_ref, u_ref):
    x = x_ref[...]
    u_ref[...] = _dot(x, wv_ref[...]) * _sigmoid(_dot(x, wg_ref[...]))


def _proj_qkv(x, w, aw, scale, tm, q_dtype):
    m, d = x.shape
    tm = _tile(m, tm)
    row = lambda width: pl.BlockSpec((tm, width), lambda i: (i, 0))
    blocks = tm * d * 6 + tm * aw * 16 + d * 3 * aw
    return pl.pallas_call(
        functools.partial(_proj_qkv_kernel, scale=scale),
        grid=(m // tm,),
        in_specs=[row(d), pl.BlockSpec((d, 3 * aw), lambda i: (0, 0), pipeline_mode=pl.Buffered(1))],
        out_specs=[row(d)] + [row(aw)] * 5,
        out_shape=[jax.ShapeDtypeStruct((m, d), BF16), jax.ShapeDtypeStruct((m, aw), q_dtype)]
        + [jax.ShapeDtypeStruct((m, aw), dt) for dt in (F32, BF16, F32, BF16)],
        compiler_params=_params(("parallel",), blocks),
        name="proj_qkv",
    )(x, w)


def _proj_glu(xb, w, col_val, col_gate, ncols, tm):
    m, d = xb.shape
    tn = _tile(ncols, 512)
    tm = _tile(m, tm)
    assert col_val % tn == 0 and col_gate % tn == 0
    blocks = tm * d * 2 + 2 * d * tn * 2 + tm * tn * 4
    return pl.pallas_call(
        _proj_glu_kernel,
        grid=(m // tm, ncols // tn),
        in_specs=[pl.BlockSpec((tm, d), lambda i, j: (i, 0)),
                  pl.BlockSpec((d, tn), lambda i, j: (0, col_val // tn + j)),
                  pl.BlockSpec((d, tn), lambda i, j: (0, col_gate // tn + j))],
        out_specs=pl.BlockSpec((tm, tn), lambda i, j: (i, j)),
        out_shape=jax.ShapeDtypeStruct((m, ncols), F32),
        compiler_params=_params(("parallel", "parallel"), blocks),
        name="proj_glu",
    )(xb, w, w)


def _rel_bucket_np(dist, num_buckets):
    n = np.maximum(dist, 0)
    max_exact = num_buckets // 2
    nf = np.maximum(n, 1).astype(np.float32)
    large = max_exact + (np.log(nf / np.float32(max_exact)) / np.float32(math.log(MAX_DISTANCE / max_exact))
                         * np.float32(num_buckets - max_exact)).astype(np.int32)
    large = np.minimum(large, num_buckets - 1)
    return np.where(n < max_exact, n, large).astype(np.int32)


def _bias_tiles_kernel(rb_ref, bucket_ref, o_ref, *, num_buckets):
    h = pl.program_id(0)
    far = rb_ref[num_buckets - 1, h]
    bucket = bucket_ref[...]
    acc = jnp.full(bucket.shape, NEG_INF, F32)
    for b in range(num_buckets):
        acc = jnp.where(bucket == b, (rb_ref[b, h] - far) * LOG2_E, acc)
    o_ref[...] = acc


def _prompt_bias_tiles(rel_bias):
    num_buckets, n_heads = rel_bias.shape
    assert MAX_DISTANCE <= MOBA_BLOCK + 1
    r = np.arange(MOBA_BLOCK)[:, None]
    c = np.arange(MOBA_BLOCK)[None, :]
    diag = np.where(r - c >= 0, _rel_bucket_np(r - c, num_buckets), -1)
    prev = _rel_bucket_np(r - c + MOBA_BLOCK, num_buckets)
    buckets = jnp.asarray(np.concatenate([prev, diag], axis=1).astype(np.int32))
    return pl.pallas_call(
        functools.partial(_bias_tiles_kernel, num_buckets=num_buckets),
        grid=(n_heads,),
        in_specs=[pl.BlockSpec(memory_space=pltpu.SMEM),
                  pl.BlockSpec((MOBA_BLOCK, 2 * MOBA_BLOCK), lambda h: (0, 0))],
        out_specs=pl.BlockSpec((None, MOBA_BLOCK, 2 * MOBA_BLOCK), lambda h: (h, 0, 0)),
        out_shape=jax.ShapeDtypeStruct((n_heads, MOBA_BLOCK, 2 * MOBA_BLOCK), F32),
        compiler_params=_params(("arbitrary",), 4 * MOBA_BLOCK * MOBA_BLOCK * 4),
        name="bias_tiles",
    )(rel_bias, buckets)


def _block_penalty_t(gate_t, n_valid):
    row = lax.broadcasted_iota(jnp.int32, gate_t.shape, 0)
    valid = row < n_valid
    pen = jnp.zeros(gate_t.shape, F32)
    for j in range(n_valid):
        gj = gate_t[j:j + 1, :]
        beats = ((gate_t > gj) | ((gate_t == gj) & (row < j))) & valid
        rank = jnp.sum(beats.astype(F32), axis=0, keepdims=True)
        pen = jnp.where(row == j, jnp.where(rank < MOBA_TOPK, 0.0, NEG_INF), pen)
    return pen


def _prompt_attn_kernel(q_ref, k_ref, v_ref, bias_ref, o_ref, kx_ref, pen_ref):
    t = q_ref.shape[0]
    dh = V7X_LANES
    heads = [slice(c, c + dh) for c in range(0, q_ref.shape[1], dh)]
    blk = MOBA_BLOCK
    nblk = t // blk
    row = lax.broadcasted_iota(jnp.int32, (t, dh), 0)
    col = lax.broadcasted_iota(jnp.int32, (t, dh), 1)
    membership = (col == lax.shift_right_logical(row, blk.bit_length() - 1)).astype(BF16)
    in_block = (lax.shift_right_logical(lax.broadcasted_iota(jnp.int32, (nblk, t), 1), blk.bit_length() - 1)
                == lax.broadcasted_iota(jnp.int32, (nblk, t), 0))
    averaging = jnp.where(in_block, 1.0 / blk, 0.0).astype(BF16)
    eye = (lax.broadcasted_iota(jnp.int32, (blk, blk), 0) == lax.broadcasted_iota(jnp.int32, (blk, blk), 1)).astype(BF16)
    km_hi, km_lo = [], []
    for hh, cs in enumerate(heads):
        kx_ref[hh, :, :dh] = k_ref[:, cs]
        kx_ref[hh, :, dh:] = membership
        km = _dot(averaging, k_ref[:, cs])
        km_hi.append(km.astype(BF16))
        km_lo.append((km - km_hi[hh].astype(F32)).astype(BF16))
    pen_ref[...] = jnp.zeros(pen_ref.shape, F32)

    def masked_logits(hh, i):
        qi = q_ref[i * blk:(i + 1) * blk, heads[hh]]
        nk = (i + 1) * blk
        if i <= MOBA_TOPK:
            return _dot_nt(qi, k_ref[0:nk, heads[hh]])
        gate_t = _dot_nt(km_hi[hh], qi) + _dot_nt(km_lo[hh], qi)
        pen_ref[hh, 0:nblk, :] = _block_penalty_t(gate_t, i)
        pen = _dot_nt(eye, pen_ref[hh].astype(BF16)).astype(BF16)
        return _dot_nt(jnp.concatenate([qi, pen], axis=1), kx_ref[hh, 0:nk, :])

    s_next = [masked_logits(hh, 0) for hh in range(len(heads))]
    for i in range(nblk):
        nk = (i + 1) * blk
        for hh, cs in enumerate(heads):
            s = s_next[hh]
            if i + 1 < nblk:
                s_next[hh] = masked_logits(hh, i + 1)
            near = min(nk, 2 * blk)
            s_near = s[:, nk - near:] + bias_ref[hh, :, 2 * blk - near:]
            s = s_near if near == nk else jnp.concatenate([s[:, :nk - near], s_near], axis=1)
            m = jnp.max(s, axis=1, keepdims=True)
            p = jnp.exp2(s - m)
            l = jnp.sum(p, axis=1, keepdims=True)
            o = _dot(p.astype(BF16), v_ref[0:nk, cs])
            o_ref[i * blk:(i + 1) * blk, cs] = (o / l).astype(o_ref.dtype)


ATTN_HEADS_PER_STEP = 4


def _prompt_attention(q, k, v, bias_tiles, n_heads):
    n, t, width = q.shape
    dh = width // n_heads
    nblk = t // MOBA_BLOCK
    assert dh == V7X_LANES and t % MOBA_BLOCK == 0 and nblk <= V7X_LANES and MOBA_BLOCK & (MOBA_BLOCK - 1) == 0
    hp = ATTN_HEADS_PER_STEP if n_heads % ATTN_HEADS_PER_STEP == 0 else 1
    seq_spec = pl.BlockSpec((None, t, hp * dh), lambda h, b: (b, 0, h))
    blocks = hp * (4 * t * dh * 2 + 2 * MOBA_BLOCK * MOBA_BLOCK * 4 + t * 2 * dh * 2 + 6 * MOBA_BLOCK * t * 4)
    return pl.pallas_call(
        _prompt_attn_kernel,
        grid=(n_heads // hp, n),
        in_specs=[seq_spec, seq_spec, seq_spec,
                  pl.BlockSpec((hp, MOBA_BLOCK, 2 * MOBA_BLOCK), lambda h, b: (h, 0, 0))],
        out_specs=seq_spec,
        out_shape=jax.ShapeDtypeStruct((n, t, width), BF16),
        scratch_shapes=[pltpu.VMEM((hp, t, 2 * dh), BF16), pltpu.VMEM((hp, dh, MOBA_BLOCK), F32)],
        compiler_params=_params(("parallel", "parallel"), blocks),
        name="prompt_attn",
    )(q, k, v, bias_tiles)


DECODE_ROWS = 16


def _lane_sum(x):
    return jnp.broadcast_to(jnp.sum(x, axis=-1, keepdims=True), x.shape)


def _page_bias(relb_ref, page_buckets):
    if len(set(page_buckets)) == 1:
        return relb_ref[page_buckets[0]][None] * LOG2_E
    return jnp.stack([relb_ref[b] for b in page_buckets]) * LOG2_E


def _decode_pages(part, q_ref, kn_ref, vn_ref, relb_ref, k_refs, v_refs, o_ref, m_ref, l_ref, g_ref, acc_ref, *,
                  n_pages, pages_per_block, buckets, other_work=()):
    pps = len(k_refs)
    parts = n_pages // pps
    page, h, dh = k_refs[0].shape
    q = q_ref[...]
    rows = min(DECODE_ROWS, page)
    steps_per_page = -(-page // (2 * rows))
    interleave = {}
    for c, job in enumerate(other_work):
        interleave.setdefault(c * pps * steps_per_page // len(other_work), []).append(job)
    for p in range(pps):
        page_buckets = [buckets[(a * pps + p) * page:(a * pps + p + 1) * page] for a in range(parts)]
        shared = set(b for pb in page_buckets for b in pb)
        shared = shared.pop() if len(shared) == 1 else None
        m = jnp.full((h, dh), NEG_INF, F32)
        l = acc = gsum = jnp.zeros((h, dh), F32)
        for r0 in range(0, page, 2 * rows):
            for job in interleave.get(p * steps_per_page + r0 // (2 * rows), ()):
                job()
            groups = [slice(r, r + rows) for r in range(r0, min(r0 + 2 * rows, page), rows)]
            for g in groups:
                s = _lane_sum(k_refs[p][g] * q[None])
                gsum = gsum + jnp.sum(s, axis=0)
                if shared is None:
                    bias = _page_bias(relb_ref, page_buckets[0][g])
                    for a in range(1, parts):
                        if page_buckets[a][g] != page_buckets[0][g]:
                            bias = jnp.where(part == a, _page_bias(relb_ref, page_buckets[a][g]), bias)
                    s = s + bias
                m_new = jnp.maximum(m, jnp.max(s, axis=0))
                scale = jnp.exp2(m - m_new)
                e = jnp.exp2(s - m_new[None])
                l = l * scale + jnp.sum(e, axis=0)
                acc = acc * scale + jnp.sum(e * v_refs[p][g], axis=0)
                m = m_new
        idx = part * pps + p
        m_ref[idx] = m if shared is None else m + relb_ref[shared] * LOG2_E
        l_ref[idx] = l
        g_ref[idx] = gsum
        acc_ref[idx] = acc

    def merge():
        n_blocks = n_pages // pages_per_block
        gate = [sum(g_ref[p] for p in range(j * pages_per_block, (j + 1) * pages_per_block)) for j in range(n_blocks)]
        s_own = _lane_sum(q * kn_ref[...]) + relb_ref[0] * LOG2_E
        m_tot = s_own
        sel = []
        for j in range(n_blocks):
            rank = jnp.zeros((h, dh), F32)
            for j2 in range(n_blocks):
                if j2 != j:
                    beats = (gate[j2] > gate[j]) | ((gate[j2] == gate[j]) & (j2 < j))
                    rank = rank + beats.astype(F32)
            sel.append(rank < MOBA_TOPK)
            for p in range(j * pages_per_block, (j + 1) * pages_per_block):
                m_tot = jnp.maximum(m_tot, jnp.where(sel[j], m_ref[p], NEG_INF))
        w_own = jnp.exp2(s_own - m_tot)
        num = w_own * vn_ref[...]
        den = w_own
        for j in range(n_blocks):
            for p in range(j * pages_per_block, (j + 1) * pages_per_block):
                w = jnp.where(sel[j], jnp.exp2(m_ref[p] - m_tot), 0.0)
                num = num + w * acc_ref[p]
                den = den + w * l_ref[p]
        o_ref[...] = (num / den).astype(o_ref.dtype)

    if parts == 1:
        merge()
    else:
        pl.when(part == parts - 1)(merge)


def _decode_attn_kernel(pt_ref, q_ref, kn_ref, vn_ref, relb_ref, *refs, n_pages, pages_per_block, buckets):
    del pt_ref
    _decode_pages(0, q_ref, kn_ref, vn_ref, relb_ref, refs[:n_pages], refs[n_pages:2 * n_pages], *refs[2 * n_pages:],
                  n_pages=n_pages, pages_per_block=pages_per_block, buckets=buckets)


def _decode_specs(page_table, cache_k, rel_bias, h, dh, pages_per_step, seq_of, part_of):
    page = cache_k.shape[1]
    n_pages = page_table.shape[1]
    past = n_pages * page
    num_buckets = rel_bias.shape[0]
    assert dh == V7X_LANES and MOBA_BLOCK % page == 0 and past % MOBA_BLOCK == 0 and n_pages % pages_per_step == 0
    buckets = tuple(int(b) for b in _rel_bucket_np(past - np.arange(past), num_buckets))
    relb = jnp.broadcast_to(rel_bias[:, :, None], (num_buckets, h, dh))
    tok_spec = pl.BlockSpec((None, h, dh), lambda *g: (seq_of(*g[:-1]), 0, 0))
    relb_spec = pl.BlockSpec((num_buckets, h, dh), lambda *g: (0, 0, 0))
    page_specs = [pl.BlockSpec((None, page, h, dh),
                               lambda *g, p=p: (g[-1][seq_of(*g[:-1]), part_of(*g[:-1]) * pages_per_step + p], 0, 0, 0))
                  for p in range(pages_per_step)]
    scratch = [pltpu.VMEM((n_pages, h, dh), F32)] * 4
    return buckets, relb, tok_spec, relb_spec, page_specs, scratch


def _decode_attention(q, k_new, v_new, cache_k, cache_v, page_table, rel_bias):
    s, h, dh = q.shape
    page = cache_k.shape[1]
    n_pages = page_table.shape[1]
    buckets, relb, tok_spec, relb_spec, page_specs, scratch = _decode_specs(
        page_table, cache_k, rel_bias, h, dh, n_pages, seq_of=lambda b: b, part_of=lambda b: 0)
    blocks = 2 * n_pages * page * h * dh * 4 + relb.size * 4
    kernel = functools.partial(_decode_attn_kernel, n_pages=n_pages, pages_per_block=MOBA_BLOCK // page,
                               buckets=buckets)
    return pl.pallas_call(
        kernel,
        grid_spec=pltpu.PrefetchScalarGridSpec(
            num_scalar_prefetch=1,
            grid=(s,),
            in_specs=[tok_spec, tok_spec, tok_spec, relb_spec] + page_specs + page_specs,
            out_specs=tok_spec,
            scratch_shapes=scratch),
        out_shape=jax.ShapeDtypeStruct((s, h, dh), BF16),
        compiler_params=_params(("parallel",), blocks),
        name="decode_attn",
    )(page_table, q, k_new, v_new, relb, *([cache_k] * n_pages), *([cache_v] * n_pages))


CONV_HALO = 32
CONV_ROWS = 256
GLU_CHUNK = 256


def _causal_conv_columns(ext_ref, w_ref, conv_ref, c0, tt):
    kw = w_ref.shape[0]
    first = CONV_HALO - (kw - 1)
    rows = min(CONV_ROWS, tt)
    span = rows + CONV_HALO
    cs = slice(c0, c0 + V7X_LANES)
    for r0 in range(0, tt, rows):
        x = ext_ref[r0:r0 + span, cs]
        acc = jnp.zeros((rows, V7X_LANES), F32)
        for b in range(V7X_SUBLANES):
            taps = [s for s in range(first, first + kw) if s % V7X_SUBLANES == b]
            assert all(s + rows <= span for s in taps)
            xb = x if b == 0 else pltpu.roll(x, span - b, axis=0)
            for s in taps:
                acc = acc + xb[s - b:s - b + rows] * w_ref[s - first:s - first + 1, cs]
        conv_ref[r0:r0 + rows, cs] = acc


def _glu_conv_kernel(x_ref, wv_ref, wg_ref, w_ref, cb_ref, g_ref, b_ref, *refs):
    n_hosted = (len(refs) - 4) // 2
    o_ref, tail_ref = refs[n_hosted:n_hosted + 2]
    ext_ref, conv_ref = refs[-2:]
    for src_ref, dst_ref in zip(refs[:n_hosted], refs[n_hosted + 2:-2]):
        dst_ref[...] = src_ref[...].astype(dst_ref.dtype)
    tt = x_ref.shape[0]
    ch = wv_ref.shape[1]
    i = pl.program_id(1)

    @pl.when(i == 0)
    def _():
        ext_ref[0:CONV_HALO, :] = jnp.zeros((CONV_HALO, ch), F32)

    @pl.when(i > 0)
    def _():
        ext_ref[0:CONV_HALO, :] = ext_ref[tt:tt + CONV_HALO, :]

    x = x_ref[...]
    chunk = min(GLU_CHUNK, ch)
    for c0 in range(0, ch, chunk):
        cols = slice(c0, c0 + chunk)
        ext_ref[CONV_HALO:CONV_HALO + tt, cols] = _dot(x, wv_ref[:, cols]) * _sigmoid(_dot(x, wg_ref[:, cols]))
        for c in range(c0, c0 + chunk, V7X_LANES):
            _causal_conv_columns(ext_ref, w_ref, conv_ref, c, tt)
    y = _layer_norm(conv_ref[...] + cb_ref[...], g_ref[...], b_ref[...])
    o_ref[...] = (y * _sigmoid(y)).astype(o_ref.dtype)

    @pl.when(i == pl.num_programs(1) - 1)
    def _():
        tail_ref[...] = ext_ref[tt:tt + CONV_HALO, :]


def _prompt_glu_conv(xb, w, col_val, col_gate, conv_w, conv_b, ln_g, ln_b, to_bf16):
    n, t, d = xb.shape
    kw, ch = conv_w.shape
    tt = _tile(t, 512)
    nt = t // tt
    assert kw - 1 <= CONV_HALO <= tt and tt % V7X_SUBLANES == 0 and ch % min(GLU_CHUNK, ch) == 0
    assert col_val % ch == 0 and col_gate % ch == 0
    hosted = to_bf16 if all(a.shape[0] % (n * nt * 2 * V7X_SUBLANES) == 0 for a in to_bf16) else ()
    slab = lambda a: pl.BlockSpec((a.shape[0] // (n * nt), a.shape[1]), lambda b, i: (b * nt + i, 0))
    vec = pl.BlockSpec((1, ch), lambda b, i: (0, 0))
    wspec = lambda col: pl.BlockSpec((d, ch), lambda b, i: (0, col // ch), pipeline_mode=pl.Buffered(1))
    blocks = (tt * d * 2 + d * ch * 2 + (2 * tt + 2 * CONV_HALO) * ch * 4 + tt * ch * 2
              + sum(a.size * 6 // (n * nt) for a in hosted))
    cn, tail, *copies = pl.pallas_call(
        _glu_conv_kernel,
        grid=(n, nt),
        in_specs=[pl.BlockSpec((None, tt, d), lambda b, i: (b, i, 0)), wspec(col_val), wspec(col_gate),
                  pl.BlockSpec((kw, ch), lambda b, i: (0, 0)), vec, vec, vec] + [slab(a) for a in hosted],
        out_specs=[pl.BlockSpec((None, tt, ch), lambda b, i: (b, i, 0)),
                   pl.BlockSpec((None, CONV_HALO, ch), lambda b, i: (b, 0, 0))] + [slab(a) for a in hosted],
        out_shape=[jax.ShapeDtypeStruct((n, t, ch), BF16), jax.ShapeDtypeStruct((n, CONV_HALO, ch), F32)]
        + [jax.ShapeDtypeStruct(a.shape, BF16) for a in hosted],
        scratch_shapes=[pltpu.VMEM((CONV_HALO + tt, ch), F32), pltpu.VMEM((tt, ch), F32)],
        compiler_params=_params(("arbitrary", "arbitrary"), blocks),
        name="glu_conv",
    )(xb, w, w, conv_w, conv_b.reshape(1, ch), ln_g.reshape(1, ch), ln_b.reshape(1, ch), *hosted)
    return cn, tail, (copies if hosted else [a.astype(BF16) for a in to_bf16])


def _decode_conv_kernel(state_ref, u_ref, w_ref, cb_ref, g_ref, b_ref, o_ref, new_state_ref):
    kw = w_ref.shape[0]
    u = u_ref[...]
    conv = u * w_ref[kw - 1:kw, :]
    for k in range(kw - 1):
        conv = conv + state_ref[k] * w_ref[k:k + 1, :]
    y = _layer_norm(conv + cb_ref[...], g_ref[...], b_ref[...])
    o_ref[...] = (y * _sigmoid(y)).astype(o_ref.dtype)
    for k in range(kw - 2):
        new_state_ref[k] = state_ref[k + 1]
    new_state_ref[kw - 2] = u


def _decode_conv(states, layer, u, conv_w, conv_b, ln_g, ln_b):
    _, s, hist, ch = states.shape
    kw = conv_w.shape[0]
    ts = _tile(s, 16)
    vec = pl.BlockSpec((1, ch), lambda i: (0, 0))
    tok = pl.BlockSpec((ts, ch), lambda i: (i, 0))
    blocks = 2 * hist * ts * ch * 4 + ts * ch * 6
    out, new_state = pl.pallas_call(
        _decode_conv_kernel,
        grid=(s // ts,),
        in_specs=[pl.BlockSpec((None, hist, ts, ch), lambda i: (layer, 0, i, 0)), tok,
                  pl.BlockSpec((kw, ch), lambda i: (0, 0)), vec, vec, vec],
        out_specs=[tok, pl.BlockSpec((hist, ts, ch), lambda i: (0, i, 0))],
        out_shape=[jax.ShapeDtypeStruct((s, ch), BF16), jax.ShapeDtypeStruct((hist, s, ch), F32)],
        compiler_params=_params(("parallel",), blocks),
        name="decode_conv",
    )(jnp.transpose(states, (0, 2, 1, 3)), u, conv_w, conv_b.reshape(1, ch), ln_g.reshape(1, ch), ln_b.reshape(1, ch))
    return out, jnp.transpose(new_state, (1, 0, 2))


def _merge_kernel(x_ref, a_ref, c_ref, wga_ref, wgc_ref, wao_ref, wco_ref, o_ref):
    x = x_ref[...]
    mixed = (_sigmoid(_dot(x, wga_ref[...])) * _dot(a_ref[...], wao_ref[...])
             + _sigmoid(_dot(x, wgc_ref[...])) * _dot(c_ref[...], wco_ref[...]))
    o_ref[...] = mixed.astype(o_ref.dtype)


def _merge(xb, attn, cn, w_in, col_ga, col_gc, w_ao, w_co, tm):
    m, d = xb.shape
    wa = attn.shape[1]
    wc = cn.shape[1]
    tn = _tile(d, 512)
    tm = _tile(m, tm)
    assert col_ga % tn == 0 and col_gc % tn == 0
    blocks = tm * (d + wa + wc + tn) * 2 + (2 * d + wa + wc) * tn * 2
    return pl.pallas_call(
        _merge_kernel,
        grid=(m // tm, d // tn),
        in_specs=[pl.BlockSpec((tm, d), lambda i, j: (i, 0)),
                  pl.BlockSpec((tm, wa), lambda i, j: (i, 0)),
                  pl.BlockSpec((tm, wc), lambda i, j: (i, 0)),
                  pl.BlockSpec((d, tn), lambda i, j: (0, col_ga // tn + j)),
                  pl.BlockSpec((d, tn), lambda i, j: (0, col_gc // tn + j)),
                  pl.BlockSpec((wa, tn), lambda i, j: (0, j)),
                  pl.BlockSpec((wc, tn), lambda i, j: (0, j))],
        out_specs=pl.BlockSpec((tm, tn), lambda i, j: (i, j)),
        out_shape=jax.ShapeDtypeStruct((m, d), BF16),
        compiler_params=_params(("parallel", "parallel"), blocks),
        name="merge",
    )(xb, attn, cn, w_in, w_in, w_ao, w_co)


def _out_ln_kernel(x_ref, mixed_ref, w_ref, g_ref, b_ref, of_ref, *maybe_ob_ref, alpha):
    tm = x_ref.shape[0]
    halves = [slice(0, tm // 2), slice(tm // 2, tm)] if tm % (4 * V7X_SUBLANES) == 0 else [slice(0, tm)]
    acc = [_dot(mixed_ref[rows, :], w_ref[...]) for rows in halves]
    for rows, a in zip(halves, acc):
        y = _layer_norm(alpha * x_ref[rows, :] + a, g_ref[...], b_ref[...])
        of_ref[rows, :] = y
        for ob_ref in maybe_ob_ref:
            ob_ref[rows, :] = y.astype(ob_ref.dtype)


def _out_ln(x, mixed, w_out, g, b, alpha, tm, emit_bf16):
    m, d = x.shape
    tm = _tile(m, tm)
    vec = pl.BlockSpec((1, d), lambda i: (0, 0))
    row = pl.BlockSpec((tm, d), lambda i: (i, 0))
    blocks = tm * d * (4 + 2 + 4 + 2) + d * d * 2
    return pl.pallas_call(
        functools.partial(_out_ln_kernel, alpha=alpha),
        grid=(m // tm,),
        in_specs=[row, row, pl.BlockSpec((d, d), lambda i: (0, 0)), vec, vec],
        out_specs=[row, row] if emit_bf16 else [row],
        out_shape=[jax.ShapeDtypeStruct((m, d), F32)] + ([jax.ShapeDtypeStruct((m, d), BF16)] if emit_bf16 else []),
        compiler_params=_params(("parallel",), blocks),
        name="out_ln",
    )(x, mixed, w_out, g.reshape(1, d), b.reshape(1, d))


def _mlp_ln_kernel(xf_ref, xb_ref, w1_ref, w2_ref, g_ref, b_ref, o_ref, *, alpha):
    f = pl.program_id(1)

    @pl.when(f == 0)
    def _():
        o_ref[...] = alpha * xf_ref[...]

    hid = jnp.maximum(_dot(xb_ref[...], w1_ref[...]), 0.0)
    o_ref[...] += _dot((hid * hid).astype(BF16), w2_ref[...])

    @pl.when(f == pl.num_programs(1) - 1)
    def _():
        o_ref[...] = _layer_norm(o_ref[...], g_ref[...], b_ref[...])


def _mlp_ln(xf, xb, w1, w2, g, b, alpha, tm):
    m, d = xf.shape
    dff = w1.shape[1]
    tf = _tile(dff, 1024)
    tm = _tile(m, tm)
    vec = pl.BlockSpec((1, d), lambda i, f: (0, 0))
    row = pl.BlockSpec((tm, d), lambda i, f: (i, 0))
    blocks = tm * d * (4 + 2 + 4) + 2 * d * tf * 2 + tm * tf * 4
    return pl.pallas_call(
        functools.partial(_mlp_ln_kernel, alpha=alpha),
        grid=(m // tm, dff // tf),
        in_specs=[row, row, pl.BlockSpec((d, tf), lambda i, f: (0, f)), pl.BlockSpec((tf, d), lambda i, f: (f, 0)),
                  vec, vec],
        out_specs=row,
        out_shape=jax.ShapeDtypeStruct((m, d), F32),
        compiler_params=_params(("parallel", "arbitrary"), blocks),
        name="mlp_ln",
    )(xf, xb, w1, w2, g.reshape(1, d), b.reshape(1, d))


def _mlp_ln_decode_kernel(pt_ref, xf_ref, w1_ref, w2_ref, g_ref, b_ref, q_ref, kn_ref, vn_ref, relb_ref, *refs,
                          alpha, pages_per_step, n_pages, pages_per_block, buckets):
    del pt_ref
    k_refs, v_refs = refs[:pages_per_step], refs[pages_per_step:2 * pages_per_step]
    o_ref, ao_ref, xb_ref, hid_ref, m_ref, l_ref, gs_ref, acc_ref = refs[2 * pages_per_step:]
    part = pl.program_id(2)
    f = pl.program_id(1) * pl.num_programs(2) + part
    nf = pl.num_programs(1) * pl.num_programs(2)

    @pl.when(f == 0)
    def _():
        x = xf_ref[...]
        o_ref[...] = alpha * x
        xb_ref[...] = x.astype(BF16)

    tf, d = w2_ref.shape
    wh = math.gcd(tf, MLP_CHUNK)
    wo = math.gcd(d, MLP_CHUNK)

    def hid_chunk(c):
        hid = jnp.maximum(_dot(xb_ref[...], w1_ref[:, c * wh:(c + 1) * wh]), 0.0)
        hid_ref[:, c * wh:(c + 1) * wh] = (hid * hid).astype(BF16)

    def out_chunk(c):
        o_ref[:, c * wo:(c + 1) * wo] += _dot(hid_ref[...], w2_ref[:, c * wo:(c + 1) * wo])

    chunks = ([functools.partial(hid_chunk, c) for c in range(tf // wh)]
              + [functools.partial(out_chunk, c) for c in range(d // wo)])
    _decode_pages(part, q_ref, kn_ref, vn_ref, relb_ref, k_refs, v_refs, ao_ref, m_ref, l_ref, gs_ref, acc_ref,
                  n_pages=n_pages, pages_per_block=pages_per_block, buckets=buckets, other_work=chunks)

    @pl.when(f == nf - 1)
    def _():
        o_ref[...] = _layer_norm(o_ref[...], g_ref[...], b_ref[...])


MLP_CHUNK = 256
FUSED_MLP_ROWS = 512
FUSED_MLP_COLS = 1024


def _fused_pages_per_step(m, dff, s, n_pages):
    nf = dff // _tile(dff, FUSED_MLP_COLS)
    steps = (m // _tile(m, FUSED_MLP_ROWS)) * nf
    if (s * n_pages) % steps or n_pages % ((s * n_pages) // steps) or nf % (n_pages // ((s * n_pages) // steps)):
        return None
    return (s * n_pages) // steps


def _mlp_ln_decode(xf, w1, w2, g, b, alpha, q, k_new, v_new, cache_k, cache_v, page_table, rel_bias, pps):
    m, d = xf.shape
    dff = w1.shape[1]
    tm, tf = _tile(m, FUSED_MLP_ROWS), _tile(dff, FUSED_MLP_COLS)
    nf = dff // tf
    s, h, dh = q.shape
    page = cache_k.shape[1]
    n_pages = page_table.shape[1]
    parts = n_pages // pps
    spr = nf // parts
    assert (m // tm) * nf == s * parts and nf == spr * parts
    buckets, relb, tok_spec, relb_spec, page_specs, scratch = _decode_specs(
        page_table, cache_k, rel_bias, h, dh, pps,
        seq_of=lambda i, j, a: i * spr + j, part_of=lambda i, j, a: a)
    vec = pl.BlockSpec((1, d), lambda i, j, a, pt: (0, 0))
    row = pl.BlockSpec((tm, d), lambda i, j, a, pt: (i, 0))
    blocks = tm * d * 9 + 2 * d * tf * 2 + tm * tf * 4 + 2 * pps * page * h * dh * 4 + relb.size * 4
    kernel = functools.partial(_mlp_ln_decode_kernel, alpha=alpha, pages_per_step=pps, n_pages=n_pages,
                               pages_per_block=MOBA_BLOCK // page, buckets=buckets)
    return pl.pallas_call(
        kernel,
        grid_spec=pltpu.PrefetchScalarGridSpec(
            num_scalar_prefetch=1,
            grid=(m // tm, spr, parts),
            in_specs=[row, pl.BlockSpec((d, tf), lambda i, j, a, pt: (0, j * parts + a)),
                      pl.BlockSpec((tf, d), lambda i, j, a, pt: (j * parts + a, 0)),
                      vec, vec, tok_spec, tok_spec, tok_spec, relb_spec] + page_specs + page_specs,
            out_specs=[row, tok_spec],
            scratch_shapes=[pltpu.VMEM((tm, d), BF16), pltpu.VMEM((tm, tf), BF16)] + scratch),
        out_shape=[jax.ShapeDtypeStruct((m, d), F32), jax.ShapeDtypeStruct((s, h, dh), BF16)],
        compiler_params=_params(("arbitrary", "arbitrary", "arbitrary"), blocks),
        name="mlp_ln_decode_attn",
    )(page_table, xf, w1, w2, g.reshape(1, d), b.reshape(1, d), q, k_new, v_new, relb,
      *([cache_k] * pps), *([cache_v] * pps))


def _merge_out(x2d, xb, attn, cn, lw, alpha, tm, emit_bf16):
    mixed = _merge(xb, attn, cn, lw["w_in"], lw["col_ga"], lw["col_gc"], lw["w_attn_out"], lw["w_conv_out"], tm=2 * tm)
    return _out_ln(x2d, mixed, lw["w_out"], lw["ln1_g"], lw["ln1_b"], alpha, tm, emit_bf16)


def kernel(x_prompt, x_sample, cache_k, cache_v, state_conv, page_table, rel_bias, w_in, w_attn_out, conv_w, conv_b,
           conv_ln_g, conv_ln_b, w_conv_out, w_out, ln1_g, ln1_b, w_ff1, w_ff2, ln2_g, ln2_b):
    depth = w_in.shape[0]
    n, t, d = x_prompt.shape
    s, ts, _ = x_sample.shape
    n_heads, dh = cache_k.shape[-2:]
    aw = n_heads * dh
    ch = conv_w.shape[-1]
    alpha = (2 * depth) ** 0.25
    scale = dh ** -0.5 * LOG2_E
    assert ts == 1 and w_in.shape[-1] == 3 * aw + 2 * ch + 2 * d
    col_val, col_gate = 3 * aw, 3 * aw + ch
    col_ga, col_gc = 3 * aw + 2 * ch, 3 * aw + 2 * ch + d

    bias_tiles = _prompt_bias_tiles(rel_bias)
    hp = x_prompt.reshape(n * t, d)
    hs = x_sample.reshape(s, d)
    outs = [[] for _ in range(6)]
    for l in range(depth):
        wi = w_in[l].astype(BF16)

        xb, q, kf, kb, vf, vb = _proj_qkv(hp, wi, aw, scale, tm=512, q_dtype=BF16)
        attn = _prompt_attention(q.reshape(n, t, aw), kb.reshape(n, t, aw), vb.reshape(n, t, aw), bias_tiles, n_heads)
        late = dict(w_attn_out=w_attn_out[l], w_conv_out=w_conv_out[l], w_out=w_out[l], w_ff1=w_ff1[l], w_ff2=w_ff2[l])
        cn, u_tail, late_bf16 = _prompt_glu_conv(xb.reshape(n, t, d), wi, col_val, col_gate, conv_w[l], conv_b[l],
                                                 conv_ln_g[l], conv_ln_b[l], to_bf16=tuple(late.values()))
        lw = dict(zip(late, late_bf16), w_in=wi, ln1_g=ln1_g[l], ln1_b=ln1_b[l], ln2_g=ln2_g[l], ln2_b=ln2_b[l],
                  col_ga=col_ga, col_gc=col_gc)
        outs[0].append(kf.reshape(n, t, n_heads, dh))
        outs[1].append(vf.reshape(n, t, n_heads, dh))
        outs[2].append(u_tail[:, CONV_HALO - (conv_w.shape[1] - 1):, :])

        xsb, qs, ksf, _, vsf, _ = _proj_qkv(hs, wi, aw, scale, tm=s, q_dtype=F32)
        us = _proj_glu(xsb, wi, col_val, col_gate, ch, tm=s)
        dec = (qs.reshape(s, n_heads, dh), ksf.reshape(s, n_heads, dh), vsf.reshape(s, n_heads, dh),
               cache_k[l], cache_v[l], page_table, rel_bias)

        mlp_w = (lw["w_ff1"], lw["w_ff2"], lw["ln2_g"], lw["ln2_b"], alpha)
        pps = _fused_pages_per_step(n * t, w_ff1.shape[-1], s, page_table.shape[1])
        x1 = _merge_out(hp, xb, attn.reshape(n * t, aw), cn.reshape(n * t, ch), lw, alpha, 512, emit_bf16=pps is None)
        if pps is None:
            hp_next = _mlp_ln(x1[0], x1[1], *mlp_w, tm=512)
            attn_s = _decode_attention(*dec)
        else:
            hp_next, attn_s = _mlp_ln_decode(x1[0], *mlp_w, *dec, pps)

        cn_s, new_state = _decode_conv(state_conv, l, us, conv_w[l], conv_b[l], conv_ln_g[l], conv_ln_b[l])
        x1s = _merge_out(hs, xsb, attn_s.reshape(s, aw), cn_s, lw, alpha, s, emit_bf16=True)
        hs_next = _mlp_ln(x1s[0], x1s[1], *mlp_w, tm=s)
        outs[3].append(ksf.reshape(s, 1, n_heads, dh))
        outs[4].append(vsf.reshape(s, 1, n_heads, dh))
        outs[5].append(new_state)
        hp, hs = hp_next, hs_next

    return (hp.reshape(n, t, d), hs.reshape(s, 1, d)) + tuple(jnp.stack(o) for o in outs)
```
